```python
import math
import jax, jax.numpy as jnp
from jax import lax
import numpy as np

D_MODEL = 2048
BATCH = 8
SEQ = 4096
DEPTH = 2

CTX_LEN = 256
GRID_W = 64
MIX_WIDTH = D_MODEL
N_MIXERS = 4
GROUP_W = MIX_WIDTH // N_MIXERS
S5_CH = 16
S5_GROUPS = GROUP_W // S5_CH
S5_STATE = 64
S5_DT_MIN = 1e-3
S5_DT_MAX = 1e-1
SG_HEADS = 4
SG_HEAD_DIM = GROUP_W // SG_HEADS
SG_CHUNK = 128
POOL_WINDOWS = (2, 4, 8, 16)
POOL_DIM = GROUP_W // len(POOL_WINDOWS)
M2_HEAD_DIM = 64
M2_HEADS = GROUP_W // M2_HEAD_DIM
M2_STATE = 128
M2_GROUPS = 2
M2_CONV = 4
M2_PAD = (M2_CONV // 2, M2_CONV - 1 - M2_CONV // 2)
M2_CHUNK = 128
M2_XBC = GROUP_W + 2 * M2_GROUPS * M2_STATE
M2_DT_MIN = 1e-3
M2_DT_MAX = 1e-1
IN_SIZES = (GROUP_W, GROUP_W, GROUP_W, GROUP_W, GROUP_W, M2_XBC, 2 * M2_HEADS)
IN_WIDTH = 5 * GROUP_W + M2_XBC + 2 * M2_HEADS
FFN_HIDDEN = ((8 * D_MODEL // 3 + 255) // 256) * 256
FFN_CONV = 3
DEEPNORM_ALPHA = (2 * DEPTH) ** 0.25
DEEPNORM_BETA = (8 * DEPTH) ** -0.25
LN_EPS = 1e-5
RMS_EPS = 1e-5

kernel_name = "hybrid_parallel_s5_gmlp_pool_ssd_dit"


def layer_norm(x, g, b):
    xf = x.astype(jnp.float32)
    mu = jnp.mean(xf, -1, keepdims=True)
    var = jnp.mean(jnp.square(xf - mu), -1, keepdims=True)
    return ((xf - mu) * lax.rsqrt(var + LN_EPS) * g + b).astype(x.dtype)


def modulate(x, shift, scale):
    return x * (1 + scale) + shift


def split_in(p):
    offs, o = [], 0
    for s in IN_SIZES[:-1]:
        o += s
        offs.append(o)
    return jnp.split(p, offs, axis=-1)


def dw_conv1d(x, w, b, pad):
    y = lax.conv_general_dilated(x, w[:, None, :].astype(x.dtype), window_strides=(1,), padding=[pad],
                                 dimension_numbers=('NWC', 'WIO', 'NWC'), feature_group_count=x.shape[-1])
    return y + b.astype(x.dtype)


def _lin_combine(e1, e2):
    a1, b1 = e1
    a2, b2 = e2
    return a2 * a1, a2 * b1 + b2


def s5_states(u, a_bar, b_bar, h0, reverse):
    bu = jnp.einsum('blgh,gph->blgp', u.astype(jnp.complex64), b_bar)
    a = jnp.broadcast_to(a_bar, (1, u.shape[1]) + a_bar.shape)
    a_cum, h = lax.associative_scan(_lin_combine, (a, bu), reverse=reverse, axis=1)
    return h + a_cum * h0[:, None]


def s5_mixer(u_ctx, u_lat, a_re, a_im, b_re, b_im, c_re, c_im, log_step, d, glu_w, glu_b, need_ctx):
    f32 = jnp.float32

    def groups(u):
        return u.astype(f32).reshape(u.shape[0], u.shape[1], S5_GROUPS, S5_CH)

    uc, ul = groups(u_ctx), groups(u_lat)
    ys_ctx, ys_lat = [], []
    for direction, reverse in enumerate((False, True)):
        lam = lax.complex(a_re[direction].astype(f32), a_im[direction].astype(f32))
        step = jnp.exp(log_step[direction].astype(f32))[:, None]
        a_bar = jnp.exp(lam * step)
        b_bar = ((a_bar - 1.0) / lam)[..., None] * lax.complex(b_re[direction].astype(f32),
                                                               b_im[direction].astype(f32))
        c_mat = lax.complex(c_re[direction].astype(f32), c_im[direction].astype(f32))
        h0 = jnp.zeros((uc.shape[0], S5_GROUPS, S5_STATE), jnp.complex64)
        hc = s5_states(uc, a_bar, b_bar, h0, reverse)
        hc_final = hc[:, 0] if reverse else hc[:, -1]
        hl = s5_states(ul, a_bar, b_bar, hc_final, reverse)
        ys_lat.append(jnp.einsum('blgp,ghp->blgh', hl, c_mat).real)
        if need_ctx:
            ys_ctx.append(jnp.einsum('blgp,ghp->blgh', hc, c_mat).real)

    def finish(ys, u, u_orig):
        y = (ys[0] + ys[1] + d.astype(f32).reshape(S5_GROUPS, S5_CH) * u).reshape(u_orig.shape)
        z = jax.nn.gelu(y)
        return (z * jax.nn.sigmoid(z @ glu_w.astype(f32) + glu_b.astype(f32))).astype(u_orig.dtype)

    y_lat = finish(ys_lat, ul, u_lat)
    y_ctx = finish(ys_ctx, uc, u_ctx) if need_ctx else None
    return y_ctx, y_lat


def chunk_gating(u, v, ln_g, ln_b, w_s, b_s):
    bsz, l, _ = u.shape
    u = jax.nn.gelu(u)
    v = jax.nn.gelu(v).reshape(bsz, l // SG_CHUNK, SG_CHUNK, SG_HEADS, SG_HEAD_DIM)
    v = layer_norm(v, ln_g.reshape(SG_HEADS, SG_HEAD_DIM), ln_b.reshape(SG_HEADS, SG_HEAD_DIM))
    s = jnp.einsum('hij,bcjhd->bcihd', w_s, v) + b_s.T[None, None, :, :, None]
    return u * s.reshape(bsz, l, GROUP_W)


def pool_mixer(p, w, bias, scale):
    bsz, l, _ = p.shape
    pf = p.astype(jnp.float32)
    csum = jnp.pad(jnp.cumsum(pf, axis=1), ((0, 0), (1, 0), (0, 0)))
    t = jnp.arange(l)
    outs = []
    for g, win in enumerate(POOL_WINDOWS):
        lo = jnp.clip(t - win // 2, 0, l - 1)
        hi = jnp.clip(t + win // 2 - 1, 0, l - 1)
        sl = slice(g * POOL_DIM, (g + 1) * POOL_DIM)
        cs = csum[..., sl]
        total = jnp.take(cs, hi + 1, axis=1) - jnp.take(cs, lo, axis=1)
        mean = total / (hi - lo + 1).astype(jnp.float32)[None, :, None]
        outs.append(jnp.einsum('blc,cd->bld', mean - pf[..., sl], w[g]))
    y = jnp.concatenate(outs, axis=-1) + bias
    return (y * scale).astype(p.dtype)


def m2_prepare(xbc, dt_raw, conv_w, conv_b, dt_bias):
    f32 = jnp.float32
    xbc = jax.nn.silu(dw_conv1d(xbc, conv_w, conv_b, M2_PAD)).astype(f32)
    bsz, l, _ = xbc.shape
    n_bc = M2_GROUPS * M2_STATE
    rep = M2_HEADS // M2_GROUPS
    xs = xbc[..., :GROUP_W].reshape(bsz, l, M2_HEADS, M2_HEAD_DIM)
    bm = jnp.repeat(xbc[..., GROUP_W:GROUP_W + n_bc].reshape(bsz, l, M2_GROUPS, M2_STATE), rep, axis=2)
    cm = jnp.repeat(xbc[..., GROUP_W + n_bc:].reshape(bsz, l, M2_GROUPS, M2_STATE), rep, axis=2)
    dt = jax.nn.softplus(dt_raw.astype(f32).reshape(bsz, l, 2, M2_HEADS) + dt_bias.astype(f32))
    return xs, bm, cm, dt


def ssd_scan(xs, dt, a, bm, cm, h0, need_y):
    bsz, l, nh, hp = xs.shape
    nc = l // M2_CHUNK

    def chunks(t):
        return t.reshape((bsz, nc, M2_CHUNK) + t.shape[2:])

    xd = chunks(xs * dt[..., None])
    bc, cc = chunks(bm), chunks(cm)
    a_cum = jnp.cumsum(chunks(dt * a), axis=2)
    a_tot = a_cum[:, :, -1]
    decay_end = jnp.exp(a_tot[:, :, None] - a_cum)
    chunk_states = jnp.einsum('bcqhn,bcqh,bcqhp->bchpn', bc, decay_end, xd)

    def step(s, inp):
        tot, st = inp
        return jnp.exp(tot)[..., None, None] * s + st, s

    h_final, h_prev = lax.scan(step, h0, (jnp.moveaxis(a_tot, 1, 0), jnp.moveaxis(chunk_states, 1, 0)))
    if not need_y:
        return None, h_final
    h_prev = jnp.moveaxis(h_prev, 0, 1)
    seg = a_cum[:, :, :, None, :] - a_cum[:, :, None, :, :]
    lower = jnp.tril(jnp.ones((M2_CHUNK, M2_CHUNK), bool))[None, None, :, :, None]
    decay = jnp.exp(jnp.where(lower, seg, -jnp.inf))
    scores = jnp.einsum('bcihn,bcjhn->bcijh', cc, bc) * decay
    y = (jnp.einsum('bcijh,bcjhp->bcihp', scores, xd)
         + jnp.einsum('bcihn,bchpn->bcihp', cc, h_prev) * jnp.exp(a_cum)[..., None])
    return y.reshape(bsz, l, nh, hp), h_final


def ssd_direction(inputs, direction, a, h0, need_y):
    xs, bm, cm, dt = inputs
    dt = dt[:, :, direction]
    if direction == 1:
        xs, bm, cm, dt = (jnp.flip(t, 1) for t in (xs, bm, cm, dt))
    y, h_final = ssd_scan(xs, dt, a, bm, cm, h0, need_y)
    if direction == 1 and y is not None:
        y = jnp.flip(y, 1)
    return y, h_final


def gated_rmsnorm(y, z, w):
    bsz, l, _ = z.shape
    g = (y * jax.nn.silu(z.astype(jnp.float32))).reshape(bsz, l, M2_GROUPS, GROUP_W // M2_GROUPS)
    g = g * lax.rsqrt(jnp.mean(jnp.square(g), -1, keepdims=True) + RMS_EPS)
    return (g.reshape(bsz, l, GROUP_W) * w.astype(jnp.float32)).astype(z.dtype)


def mamba2_mixer(z_ctx, xbc_ctx, dt_ctx, z_lat, xbc_lat, dt_lat, conv_w, conv_b, dt_bias, a_log, d,
                 norm_w, need_ctx):
    ctx_in = m2_prepare(xbc_ctx, dt_ctx, conv_w, conv_b, dt_bias)
    lat_in = m2_prepare(xbc_lat, dt_lat, conv_w, conv_b, dt_bias)
    a = -jnp.exp(a_log.astype(jnp.float32))
    ys_ctx, ys_lat = [], []
    for direction in range(2):
        h0 = jnp.zeros((ctx_in[0].shape[0], M2_HEADS, M2_HEAD_DIM, M2_STATE), jnp.float32)
        y_c, h_c = ssd_direction(ctx_in, direction, a[direction], h0, need_ctx)
        y_l, _ = ssd_direction(lat_in, direction, a[direction], h_c, True)
        ys_lat.append(y_l)
        if need_ctx:
            ys_ctx.append(y_c)
    d_h = d.astype(jnp.float32)[None, None, :, None]

    def finish(ys, xs, z):
        y = (ys[0] + ys[1] + d_h * xs).reshape(z.shape[0], z.shape[1], GROUP_W)
        return gated_rmsnorm(y, z, norm_w)

    y_lat = finish(ys_lat, lat_in[0], z_lat)
    y_ctx = finish(ys_ctx, ctx_in[0], z_ctx) if need_ctx else None
    return y_ctx, y_lat


def conv_ffn(h, w_up, conv_w, conv_b, w_down, rows):
    gate, val = jnp.split(h @ w_up, 2, axis=-1)
    bsz, l, f = gate.shape
    if rows is None:
        gate = dw_conv1d(gate, conv_w[FFN_CONV // 2], conv_b, (FFN_CONV // 2, FFN_CONV // 2))
    else:
        g2 = gate.reshape(bsz, rows, GRID_W, f)
        g2 = lax.conv_general_dilated(g2, conv_w[:, :, None, :].astype(g2.dtype), (1, 1), 'SAME',
                                      dimension_numbers=('NHWC', 'HWIO', 'NHWC'), feature_group_count=f)
        gate = g2.reshape(bsz, l, f) + conv_b.astype(gate.dtype)
    return (jax.nn.gelu(gate) * val) @ w_down


def _fwd_setup_inputs(seed: int = 0) -> dict:
    key = jax.random.key(seed)
    ks = iter(jax.random.split(key, 64))
    f32 = jnp.float32
    L, D, F = DEPTH, D_MODEL, FFN_HIDDEN

    def nrm(shape, s):
        return s * jax.random.normal(next(ks), shape, f32)

    def unif(shape, lo, hi):
        return jax.random.uniform(next(ks), shape, f32, lo, hi)

    n_idx = jnp.arange(S5_STATE, dtype=f32)
    m2_dt = jnp.exp(unif((L, 2, M2_HEADS), math.log(M2_DT_MIN), math.log(M2_DT_MAX)))
    return {
        'x': nrm((BATCH, SEQ, D), 1.0),
        'c': nrm((BATCH, D), 1.0),
        'ctx': nrm((BATCH, CTX_LEN, D), 1.0),
        'c_ctx': nrm((D,), 1.0),
        'w_ada': nrm((L, D, 6 * D), 0.5 * D ** -0.5),
        'b_ada': nrm((L, 6 * D), 0.01),
        'w_in': nrm((L, D, IN_WIDTH), D ** -0.5),
        'w_out': nrm((L, MIX_WIDTH, D), DEEPNORM_BETA * MIX_WIDTH ** -0.5),
        'ln1_g': 1.0 + nrm((L, D), 0.02),
        'ln1_b': nrm((L, D), 0.02),
        'ln2_g': 1.0 + nrm((L, D), 0.02),
        'ln2_b': nrm((L, D), 0.02),
        's5_a_re': -0.5 + nrm((L, 2, S5_GROUPS, S5_STATE), 0.01),
        's5_a_im': math.pi * n_idx + nrm((L, 2, S5_GROUPS, S5_STATE), 0.01),
        's5_b_re': nrm((L, 2, S5_GROUPS, S5_STATE, S5_CH), (2 * S5_CH) ** -0.5),
        's5_b_im': nrm((L, 2, S5_GROUPS, S5_STATE, S5_CH), (2 * S5_CH) ** -0.5),
        's5_c_re': nrm((L, 2, S5_GROUPS, S5_CH, S5_STATE), 0.5),
        's5_c_im': nrm((L, 2, S5_GROUPS, S5_CH, S5_STATE), 0.5),
        's5_log_step': unif((L, 2, S5_GROUPS), math.log(S5_DT_MIN), math.log(S5_DT_MAX)),
        's5_d': nrm((L, GROUP_W), 1.0),
        's5_glu_w': nrm((L, GROUP_W, GROUP_W), GROUP_W ** -0.5),
        's5_glu_b': nrm((L, GROUP_W), 0.01),
        'sg_ln_g': 1.0 + nrm((L, GROUP_W), 0.02),
        'sg_ln_b': nrm((L, GROUP_W), 0.02),
        'sg_w': nrm((L, SG_HEADS, SG_CHUNK, SG_CHUNK), SG_CHUNK ** -0.5),
        'sg_b': 1.0 + nrm((L, SG_HEADS, SG_CHUNK), 0.01),
        'pool_w': nrm((L, len(POOL_WINDOWS), POOL_DIM, POOL_DIM), POOL_DIM ** -0.5),
        'pool_b': nrm((L, GROUP_W), 0.01),
        'pool_scale': 1.0 + nrm((L, GROUP_W), 0.02),
        'm2_conv_w': nrm((L, M2_CONV, M2_XBC), M2_CONV ** -0.5),
        'm2_conv_b': nrm((L, M2_XBC), 0.01),
        'm2_dt_bias': m2_dt + jnp.log(-jnp.expm1(-m2_dt)),
        'm2_a_log': jnp.log(unif((L, 2, M2_HEADS), 1.0, 16.0)),
        'm2_d': 1.0 + nrm((L, M2_HEADS), 0.01),
        'm2_norm_w': 1.0 + nrm((L, GROUP_W), 0.02),
        'ffn_w_up': nrm((L, D, 2 * F), D ** -0.5),
        'ffn_conv_w': nrm((L, FFN_CONV, FFN_CONV, F), 1.0 / FFN_CONV),
        'ffn_conv_b': nrm((L, F), 0.01),
        'ffn_w_down': nrm((L, F, D), DEEPNORM_BETA * F ** -0.5),
    }


def _fwd_reference(x, c, ctx, c_ctx, w_ada, b_ada, w_in, w_out, ln1_g, ln1_b, ln2_g, ln2_b,
              s5_a_re, s5_a_im, s5_b_re, s5_b_im, s5_c_re, s5_c_im, s5_log_step, s5_d, s5_glu_w, s5_glu_b,
              sg_ln_g, sg_ln_b, sg_w, sg_b, pool_w, pool_b, pool_scale,
              m2_conv_w, m2_conv_b, m2_dt_bias, m2_a_log, m2_d, m2_norm_w,
              ffn_w_up, ffn_conv_w, ffn_conv_b, ffn_w_down):
    h_lat, h_ctx = x, ctx
    rows = h_lat.shape[1] // GRID_W
    silu_c = jax.nn.silu(c)[:, None, :]
    silu_cc = jax.nn.silu(c_ctx)[None, None, :]
    for i in range(DEPTH):
        need_ctx = i < DEPTH - 1
        m_lat = jnp.split(silu_c @ w_ada[i] + b_ada[i], 6, axis=-1)
        m_ctx = jnp.split(silu_cc @ w_ada[i] + b_ada[i], 6, axis=-1)

        pl = split_in(modulate(h_lat, m_lat[0], m_lat[1]) @ w_in[i])
        pc = split_in(modulate(h_ctx, m_ctx[0], m_ctx[1]) @ w_in[i])
        ya_ctx, ya_lat = s5_mixer(pc[0], pl[0], s5_a_re[i], s5_a_im[i], s5_b_re[i], s5_b_im[i],
                                  s5_c_re[i], s5_c_im[i], s5_log_step[i], s5_d[i], s5_glu_w[i],
                                  s5_glu_b[i], need_ctx)
        yd_ctx, yd_lat = mamba2_mixer(pc[4], pc[5], pc[6], pl[4], pl[5], pl[6], m2_conv_w[i], m2_conv_b[i],
                                      m2_dt_bias[i], m2_a_log[i], m2_d[i], m2_norm_w[i], need_ctx)
        mix_lat = jnp.concatenate([
            ya_lat,
            chunk_gating(pl[1], pl[2], sg_ln_g[i], sg_ln_b[i], sg_w[i], sg_b[i]),
            pool_mixer(pl[3], pool_w[i], pool_b[i], pool_scale[i]),
            yd_lat], axis=-1) @ w_out[i]
        h_lat = layer_norm(DEEPNORM_ALPHA * h_lat + m_lat[2] * mix_lat, ln1_g[i], ln1_b[i])
        if need_ctx:
            mix_ctx = jnp.concatenate([
                ya_ctx,
                chunk_gating(pc[1], pc[2], sg_ln_g[i], sg_ln_b[i], sg_w[i], sg_b[i]),
                pool_mixer(pc[3], pool_w[i], pool_b[i], pool_scale[i]),
                yd_ctx], axis=-1) @ w_out[i]
            h_ctx = layer_norm(DEEPNORM_ALPHA * h_ctx + m_ctx[2] * mix_ctx, ln1_g[i], ln1_b[i])

        f_lat = conv_ffn(modulate(h_lat, m_lat[3], m_lat[4]), ffn_w_up[i], ffn_conv_w[i], ffn_conv_b[i],
                         ffn_w_down[i], rows)
        h_lat = layer_norm(DEEPNORM_ALPHA * h_lat + m_lat[5] * f_lat, ln2_g[i], ln2_b[i])
        if need_ctx:
            f_ctx = conv_ffn(modulate(h_ctx, m_ctx[3], m_ctx[4]), ffn_w_up[i], ffn_conv_w[i], ffn_conv_b[i],
                             ffn_w_down[i], None)
            h_ctx = layer_norm(DEEPNORM_ALPHA * h_ctx + m_ctx[5] * f_ctx, ln2_g[i], ln2_b[i])
    return h_lat


import jax as _jax
import jax.numpy as _jnp

TWIN_FORMAT = 'train_step'
FWD_PARAMS = ['x', 'c', 'ctx', 'c_ctx', 'w_ada', 'b_ada', 'w_in', 'w_out', 'ln1_g', 'ln1_b', 'ln2_g', 'ln2_b', 's5_a_re', 's5_a_im', 's5_b_re', 's5_b_im', 's5_c_re', 's5_c_im', 's5_log_step', 's5_d', 's5_glu_w', 's5_glu_b', 'sg_ln_g', 'sg_ln_b', 'sg_w', 'sg_b', 'pool_w', 'pool_b', 'pool_scale', 'm2_conv_w', 'm2_conv_b', 'm2_dt_bias', 'm2_a_log', 'm2_d', 'm2_norm_w', 'ffn_w_up', 'ffn_conv_w', 'ffn_conv_b', 'ffn_w_down']
TWIN_WEIGHTS = ['c_ctx', 'w_ada', 'b_ada', 'w_in', 'w_out', 'ln1_g', 'ln1_b', 'ln2_g', 'ln2_b', 's5_a_re', 's5_a_im', 's5_b_re', 's5_b_im', 's5_c_re', 's5_c_im', 's5_log_step', 's5_d', 's5_glu_w', 's5_glu_b', 'sg_ln_g', 'sg_ln_b', 'sg_w', 'sg_b', 'pool_w', 'pool_b', 'pool_scale', 'm2_conv_w', 'm2_conv_b', 'm2_dt_bias', 'm2_a_log', 'm2_d', 'm2_norm_w', 'ffn_w_up', 'ffn_conv_w', 'ffn_conv_b', 'ffn_w_down']
TWIN_DIFF_INPUT = 'x'
TWIN_INPUTS = ['x', 'c', 'ctx', 'c_ctx', 'w_ada', 'b_ada', 'w_in', 'w_out', 'ln1_g', 'ln1_b', 'ln2_g', 'ln2_b', 's5_a_re', 's5_a_im', 's5_b_re', 's5_b_im', 's5_c_re', 's5_c_im', 's5_log_step', 's5_d', 's5_glu_w', 's5_glu_b', 'sg_ln_g', 'sg_ln_b', 'sg_w', 'sg_b', 'pool_w', 'pool_b', 'pool_scale', 'm2_conv_w', 'm2_conv_b', 'm2_dt_bias', 'm2_a_log', 'm2_d', 'm2_norm_w', 'ffn_w_up', 'ffn_conv_w', 'ffn_conv_b', 'ffn_w_down', 'loss_target', 'm_c_ctx', 'm_w_ada', 'm_b_ada', 'm_w_in', 'm_w_out', 'm_ln1_g', 'm_ln1_b', 'm_ln2_g', 'm_ln2_b', 'm_s5_a_re', 'm_s5_a_im', 'm_s5_b_re', 'm_s5_b_im', 'm_s5_c_re', 'm_s5_c_im', 'm_s5_log_step', 'm_s5_d', 'm_s5_glu_w', 'm_s5_glu_b', 'm_sg_ln_g', 'm_sg_ln_b', 'm_sg_w', 'm_sg_b', 'm_pool_w', 'm_pool_b', 'm_pool_scale', 'm_m2_conv_w', 'm_m2_conv_b', 'm_m2_dt_bias', 'm_m2_a_log', 'm_m2_d', 'm_m2_norm_w', 'm_ffn_w_up', 'm_ffn_conv_w', 'm_ffn_conv_b', 'm_ffn_w_down', 'v_c_ctx', 'v_w_ada', 'v_b_ada', 'v_w_in', 'v_w_out', 'v_ln1_g', 'v_ln1_b', 'v_ln2_g', 'v_ln2_b', 'v_s5_a_re', 'v_s5_a_im', 'v_s5_b_re', 'v_s5_b_im', 'v_s5_c_re', 'v_s5_c_im', 'v_s5_log_step', 'v_s5_d', 'v_s5_glu_w', 'v_s5_glu_b', 'v_sg_ln_g', 'v_sg_ln_b', 'v_sg_w', 'v_sg_b', 'v_pool_w', 'v_pool_b', 'v_pool_scale', 'v_m2_conv_w', 'v_m2_conv_b', 'v_m2_dt_bias', 'v_m2_a_log', 'v_m2_d', 'v_m2_norm_w', 'v_ffn_w_up', 'v_ffn_conv_w', 'v_ffn_conv_b', 'v_ffn_w_down']
TWIN_OUTPUTS = ['loss', 'grad_x', 'grad_c_ctx', 'grad_w_ada', 'grad_b_ada', 'grad_w_in', 'grad_w_out', 'grad_ln1_g', 'grad_ln1_b', 'grad_ln2_g', 'grad_ln2_b', 'grad_s5_a_re', 'grad_s5_a_im', 'grad_s5_b_re', 'grad_s5_b_im', 'grad_s5_c_re', 'grad_s5_c_im', 'grad_s5_log_step', 'grad_s5_d', 'grad_s5_glu_w', 'grad_s5_glu_b', 'grad_sg_ln_g', 'grad_sg_ln_b', 'grad_sg_w', 'grad_sg_b', 'grad_pool_w', 'grad_pool_b', 'grad_pool_scale', 'grad_m2_conv_w', 'grad_m2_conv_b', 'grad_m2_dt_bias', 'grad_m2_a_log', 'grad_m2_d', 'grad_m2_norm_w', 'grad_ffn_w_up', 'grad_ffn_conv_w', 'grad_ffn_conv_b', 'grad_ffn_w_down', 'delta_c_ctx', 'delta_w_ada', 'delta_b_ada', 'delta_w_in', 'delta_w_out', 'delta_ln1_g', 'delta_ln1_b', 'delta_ln2_g', 'delta_ln2_b', 'delta_s5_a_re', 'delta_s5_a_im', 'delta_s5_b_re', 'delta_s5_b_im', 'delta_s5_c_re', 'delta_s5_c_im', 'delta_s5_log_step', 'delta_s5_d', 'delta_s5_glu_w', 'delta_s5_glu_b', 'delta_sg_ln_g', 'delta_sg_ln_b', 'delta_sg_w', 'delta_sg_b', 'delta_pool_w', 'delta_pool_b', 'delta_pool_scale', 'delta_m2_conv_w', 'delta_m2_conv_b', 'delta_m2_dt_bias', 'delta_m2_a_log', 'delta_m2_d', 'delta_m2_norm_w', 'delta_ffn_w_up', 'delta_ffn_conv_w', 'delta_ffn_conv_b', 'delta_ffn_w_down', 'new_m_c_ctx', 'new_m_w_ada', 'new_m_b_ada', 'new_m_w_in', 'new_m_w_out', 'new_m_ln1_g', 'new_m_ln1_b', 'new_m_ln2_g', 'new_m_ln2_b', 'new_m_s5_a_re', 'new_m_s5_a_im', 'new_m_s5_b_re', 'new_m_s5_b_im', 'new_m_s5_c_re', 'new_m_s5_c_im', 'new_m_s5_log_step', 'new_m_s5_d', 'new_m_s5_glu_w', 'new_m_s5_glu_b', 'new_m_sg_ln_g', 'new_m_sg_ln_b', 'new_m_sg_w', 'new_m_sg_b', 'new_m_pool_w', 'new_m_pool_b', 'new_m_pool_scale', 'new_m_m2_conv_w', 'new_m_m2_conv_b', 'new_m_m2_dt_bias', 'new_m_m2_a_log', 'new_m_m2_d', 'new_m_m2_norm_w', 'new_m_ffn_w_up', 'new_m_ffn_conv_w', 'new_m_ffn_conv_b', 'new_m_ffn_w_down', 'new_v_c_ctx', 'new_v_w_ada', 'new_v_b_ada', 'new_v_w_in', 'new_v_w_out', 'new_v_ln1_g', 'new_v_ln1_b', 'new_v_ln2_g', 'new_v_ln2_b', 'new_v_s5_a_re', 'new_v_s5_a_im', 'new_v_s5_b_re', 'new_v_s5_b_im', 'new_v_s5_c_re', 'new_v_s5_c_im', 'new_v_s5_log_step', 'new_v_s5_d', 'new_v_s5_glu_w', 'new_v_s5_glu_b', 'new_v_sg_ln_g', 'new_v_sg_ln_b', 'new_v_sg_w', 'new_v_sg_b', 'new_v_pool_w', 'new_v_pool_b', 'new_v_pool_scale', 'new_v_m2_conv_w', 'new_v_m2_conv_b', 'new_v_m2_dt_bias', 'new_v_m2_a_log', 'new_v_m2_d', 'new_v_m2_norm_w', 'new_v_ffn_w_up', 'new_v_ffn_conv_w', 'new_v_ffn_conv_b', 'new_v_ffn_w_down']
TWIN_LEAF_KINDS = {'loss': 'loss', 'grad_x': 'grad_x', 'grad_c_ctx': 'grad_w', 'grad_w_ada': 'grad_w', 'grad_b_ada': 'grad_w', 'grad_w_in': 'grad_w', 'grad_w_out': 'grad_w', 'grad_ln1_g': 'grad_w', 'grad_ln1_b': 'grad_w', 'grad_ln2_g': 'grad_w', 'grad_ln2_b': 'grad_w', 'grad_s5_a_re': 'grad_w', 'grad_s5_a_im': 'grad_w', 'grad_s5_b_re': 'grad_w', 'grad_s5_b_im': 'grad_w', 'grad_s5_c_re': 'grad_w', 'grad_s5_c_im': 'grad_w', 'grad_s5_log_step': 'grad_w', 'grad_s5_d': 'grad_w', 'grad_s5_glu_w': 'grad_w', 'grad_s5_glu_b': 'grad_w', 'grad_sg_ln_g': 'grad_w', 'grad_sg_ln_b': 'grad_w', 'grad_sg_w': 'grad_w', 'grad_sg_b': 'grad_w', 'grad_pool_w': 'grad_w', 'grad_pool_b': 'grad_w', 'grad_pool_scale': 'grad_w', 'grad_m2_conv_w': 'grad_w', 'grad_m2_conv_b': 'grad_w', 'grad_m2_dt_bias': 'grad_w', 'grad_m2_a_log': 'grad_w', 'grad_m2_d': 'grad_w', 'grad_m2_norm_w': 'grad_w', 'grad_ffn_w_up': 'grad_w', 'grad_ffn_conv_w': 'grad_w', 'grad_ffn_conv_b': 'grad_w', 'grad_ffn_w_down': 'grad_w', 'delta_c_ctx': 'delta_w', 'delta_w_ada': 'delta_w', 'delta_b_ada': 'delta_w', 'delta_w_in': 'delta_w', 'delta_w_out': 'delta_w', 'delta_ln1_g': 'delta_w', 'delta_ln1_b': 'delta_w', 'delta_ln2_g': 'delta_w', 'delta_ln2_b': 'delta_w', 'delta_s5_a_re': 'delta_w', 'delta_s5_a_im': 'delta_w', 'delta_s5_b_re': 'delta_w', 'delta_s5_b_im': 'delta_w', 'delta_s5_c_re': 'delta_w', 'delta_s5_c_im': 'delta_w', 'delta_s5_log_step': 'delta_w', 'delta_s5_d': 'delta_w', 'delta_s5_glu_w': 'delta_w', 'delta_s5_glu_b': 'delta_w', 'delta_sg_ln_g': 'delta_w', 'delta_sg_ln_b': 'delta_w', 'delta_sg_w': 'delta_w', 'delta_sg_b': 'delta_w', 'delta_pool_w': 'delta_w', 'delta_pool_b': 'delta_w', 'delta_pool_scale': 'delta_w', 'delta_m2_conv_w': 'delta_w', 'delta_m2_conv_b': 'delta_w', 'delta_m2_dt_bias': 'delta_w', 'delta_m2_a_log': 'delta_w', 'delta_m2_d': 'delta_w', 'delta_m2_norm_w': 'delta_w', 'delta_ffn_w_up': 'delta_w', 'delta_ffn_conv_w': 'delta_w', 'delta_ffn_conv_b': 'delta_w', 'delta_ffn_w_down': 'delta_w', 'new_m_c_ctx': 'new_m', 'new_m_w_ada': 'new_m', 'new_m_b_ada': 'new_m', 'new_m_w_in': 'new_m', 'new_m_w_out': 'new_m', 'new_m_ln1_g': 'new_m', 'new_m_ln1_b': 'new_m', 'new_m_ln2_g': 'new_m', 'new_m_ln2_b': 'new_m', 'new_m_s5_a_re': 'new_m', 'new_m_s5_a_im': 'new_m', 'new_m_s5_b_re': 'new_m', 'new_m_s5_b_im': 'new_m', 'new_m_s5_c_re': 'new_m', 'new_m_s5_c_im': 'new_m', 'new_m_s5_log_step': 'new_m', 'new_m_s5_d': 'new_m', 'new_m_s5_glu_w': 'new_m', 'new_m_s5_glu_b': 'new_m', 'new_m_sg_ln_g': 'new_m', 'new_m_sg_ln_b': 'new_m', 'new_m_sg_w': 'new_m', 'new_m_sg_b': 'new_m', 'new_m_pool_w': 'new_m', 'new_m_pool_b': 'new_m', 'new_m_pool_scale': 'new_m', 'new_m_m2_conv_w': 'new_m', 'new_m_m2_conv_b': 'new_m', 'new_m_m2_dt_bias': 'new_m', 'new_m_m2_a_log': 'new_m', 'new_m_m2_d': 'new_m', 'new_m_m2_norm_w': 'new_m', 'new_m_ffn_w_up': 'new_m', 'new_m_ffn_conv_w': 'new_m', 'new_m_ffn_conv_b': 'new_m', 'new_m_ffn_w_down': 'new_m', 'new_v_c_ctx': 'new_v', 'new_v_w_ada': 'new_v', 'new_v_b_ada': 'new_v', 'new_v_w_in': 'new_v', 'new_v_w_out': 'new_v', 'new_v_ln1_g': 'new_v', 'new_v_ln1_b': 'new_v', 'new_v_ln2_g': 'new_v', 'new_v_ln2_b': 'new_v', 'new_v_s5_a_re': 'new_v', 'new_v_s5_a_im': 'new_v', 'new_v_s5_b_re': 'new_v', 'new_v_s5_b_im': 'new_v', 'new_v_s5_c_re': 'new_v', 'new_v_s5_c_im': 'new_v', 'new_v_s5_log_step': 'new_v', 'new_v_s5_d': 'new_v', 'new_v_s5_glu_w': 'new_v', 'new_v_s5_glu_b': 'new_v', 'new_v_sg_ln_g': 'new_v', 'new_v_sg_ln_b': 'new_v', 'new_v_sg_w': 'new_v', 'new_v_sg_b': 'new_v', 'new_v_pool_w': 'new_v', 'new_v_pool_b': 'new_v', 'new_v_pool_scale': 'new_v', 'new_v_m2_conv_w': 'new_v', 'new_v_m2_conv_b': 'new_v', 'new_v_m2_dt_bias': 'new_v', 'new_v_m2_a_log': 'new_v', 'new_v_m2_d': 'new_v', 'new_v_m2_norm_w': 'new_v', 'new_v_ffn_w_up': 'new_v', 'new_v_ffn_conv_w': 'new_v', 'new_v_ffn_conv_b': 'new_v', 'new_v_ffn_w_down': 'new_v'}


def _forward(args):
    return _fwd_reference(*[args[k] for k in FWD_PARAMS])


def _output_shape():
    def fwd():
        inp = _fwd_setup_inputs(0)
        return _fwd_reference(*[inp[k] for k in FWD_PARAMS])
    out = _jax.eval_shape(fwd)
    return out.shape, out.dtype

N_MICROBATCH = 1
ADAM_LR = 0.001
ADAM_B1 = 0.9
ADAM_B2 = 0.999
ADAM_EPS = 1e-08
ADAM_WD = 0.01
ADAM_STEP = 10
PER_EXAMPLE_BATCH_AXIS = {'x': 0, 'c': 0, 'ctx': 0, 'loss_target': 0}
SHARED_INPUTS = []
_WEIGHT_DTYPES = {'c_ctx': _jnp.float32, 'w_ada': _jnp.float32, 'b_ada': _jnp.float32, 'w_in': _jnp.float32, 'w_out': _jnp.float32, 'ln1_g': _jnp.float32, 'ln1_b': _jnp.float32, 'ln2_g': _jnp.float32, 'ln2_b': _jnp.float32, 's5_a_re': _jnp.float32, 's5_a_im': _jnp.float32, 's5_b_re': _jnp.float32, 's5_b_im': _jnp.float32, 's5_c_re': _jnp.float32, 's5_c_im': _jnp.float32, 's5_log_step': _jnp.float32, 's5_d': _jnp.float32, 's5_glu_w': _jnp.float32, 's5_glu_b': _jnp.float32, 'sg_ln_g': _jnp.float32, 'sg_ln_b': _jnp.float32, 'sg_w': _jnp.float32, 'sg_b': _jnp.float32, 'pool_w': _jnp.float32, 'pool_b': _jnp.float32, 'pool_scale': _jnp.float32, 'm2_conv_w': _jnp.float32, 'm2_conv_b': _jnp.float32, 'm2_dt_bias': _jnp.float32, 'm2_a_log': _jnp.float32, 'm2_d': _jnp.float32, 'm2_norm_w': _jnp.float32, 'ffn_w_up': _jnp.float32, 'ffn_conv_w': _jnp.float32, 'ffn_conv_b': _jnp.float32, 'ffn_w_down': _jnp.float32}
MOMENT_SCALE = {'c_ctx': 8.081026e-04, 'w_ada': 1.087873e-02, 'b_ada': 1.941162e-02, 'w_in': 7.910805e-03, 'w_out': 1.728303e-02, 'ln1_g': 5.921014e-01, 'ln1_b': 2.745330e-01, 'ln2_g': 1.134764e+01, 'ln2_b': 4.897750e-01, 's5_a_re': 1.258273e-03, 's5_a_im': 1.481784e-03, 's5_b_re': 9.516035e-04, 's5_b_im': 8.853102e-04, 's5_c_re': 3.276151e-04, 's5_c_im': 3.218953e-04, 's5_log_step': 7.479817e-01, 's5_d': 5.054192e-03, 's5_glu_w': 1.509379e-03, 's5_glu_b': 1.999541e-03, 'sg_ln_g': 6.740095e-03, 'sg_ln_b': 7.109153e-03, 'sg_w': 6.747514e-03, 'sg_b': 6.922742e-03, 'pool_w': 8.549196e-03, 'pool_b': 1.435646e-02, 'pool_scale': 8.659389e-03, 'm2_conv_w': 6.631674e-03, 'm2_conv_b': 9.898381e-03, 'm2_dt_bias': 1.574985e-02, 'm2_a_log': 3.250781e-02, 'm2_d': 3.651559e-02, 'm2_norm_w': 9.580784e-03, 'ffn_w_up': 4.752637e-03, 'ffn_conv_w': 4.846631e-03, 'ffn_conv_b': 4.250673e-03, 'ffn_w_down': 1.552897e-02}


def _to_microbatches(a, axis):
    t = _jnp.moveaxis(a, axis, 0)
    t = t.reshape((N_MICROBATCH, t.shape[0] // N_MICROBATCH) + t.shape[1:])
    return _jnp.moveaxis(t, 1, axis + 1)


def setup_inputs(seed: int = 0) -> dict:
    inp = _fwd_setup_inputs(seed)
    key = _jax.random.fold_in(_jax.random.key(seed), 7919)
    shape, _ = _output_shape()
    out = dict(inp)
    out["loss_target"] = _jax.random.normal(_jax.random.fold_in(key, 0), shape, _jnp.float32)
    for i, name in enumerate(TWIN_WEIGHTS):
        w = inp[name].astype(_jnp.float32)
        if MOMENT_SCALE is None:
            s = _jnp.sqrt(_jnp.mean(_jnp.square(w)) + 1e-30)
        else:
            s = MOMENT_SCALE[name]
        km, kv = _jax.random.split(_jax.random.fold_in(key, i + 1))
        out[name] = w
        out["m_" + name] = s * _jax.random.normal(km, w.shape, _jnp.float32)
        out["v_" + name] = (s * s) * _jax.random.uniform(kv, w.shape, _jnp.float32, 0.5, 1.5)
    if N_MICROBATCH > 1:
        for name, axis in PER_EXAMPLE_BATCH_AXIS.items():
            out[name] = _to_microbatches(out[name], axis)
    return {'x': out['x'], 'c': out['c'], 'ctx': out['ctx'], 'c_ctx': out['c_ctx'], 'w_ada': out['w_ada'], 'b_ada': out['b_ada'], 'w_in': out['w_in'], 'w_out': out['w_out'], 'ln1_g': out['ln1_g'], 'ln1_b': out['ln1_b'], 'ln2_g': out['ln2_g'], 'ln2_b': out['ln2_b'], 's5_a_re': out['s5_a_re'], 's5_a_im': out['s5_a_im'], 's5_b_re': out['s5_b_re'], 's5_b_im': out['s5_b_im'], 's5_c_re': out['s5_c_re'], 's5_c_im': out['s5_c_im'], 's5_log_step': out['s5_log_step'], 's5_d': out['s5_d'], 's5_glu_w': out['s5_glu_w'], 's5_glu_b': out['s5_glu_b'], 'sg_ln_g': out['sg_ln_g'], 'sg_ln_b': out['sg_ln_b'], 'sg_w': out['sg_w'], 'sg_b': out['sg_b'], 'pool_w': out['pool_w'], 'pool_b': out['pool_b'], 'pool_scale': out['pool_scale'], 'm2_conv_w': out['m2_conv_w'], 'm2_conv_b': out['m2_conv_b'], 'm2_dt_bias': out['m2_dt_bias'], 'm2_a_log': out['m2_a_log'], 'm2_d': out['m2_d'], 'm2_norm_w': out['m2_norm_w'], 'ffn_w_up': out['ffn_w_up'], 'ffn_conv_w': out['ffn_conv_w'], 'ffn_conv_b': out['ffn_conv_b'], 'ffn_w_down': out['ffn_w_down'], 'loss_target': out['loss_target'], 'm_c_ctx': out['m_c_ctx'], 'm_w_ada': out['m_w_ada'], 'm_b_ada': out['m_b_ada'], 'm_w_in': out['m_w_in'], 'm_w_out': out['m_w_out'], 'm_ln1_g': out['m_ln1_g'], 'm_ln1_b': out['m_ln1_b'], 'm_ln2_g': out['m_ln2_g'], 'm_ln2_b': out['m_ln2_b'], 'm_s5_a_re': out['m_s5_a_re'], 'm_s5_a_im': out['m_s5_a_im'], 'm_s5_b_re': out['m_s5_b_re'], 'm_s5_b_im': out['m_s5_b_im'], 'm_s5_c_re': out['m_s5_c_re'], 'm_s5_c_im': out['m_s5_c_im'], 'm_s5_log_step': out['m_s5_log_step'], 'm_s5_d': out['m_s5_d'], 'm_s5_glu_w': out['m_s5_glu_w'], 'm_s5_glu_b': out['m_s5_glu_b'], 'm_sg_ln_g': out['m_sg_ln_g'], 'm_sg_ln_b': out['m_sg_ln_b'], 'm_sg_w': out['m_sg_w'], 'm_sg_b': out['m_sg_b'], 'm_pool_w': out['m_pool_w'], 'm_pool_b': out['m_pool_b'], 'm_pool_scale': out['m_pool_scale'], 'm_m2_conv_w': out['m_m2_conv_w'], 'm_m2_conv_b': out['m_m2_conv_b'], 'm_m2_dt_bias': out['m_m2_dt_bias'], 'm_m2_a_log': out['m_m2_a_log'], 'm_m2_d': out['m_m2_d'], 'm_m2_norm_w': out['m_m2_norm_w'], 'm_ffn_w_up': out['m_ffn_w_up'], 'm_ffn_conv_w': out['m_ffn_conv_w'], 'm_ffn_conv_b': out['m_ffn_conv_b'], 'm_ffn_w_down': out['m_ffn_w_down'], 'v_c_ctx': out['v_c_ctx'], 'v_w_ada': out['v_w_ada'], 'v_b_ada': out['v_b_ada'], 'v_w_in': out['v_w_in'], 'v_w_out': out['v_w_out'], 'v_ln1_g': out['v_ln1_g'], 'v_ln1_b': out['v_ln1_b'], 'v_ln2_g': out['v_ln2_g'], 'v_ln2_b': out['v_ln2_b'], 'v_s5_a_re': out['v_s5_a_re'], 'v_s5_a_im': out['v_s5_a_im'], 'v_s5_b_re': out['v_s5_b_re'], 'v_s5_b_im': out['v_s5_b_im'], 'v_s5_c_re': out['v_s5_c_re'], 'v_s5_c_im': out['v_s5_c_im'], 'v_s5_log_step': out['v_s5_log_step'], 'v_s5_d': out['v_s5_d'], 'v_s5_glu_w': out['v_s5_glu_w'], 'v_s5_glu_b': out['v_s5_glu_b'], 'v_sg_ln_g': out['v_sg_ln_g'], 'v_sg_ln_b': out['v_sg_ln_b'], 'v_sg_w': out['v_sg_w'], 'v_sg_b': out['v_sg_b'], 'v_pool_w': out['v_pool_w'], 'v_pool_b': out['v_pool_b'], 'v_pool_scale': out['v_pool_scale'], 'v_m2_conv_w': out['v_m2_conv_w'], 'v_m2_conv_b': out['v_m2_conv_b'], 'v_m2_dt_bias': out['v_m2_dt_bias'], 'v_m2_a_log': out['v_m2_a_log'], 'v_m2_d': out['v_m2_d'], 'v_m2_norm_w': out['v_m2_norm_w'], 'v_ffn_w_up': out['v_ffn_w_up'], 'v_ffn_conv_w': out['v_ffn_conv_w'], 'v_ffn_conv_b': out['v_ffn_conv_b'], 'v_ffn_w_down': out['v_ffn_w_down']}


def _loss(weights, diff, rest, loss_target):
    with _jax.named_scope("forward"):
        args = {**rest, TWIN_DIFF_INPUT: diff, **{k: w.astype(_WEIGHT_DTYPES[k]) for k, w in weights.items()}}
        y = _forward(args)
    with _jax.named_scope("loss_head"):
        err = _jnp.square(y.astype(_jnp.float32) - loss_target)
        return 0.5 * _jnp.sum(_jnp.mean(err, axis=-1)) if err.ndim else 0.5 * err


def _adamw(w, g, m, v):
    m = ADAM_B1 * m + (1.0 - ADAM_B1) * g
    v = ADAM_B2 * v + (1.0 - ADAM_B2) * _jnp.square(g)
    m_hat = m / (1.0 - ADAM_B1 ** ADAM_STEP)
    v_hat = v / (1.0 - ADAM_B2 ** ADAM_STEP)
    delta = -ADAM_LR * (m_hat / (_jnp.sqrt(v_hat) + ADAM_EPS) + ADAM_WD * w)
    return delta, m, v


def reference(x, c, ctx, c_ctx, w_ada, b_ada, w_in, w_out, ln1_g, ln1_b, ln2_g, ln2_b, s5_a_re, s5_a_im, s5_b_re, s5_b_im, s5_c_re, s5_c_im, s5_log_step, s5_d, s5_glu_w, s5_glu_b, sg_ln_g, sg_ln_b, sg_w, sg_b, pool_w, pool_b, pool_scale, m2_conv_w, m2_conv_b, m2_dt_bias, m2_a_log, m2_d, m2_norm_w, ffn_w_up, ffn_conv_w, ffn_conv_b, ffn_w_down, loss_target, m_c_ctx, m_w_ada, m_b_ada, m_w_in, m_w_out, m_ln1_g, m_ln1_b, m_ln2_g, m_ln2_b, m_s5_a_re, m_s5_a_im, m_s5_b_re, m_s5_b_im, m_s5_c_re, m_s5_c_im, m_s5_log_step, m_s5_d, m_s5_glu_w, m_s5_glu_b, m_sg_ln_g, m_sg_ln_b, m_sg_w, m_sg_b, m_pool_w, m_pool_b, m_pool_scale, m_m2_conv_w, m_m2_conv_b, m_m2_dt_bias, m_m2_a_log, m_m2_d, m_m2_norm_w, m_ffn_w_up, m_ffn_conv_w, m_ffn_conv_b, m_ffn_w_down, v_c_ctx, v_w_ada, v_b_ada, v_w_in, v_w_out, v_ln1_g, v_ln1_b, v_ln2_g, v_ln2_b, v_s5_a_re, v_s5_a_im, v_s5_b_re, v_s5_b_im, v_s5_c_re, v_s5_c_im, v_s5_log_step, v_s5_d, v_s5_glu_w, v_s5_glu_b, v_sg_ln_g, v_sg_ln_b, v_sg_w, v_sg_b, v_pool_w, v_pool_b, v_pool_scale, v_m2_conv_w, v_m2_conv_b, v_m2_dt_bias, v_m2_a_log, v_m2_d, v_m2_norm_w, v_ffn_w_up, v_ffn_conv_w, v_ffn_conv_b, v_ffn_w_down):
    given = dict(x=x, c=c, ctx=ctx, c_ctx=c_ctx, w_ada=w_ada, b_ada=b_ada, w_in=w_in, w_out=w_out, ln1_g=ln1_g, ln1_b=ln1_b, ln2_g=ln2_g, ln2_b=ln2_b, s5_a_re=s5_a_re, s5_a_im=s5_a_im, s5_b_re=s5_b_re, s5_b_im=s5_b_im, s5_c_re=s5_c_re, s5_c_im=s5_c_im, s5_log_step=s5_log_step, s5_d=s5_d, s5_glu_w=s5_glu_w, s5_glu_b=s5_glu_b, sg_ln_g=sg_ln_g, sg_ln_b=sg_ln_b, sg_w=sg_w, sg_b=sg_b, pool_w=pool_w, pool_b=pool_b, pool_scale=pool_scale, m2_conv_w=m2_conv_w, m2_conv_b=m2_conv_b, m2_dt_bias=m2_dt_bias, m2_a_log=m2_a_log, m2_d=m2_d, m2_norm_w=m2_norm_w, ffn_w_up=ffn_w_up, ffn_conv_w=ffn_conv_w, ffn_conv_b=ffn_conv_b, ffn_w_down=ffn_w_down, loss_target=loss_target, m_c_ctx=m_c_ctx, m_w_ada=m_w_ada, m_b_ada=m_b_ada, m_w_in=m_w_in, m_w_out=m_w_out, m_ln1_g=m_ln1_g, m_ln1_b=m_ln1_b, m_ln2_g=m_ln2_g, m_ln2_b=m_ln2_b, m_s5_a_re=m_s5_a_re, m_s5_a_im=m_s5_a_im, m_s5_b_re=m_s5_b_re, m_s5_b_im=m_s5_b_im, m_s5_c_re=m_s5_c_re, m_s5_c_im=m_s5_c_im, m_s5_log_step=m_s5_log_step, m_s5_d=m_s5_d, m_s5_glu_w=m_s5_glu_w, m_s5_glu_b=m_s5_glu_b, m_sg_ln_g=m_sg_ln_g, m_sg_ln_b=m_sg_ln_b, m_sg_w=m_sg_w, m_sg_b=m_sg_b, m_pool_w=m_pool_w, m_pool_b=m_pool_b, m_pool_scale=m_pool_scale, m_m2_conv_w=m_m2_conv_w, m_m2_conv_b=m_m2_conv_b, m_m2_dt_bias=m_m2_dt_bias, m_m2_a_log=m_m2_a_log, m_m2_d=m_m2_d, m_m2_norm_w=m_m2_norm_w, m_ffn_w_up=m_ffn_w_up, m_ffn_conv_w=m_ffn_conv_w, m_ffn_conv_b=m_ffn_conv_b, m_ffn_w_down=m_ffn_w_down, v_c_ctx=v_c_ctx, v_w_ada=v_w_ada, v_b_ada=v_b_ada, v_w_in=v_w_in, v_w_out=v_w_out, v_ln1_g=v_ln1_g, v_ln1_b=v_ln1_b, v_ln2_g=v_ln2_g, v_ln2_b=v_ln2_b, v_s5_a_re=v_s5_a_re, v_s5_a_im=v_s5_a_im, v_s5_b_re=v_s5_b_re, v_s5_b_im=v_s5_b_im, v_s5_c_re=v_s5_c_re, v_s5_c_im=v_s5_c_im, v_s5_log_step=v_s5_log_step, v_s5_d=v_s5_d, v_s5_glu_w=v_s5_glu_w, v_s5_glu_b=v_s5_glu_b, v_sg_ln_g=v_sg_ln_g, v_sg_ln_b=v_sg_ln_b, v_sg_w=v_sg_w, v_sg_b=v_sg_b, v_pool_w=v_pool_w, v_pool_b=v_pool_b, v_pool_scale=v_pool_scale, v_m2_conv_w=v_m2_conv_w, v_m2_conv_b=v_m2_conv_b, v_m2_dt_bias=v_m2_dt_bias, v_m2_a_log=v_m2_a_log, v_m2_d=v_m2_d, v_m2_norm_w=v_m2_norm_w, v_ffn_w_up=v_ffn_w_up, v_ffn_conv_w=v_ffn_conv_w, v_ffn_conv_b=v_ffn_conv_b, v_ffn_w_down=v_ffn_w_down)
    weights = {n: given[n] for n in TWIN_WEIGHTS}
    shared = {n: given[n] for n in SHARED_INPUTS}
    per_example = {n: given[n] for n in ['x', 'c', 'ctx']}
    grad_fn = _jax.value_and_grad(_loss, argnums=(0, 1))

    def one_microbatch(ex, loss_target):
        ex = dict(ex)
        diff = ex.pop(TWIN_DIFF_INPUT)
        return grad_fn(weights, diff, {**shared, **ex}, loss_target)

    if N_MICROBATCH == 1:
        loss, (grad_w, grad_x) = one_microbatch(per_example, given["loss_target"])
    else:
        def body(carry, xs):
            loss_sum, grad_sum = carry
            l_k, (gw_k, gx_k) = one_microbatch(xs[0], xs[1])
            with _jax.named_scope("update"):
                return (loss_sum + l_k, _jax.tree.map(_jnp.add, grad_sum, gw_k)), gx_k

        init = (_jnp.zeros((), _jnp.float32), _jax.tree.map(_jnp.zeros_like, weights))
        (loss, grad_w), grad_x = _jax.lax.scan(body, init, (per_example, given["loss_target"]))
    with _jax.named_scope("update"):
        delta_w, new_m, new_v = {}, {}, {}
        for n in TWIN_WEIGHTS:
            delta_w[n], new_m[n], new_v[n] = _adamw(weights[n], grad_w[n], given["m_" + n], given["v_" + n])
    return (loss, grad_x, *[grad_w[n] for n in TWIN_WEIGHTS], *[delta_w[n] for n in TWIN_WEIGHTS],
            *[new_m[n] for n in TWIN_WEIGHTS], *[new_v[n] for n in TWIN_WEIGHTS])
```

```python
import functools

import jax
import jax.numpy as jnp
import numpy as np
from jax import lax
from jax.experimental import pallas as pl
from jax.experimental.pallas import tpu as pltpu

F32 = jnp.float32
BF16 = jnp.bfloat16
MESH = pl.DeviceIdType.MESH
ANY = pl.BlockSpec(memory_space=pl.ANY)

N_DEV = 8
N_CHIP = 4
LANES = 128
SUBLANES = 8
VMEM_LIMIT_BYTES = 48 * 1024 * 1024
MATMUL_VMEM_BUDGET = 36 * 1024 * 1024
PACK_ROWS = 512

GRID_W = 64
S5_CH = 16
S5_STATE = 64
SG_HEADS = 4
SG_CHUNK = 128
POOL_WINDOWS = (2, 4, 8, 16)
M2_HEAD_DIM = 64
M2_STATE = 128
M2_GROUPS = 2
M2_CONV = 4
M2_PAD = (M2_CONV // 2, M2_CONV - 1 - M2_CONV // 2)
M2_CHUNK = 128
FFN_CONV = 3
LN_EPS = 1e-5
RMS_EPS = 1e-5
ADAM_LR = 0.001
ADAM_B1 = 0.9
ADAM_B2 = 0.999
ADAM_EPS = 1e-08
ADAM_WD = 0.01
ADAM_STEP = 10

WEIGHTS = ['c_ctx', 'w_ada', 'b_ada', 'w_in', 'w_out', 'ln1_g', 'ln1_b', 'ln2_g', 'ln2_b', 's5_a_re', 's5_a_im',
           's5_b_re', 's5_b_im', 's5_c_re', 's5_c_im', 's5_log_step', 's5_d', 's5_glu_w', 's5_glu_b', 'sg_ln_g',
           'sg_ln_b', 'sg_w', 'sg_b', 'pool_w', 'pool_b', 'pool_scale', 'm2_conv_w', 'm2_conv_b', 'm2_dt_bias',
           'm2_a_log', 'm2_d', 'm2_norm_w', 'ffn_w_up', 'ffn_conv_w', 'ffn_conv_b', 'ffn_w_down']
BIG = ('w_in', 'w_out', 'ffn_w_up', 'ffn_w_down')
COL_SHARDED_SMALL = ('m2_conv_w', 'ffn_conv_w')
ROW_SHARDED_SMALL = ('s5_glu_w',)
REPLICATED = [n for n in WEIGHTS if n not in BIG + COL_SHARDED_SMALL + ROW_SHARDED_SMALL + ('w_ada', 'c_ctx')]


def _round_up(n, m):
    return (n + m - 1) // m * m


def _my_coords():
    return lax.axis_index('x'), lax.axis_index('y'), lax.axis_index('c')


def _my_index():
    x, y, c = _my_coords()
    return 4 * x + 2 * y + c


def _first_divisor(n, cands):
    for c in cands:
        if n % c == 0:
            return c
    return n


def _matmul_tiles(mode, m, n, k, a_bytes, b_bytes):
    lane_c = (1024, 512, 384, 256, 128)
    sub_c = (1088, 1024, 544, 512, 256, 128, 64, 32, 16)
    tm = _first_divisor(m, lane_c if mode == 'tn' else sub_c)
    tn = _first_divisor(n, lane_c)
    k_c = [c for c in (5632, 4096, 2048, 1408, 1088, 1024, 544, 512, 256, 128) if k % c == 0] or [k]
    if mode == 'tn':
        k_c = [c for c in k_c if c <= 1088] or [k_c[-1]]
    for tk in k_c:
        use = 2 * (tm * tk * a_bytes + tk * tn * b_bytes) + 3 * tm * tn * 4
        if use <= MATMUL_VMEM_BUDGET:
            return tm, tn, tk
    return tm, tn, k_c[-1]


def _matmul(a, b, mode, name, out_dtype=F32):
    if mode == 'nn':
        (m, k), (k2, n) = a.shape, b.shape
    elif mode == 'nt':
        (m, k), (n, k2) = a.shape, b.shape
    else:
        (k, m), (k2, n) = a.shape, b.shape
    assert k == k2, (mode, a.shape, b.shape)
    tm, tn, tk = _matmul_tiles(mode, m, n, k, a.dtype.itemsize, b.dtype.itemsize)
    nk = k // tk
    if mode == 'nn':
        a_spec = pl.BlockSpec((tm, tk), lambda i, j, kk: (i, kk))
        b_spec = pl.BlockSpec((tk, tn), lambda i, j, kk: (kk, j))
        dims = (((1,), (0,)), ((), ()))
    elif mode == 'nt':
        a_spec = pl.BlockSpec((tm, tk), lambda i, j, kk: (i, kk))
        b_spec = pl.BlockSpec((tn, tk), lambda i, j, kk: (j, kk))
        dims = (((1,), (1,)), ((), ()))
    else:
        a_spec = pl.BlockSpec((tk, tm), lambda i, j, kk: (kk, i))
        b_spec = pl.BlockSpec((tk, tn), lambda i, j, kk: (kk, j))
        dims = (((0,), (0,)), ((), ()))

    def body(a_ref, b_ref, o_ref, acc_ref):
        kk = pl.program_id(2)

        @pl.when(kk == 0)
        def _():
            acc_ref[...] = jnp.zeros_like(acc_ref)

        acc_ref[...] += lax.dot_general(a_ref[...].astype(BF16), b_ref[...].astype(BF16), dims,
                                        preferred_element_type=F32)

        @pl.when(kk == nk - 1)
        def _():
            o_ref[...] = acc_ref[...].astype(o_ref.dtype)

    return pl.pallas_call(
        body,
        name=name,
        grid=(m // tm, n // tn, nk),
        in_specs=[a_spec, b_spec],
        out_specs=pl.BlockSpec((tm, tn), lambda i, j, kk: (i, j)),
        out_shape=jax.ShapeDtypeStruct((m, n), out_dtype),
        scratch_shapes=[pltpu.VMEM((tm, tn), F32)],
        compiler_params=pltpu.CompilerParams(dimension_semantics=('parallel', 'parallel', 'arbitrary'),
                                             vmem_limit_bytes=VMEM_LIMIT_BYTES),
    )(a, b)


@functools.partial(jax.custom_vjp, nondiff_argnums=(2,))
def _linear(x, w, name):
    return _matmul(x, w, 'nn', name + '_fwd')


def _linear_fwd(x, w, name):
    return _matmul(x, w, 'nn', name + '_fwd'), (x, w)


def _linear_bwd(name, res, dy):
    x, w = res
    return _matmul(dy, w, 'nt', name + '_dx'), _matmul(x, dy, 'tn', name + '_dw')


_linear.defvjp(_linear_fwd, _linear_bwd)


@functools.partial(jax.custom_vjp, nondiff_argnums=(3,))
def _linear_tap(x, w, tap, name):
    del tap
    return _matmul(x, w, 'nn', name + '_fwd')


def _linear_tap_fwd(x, w, tap, name):
    del tap
    return _matmul(x, w, 'nn', name + '_fwd'), (x, w)


def _linear_tap_bwd(name, res, dy):
    x, w = res
    return _matmul(dy, w, 'nt', name + '_dx'), jnp.zeros_like(w), _matmul(x, dy, 'tn', name + '_dw')


_linear_tap.defvjp(_linear_tap_fwd, _linear_tap_bwd)


SCAN_ROWS = 128
SCAN_LANES = 512


def _scan_lanes(n):
    return SCAN_LANES if n % SCAN_LANES == 0 else n


def _interleave(re, im):
    r, n = re.shape
    lanes = _scan_lanes(n)
    return jnp.stack([re.reshape(r, n // lanes, lanes), im.reshape(r, n // lanes, lanes)], axis=2).reshape(r, 2 * n)


def _scan_call(a, x, h, n_ctx_blocks, ctx_first, reverse_rows, name):
    t, n2 = x.shape
    n = n2 // 2
    nblk = t // SCAN_ROWS
    nlat = nblk - n_ctx_blocks
    lanes = _scan_lanes(n)
    with_da = h is not None

    def block_of(i):
        if ctx_first:
            first_n, first_0, second_0, second_n = n_ctx_blocks, 0, n_ctx_blocks, nlat
        else:
            first_n, first_0, second_0, second_n = nlat, n_ctx_blocks, 0, n_ctx_blocks
        if reverse_rows:
            in_first = first_0 + first_n - 1 - i
            in_second = second_0 + second_n - 1 - (i - first_n)
        else:
            in_first = first_0 + i
            in_second = second_0 + (i - first_n)
        return jnp.where(i < first_n, in_first, in_second)

    groups = SCAN_ROWS // SUBLANES
    first_row = SUBLANES - 1 if reverse_rows else 0
    to_previous = SUBLANES - 1 if reverse_rows else 1

    def body(*refs):
        if with_da:
            a_ref, x_ref, h_ref, o_ref, da_ref, st_ref = refs
        else:
            a_ref, x_ref, o_ref, st_ref = refs

        @pl.when(pl.program_id(1) == 0)
        def _():
            st_ref[...] = jnp.zeros_like(st_ref)
            if with_da:
                da_ref[...] = jnp.zeros_like(da_ref)

        ar = jnp.broadcast_to(a_ref[0:1, :], (SUBLANES, lanes))
        ai = jnp.broadcast_to(a_ref[1:2, :], (SUBLANES, lanes))
        row_id = lax.broadcasted_iota(jnp.int32, (SUBLANES, lanes), 0)

        def group(g, carry):
            sr, si = carry
            in_r, in_i = sr, si
            gi = (groups - 1 - g) if reverse_rows else g
            start = pl.multiple_of(gi * SUBLANES, SUBLANES)
            tile_r = x_ref[pl.ds(start, SUBLANES), pl.ds(0, lanes)]
            tile_i = x_ref[pl.ds(start, SUBLANES), pl.ds(lanes, lanes)]
            out_r = jnp.zeros((SUBLANES, lanes), F32)
            out_i = jnp.zeros((SUBLANES, lanes), F32)
            order = range(SUBLANES - 1, -1, -1) if reverse_rows else range(SUBLANES)
            for r in order:
                xr = jnp.broadcast_to(tile_r[r:r + 1, :], (SUBLANES, lanes))
                xi = jnp.broadcast_to(tile_i[r:r + 1, :], (SUBLANES, lanes))
                sr, si = ar * sr - ai * si + xr, ar * si + ai * sr + xi
                out_r = jnp.where(row_id == r, sr, out_r)
                out_i = jnp.where(row_id == r, si, out_i)
            o_ref[pl.ds(start, SUBLANES), pl.ds(0, lanes)] = out_r
            o_ref[pl.ds(start, SUBLANES), pl.ds(lanes, lanes)] = out_i
            if with_da:
                pr = jnp.where(row_id == first_row, in_r, pltpu.roll(out_r, to_previous, axis=0))
                pi = jnp.where(row_id == first_row, in_i, pltpu.roll(out_i, to_previous, axis=0))
                hr = h_ref[pl.ds(start, SUBLANES), pl.ds(0, lanes)]
                hi = h_ref[pl.ds(start, SUBLANES), pl.ds(lanes, lanes)]
                da_ref[0:SUBLANES, :] += hr * pr + hi * pi
                da_ref[SUBLANES:2 * SUBLANES, :] += hr * pi - hi * pr
            return sr, si

        sr, si = lax.fori_loop(0, groups, group, (st_ref[0], st_ref[1]))
        st_ref[0] = sr
        st_ref[1] = si

    row_spec = pl.BlockSpec((SCAN_ROWS, 2 * lanes), lambda j, i: (block_of(i), j))
    in_specs = [pl.BlockSpec((2, lanes), lambda j, i: (0, j)), row_spec]
    out_specs = [row_spec]
    out_shape = [jax.ShapeDtypeStruct((t, n2), F32)]
    operands = [a, x]
    if with_da:
        in_specs.append(row_spec)
        operands.append(h)
        out_specs.append(pl.BlockSpec((2 * SUBLANES, lanes), lambda j, i: (0, j)))
        out_shape.append(jax.ShapeDtypeStruct((2 * SUBLANES, n), F32))
    return pl.pallas_call(
        body,
        name=name,
        grid=(n // lanes, nblk),
        in_specs=in_specs,
        out_specs=out_specs,
        out_shape=out_shape,
        scratch_shapes=[pltpu.VMEM((2, SUBLANES, lanes), F32)],
        compiler_params=pltpu.CompilerParams(dimension_semantics=('parallel', 'arbitrary')),
    )(*operands)


@functools.partial(jax.custom_vjp, nondiff_argnums=(2, 3, 4))
def _s5_scan(a, bu, t_ctx, reverse, name):
    return _scan_call(a, bu, None, t_ctx // SCAN_ROWS, True, reverse, name + '_fwd')[0]


def _s5_scan_fwd(a, bu, t_ctx, reverse, name):
    h = _scan_call(a, bu, None, t_ctx // SCAN_ROWS, True, reverse, name + '_fwd')[0]
    return h, (a, h)


def _s5_scan_bwd(t_ctx, reverse, name, res, dh):
    a, h = res
    a_conj = a * jnp.array([[1.0], [-1.0]], F32)
    g, da = _scan_call(a_conj, dh, h, t_ctx // SCAN_ROWS, False, not reverse, name + '_bwd')
    da = jnp.stack([jnp.sum(da[:SUBLANES], axis=0), jnp.sum(da[SUBLANES:], axis=0)])
    return da, g


_s5_scan.defvjp(_s5_scan_fwd, _s5_scan_bwd)


FFN_LANES = 128
CONV_TAPS = FFN_CONV * FFN_CONV
CONV_W_ROWS = 16
GELU_K = 0.7978845608028654
GELU_C = 0.044715


def _gelu_and_slope(x):
    x2 = x * x
    th = jnp.tanh(GELU_K * (x + GELU_C * x * x2))
    cdf = 0.5 * (1.0 + th)
    slope = cdf + 0.5 * x * (1.0 - th * th) * (GELU_K * (1.0 + 3.0 * GELU_C * x2))
    return x * cdf, slope


def _band_rows(t_lat):
    return _first_divisor(t_lat // GRID_W, (8, 4, 2, 1)) * GRID_W


def _shifted_ctx(x):
    n = x.shape[0]
    row = lax.broadcasted_iota(jnp.int32, x.shape, 0)
    left = jnp.where(row == 0, 0.0, pltpu.roll(x, 1, axis=0))
    right = jnp.where(row == n - 1, 0.0, pltpu.roll(x, n - 1, axis=0))
    return left, x, right


def _shifted_band(src_ref, bidx, n_bands, band, t_ctx, t):
    start = pl.multiple_of(t_ctx + bidx * band, GRID_W)
    top_start = pl.multiple_of(jnp.maximum(start - GRID_W, 0), SUBLANES)
    bot_start = pl.multiple_of(jnp.minimum(start + band, t - GRID_W), SUBLANES)
    top = jnp.where(bidx > 0, src_ref[pl.ds(top_start, GRID_W), :], 0.0)
    bot = jnp.where(bidx < n_bands - 1, src_ref[pl.ds(bot_start, GRID_W), :], 0.0)
    ext = jnp.concatenate([top, src_ref[pl.ds(start, band), :], bot], axis=0)
    n_ext = band + 2 * GRID_W
    col = lax.broadcasted_iota(jnp.int32, ext.shape, 0) & (GRID_W - 1)
    left = jnp.where(col == 0, 0.0, pltpu.roll(ext, 1, axis=0))
    right = jnp.where(col == GRID_W - 1, 0.0, pltpu.roll(ext, n_ext - 1, axis=0))
    return left, ext, right


def _conv_sum(shifted, w_rows, band, flip):
    acc = None
    for i in range(FFN_CONV):
        for j in range(FFN_CONV):
            w = w_rows[(FFN_CONV - 1 - i) * FFN_CONV + (FFN_CONV - 1 - j)] if flip else w_rows[i * FFN_CONV + j]
            term = shifted[j][i * GRID_W:i * GRID_W + band] * w
            acc = term if acc is None else acc + term
    return acc


def _conv_operand(conv_w, conv_b):
    f = conv_b.shape[0]
    return jnp.concatenate([conv_w.reshape(CONV_TAPS, f), conv_b[None],
                            jnp.zeros((CONV_W_ROWS - CONV_TAPS - 1, f), F32)], axis=0)


def _ffn_mid_fwd(gv, w16, t_ctx, name):
    t, f2 = gv.shape
    f = f2 // 2
    lanes = FFN_LANES
    nf = f // lanes
    band = _band_rows(t - t_ctx)
    n_bands = (t - t_ctx) // band
    mid = (FFN_CONV // 2) * FFN_CONV

    def body(g_ref, v_ref, w_ref, o_ref):
        w_rows = [w_ref[k:k + 1, :] for k in range(CONV_TAPS)]
        bias = w_ref[CONV_TAPS:CONV_TAPS + 1, :]
        if t_ctx:
            sh = _shifted_ctx(g_ref[0:t_ctx, :])
            pre = sh[0] * w_rows[mid] + sh[1] * w_rows[mid + 1] + sh[2] * w_rows[mid + 2] + bias
            o_ref[0:t_ctx, :] = (_gelu_and_slope(pre)[0] * v_ref[0:t_ctx, :]).astype(o_ref.dtype)

        def one_band(b, carry):
            start = pl.multiple_of(t_ctx + b * band, GRID_W)
            pre = _conv_sum(_shifted_band(g_ref, b, n_bands, band, t_ctx, t), w_rows, band, False) + bias
            o_ref[pl.ds(start, band), :] = (_gelu_and_slope(pre)[0] * v_ref[pl.ds(start, band), :]).astype(o_ref.dtype)
            return carry

        lax.fori_loop(0, n_bands, one_band, 0)

    return pl.pallas_call(
        body, name=name,
        grid=(nf,),
        in_specs=[pl.BlockSpec((t, lanes), lambda j: (0, j)),
                  pl.BlockSpec((t, lanes), lambda j: (0, nf + j)),
                  pl.BlockSpec((CONV_W_ROWS, lanes), lambda j: (0, j))],
        out_specs=pl.BlockSpec((t, lanes), lambda j: (0, j)),
        out_shape=jax.ShapeDtypeStruct((t, f), BF16),
        compiler_params=pltpu.CompilerParams(dimension_semantics=('parallel',), vmem_limit_bytes=VMEM_LIMIT_BYTES),
    )(gv, gv, w16)


def _ffn_mid_bwd(d_act, gv, w16, t_ctx, name):
    t, f2 = gv.shape
    f = f2 // 2
    lanes = FFN_LANES
    nf = f // lanes
    band = _band_rows(t - t_ctx)
    n_bands = (t - t_ctx) // band
    mid = (FFN_CONV // 2) * FFN_CONV
    n_acc = CONV_TAPS + 1

    def tile_sum(x):
        return jnp.sum(x.reshape(x.shape[0] // SUBLANES, SUBLANES, lanes), axis=0)

    def body(da_ref, g_ref, v_ref, w_ref, dg_ref, dv_ref, dw_ref, dp_ref):
        w_rows = [w_ref[k:k + 1, :] for k in range(CONV_TAPS)]
        bias = w_ref[CONV_TAPS:CONV_TAPS + 1, :]
        acc = [jnp.zeros((SUBLANES, lanes), F32) for _ in range(n_acc)]
        if t_ctx:
            sh = _shifted_ctx(g_ref[0:t_ctx, :])
            pre = sh[0] * w_rows[mid] + sh[1] * w_rows[mid + 1] + sh[2] * w_rows[mid + 2] + bias
            ge, slope = _gelu_and_slope(pre)
            da = da_ref[0:t_ctx, :]
            dv_ref[0:t_ctx, :] = da * ge
            dpre = da * v_ref[0:t_ctx, :] * slope
            dp_ref[0:t_ctx, :] = dpre
            for j in range(FFN_CONV):
                acc[mid + j] = acc[mid + j] + tile_sum(sh[j] * dpre)
            acc[CONV_TAPS] = acc[CONV_TAPS] + tile_sum(dpre)
            back = _shifted_ctx(dpre)
            dg_ref[0:t_ctx, :] = back[2] * w_rows[mid] + back[1] * w_rows[mid + 1] + back[0] * w_rows[mid + 2]

        def first_pass(b, acc):
            acc = list(acc)
            start = pl.multiple_of(t_ctx + b * band, GRID_W)
            sh = _shifted_band(g_ref, b, n_bands, band, t_ctx, t)
            ge, slope = _gelu_and_slope(_conv_sum(sh, w_rows, band, False) + bias)
            da = da_ref[pl.ds(start, band), :]
            dv_ref[pl.ds(start, band), :] = da * ge
            dpre = da * v_ref[pl.ds(start, band), :] * slope
            dp_ref[pl.ds(start, band), :] = dpre
            for i in range(FFN_CONV):
                for j in range(FFN_CONV):
                    k = i * FFN_CONV + j
                    acc[k] = acc[k] + tile_sum(sh[j][i * GRID_W:i * GRID_W + band] * dpre)
            acc[CONV_TAPS] = acc[CONV_TAPS] + tile_sum(dpre)
            return tuple(acc)

        acc = lax.fori_loop(0, n_bands, first_pass, tuple(acc))
        for k in range(n_acc):
            dw_ref[k * SUBLANES:(k + 1) * SUBLANES, :] = acc[k]

        def second_pass(b, carry):
            start = pl.multiple_of(t_ctx + b * band, GRID_W)
            dg_ref[pl.ds(start, band), :] = _conv_sum(_shifted_band(dp_ref, b, n_bands, band, t_ctx, t),
                                                      w_rows, band, True)
            return carry

        lax.fori_loop(0, n_bands, second_pass, 0)

    col = pl.BlockSpec((t, lanes), lambda j: (0, j))
    return pl.pallas_call(
        body, name=name,
        grid=(nf,),
        in_specs=[col, col, pl.BlockSpec((t, lanes), lambda j: (0, nf + j)),
                  pl.BlockSpec((CONV_W_ROWS, lanes), lambda j: (0, j))],
        out_specs=[col, col, pl.BlockSpec((n_acc * SUBLANES, lanes), lambda j: (0, j))],
        out_shape=[jax.ShapeDtypeStruct((t, f), F32), jax.ShapeDtypeStruct((t, f), F32),
                   jax.ShapeDtypeStruct((n_acc * SUBLANES, f), F32)],
        scratch_shapes=[pltpu.VMEM((t, lanes), F32)],
        compiler_params=pltpu.CompilerParams(dimension_semantics=('parallel',), vmem_limit_bytes=VMEM_LIMIT_BYTES),
    )(d_act, gv, gv, w16)


@functools.partial(jax.custom_vjp, nondiff_argnums=(5, 6))
def _ffn_tail(gv, conv_w, conv_b, w_down, tap, t_ctx, name):
    del tap
    act = _ffn_mid_fwd(gv, _conv_operand(conv_w, conv_b), t_ctx, name + '_mid_fwd')
    return _matmul(act, w_down, 'nn', name + '_down_fwd')


def _ffn_tail_fwd(gv, conv_w, conv_b, w_down, tap, t_ctx, name):
    del tap
    w16 = _conv_operand(conv_w, conv_b)
    act = _ffn_mid_fwd(gv, w16, t_ctx, name + '_mid_fwd')
    return _matmul(act, w_down, 'nn', name + '_down_fwd'), (gv, w16, w_down, act)


def _ffn_tail_bwd(t_ctx, name, res, d_out):
    gv, w16, w_down, act = res
    f = act.shape[1]
    d_act = _matmul(d_out, w_down, 'nt', name + '_down_dx')
    d_w_down = _matmul(act, d_out, 'tn', name + '_down_dw')
    d_gate, d_val, d_w = _ffn_mid_bwd(d_act, gv, w16, t_ctx, name + '_mid_bwd')
    d_w = jnp.sum(d_w.reshape(CONV_TAPS + 1, SUBLANES, f), axis=1)
    return (jnp.concatenate([d_gate, d_val], axis=1), d_w[:CONV_TAPS].reshape(FFN_CONV, FFN_CONV, f),
            d_w[CONV_TAPS], jnp.zeros_like(w_down), d_w_down)


_ffn_tail.defvjp(_ffn_tail_fwd, _ffn_tail_bwd)


def _window(ref, j, rows, cols, axis):
    if axis == 0:
        return ref.at[pl.ds(j * rows, rows), :]
    return ref.at[:, pl.ds(j * cols, cols)]


def _all_gather(block, axis, name):
    rows, cols = block.shape

    def body(x_ref, out_ref, send_sems, recv_sems, local_sem):
        x, y, c = _my_coords()
        me, sibling = (x, y, c), (x, y, 1 - c)
        chips = [(1 - x, y), (x, 1 - y), (1 - x, 1 - y)]

        def win(px, py, pc):
            return _window(out_ref, 4 * px + 2 * py + pc, rows, cols, axis)

        def copy(k, blk, to, src=None):
            return pltpu.make_async_remote_copy(
                src_ref=win(*blk) if src is None else src, dst_ref=win(*blk),
                send_sem=send_sems.at[k], recv_sem=recv_sems.at[k], device_id=to, device_id_type=MESH)

        mine = pltpu.make_async_copy(x_ref, win(*me), local_sem)
        mine.start()
        first = [copy(0, me, sibling, src=x_ref)]
        first += [copy(1 + j, me, (*chip, c), src=x_ref) for j, chip in enumerate(chips)]
        for cp in first:
            cp.start()
        passed = [copy(4 + j, (*chip, c), sibling) for j, chip in enumerate(chips)]
        for j, chip in enumerate(chips):
            copy(1 + j, (*chip, c), me).wait_recv()
            passed[j].start()
        copy(0, sibling, me).wait_recv()
        for j, chip in enumerate(chips):
            copy(4 + j, (*chip, 1 - c), me).wait_recv()
        for cp in first + passed:
            cp.wait_send()
        mine.wait()

    out_shape = (N_DEV * rows, cols) if axis == 0 else (rows, N_DEV * cols)
    return pl.pallas_call(
        body, name=name,
        out_shape=jax.ShapeDtypeStruct(out_shape, block.dtype),
        in_specs=[ANY], out_specs=ANY,
        scratch_shapes=[pltpu.SemaphoreType.DMA((7,)), pltpu.SemaphoreType.DMA((7,)), pltpu.SemaphoreType.DMA],
    )(block)


def _exchange_pair(full, rows, cols, axis, name):
    def body(g_ref, land_ref, send_sems, recv_sems):
        x, y, c = _my_coords()
        copies = []
        for k in range(N_CHIP):
            j = 2 * k + (1 - c)
            copies.append(pltpu.make_async_remote_copy(
                src_ref=_window(g_ref, j, rows, cols, axis), dst_ref=land_ref.at[k],
                send_sem=send_sems.at[k], recv_sem=recv_sems.at[k], device_id=(x, y, 1 - c), device_id_type=MESH))
        for cp in copies:
            cp.start()
        for cp in copies:
            cp.wait()

    return pl.pallas_call(
        body, name=name,
        out_shape=jax.ShapeDtypeStruct((N_CHIP, rows, cols), full.dtype),
        in_specs=[ANY], out_specs=ANY,
        scratch_shapes=[pltpu.SemaphoreType.DMA((N_CHIP,)), pltpu.SemaphoreType.DMA((N_CHIP,))],
    )(full)


def _elementwise_tiles(rows, cols):
    tc = _first_divisor(cols, (1024, 512, 384, 256, 128))
    tr = _first_divisor(rows, (512, 256, 128, 64, 32, 16, 8))
    return tr, tc


def _pair_add(full, land, rows, cols, axis, name):
    tr, tc = _elementwise_tiles(rows, cols)
    c_arr = jnp.reshape(lax.axis_index('c'), (1,)).astype(jnp.int32)

    def full_map(k, i, j, c_ref):
        blk = 2 * k + c_ref[0]
        if axis == 0:
            return (blk * (rows // tr) + i, j)
        return (i, blk * (cols // tc) + j)

    def body(c_ref, g_ref, l_ref, o_ref):
        del c_ref
        o_ref[0] = g_ref[...] + l_ref[0]

    return pl.pallas_call(
        body, name=name,
        grid_spec=pltpu.PrefetchScalarGridSpec(
            num_scalar_prefetch=1,
            grid=(N_CHIP, rows // tr, cols // tc),
            in_specs=[pl.BlockSpec((tr, tc), full_map),
                      pl.BlockSpec((1, tr, tc), lambda k, i, j, c_ref: (k, i, j))],
            out_specs=pl.BlockSpec((1, tr, tc), lambda k, i, j, c_ref: (k, i, j)),
        ),
        out_shape=jax.ShapeDtypeStruct((N_CHIP, rows, cols), F32),
        compiler_params=pltpu.CompilerParams(dimension_semantics=('parallel', 'parallel', 'parallel')),
    )(c_arr, full, land)


def _exchange_chips(sums, name):
    _, rows, cols = sums.shape

    def body(s_ref, land_ref, send_sems, recv_sems, local_sem):
        x, y, c = _my_coords()
        my_chip = 2 * x + y
        mine = pltpu.make_async_copy(s_ref.at[my_chip], land_ref.at[my_chip], local_sem)
        mine.start()
        chips = [(1 - x, y), (x, 1 - y), (1 - x, 1 - y)]
        sends = []
        for t, (px, py) in enumerate(chips):
            sends.append(pltpu.make_async_remote_copy(
                src_ref=s_ref.at[2 * px + py], dst_ref=land_ref.at[my_chip],
                send_sem=send_sems.at[t], recv_sem=recv_sems.at[t], device_id=(px, py, c), device_id_type=MESH))
        for cp in sends:
            cp.start()
        for t, (px, py) in enumerate(chips):
            pltpu.make_async_remote_copy(
                src_ref=s_ref.at[my_chip], dst_ref=land_ref.at[2 * px + py],
                send_sem=send_sems.at[t], recv_sem=recv_sems.at[t], device_id=(px, py, c),
                device_id_type=MESH).wait_recv()
        for cp in sends:
            cp.wait_send()
        mine.wait()

    return pl.pallas_call(
        body, name=name,
        out_shape=jax.ShapeDtypeStruct((N_CHIP, rows, cols), sums.dtype),
        in_specs=[ANY], out_specs=ANY,
        scratch_shapes=[pltpu.SemaphoreType.DMA((3,)), pltpu.SemaphoreType.DMA((3,)), pltpu.SemaphoreType.DMA],
    )(sums)


def _reduce_scatter_slots(full, rows, cols, axis, name):
    land = _exchange_pair(full, rows, cols, axis, name + '_pair')
    sums = _pair_add(full, land, rows, cols, axis, name + '_add')
    return _exchange_chips(sums, name + '_chips')


def _sum_slots(slots, name):
    n_slots, rows, cols = slots.shape
    tr, tc = _elementwise_tiles(rows, cols)

    def body(s_ref, o_ref):
        g = s_ref[0]
        for s in range(1, n_slots):
            g = g + s_ref[s]
        o_ref[...] = g

    return pl.pallas_call(
        body, name=name,
        grid=(rows // tr, cols // tc),
        in_specs=[pl.BlockSpec((n_slots, tr, tc), lambda i, j: (0, i, j))],
        out_specs=pl.BlockSpec((tr, tc), lambda i, j: (i, j)),
        out_shape=jax.ShapeDtypeStruct((rows, cols), F32),
        compiler_params=pltpu.CompilerParams(dimension_semantics=('parallel', 'parallel')),
    )(slots)


def _sum_adamw(slots, w, m, v, name):
    n_slots, rows, cols = slots.shape
    tr, tc = _elementwise_tiles(rows, cols)
    c1 = 1.0 - ADAM_B1 ** ADAM_STEP
    c2 = 1.0 - ADAM_B2 ** ADAM_STEP

    def body(s_ref, w_ref, m_ref, v_ref, g_out, d_out, m_out, v_out):
        g = s_ref[0]
        for s in range(1, n_slots):
            g = g + s_ref[s]
        m_new = ADAM_B1 * m_ref[...] + (1.0 - ADAM_B1) * g
        v_new = ADAM_B2 * v_ref[...] + (1.0 - ADAM_B2) * (g * g)
        m_hat = m_new / c1
        v_hat = v_new / c2
        g_out[...] = g
        d_out[...] = -ADAM_LR * (m_hat / (jnp.sqrt(v_hat) + ADAM_EPS) + ADAM_WD * w_ref[...])
        m_out[...] = m_new
        v_out[...] = v_new

    blk = pl.BlockSpec((tr, tc), lambda i, j: (i, j))
    shape = jax.ShapeDtypeStruct((rows, cols), F32)
    return pl.pallas_call(
        body, name=name,
        grid=(rows // tr, cols // tc),
        in_specs=[pl.BlockSpec((n_slots, tr, tc), lambda i, j: (0, i, j)), blk, blk, blk],
        out_specs=[blk, blk, blk, blk],
        out_shape=[shape, shape, shape, shape],
        compiler_params=pltpu.CompilerParams(dimension_semantics=('parallel', 'parallel')),
    )(slots, w, m, v)


def _rows_of(shape):
    size = int(np.prod(shape)) if len(shape) else 1
    return _round_up(size, LANES) // LANES


def _pack(arrays, rows_multiple):
    parts = []
    for arr in arrays:
        flat = jnp.ravel(arr).astype(F32)
        rows = _rows_of(arr.shape)
        parts.append(jnp.pad(flat, (0, rows * LANES - flat.shape[0])).reshape(rows, LANES))
    total = sum(p.shape[0] for p in parts)
    pad = _round_up(total, rows_multiple) - total
    if pad:
        parts.append(jnp.zeros((pad, LANES), F32))
    return jnp.concatenate(parts, axis=0)


def _unpack(slab, shapes):
    out, r0 = [], 0
    for s in shapes:
        size = int(np.prod(s)) if len(s) else 1
        rows = _rows_of(s)
        out.append(slab[r0:r0 + rows].reshape(-1)[:size].reshape(s))
        r0 += rows
    return out


def _layer_norm(x, g, b):
    mu = jnp.mean(x, -1, keepdims=True)
    var = jnp.mean(jnp.square(x - mu), -1, keepdims=True)
    return (x - mu) * lax.rsqrt(var + LN_EPS) * g + b


def _modulate(x, shift, scale):
    return x * (1 + scale) + shift


def _take_cols(p, per_dev, per_dev_padded, lo, hi):
    parts = []
    while lo < hi:
        dev, off = divmod(lo, per_dev)
        n = min(hi - lo, per_dev - off)
        parts.append(p[:, dev * per_dev_padded + off: dev * per_dev_padded + off + n])
        lo += n
    return parts[0] if len(parts) == 1 else jnp.concatenate(parts, axis=1)


def _block_diag(blocks):
    g, a, b = blocks.shape
    eye = jnp.eye(g, dtype=blocks.dtype)
    return (blocks[:, :, None, :] * eye[:, None, :, None]).reshape(g * a, g * b)


def _s5_mixer(u, t_ctx, prm, name):
    t, gw = u.shape
    groups = gw // S5_CH
    ys = []
    for direction, reverse in enumerate((False, True)):
        lam = lax.complex(prm['s5_a_re'][direction], prm['s5_a_im'][direction])
        step = jnp.exp(prm['s5_log_step'][direction])[:, None]
        a_bar = jnp.exp(lam * step)
        b_bar = ((a_bar - 1.0) / lam)[..., None] * lax.complex(prm['s5_b_re'][direction],
                                                               prm['s5_b_im'][direction])
        c_mat = lax.complex(prm['s5_c_re'][direction], prm['s5_c_im'][direction])
        b_t = jnp.swapaxes(b_bar, 1, 2)
        b_mat = _interleave(_block_diag(jnp.real(b_t)), _block_diag(jnp.imag(b_t)))
        c_t = jnp.swapaxes(c_mat, 1, 2)
        c_blk = _interleave(_block_diag(jnp.real(c_t)).T, -_block_diag(jnp.imag(c_t)).T).T
        a_rows = jnp.stack([jnp.real(a_bar).reshape(-1), jnp.imag(a_bar).reshape(-1)])
        bu = _linear(u, b_mat, f'{name}_bu{direction}')
        h = _s5_scan(a_rows, bu, t_ctx, reverse, f'{name}_scan{direction}')
        ys.append(_linear(h, c_blk, f'{name}_y{direction}'))
    y = ys[0] + ys[1] + prm['s5_d'][None, :] * u
    z = jax.nn.gelu(y)
    return z * jax.nn.sigmoid(_linear(z, prm['s5_glu_w'], f'{name}_glu') + prm['s5_glu_b'])


def _chunk_gating(u, v, prm):
    t, gw = u.shape
    hd = gw // SG_HEADS
    u = jax.nn.gelu(u)
    v = jax.nn.gelu(v).reshape(t // SG_CHUNK, SG_CHUNK, SG_HEADS, hd)
    v = _layer_norm(v, prm['sg_ln_g'].reshape(SG_HEADS, hd), prm['sg_ln_b'].reshape(SG_HEADS, hd))
    s = jnp.einsum('hij,cjhd->cihd', prm['sg_w'], v) + prm['sg_b'].T[None, :, :, None]
    return u * s.reshape(t, gw)


def _pool_mixer(p, prm):
    l, gw = p.shape
    pd = gw // len(POOL_WINDOWS)
    t = np.arange(l)
    outs = []
    for g, win in enumerate(POOL_WINDOWS):
        lo = np.clip(t - win // 2, 0, l - 1)
        hi = np.clip(t + win // 2 - 1, 0, l - 1)
        pg = p[:, g * pd:(g + 1) * pd]
        padded = jnp.pad(pg, ((win // 2, win // 2), (0, 0)))
        total = padded[0:l]
        for d in range(1, win):
            total = total + padded[d:d + l]
        mean = total / jnp.asarray((hi - lo + 1).astype(np.float32))[:, None]
        outs.append(jnp.einsum('lc,cd->ld', mean - pg, prm['pool_w'][g]))
    y = jnp.concatenate(outs, axis=-1) + prm['pool_b']
    return y * prm['pool_scale']


def _dw_conv1d(x, w, b, pad):
    l = x.shape[0]
    xp = jnp.pad(x, (pad, (0, 0)))
    y = xp[0:l] * w[0]
    for k in range(1, w.shape[0]):
        y = y + xp[k:k + l] * w[k]
    return y + b


def _m2_prepare(xbc, dt_raw, prm, gw):
    heads = gw // M2_HEAD_DIM
    xbc = jax.nn.silu(_dw_conv1d(xbc, prm['m2_conv_w'], prm['m2_conv_b'], M2_PAD))
    l = xbc.shape[0]
    n_bc = M2_GROUPS * M2_STATE
    rep = heads // M2_GROUPS
    xs = xbc[:, :gw].reshape(l, heads, M2_HEAD_DIM)
    bm = jnp.repeat(xbc[:, gw:gw + n_bc].reshape(l, M2_GROUPS, M2_STATE), rep, axis=1)
    cm = jnp.repeat(xbc[:, gw + n_bc:].reshape(l, M2_GROUPS, M2_STATE), rep, axis=1)
    dt = jax.nn.softplus(dt_raw.reshape(l, 2, heads) + prm['m2_dt_bias'])
    return xs, bm, cm, dt


def _ssd_scan(xs, dt, a, bm, cm, h0, need_y):
    l, nh, hp = xs.shape
    nc = l // M2_CHUNK

    def chunks(t):
        return t.reshape((nc, M2_CHUNK) + t.shape[1:])

    xd = chunks(xs * dt[..., None])
    bc, cc = chunks(bm), chunks(cm)
    a_cum = jnp.cumsum(chunks(dt * a), axis=1)
    a_tot = a_cum[:, -1]
    decay_end = jnp.exp(a_tot[:, None] - a_cum)
    chunk_states = jnp.einsum('cqhn,cqh,cqhp->chpn', bc, decay_end, xd)

    def step(s, inp):
        tot, st = inp
        return jnp.exp(tot)[..., None, None] * s + st, s

    h_final, h_prev = lax.scan(step, h0, (a_tot, chunk_states))
    if not need_y:
        return None, h_final
    seg = a_cum[:, :, None, :] - a_cum[:, None, :, :]
    lower = jnp.tril(jnp.ones((M2_CHUNK, M2_CHUNK), bool))[None, :, :, None]
    decay = jnp.exp(jnp.where(lower, seg, -jnp.inf))
    scores = jnp.einsum('cihn,cjhn->cijh', cc, bc) * decay
    y = (jnp.einsum('cijh,cjhp->cihp', scores, xd)
         + jnp.einsum('cihn,chpn->cihp', cc, h_prev) * jnp.exp(a_cum)[..., None])
    return y.reshape(l, nh, hp), h_final


def _ssd_direction(inputs, direction, a, h0, need_y):
    xs, bm, cm, dt = inputs
    dt = dt[:, direction]
    if direction == 1:
        xs, bm, cm, dt = (jnp.flip(t, 0) for t in (xs, bm, cm, dt))
    y, h_final = _ssd_scan(xs, dt, a, bm, cm, h0, need_y)
    if direction == 1 and y is not None:
        y = jnp.flip(y, 0)
    return y, h_final


def _gated_rmsnorm(y, z, w):
    l, gw = z.shape
    g = (y * jax.nn.silu(z)).reshape(l, M2_GROUPS, gw // M2_GROUPS)
    g = g * lax.rsqrt(jnp.mean(jnp.square(g), -1, keepdims=True) + RMS_EPS)
    return g.reshape(l, gw) * w


def _mamba2_mixer(z, xbc, dt_raw, t_ctx, prm, need_ctx):
    gw = z.shape[1]
    heads = gw // M2_HEAD_DIM
    ctx_in = _m2_prepare(xbc[:t_ctx], dt_raw[:t_ctx], prm, gw)
    lat_in = _m2_prepare(xbc[t_ctx:], dt_raw[t_ctx:], prm, gw)
    a = -jnp.exp(prm['m2_a_log'])
    ys_ctx, ys_lat = [], []
    for direction in range(2):
        h0 = jnp.zeros((heads, M2_HEAD_DIM, M2_STATE), F32)
        y_c, h_c = _ssd_direction(ctx_in, direction, a[direction], h0, need_ctx)
        y_l, _ = _ssd_direction(lat_in, direction, a[direction], h_c, True)
        ys_lat.append(y_l)
        if need_ctx:
            ys_ctx.append(y_c)
    d_h = prm['m2_d'][None, :, None]

    def finish(ys, xs, zz):
        y = (ys[0] + ys[1] + d_h * xs).reshape(zz.shape[0], gw)
        return _gated_rmsnorm(y, zz, prm['m2_norm_w'])

    y_lat = finish(ys_lat, lat_in[0], z[t_ctx:])
    if need_ctx:
        return jnp.concatenate([finish(ys_ctx, ctx_in[0], z[:t_ctx]), y_lat], axis=0)
    return y_lat


def _forward_loss(dp, consts, dims):
    depth, t_ctx, d_model = dims['depth'], dims['t_ctx'], dims['d_model']
    gw = d_model // 4
    alpha = (2 * depth) ** 0.25
    in_sizes = (gw, gw, gw, gw, gw, gw + 2 * M2_GROUPS * M2_STATE, 2 * (gw // M2_HEAD_DIM))
    in_offs = np.concatenate([[0], np.cumsum(in_sizes)])
    h_lat, h_ctx = dp['x'], consts['ctx']
    for i in range(depth):
        need_ctx = i < depth - 1
        prm = {k: v[i] for k, v in dp['small'].items()}
        ml = [dp['mod_lat'][i, k][None, :] for k in range(6)]
        mc = [dp['mod_ctx'][i, k][None, :] for k in range(6)]

        hm = jnp.concatenate([_modulate(h_ctx, mc[0], mc[1]), _modulate(h_lat, ml[0], ml[1])], axis=0)
        p = _linear_tap(hm, consts['w_in'][i], dp['taps']['w_in'][i], f'in{i}')
        seg = [_take_cols(p, dims['in_per_dev'], dims['in_per_dev_pad'], int(in_offs[s]), int(in_offs[s + 1]))
               for s in range(7)]
        ya = _s5_mixer(seg[0], t_ctx, prm, f's5_{i}')
        yb = _chunk_gating(seg[1], seg[2], prm)
        yc = jnp.concatenate([_pool_mixer(seg[3][:t_ctx], prm), _pool_mixer(seg[3][t_ctx:], prm)], axis=0)
        yd = _mamba2_mixer(seg[4], seg[5], seg[6], t_ctx, prm, need_ctx)
        if need_ctx:
            mix_in = jnp.concatenate([ya, yb, yc, yd], axis=1)
        else:
            mix_in = jnp.concatenate([ya[t_ctx:], yb[t_ctx:], yc[t_ctx:], yd], axis=1)
        mix = _linear_tap(mix_in, consts['w_out'][i], dp['taps']['w_out'][i], f'out{i}')
        if need_ctx:
            h_ctx = _layer_norm(alpha * h_ctx + mc[2] * mix[:t_ctx], prm['ln1_g'], prm['ln1_b'])
            h_lat = _layer_norm(alpha * h_lat + ml[2] * mix[t_ctx:], prm['ln1_g'], prm['ln1_b'])
            hm = jnp.concatenate([_modulate(h_ctx, mc[3], mc[4]), _modulate(h_lat, ml[3], ml[4])], axis=0)
        else:
            h_lat = _layer_norm(alpha * h_lat + ml[2] * mix, prm['ln1_g'], prm['ln1_b'])
            hm = _modulate(h_lat, ml[3], ml[4])
        t_c = t_ctx if need_ctx else 0

        up = _linear_tap(hm, consts['ffn_w_up'][i], dp['taps']['ffn_w_up'][i], f'up{i}')
        f_hidden = dims['ffn_hidden']
        if dims['up_per_dev'] == dims['up_per_dev_pad']:
            gv = up
        else:
            gv = jnp.concatenate([_take_cols(up, dims['up_per_dev'], dims['up_per_dev_pad'], 0, f_hidden),
                                  _take_cols(up, dims['up_per_dev'], dims['up_per_dev_pad'], f_hidden, 2 * f_hidden)],
                                 axis=1)
        f_out = _ffn_tail(gv, prm['ffn_conv_w'], prm['ffn_conv_b'], consts['ffn_w_down'][i],
                          dp['taps']['ffn_w_down'][i], t_c, f'ffn{i}')
        if need_ctx:
            h_ctx = _layer_norm(alpha * h_ctx + mc[5] * f_out[:t_ctx], prm['ln2_g'], prm['ln2_b'])
            h_lat = _layer_norm(alpha * h_lat + ml[5] * f_out[t_ctx:], prm['ln2_g'], prm['ln2_b'])
        else:
            h_lat = _layer_norm(alpha * h_lat + ml[5] * f_out, prm['ln2_g'], prm['ln2_b'])
    err = jnp.square(h_lat - consts['target'])
    return 0.5 * jnp.sum(jnp.mean(err, axis=-1))


def _silu_grad(x):
    s = jax.nn.sigmoid(x)
    return s * (1 + x * (1 - s))


def _step(w, m, v, x, c, ctx, target):
    depth, d_model, ada_cols = w['w_ada'].shape
    t_ctx, t_lat = ctx.shape[1], x.shape[1]
    me = _my_index()
    in_per_dev = w['w_in'].shape[2]
    in_pad = _round_up(in_per_dev, LANES)
    up_per_dev = w['ffn_w_up'].shape[2]
    up_pad = _round_up(up_per_dev, LANES)
    f_hidden = w['ffn_w_down'].shape[1] * N_DEV
    dims = dict(depth=depth, t_ctx=t_ctx, d_model=d_model, in_per_dev=in_per_dev, in_per_dev_pad=in_pad,
                up_per_dev=up_per_dev, up_per_dev_pad=up_pad, ffn_hidden=f_hidden)

    rows16 = 2 * SUBLANES
    silu_c_all = _all_gather(jnp.pad(jax.nn.silu(c), ((0, SUBLANES - 1), (0, 0))), 0, 'ag_c')
    silu_c_all = silu_c_all.reshape(N_DEV, SUBLANES, d_model)[:, 0]
    silu_cc = jax.nn.silu(w['c_ctx'])
    ada_in = jnp.concatenate([silu_c_all, silu_cc[None], jnp.zeros((rows16 - N_DEV - 1, d_model), F32)], axis=0)
    mod_loc = jnp.concatenate([_matmul(ada_in, w['w_ada'][i], 'nn', 'ada_fwd') for i in range(depth)], axis=0)
    mod_all = _all_gather(mod_loc, 1, 'ag_mod').reshape(depth, rows16, 6 * d_model)
    mod_all = mod_all + w['b_ada'][:, None, :]
    mod_lat = lax.dynamic_index_in_dim(mod_all, me, axis=1, keepdims=False).reshape(depth, 6, d_model)
    mod_ctx = mod_all[:, N_DEV].reshape(depth, 6, d_model)

    def pad_cols(a, to):
        return jnp.pad(a, ((0, 0), (0, to - a.shape[1])))

    gathered = {
        'w_in': [_all_gather(pad_cols(w['w_in'][i], in_pad).astype(BF16), 1, 'ag_w_in') for i in range(depth)],
        'ffn_w_up': [_all_gather(pad_cols(w['ffn_w_up'][i], up_pad).astype(BF16), 1, 'ag_w_up')
                     for i in range(depth)],
        'w_out': [_all_gather(w['w_out'][i].astype(BF16), 0, 'ag_w_out') for i in range(depth)],
        'ffn_w_down': [_all_gather(w['ffn_w_down'][i].astype(BF16), 0, 'ag_w_down') for i in range(depth)],
    }
    sharded_small = [w[n] for n in COL_SHARDED_SMALL + ROW_SHARDED_SMALL]
    small_slab = _pack(sharded_small, PACK_ROWS)
    slab_rows = small_slab.shape[0]
    small_all = _all_gather(small_slab, 0, 'ag_small').reshape(N_DEV, slab_rows, LANES)
    per_dev = [_unpack(small_all[d], [a.shape for a in sharded_small]) for d in range(N_DEV)]
    small_full = {}
    for k, n in enumerate(COL_SHARDED_SMALL):
        small_full[n] = jnp.concatenate([per_dev[d][k] for d in range(N_DEV)], axis=-1)
    for k, n in enumerate(ROW_SHARDED_SMALL):
        small_full[n] = jnp.concatenate([per_dev[d][len(COL_SHARDED_SMALL) + k] for d in range(N_DEV)], axis=1)

    small = {n: w[n] for n in REPLICATED}
    small.update(small_full)
    taps = {n: [jnp.zeros(gathered[n][i].shape, F32) for i in range(depth)] for n in BIG}
    dp = dict(x=x[0], small=small, mod_lat=mod_lat, mod_ctx=mod_ctx, taps=taps)
    consts = dict(ctx=ctx[0], target=target[0], **gathered)
    loss_local, grads = jax.value_and_grad(functools.partial(_forward_loss, consts=consts, dims=dims))(dp)

    out_g, out_d, out_m, out_v = {}, {}, {}, {}

    def finish_big(name, i, rows, cols, axis, keep_cols):
        slots = _reduce_scatter_slots(grads['taps'][name][i], rows, cols, axis, f'rs_{name}')
        shard = [pad_cols(t[name][i], cols) for t in (w, m, v)]
        res = _sum_adamw(slots, *shard, f'adamw_{name}')
        return [r[:, :keep_cols] for r in res]

    big_specs = {
        'w_in': (d_model, in_pad, 1, in_per_dev),
        'ffn_w_up': (d_model, up_pad, 1, up_per_dev),
        'w_out': (w['w_out'].shape[1], d_model, 0, d_model),
        'ffn_w_down': (w['ffn_w_down'].shape[1], d_model, 0, d_model),
    }
    for name, (rows, cols, axis, keep) in big_specs.items():
        per_layer = [finish_big(name, i, rows, cols, axis, keep) for i in range(depth)]
        for k, dst in enumerate((out_g, out_d, out_m, out_v)):
            dst[name] = jnp.stack([per_layer[i][k] for i in range(depth)])

    d_lat = grads['mod_lat'].reshape(depth, 6 * d_model)
    d_ctx = grads['mod_ctx'].reshape(depth, 6 * d_model)
    d_rows = jnp.concatenate([d_lat, d_ctx, jnp.zeros((SUBLANES - 2 * depth, 6 * d_model), F32)], axis=0)
    d_all = _all_gather(d_rows, 0, 'ag_dmod').reshape(N_DEV, SUBLANES, 6 * d_model)
    d_lat_all = d_all[:, :depth]
    d_ctx_sum = d_all[0, depth:2 * depth]
    for d in range(1, N_DEV):
        d_ctx_sum = d_ctx_sum + d_all[d, depth:2 * depth]
    g_b_ada = d_ctx_sum
    for d in range(N_DEV):
        g_b_ada = g_b_ada + d_lat_all[d]
    g_w_ada, c_ctx_part = [], jnp.zeros((d_model,), F32)
    for i in range(depth):
        d_mat = jnp.concatenate([d_lat_all[:, i], d_ctx_sum[i][None],
                                 jnp.zeros((rows16 - N_DEV - 1, 6 * d_model), F32)], axis=0)
        d_mine = lax.dynamic_slice_in_dim(d_mat, me * ada_cols, ada_cols, axis=1)
        g_w_ada.append(_matmul(ada_in, d_mine, 'tn', 'ada_dw'))
        back = _matmul(d_mine, w['w_ada'][i], 'nt', 'ada_dx')
        c_ctx_part = c_ctx_part + back[N_DEV]
    c_ctx_part = c_ctx_part * _silu_grad(w['c_ctx'])
    g_w_ada = jnp.stack(g_w_ada).reshape(1, depth * d_model, ada_cols)
    res = _sum_adamw(g_w_ada, *[t['w_ada'].reshape(depth * d_model, ada_cols) for t in (w, m, v)], 'adamw_w_ada')
    for k, dst in enumerate((out_g, out_d, out_m, out_v)):
        dst['w_ada'] = res[k].reshape(depth, d_model, ada_cols)

    reduced_names = REPLICATED[:]
    reduced_names.remove('b_ada')
    reduced_names += list(COL_SHARDED_SMALL + ROW_SHARDED_SMALL)
    to_reduce = [grads['small'][n] for n in reduced_names] + [c_ctx_part, loss_local]
    slab = _pack(to_reduce, N_DEV * PACK_ROWS)
    chunk_rows = slab.shape[0] // N_DEV
    slots = _reduce_scatter_slots(slab, chunk_rows, LANES, 0, 'rs_small')
    mine = _sum_slots(slots, 'sum_small')
    summed = _all_gather(mine, 0, 'ag_small_sum')
    parts = _unpack(summed, [a.shape for a in to_reduce])
    g_small = dict(zip(reduced_names, parts[:len(reduced_names)]))
    g_small['c_ctx'] = parts[-2]
    g_small['b_ada'] = g_b_ada
    loss = parts[-1]

    def my_shard(name, full):
        if name in COL_SHARDED_SMALL:
            n = full.shape[-1] // N_DEV
            return lax.dynamic_slice_in_dim(full, me * n, n, axis=full.ndim - 1)
        if name in ROW_SHARDED_SMALL:
            n = full.shape[1] // N_DEV
            return lax.dynamic_slice_in_dim(full, me * n, n, axis=1)
        return full

    small_names = [n for n in WEIGHTS if n not in BIG + ('w_ada',)]
    g_list = [my_shard(n, g_small[n]) for n in small_names]
    g_slab = _pack(g_list, PACK_ROWS)
    res = _sum_adamw(g_slab[None], *[_pack([t[n] for n in small_names], PACK_ROWS) for t in (w, m, v)],
                     'adamw_small')
    shapes = [w[n].shape for n in small_names]
    for k, dst in enumerate((out_g, out_d, out_m, out_v)):
        if k == 0:
            dst.update(dict(zip(small_names, g_list)))
        else:
            dst.update(dict(zip(small_names, _unpack(res[k], shapes))))

    grad_x = grads['x'][None]
    return (loss, grad_x, *[out_g[n] for n in WEIGHTS], *[out_d[n] for n in WEIGHTS],
            *[out_m[n] for n in WEIGHTS], *[out_v[n] for n in WEIGHTS])


def kernel(x, c, ctx, c_ctx, w_ada, b_ada, w_in, w_out, ln1_g, ln1_b, ln2_g, ln2_b, s5_a_re, s5_a_im, s5_b_re, s5_b_im, s5_c_re, s5_c_im, s5_log_step, s5_d, s5_glu_w, s5_glu_b, sg_ln_g, sg_ln_b, sg_w, sg_b, pool_w, pool_b, pool_scale, m2_conv_w, m2_conv_b, m2_dt_bias, m2_a_log, m2_d, m2_norm_w, ffn_w_up, ffn_conv_w, ffn_conv_b, ffn_w_down, loss_target, m_c_ctx, m_w_ada, m_b_ada, m_w_in, m_w_out, m_ln1_g, m_ln1_b, m_ln2_g, m_ln2_b, m_s5_a_re, m_s5_a_im, m_s5_b_re, m_s5_b_im, m_s5_c_re, m_s5_c_im, m_s5_log_step, m_s5_d, m_s5_glu_w, m_s5_glu_b, m_sg_ln_g, m_sg_ln_b, m_sg_w, m_sg_b, m_pool_w, m_pool_b, m_pool_scale, m_m2_conv_w, m_m2_conv_b, m_m2_dt_bias, m_m2_a_log, m_m2_d, m_m2_norm_w, m_ffn_w_up, m_ffn_conv_w, m_ffn_conv_b, m_ffn_w_down, v_c_ctx, v_w_ada, v_b_ada, v_w_in, v_w_out, v_ln1_g, v_ln1_b, v_ln2_g, v_ln2_b, v_s5_a_re, v_s5_a_im, v_s5_b_re, v_s5_b_im, v_s5_c_re, v_s5_c_im, v_s5_log_step, v_s5_d, v_s5_glu_w, v_s5_glu_b, v_sg_ln_g, v_sg_ln_b, v_sg_w, v_sg_b, v_pool_w, v_pool_b, v_pool_scale, v_m2_conv_w, v_m2_conv_b, v_m2_dt_bias, v_m2_a_log, v_m2_d, v_m2_norm_w, v_ffn_w_up, v_ffn_conv_w, v_ffn_conv_b, v_ffn_w_down):
    given = dict(locals())
    w = {n: given[n] for n in WEIGHTS}
    m = {n: given['m_' + n] for n in WEIGHTS}
    v = {n: given['v_' + n] for n in WEIGHTS}
    return _step(w, m, v, x, c, ctx, loss_target)
```

```python
import functools

import jax
import jax.numpy as jnp
import numpy as np
from jax import lax
from jax.experimental import pallas as pl
from jax.experimental.pallas import tpu as pltpu

F32 = jnp.float32
BF16 = jnp.bfloat16
MESH = pl.DeviceIdType.MESH
ANY = pl.BlockSpec(memory_space=pl.ANY)

N_DEV = 8
N_CHIP = 4
LANES = 128
SUBLANES = 8
VMEM_LIMIT_BYTES = 48 * 1024 * 1024
MATMUL_VMEM_BUDGET = 36 * 1024 * 1024
PACK_ROWS = 512

GRID_W = 64
S5_CH = 16
S5_STATE = 64
SG_HEADS = 4
SG_CHUNK = 128
POOL_WINDOWS = (2, 4, 8, 16)
M2_HEAD_DIM = 64
M2_STATE = 128
M2_GROUPS = 2
M2_CONV = 4
M2_PAD = (M2_CONV // 2, M2_CONV - 1 - M2_CONV // 2)
M2_CHUNK = 128
FFN_CONV = 3
LN_EPS = 1e-5
RMS_EPS = 1e-5
ADAM_LR = 0.001
ADAM_B1 = 0.9
ADAM_B2 = 0.999
ADAM_EPS = 1e-08
ADAM_WD = 0.01
ADAM_STEP = 10

WEIGHTS = ['c_ctx', 'w_ada', 'b_ada', 'w_in', 'w_out', 'ln1_g', 'ln1_b', 'ln2_g', 'ln2_b', 's5_a_re', 's5_a_im',
           's5_b_re', 's5_b_im', 's5_c_re', 's5_c_im', 's5_log_step', 's5_d', 's5_glu_w', 's5_glu_b', 'sg_ln_g',
           'sg_ln_b', 'sg_w', 'sg_b', 'pool_w', 'pool_b', 'pool_scale', 'm2_conv_w', 'm2_conv_b', 'm2_dt_bias',
           'm2_a_log', 'm2_d', 'm2_norm_w', 'ffn_w_up', 'ffn_conv_w', 'ffn_conv_b', 'ffn_w_down']
BIG = ('w_in', 'w_out', 'ffn_w_up', 'ffn_w_down')
COL_SHARDED_SMALL = ('m2_conv_w', 'ffn_conv_w')
ROW_SHARDED_SMALL = ('s5_glu_w',)
REPLICATED = [n for n in WEIGHTS if n not in BIG + COL_SHARDED_SMALL + ROW_SHARDED_SMALL + ('w_ada', 'c_ctx')]


def _round_up(n, m):
    return (n + m - 1) // m * m


def _my_coords():
    return lax.axis_index('x'), lax.axis_index('y'), lax.axis_index('c')


def _my_index():
    x, y, c = _my_coords()
    return 4 * x + 2 * y + c


def _first_divisor(n, cands):
    for c in cands:
        if n % c == 0:
            return c
    return n


def _matmul_tiles(mode, m, n, k, a_bytes, b_bytes):
    lane_c = (1024, 512, 384, 256, 128)
    sub_c = (1088, 1024, 544, 512, 256, 128, 64, 32, 16)
    tm = _first_divisor(m, lane_c if mode == 'tn' else sub_c)
    tn = _first_divisor(n, lane_c)
    k_c = [c for c in (5632, 4096, 2048, 1408, 1088, 1024, 544, 512, 256, 128) if k % c == 0] or [k]
    if mode == 'tn':
        k_c = [c for c in k_c if c <= 1088] or [k_c[-1]]
    for tk in k_c:
        use = 2 * (tm * tk * a_bytes + tk * tn * b_bytes) + 3 * tm * tn * 4
        if use <= MATMUL_VMEM_BUDGET:
            return tm, tn, tk
    return tm, tn, k_c[-1]


def _matmul(a, b, mode, name, out_dtype=F32):
    if mode == 'nn':
        (m, k), (k2, n) = a.shape, b.shape
    elif mode == 'nt':
        (m, k), (n, k2) = a.shape, b.shape
    else:
        (k, m), (k2, n) = a.shape, b.shape
    assert k == k2, (mode, a.shape, b.shape)
    tm, tn, tk = _matmul_tiles(mode, m, n, k, a.dtype.itemsize, b.dtype.itemsize)
    nk = k // tk
    if mode == 'nn':
        a_spec = pl.BlockSpec((tm, tk), lambda i, j, kk: (i, kk))
        b_spec = pl.BlockSpec((tk, tn), lambda i, j, kk: (kk, j))
        dims = (((1,), (0,)), ((), ()))
    elif mode == 'nt':
        a_spec = pl.BlockSpec((tm, tk), lambda i, j, kk: (i, kk))
        b_spec = pl.BlockSpec((tn, tk), lambda i, j, kk: (j, kk))
        dims = (((1,), (1,)), ((), ()))
    else:
        a_spec = pl.BlockSpec((tk, tm), lambda i, j, kk: (kk, i))
        b_spec = pl.BlockSpec((tk, tn), lambda i, j, kk: (kk, j))
        dims = (((0,), (0,)), ((), ()))

    def body(a_ref, b_ref, o_ref, acc_ref):
        kk = pl.program_id(2)

        @pl.when(kk == 0)
        def _():
            acc_ref[...] = jnp.zeros_like(acc_ref)

        acc_ref[...] += lax.dot_general(a_ref[...].astype(BF16), b_ref[...].astype(BF16), dims,
                                        preferred_element_type=F32)

        @pl.when(kk == nk - 1)
        def _():
            o_ref[...] = acc_ref[...].astype(o_ref.dtype)

    return pl.pallas_call(
        body,
        name=name,
        grid=(m // tm, n // tn, nk),
        in_specs=[a_spec, b_spec],
        out_specs=pl.BlockSpec((tm, tn), lambda i, j, kk: (i, j)),
        out_shape=jax.ShapeDtypeStruct((m, n), out_dtype),
        scratch_shapes=[pltpu.VMEM((tm, tn), F32)],
        compiler_params=pltpu.CompilerParams(dimension_semantics=('parallel', 'parallel', 'arbitrary'),
                                             vmem_limit_bytes=VMEM_LIMIT_BYTES),
    )(a, b)


@functools.partial(jax.custom_vjp, nondiff_argnums=(2,))
def _linear(x, w, name):
    return _matmul(x, w, 'nn', name + '_fwd')


def _linear_fwd(x, w, name):
    return _matmul(x, w, 'nn', name + '_fwd'), (x, w)


def _linear_bwd(name, res, dy):
    x, w = res
    return _matmul(dy, w, 'nt', name + '_dx'), _matmul(x, dy, 'tn', name + '_dw')


_linear.defvjp(_linear_fwd, _linear_bwd)


@functools.partial(jax.custom_vjp, nondiff_argnums=(3,))
def _linear_tap(x, w, tap, name):
    del tap
    return _matmul(x, w, 'nn', name + '_fwd')


def _linear_tap_fwd(x, w, tap, name):
    del tap
    return _matmul(x, w, 'nn', name + '_fwd'), (x, w)


def _linear_tap_bwd(name, res, dy):
    x, w = res
    return _matmul(dy, w, 'nt', name + '_dx'), jnp.zeros_like(w), _matmul(x, dy, 'tn', name + '_dw')


_linear_tap.defvjp(_linear_tap_fwd, _linear_tap_bwd)


SCAN_ROWS = 128
SCAN_LANES = 512


def _scan_lanes(n):
    return SCAN_LANES if n % SCAN_LANES == 0 else n


def _interleave(re, im):
    r, n = re.shape
    lanes = _scan_lanes(n)
    return jnp.stack([re.reshape(r, n // lanes, lanes), im.reshape(r, n // lanes, lanes)], axis=2).reshape(r, 2 * n)


def _scan_call(a, x, h, n_ctx_blocks, ctx_first, reverse_rows, name):
    t, n2 = x.shape
    n = n2 // 2
    nblk = t // SCAN_ROWS
    nlat = nblk - n_ctx_blocks
    lanes = _scan_lanes(n)
    with_da = h is not None

    def block_of(i):
        if ctx_first:
            first_n, first_0, second_0, second_n = n_ctx_blocks, 0, n_ctx_blocks, nlat
        else:
            first_n, first_0, second_0, second_n = nlat, n_ctx_blocks, 0, n_ctx_blocks
        if reverse_rows:
            in_first = first_0 + first_n - 1 - i
            in_second = second_0 + second_n - 1 - (i - first_n)
        else:
            in_first = first_0 + i
            in_second = second_0 + (i - first_n)
        return jnp.where(i < first_n, in_first, in_second)

    groups = SCAN_ROWS // SUBLANES
    first_row = SUBLANES - 1 if reverse_rows else 0
    to_previous = SUBLANES - 1 if reverse_rows else 1

    def body(*refs):
        if with_da:
            a_ref, x_ref, h_ref, o_ref, da_ref, st_ref = refs
        else:
            a_ref, x_ref, o_ref, st_ref = refs

        @pl.when(pl.program_id(1) == 0)
        def _():
            st_ref[...] = jnp.zeros_like(st_ref)
            if with_da:
                da_ref[...] = jnp.zeros_like(da_ref)

        row_id = lax.broadcasted_iota(jnp.int32, (SUBLANES, lanes), 0)
        behind = (SUBLANES - 1 - row_id) if reverse_rows else row_id

        def cmul(pr, pi, qr, qi):
            return pr * qr - pi * qi, pr * qi + pi * qr

        a1 = (jnp.broadcast_to(a_ref[0:1, :], (SUBLANES, lanes)), jnp.broadcast_to(a_ref[1:2, :], (SUBLANES, lanes)))
        a2 = cmul(*a1, *a1)
        a4 = cmul(*a2, *a2)
        pw = a1
        for bit, ak in ((1, a1), (2, a2), (4, a4)):
            nxt = cmul(*pw, *ak)
            pw = (jnp.where((behind & bit) != 0, nxt[0], pw[0]), jnp.where((behind & bit) != 0, nxt[1], pw[1]))

        def group(g, carry):
            in_r, in_i = carry
            gi = (groups - 1 - g) if reverse_rows else g
            start = pl.multiple_of(gi * SUBLANES, SUBLANES)
            xr = x_ref[pl.ds(start, SUBLANES), pl.ds(0, lanes)]
            xi = x_ref[pl.ds(start, SUBLANES), pl.ds(lanes, lanes)]
            for k, ak in ((1, a1), (2, a2), (4, a4)):
                shift = (SUBLANES - k) if reverse_rows else k
                pr = jnp.where(behind >= k, pltpu.roll(xr, shift, axis=0), 0.0)
                pi = jnp.where(behind >= k, pltpu.roll(xi, shift, axis=0), 0.0)
                qr, qi = cmul(*ak, pr, pi)
                xr, xi = xr + qr, xi + qi
            cr, ci = cmul(*pw, in_r, in_i)
            out_r, out_i = xr + cr, xi + ci
            last = 0 if reverse_rows else SUBLANES - 1
            sr = jnp.broadcast_to(out_r[last:last + 1, :], (SUBLANES, lanes))
            si = jnp.broadcast_to(out_i[last:last + 1, :], (SUBLANES, lanes))
            o_ref[pl.ds(start, SUBLANES), pl.ds(0, lanes)] = out_r
            o_ref[pl.ds(start, SUBLANES), pl.ds(lanes, lanes)] = out_i
            if with_da:
                pr = jnp.where(row_id == first_row, in_r, pltpu.roll(out_r, to_previous, axis=0))
                pi = jnp.where(row_id == first_row, in_i, pltpu.roll(out_i, to_previous, axis=0))
                hr = h_ref[pl.ds(start, SUBLANES), pl.ds(0, lanes)]
                hi = h_ref[pl.ds(start, SUBLANES), pl.ds(lanes, lanes)]
                da_ref[0:SUBLANES, :] += hr * pr + hi * pi
                da_ref[SUBLANES:2 * SUBLANES, :] += hr * pi - hi * pr
            return sr, si

        sr, si = lax.fori_loop(0, groups, group, (st_ref[0], st_ref[1]))
        st_ref[0] = sr
        st_ref[1] = si

    row_spec = pl.BlockSpec((SCAN_ROWS, 2 * lanes), lambda j, i: (block_of(i), j))
    in_specs = [pl.BlockSpec((2, lanes), lambda j, i: (0, j)), row_spec]
    out_specs = [row_spec]
    out_shape = [jax.ShapeDtypeStruct((t, n2), F32)]
    operands = [a, x]
    if with_da:
        in_specs.append(row_spec)
        operands.append(h)
        out_specs.append(pl.BlockSpec((2 * SUBLANES, lanes), lambda j, i: (0, j)))
        out_shape.append(jax.ShapeDtypeStruct((2 * SUBLANES, n), F32))
    return pl.pallas_call(
        body,
        name=name,
        grid=(n // lanes, nblk),
        in_specs=in_specs,
        out_specs=out_specs,
        out_shape=out_shape,
        scratch_shapes=[pltpu.VMEM((2, SUBLANES, lanes), F32)],
        compiler_params=pltpu.CompilerParams(dimension_semantics=('parallel', 'arbitrary')),
    )(*operands)


@functools.partial(jax.custom_vjp, nondiff_argnums=(2, 3, 4))
def _s5_scan(a, bu, t_ctx, reverse, name):
    return _scan_call(a, bu, None, t_ctx // SCAN_ROWS, True, reverse, name + '_fwd')[0]


def _s5_scan_fwd(a, bu, t_ctx, reverse, name):
    h = _scan_call(a, bu, None, t_ctx // SCAN_ROWS, True, reverse, name + '_fwd')[0]
    return h, (a, h)


def _s5_scan_bwd(t_ctx, reverse, name, res, dh):
    a, h = res
    a_conj = a * jnp.array([[1.0], [-1.0]], F32)
    g, da = _scan_call(a_conj, dh, h, t_ctx // SCAN_ROWS, False, not reverse, name + '_bwd')
    da = jnp.stack([jnp.sum(da[:SUBLANES], axis=0), jnp.sum(da[SUBLANES:], axis=0)])
    return da, g


_s5_scan.defvjp(_s5_scan_fwd, _s5_scan_bwd)


FFN_LANES = 128
CONV_TAPS = FFN_CONV * FFN_CONV
CONV_W_ROWS = 16
GELU_K = 0.7978845608028654
GELU_C = 0.044715


def _gelu_and_slope(x):
    x2 = x * x
    th = jnp.tanh(GELU_K * (x + GELU_C * x * x2))
    cdf = 0.5 * (1.0 + th)
    slope = cdf + 0.5 * x * (1.0 - th * th) * (GELU_K * (1.0 + 3.0 * GELU_C * x2))
    return x * cdf, slope


def _band_rows(t_lat):
    return _first_divisor(t_lat // GRID_W, (8, 4, 2, 1)) * GRID_W


def _shifted_ctx(x):
    n = x.shape[0]
    row = lax.broadcasted_iota(jnp.int32, x.shape, 0)
    left = jnp.where(row == 0, 0.0, pltpu.roll(x, 1, axis=0))
    right = jnp.where(row == n - 1, 0.0, pltpu.roll(x, n - 1, axis=0))
    return left, x, right


def _shifted_band(src_ref, bidx, n_bands, band, t_ctx, t):
    start = pl.multiple_of(t_ctx + bidx * band, GRID_W)
    top_start = pl.multiple_of(jnp.maximum(start - GRID_W, 0), SUBLANES)
    bot_start = pl.multiple_of(jnp.minimum(start + band, t - GRID_W), SUBLANES)
    top = jnp.where(bidx > 0, src_ref[pl.ds(top_start, GRID_W), :], 0.0)
    bot = jnp.where(bidx < n_bands - 1, src_ref[pl.ds(bot_start, GRID_W), :], 0.0)
    ext = jnp.concatenate([top, src_ref[pl.ds(start, band), :], bot], axis=0)
    n_ext = band + 2 * GRID_W
    col = lax.broadcasted_iota(jnp.int32, ext.shape, 0) & (GRID_W - 1)
    left = jnp.where(col == 0, 0.0, pltpu.roll(ext, 1, axis=0))
    right = jnp.where(col == GRID_W - 1, 0.0, pltpu.roll(ext, n_ext - 1, axis=0))
    return left, ext, right


def _conv_sum(shifted, w_rows, band, flip):
    acc = None
    for i in range(FFN_CONV):
        for j in range(FFN_CONV):
            w = w_rows[(FFN_CONV - 1 - i) * FFN_CONV + (FFN_CONV - 1 - j)] if flip else w_rows[i * FFN_CONV + j]
            term = shifted[j][i * GRID_W:i * GRID_W + band] * w
            acc = term if acc is None else acc + term
    return acc


def _conv_operand(conv_w, conv_b):
    f = conv_b.shape[0]
    return jnp.concatenate([conv_w.reshape(CONV_TAPS, f), conv_b[None],
                            jnp.zeros((CONV_W_ROWS - CONV_TAPS - 1, f), F32)], axis=0)


def _ffn_mid_fwd(gv, w16, t_ctx, name):
    t, f2 = gv.shape
    f = f2 // 2
    lanes = FFN_LANES
    nf = f // lanes
    band = _band_rows(t - t_ctx)
    n_bands = (t - t_ctx) // band
    mid = (FFN_CONV // 2) * FFN_CONV

    def body(g_ref, v_ref, w_ref, o_ref):
        w_rows = [w_ref[k:k + 1, :] for k in range(CONV_TAPS)]
        bias = w_ref[CONV_TAPS:CONV_TAPS + 1, :]
        if t_ctx:
            sh = _shifted_ctx(g_ref[0:t_ctx, :])
            pre = sh[0] * w_rows[mid] + sh[1] * w_rows[mid + 1] + sh[2] * w_rows[mid + 2] + bias
            o_ref[0:t_ctx, :] = (_gelu_and_slope(pre)[0] * v_ref[0:t_ctx, :]).astype(o_ref.dtype)

        def one_band(b, carry):
            start = pl.multiple_of(t_ctx + b * band, GRID_W)
            pre = _conv_sum(_shifted_band(g_ref, b, n_bands, band, t_ctx, t), w_rows, band, False) + bias
            o_ref[pl.ds(start, band), :] = (_gelu_and_slope(pre)[0] * v_ref[pl.ds(start, band), :]).astype(o_ref.dtype)
            return carry

        lax.fori_loop(0, n_bands, one_band, 0)

    return pl.pallas_call(
        body, name=name,
        grid=(nf,),
        in_specs=[pl.BlockSpec((t, lanes), lambda j: (0, j)),
                  pl.BlockSpec((t, lanes), lambda j: (0, nf + j)),
                  pl.BlockSpec((CONV_W_ROWS, lanes), lambda j: (0, j))],
        out_specs=pl.BlockSpec((t, lanes), lambda j: (0, j)),
        out_shape=jax.ShapeDtypeStruct((t, f), BF16),
        compiler_params=pltpu.CompilerParams(dimension_semantics=('parallel',), vmem_limit_bytes=VMEM_LIMIT_BYTES),
    )(gv, gv, w16)


def _ffn_mid_bwd(d_act, gv, w16, t_ctx, name):
    t, f2 = gv.shape
    f = f2 // 2
    lanes = FFN_LANES
    nf = f // lanes
    band = _band_rows(t - t_ctx)
    n_bands = (t - t_ctx) // band
    mid = (FFN_CONV // 2) * FFN_CONV
    n_acc = CONV_TAPS + 1

    def tile_sum(x):
        return jnp.sum(x.reshape(x.shape[0] // SUBLANES, SUBLANES, lanes), axis=0)

    def body(da_ref, g_ref, v_ref, w_ref, dg_ref, dv_ref, dw_ref, dp_ref):
        w_rows = [w_ref[k:k + 1, :] for k in range(CONV_TAPS)]
        bias = w_ref[CONV_TAPS:CONV_TAPS + 1, :]
        acc = [jnp.zeros((SUBLANES, lanes), F32) for _ in range(n_acc)]
        if t_ctx:
            sh = _shifted_ctx(g_ref[0:t_ctx, :])
            pre = sh[0] * w_rows[mid] + sh[1] * w_rows[mid + 1] + sh[2] * w_rows[mid + 2] + bias
            ge, slope = _gelu_and_slope(pre)
            da = da_ref[0:t_ctx, :]
            dv_ref[0:t_ctx, :] = da * ge
            dpre = da * v_ref[0:t_ctx, :] * slope
            dp_ref[0:t_ctx, :] = dpre
            for j in range(FFN_CONV):
                acc[mid + j] = acc[mid + j] + tile_sum(sh[j] * dpre)
            acc[CONV_TAPS] = acc[CONV_TAPS] + tile_sum(dpre)
            back = _shifted_ctx(dpre)
            dg_ref[0:t_ctx, :] = back[2] * w_rows[mid] + back[1] * w_rows[mid + 1] + back[0] * w_rows[mid + 2]

        def first_pass(b, acc):
            acc = list(acc)
            start = pl.multiple_of(t_ctx + b * band, GRID_W)
            sh = _shifted_band(g_ref, b, n_bands, band, t_ctx, t)
            ge, slope = _gelu_and_slope(_conv_sum(sh, w_rows, band, False) + bias)
            da = da_ref[pl.ds(start, band), :]
            dv_ref[pl.ds(start, band), :] = da * ge
            dpre = da * v_ref[pl.ds(start, band), :] * slope
            dp_ref[pl.ds(start, band), :] = dpre
            for i in range(FFN_CONV):
                for j in range(FFN_CONV):
                    k = i * FFN_CONV + j
                    acc[k] = acc[k] + tile_sum(sh[j][i * GRID_W:i * GRID_W + band] * dpre)
            acc[CONV_TAPS] = acc[CONV_TAPS] + tile_sum(dpre)
            return tuple(acc)

        acc = lax.fori_loop(0, n_bands, first_pass, tuple(acc))
        for k in range(n_acc):
            dw_ref[k * SUBLANES:(k + 1) * SUBLANES, :] = acc[k]

        def second_pass(b, carry):
            start = pl.multiple_of(t_ctx + b * band, GRID_W)
            dg_ref[pl.ds(start, band), :] = _conv_sum(_shifted_band(dp_ref, b, n_bands, band, t_ctx, t),
                                                      w_rows, band, True)
            return carry

        lax.fori_loop(0, n_bands, second_pass, 0)

    col = pl.BlockSpec((t, lanes), lambda j: (0, j))
    return pl.pallas_call(
        body, name=name,
        grid=(nf,),
        in_specs=[col, col, pl.BlockSpec((t, lanes), lambda j: (0, nf + j)),
                  pl.BlockSpec((CONV_W_ROWS, lanes), lambda j: (0, j))],
        out_specs=[col, col, pl.BlockSpec((n_acc * SUBLANES, lanes), lambda j: (0, j))],
        out_shape=[jax.ShapeDtypeStruct((t, f), F32), jax.ShapeDtypeStruct((t, f), F32),
                   jax.ShapeDtypeStruct((n_acc * SUBLANES, f), F32)],
        scratch_shapes=[pltpu.VMEM((t, lanes), F32)],
        compiler_params=pltpu.CompilerParams(dimension_semantics=('parallel',), vmem_limit_bytes=VMEM_LIMIT_BYTES),
    )(d_act, gv, gv, w16)


@functools.partial(jax.custom_vjp, nondiff_argnums=(5, 6))
def _ffn_tail(gv, conv_w, conv_b, w_down, tap, t_ctx, name):
    del tap
    act = _ffn_mid_fwd(gv, _conv_operand(conv_w, conv_b), t_ctx, name + '_mid_fwd')
    return _matmul(act, w_down, 'nn', name + '_down_fwd')


def _ffn_tail_fwd(gv, conv_w, conv_b, w_down, tap, t_ctx, name):
    del tap
    w16 = _conv_operand(conv_w, conv_b)
    act = _ffn_mid_fwd(gv, w16, t_ctx, name + '_mid_fwd')
    return _matmul(act, w_down, 'nn', name + '_down_fwd'), (gv, w16, w_down, act)


def _ffn_tail_bwd(t_ctx, name, res, d_out):
    gv, w16, w_down, act = res
    f = act.shape[1]
    d_act = _matmul(d_out, w_down, 'nt', name + '_down_dx')
    d_w_down = _matmul(act, d_out, 'tn', name + '_down_dw')
    d_gate, d_val, d_w = _ffn_mid_bwd(d_act, gv, w16, t_ctx, name + '_mid_bwd')
    d_w = jnp.sum(d_w.reshape(CONV_TAPS + 1, SUBLANES, f), axis=1)
    return (jnp.concatenate([d_gate, d_val], axis=1), d_w[:CONV_TAPS].reshape(FFN_CONV, FFN_CONV, f),
            d_w[CONV_TAPS], jnp.zeros_like(w_down), d_w_down)


_ffn_tail.defvjp(_ffn_tail_fwd, _ffn_tail_bwd)


LN_ROWS = 128
MOD_ROWS = 8


def _normalised(h, y, gate, alpha):
    r = alpha * h + gate * y
    xc = r - jnp.mean(r, axis=-1, keepdims=True)
    rstd = lax.rsqrt(jnp.mean(xc * xc, axis=-1, keepdims=True) + LN_EPS)
    return xc * rstd, rstd


def _mod_row(m_ref, k, is_ctx):
    lat = m_ref[3 + k:4 + k, :]
    return lat if is_ctx is None else jnp.where(is_ctx, m_ref[k:k + 1, :], lat)


def _res_ln_fwd_call(h, y, mods, t_ctx, alpha, with_mod, name):
    t, d = h.shape
    n_ctx_tiles = t_ctx // LN_ROWS

    def body(h_ref, y_ref, m_ref, hn_ref, *rest):
        is_ctx = (pl.program_id(0) < n_ctx_tiles) if n_ctx_tiles else None
        xhat, _ = _normalised(h_ref[...], y_ref[...], _mod_row(m_ref, 0, is_ctx), alpha)
        hn = xhat * m_ref[6:7, :] + m_ref[7:8, :]
        hn_ref[...] = hn
        if with_mod:
            rest[0][...] = hn * (1.0 + _mod_row(m_ref, 2, is_ctx)) + _mod_row(m_ref, 1, is_ctx)

    blk = pl.BlockSpec((LN_ROWS, d), lambda i: (i, 0))
    n_out = 2 if with_mod else 1
    return pl.pallas_call(
        body, name=name,
        grid=(t // LN_ROWS,),
        in_specs=[blk, blk, pl.BlockSpec((MOD_ROWS, d), lambda i: (0, 0))],
        out_specs=[blk] * n_out,
        out_shape=[jax.ShapeDtypeStruct((t, d), F32)] * n_out,
        compiler_params=pltpu.CompilerParams(dimension_semantics=('parallel',)),
    )(h, y, mods)


def _res_ln_bwd_call(h, y, mods, d_hn, d_hm, t_ctx, alpha, name):
    t, d = h.shape
    n_ctx_tiles = t_ctx // LN_ROWS
    with_mod = d_hm is not None

    def tile_sum(x):
        return jnp.sum(x.reshape(LN_ROWS // SUBLANES, SUBLANES, d), axis=0)

    def body(*refs):
        if with_mod:
            h_ref, y_ref, m_ref, dhn_ref, dhm_ref, dh_ref, dy_ref, acc_ref = refs
        else:
            h_ref, y_ref, m_ref, dhn_ref, dh_ref, dy_ref, acc_ref = refs

        @pl.when(pl.program_id(0) == 0)
        def _():
            acc_ref[...] = jnp.zeros_like(acc_ref)

        is_ctx = (pl.program_id(0) < n_ctx_tiles) if n_ctx_tiles else None
        gate = _mod_row(m_ref, 0, is_ctx)
        y = y_ref[...]
        xhat, rstd = _normalised(h_ref[...], y, gate, alpha)
        ln_g = m_ref[6:7, :]
        dhn = dhn_ref[...]
        base = 3 * SUBLANES if is_ctx is None else jnp.where(is_ctx, 0, 3 * SUBLANES)

        def add_to(row, part):
            if isinstance(row, int):
                acc_ref[row:row + SUBLANES, :] += part
            else:
                acc_ref[pl.ds(pl.multiple_of(row, SUBLANES), SUBLANES), :] += part

        if with_mod:
            dhm = dhm_ref[...]
            hn = xhat * ln_g + m_ref[7:8, :]
            add_to(base + SUBLANES, tile_sum(dhm))
            add_to(base + 2 * SUBLANES, tile_sum(dhm * hn))
            dhn = dhn + dhm * (1.0 + _mod_row(m_ref, 2, is_ctx))
        add_to(6 * SUBLANES, tile_sum(dhn * xhat))
        add_to(7 * SUBLANES, tile_sum(dhn))
        dx = dhn * ln_g
        dr = rstd * (dx - jnp.mean(dx, axis=-1, keepdims=True) - xhat * jnp.mean(dx * xhat, axis=-1, keepdims=True))
        dh_ref[...] = alpha * dr
        dy_ref[...] = gate * dr
        add_to(base, tile_sum(dr * y))

    blk = pl.BlockSpec((LN_ROWS, d), lambda i: (i, 0))
    operands = [h, y, mods, d_hn] + ([d_hm] if with_mod else [])
    return pl.pallas_call(
        body, name=name,
        grid=(t // LN_ROWS,),
        in_specs=[blk, blk, pl.BlockSpec((MOD_ROWS, d), lambda i: (0, 0)), blk] + ([blk] if with_mod else []),
        out_specs=[blk, blk, pl.BlockSpec((MOD_ROWS * SUBLANES, d), lambda i: (0, 0))],
        out_shape=[jax.ShapeDtypeStruct((t, d), F32), jax.ShapeDtypeStruct((t, d), F32),
                   jax.ShapeDtypeStruct((MOD_ROWS * SUBLANES, d), F32)],
        compiler_params=pltpu.CompilerParams(dimension_semantics=('arbitrary',)),
    )(*operands)


@functools.partial(jax.custom_vjp, nondiff_argnums=(3, 4, 5, 6))
def _res_ln_mod(h, y, mods, t_ctx, alpha, with_mod, name):
    return tuple(_res_ln_fwd_call(h, y, mods, t_ctx, alpha, with_mod, name + '_fwd'))


def _res_ln_mod_fwd(h, y, mods, t_ctx, alpha, with_mod, name):
    return tuple(_res_ln_fwd_call(h, y, mods, t_ctx, alpha, with_mod, name + '_fwd')), (h, y, mods)


def _res_ln_mod_bwd(t_ctx, alpha, with_mod, name, res, cts):
    h, y, mods = res
    d_h, d_y, acc = _res_ln_bwd_call(h, y, mods, cts[0], cts[1] if with_mod else None, t_ctx, alpha, name + '_bwd')
    return d_h, d_y, jnp.sum(acc.reshape(MOD_ROWS, SUBLANES, h.shape[1]), axis=1)


_res_ln_mod.defvjp(_res_ln_mod_fwd, _res_ln_mod_bwd)


def _window(ref, j, rows, cols, axis):
    if axis == 0:
        return ref.at[pl.ds(j * rows, rows), :]
    return ref.at[:, pl.ds(j * cols, cols)]


def _all_gather(block, axis, name):
    rows, cols = block.shape

    def body(x_ref, out_ref, send_sems, recv_sems, local_sem):
        x, y, c = _my_coords()
        me, sibling = (x, y, c), (x, y, 1 - c)
        chips = [(1 - x, y), (x, 1 - y), (1 - x, 1 - y)]

        def win(px, py, pc):
            return _window(out_ref, 4 * px + 2 * py + pc, rows, cols, axis)

        def copy(k, blk, to, src=None):
            return pltpu.make_async_remote_copy(
                src_ref=win(*blk) if src is None else src, dst_ref=win(*blk),
                send_sem=send_sems.at[k], recv_sem=recv_sems.at[k], device_id=to, device_id_type=MESH)

        mine = pltpu.make_async_copy(x_ref, win(*me), local_sem)
        mine.start()
        first = [copy(0, me, sibling, src=x_ref)]
        first += [copy(1 + j, me, (*chip, c), src=x_ref) for j, chip in enumerate(chips)]
        for cp in first:
            cp.start()
        passed = [copy(4 + j, (*chip, c), sibling) for j, chip in enumerate(chips)]
        for j, chip in enumerate(chips):
            copy(1 + j, (*chip, c), me).wait_recv()
            passed[j].start()
        copy(0, sibling, me).wait_recv()
        for j, chip in enumerate(chips):
            copy(4 + j, (*chip, 1 - c), me).wait_recv()
        for cp in first + passed:
            cp.wait_send()
        mine.wait()

    out_shape = (N_DEV * rows, cols) if axis == 0 else (rows, N_DEV * cols)
    return pl.pallas_call(
        body, name=name,
        out_shape=jax.ShapeDtypeStruct(out_shape, block.dtype),
        in_specs=[ANY], out_specs=ANY,
        scratch_shapes=[pltpu.SemaphoreType.DMA((7,)), pltpu.SemaphoreType.DMA((7,)), pltpu.SemaphoreType.DMA],
    )(block)


def _exchange_pair(full, rows, cols, axis, name):
    def body(g_ref, land_ref, send_sems, recv_sems):
        x, y, c = _my_coords()
        copies = []
        for k in range(N_CHIP):
            j = 2 * k + (1 - c)
            copies.append(pltpu.make_async_remote_copy(
                src_ref=_window(g_ref, j, rows, cols, axis), dst_ref=land_ref.at[k],
                send_sem=send_sems.at[k], recv_sem=recv_sems.at[k], device_id=(x, y, 1 - c), device_id_type=MESH))
        for cp in copies:
            cp.start()
        for cp in copies:
            cp.wait()

    return pl.pallas_call(
        body, name=name,
        out_shape=jax.ShapeDtypeStruct((N_CHIP, rows, cols), full.dtype),
        in_specs=[ANY], out_specs=ANY,
        scratch_shapes=[pltpu.SemaphoreType.DMA((N_CHIP,)), pltpu.SemaphoreType.DMA((N_CHIP,))],
    )(full)


def _elementwise_tiles(rows, cols):
    tc = _first_divisor(cols, (1024, 512, 384, 256, 128))
    tr = _first_divisor(rows, (512, 256, 128, 64, 32, 16, 8))
    return tr, tc


def _pair_add(full, land, rows, cols, axis, wire_dtype, name):
    tr, tc = _elementwise_tiles(rows, cols)
    c_arr = jnp.reshape(lax.axis_index('c'), (1,)).astype(jnp.int32)

    def full_map(k, i, j, c_ref):
        blk = 2 * k + c_ref[0]
        if axis == 0:
            return (blk * (rows // tr) + i, j)
        return (i, blk * (cols // tc) + j)

    def body(c_ref, g_ref, l_ref, o_ref):
        del c_ref
        o_ref[0] = (g_ref[...] + l_ref[0]).astype(o_ref.dtype)

    return pl.pallas_call(
        body, name=name,
        grid_spec=pltpu.PrefetchScalarGridSpec(
            num_scalar_prefetch=1,
            grid=(N_CHIP, rows // tr, cols // tc),
            in_specs=[pl.BlockSpec((tr, tc), full_map),
                      pl.BlockSpec((1, tr, tc), lambda k, i, j, c_ref: (k, i, j))],
            out_specs=pl.BlockSpec((1, tr, tc), lambda k, i, j, c_ref: (k, i, j)),
        ),
        out_shape=jax.ShapeDtypeStruct((N_CHIP, rows, cols), wire_dtype),
        compiler_params=pltpu.CompilerParams(dimension_semantics=('parallel', 'parallel', 'parallel')),
    )(c_arr, full, land)


def _exchange_chips(sums, name):
    _, rows, cols = sums.shape

    def body(s_ref, land_ref, send_sems, recv_sems, local_sem):
        x, y, c = _my_coords()
        my_chip = 2 * x + y
        mine = pltpu.make_async_copy(s_ref.at[my_chip], land_ref.at[my_chip], local_sem)
        mine.start()
        chips = [(1 - x, y), (x, 1 - y), (1 - x, 1 - y)]
        sends = []
        for t, (px, py) in enumerate(chips):
            sends.append(pltpu.make_async_remote_copy(
                src_ref=s_ref.at[2 * px + py], dst_ref=land_ref.at[my_chip],
                send_sem=send_sems.at[t], recv_sem=recv_sems.at[t], device_id=(px, py, c), device_id_type=MESH))
        for cp in sends:
            cp.start()
        for t, (px, py) in enumerate(chips):
            pltpu.make_async_remote_copy(
                src_ref=s_ref.at[my_chip], dst_ref=land_ref.at[2 * px + py],
                send_sem=send_sems.at[t], recv_sem=recv_sems.at[t], device_id=(px, py, c),
                device_id_type=MESH).wait_recv()
        for cp in sends:
            cp.wait_send()
        mine.wait()

    return pl.pallas_call(
        body, name=name,
        out_shape=jax.ShapeDtypeStruct((N_CHIP, rows, cols), sums.dtype),
        in_specs=[ANY], out_specs=ANY,
        scratch_shapes=[pltpu.SemaphoreType.DMA((3,)), pltpu.SemaphoreType.DMA((3,)), pltpu.SemaphoreType.DMA],
    )(sums)


def _reduce_scatter_slots(full, rows, cols, axis, wire_dtype, name):
    land = _exchange_pair(full, rows, cols, axis, name + '_pair')
    sums = _pair_add(full, land, rows, cols, axis, wire_dtype, name + '_add')
    return _exchange_chips(sums, name + '_chips')


def _sum_slots(slots, name):
    n_slots, rows, cols = slots.shape
    tr, tc = _elementwise_tiles(rows, cols)

    def body(s_ref, o_ref):
        g = s_ref[0]
        for s in range(1, n_slots):
            g = g + s_ref[s]
        o_ref[...] = g

    return pl.pallas_call(
        body, name=name,
        grid=(rows // tr, cols // tc),
        in_specs=[pl.BlockSpec((n_slots, tr, tc), lambda i, j: (0, i, j))],
        out_specs=pl.BlockSpec((tr, tc), lambda i, j: (i, j)),
        out_shape=jax.ShapeDtypeStruct((rows, cols), F32),
        compiler_params=pltpu.CompilerParams(dimension_semantics=('parallel', 'parallel')),
    )(slots)


def _sum_adamw(slots, w, m, v, name):
    n_slots, rows, cols = slots.shape
    tr, tc = _elementwise_tiles(rows, cols)
    c1 = 1.0 - ADAM_B1 ** ADAM_STEP
    c2 = 1.0 - ADAM_B2 ** ADAM_STEP

    def body(s_ref, w_ref, m_ref, v_ref, g_out, d_out, m_out, v_out):
        g = s_ref[0].astype(F32)
        for s in range(1, n_slots):
            g = g + s_ref[s].astype(F32)
        m_new = ADAM_B1 * m_ref[...] + (1.0 - ADAM_B1) * g
        v_new = ADAM_B2 * v_ref[...] + (1.0 - ADAM_B2) * (g * g)
        m_hat = m_new / c1
        v_hat = v_new / c2
        g_out[...] = g
        d_out[...] = -ADAM_LR * (m_hat / (jnp.sqrt(v_hat) + ADAM_EPS) + ADAM_WD * w_ref[...])
        m_out[...] = m_new
        v_out[...] = v_new

    blk = pl.BlockSpec((tr, tc), lambda i, j: (i, j))
    shape = jax.ShapeDtypeStruct((rows, cols), F32)
    return pl.pallas_call(
        body, name=name,
        grid=(rows // tr, cols // tc),
        in_specs=[pl.BlockSpec((n_slots, tr, tc), lambda i, j: (0, i, j)), blk, blk, blk],
        out_specs=[blk, blk, blk, blk],
        out_shape=[shape, shape, shape, shape],
        compiler_params=pltpu.CompilerParams(dimension_semantics=('parallel', 'parallel')),
    )(slots, w, m, v)


def _rows_of(shape):
    size = int(np.prod(shape)) if len(shape) else 1
    return _round_up(_round_up(size, LANES) // LANES, SUBLANES)


def _pack(arrays, rows_multiple):
    parts = []
    for arr in arrays:
        flat = jnp.ravel(arr).astype(F32)
        rows = _rows_of(arr.shape)
        parts.append(jnp.pad(flat, (0, rows * LANES - flat.shape[0])).reshape(rows, LANES))
    total = sum(p.shape[0] for p in parts)
    pad = _round_up(total, rows_multiple) - total
    if pad:
        parts.append(jnp.zeros((pad, LANES), F32))
    return jnp.concatenate(parts, axis=0)


def _unpack(slab, shapes):
    out, r0 = [], 0
    for s in shapes:
        size = int(np.prod(s)) if len(s) else 1
        rows = _rows_of(s)
        out.append(slab[r0:r0 + rows].reshape(-1)[:size].reshape(s))
        r0 += rows
    return out


def _layer_norm(x, g, b):
    mu = jnp.mean(x, -1, keepdims=True)
    var = jnp.mean(jnp.square(x - mu), -1, keepdims=True)
    return (x - mu) * lax.rsqrt(var + LN_EPS) * g + b


def _modulate(x, shift, scale):
    return x * (1 + scale) + shift


def _take_cols(p, per_dev, per_dev_padded, lo, hi):
    parts = []
    while lo < hi:
        dev, off = divmod(lo, per_dev)
        n = min(hi - lo, per_dev - off)
        parts.append(p[:, dev * per_dev_padded + off: dev * per_dev_padded + off + n])
        lo += n
    return parts[0] if len(parts) == 1 else jnp.concatenate(parts, axis=1)


def _block_diag(blocks):
    g, a, b = blocks.shape
    eye = jnp.eye(g, dtype=blocks.dtype)
    return (blocks[:, :, None, :] * eye[:, None, :, None]).reshape(g * a, g * b)


def _s5_mixer(u, t_ctx, prm, name):
    t, gw = u.shape
    groups = gw // S5_CH
    ys = []
    for direction, reverse in enumerate((False, True)):
        lam = lax.complex(prm['s5_a_re'][direction], prm['s5_a_im'][direction])
        step = jnp.exp(prm['s5_log_step'][direction])[:, None]
        a_bar = jnp.exp(lam * step)
        b_bar = ((a_bar - 1.0) / lam)[..., None] * lax.complex(prm['s5_b_re'][direction],
                                                               prm['s5_b_im'][direction])
        c_mat = lax.complex(prm['s5_c_re'][direction], prm['s5_c_im'][direction])
        b_t = jnp.swapaxes(b_bar, 1, 2)
        b_mat = _interleave(_block_diag(jnp.real(b_t)), _block_diag(jnp.imag(b_t)))
        c_t = jnp.swapaxes(c_mat, 1, 2)
        c_blk = _interleave(_block_diag(jnp.real(c_t)).T, -_block_diag(jnp.imag(c_t)).T).T
        a_rows = jnp.stack([jnp.real(a_bar).reshape(-1), jnp.imag(a_bar).reshape(-1)])
        bu = _linear(u, b_mat, f'{name}_bu{direction}')
        h = _s5_scan(a_rows, bu, t_ctx, reverse, f'{name}_scan{direction}')
        ys.append(_linear(h, c_blk, f'{name}_y{direction}'))
    y = ys[0] + ys[1] + prm['s5_d'][None, :] * u
    z = jax.nn.gelu(y)
    return z * jax.nn.sigmoid(_linear(z, prm['s5_glu_w'], f'{name}_glu') + prm['s5_glu_b'])


def _chunk_gating(u, v, prm):
    t, gw = u.shape
    hd = gw // SG_HEADS
    u = jax.nn.gelu(u)
    v = jax.nn.gelu(v).reshape(t // SG_CHUNK, SG_CHUNK, SG_HEADS, hd)
    v = _layer_norm(v, prm['sg_ln_g'].reshape(SG_HEADS, hd), prm['sg_ln_b'].reshape(SG_HEADS, hd))
    s = jnp.einsum('hij,cjhd->cihd', prm['sg_w'], v) + prm['sg_b'].T[None, :, :, None]
    return u * s.reshape(t, gw)


def _pool_mixer(p, prm):
    l, gw = p.shape
    pd = gw // len(POOL_WINDOWS)
    t = np.arange(l)
    outs = []
    for g, win in enumerate(POOL_WINDOWS):
        lo = np.clip(t - win // 2, 0, l - 1)
        hi = np.clip(t + win // 2 - 1, 0, l - 1)
        pg = p[:, g * pd:(g + 1) * pd]
        padded = jnp.pad(pg, ((win // 2, win // 2), (0, 0)))
        total = padded[0:l]
        for d in range(1, win):
            total = total + padded[d:d + l]
        mean = total / jnp.asarray((hi - lo + 1).astype(np.float32))[:, None]
        outs.append(jnp.einsum('lc,cd->ld', mean - pg, prm['pool_w'][g]))
    y = jnp.concatenate(outs, axis=-1) + prm['pool_b']
    return y * prm['pool_scale']


def _dw_conv1d(x, w, b, pad):
    l = x.shape[0]
    xp = jnp.pad(x, (pad, (0, 0)))
    y = xp[0:l] * w[0]
    for k in range(1, w.shape[0]):
        y = y + xp[k:k + l] * w[k]
    return y + b


def _m2_prepare(xbc, dt_raw, prm, gw):
    heads = gw // M2_HEAD_DIM
    xbc = jax.nn.silu(_dw_conv1d(xbc, prm['m2_conv_w'], prm['m2_conv_b'], M2_PAD))
    l = xbc.shape[0]
    n_bc = M2_GROUPS * M2_STATE
    rep = heads // M2_GROUPS
    xs = xbc[:, :gw].reshape(l, heads, M2_HEAD_DIM)
    bm = jnp.repeat(xbc[:, gw:gw + n_bc].reshape(l, M2_GROUPS, M2_STATE), rep, axis=1)
    cm = jnp.repeat(xbc[:, gw + n_bc:].reshape(l, M2_GROUPS, M2_STATE), rep, axis=1)
    dt = jax.nn.softplus(dt_raw.reshape(l, 2, heads) + prm['m2_dt_bias'])
    return xs, bm, cm, dt


def _ssd_scan(xs, dt, a, bm, cm, h0, need_y):
    l, nh, hp = xs.shape
    nc = l // M2_CHUNK

    def chunks(t):
        return t.reshape((nc, M2_CHUNK) + t.shape[1:])

    xd = chunks(xs * dt[..., None])
    bc, cc = chunks(bm), chunks(cm)
    a_cum = jnp.cumsum(chunks(dt * a), axis=1)
    a_tot = a_cum[:, -1]
    decay_end = jnp.exp(a_tot[:, None] - a_cum)
    chunk_states = jnp.einsum('cqhn,cqh,cqhp->chpn', bc, decay_end, xd)

    def step(s, inp):
        tot, st = inp
        return jnp.exp(tot)[..., None, None] * s + st, s

    h_final, h_prev = lax.scan(step, h0, (a_tot, chunk_states))
    if not need_y:
        return None, h_final
    seg = a_cum[:, :, None, :] - a_cum[:, None, :, :]
    lower = jnp.tril(jnp.ones((M2_CHUNK, M2_CHUNK), bool))[None, :, :, None]
    decay = jnp.exp(jnp.where(lower, seg, -jnp.inf))
    scores = jnp.einsum('cihn,cjhn->cijh', cc, bc) * decay
    y = (jnp.einsum('cijh,cjhp->cihp', scores, xd)
         + jnp.einsum('cihn,chpn->cihp', cc, h_prev) * jnp.exp(a_cum)[..., None])
    return y.reshape(l, nh, hp), h_final


def _ssd_direction(inputs, direction, a, h0, need_y):
    xs, bm, cm, dt = inputs
    dt = dt[:, direction]
    if direction == 1:
        xs, bm, cm, dt = (jnp.flip(t, 0) for t in (xs, bm, cm, dt))
    y, h_final = _ssd_scan(xs, dt, a, bm, cm, h0, need_y)
    if direction == 1 and y is not None:
        y = jnp.flip(y, 0)
    return y, h_final


def _gated_rmsnorm(y, z, w):
    l, gw = z.shape
    g = (y * jax.nn.silu(z)).reshape(l, M2_GROUPS, gw // M2_GROUPS)
    g = g * lax.rsqrt(jnp.mean(jnp.square(g), -1, keepdims=True) + RMS_EPS)
    return g.reshape(l, gw) * w


def _mamba2_mixer(z, xbc, dt_raw, t_ctx, prm, need_ctx):
    gw = z.shape[1]
    heads = gw // M2_HEAD_DIM
    ctx_in = _m2_prepare(xbc[:t_ctx], dt_raw[:t_ctx], prm, gw)
    lat_in = _m2_prepare(xbc[t_ctx:], dt_raw[t_ctx:], prm, gw)
    a = -jnp.exp(prm['m2_a_log'])
    ys_ctx, ys_lat = [], []
    for direction in range(2):
        h0 = jnp.zeros((heads, M2_HEAD_DIM, M2_STATE), F32)
        y_c, h_c = _ssd_direction(ctx_in, direction, a[direction], h0, need_ctx)
        y_l, _ = _ssd_direction(lat_in, direction, a[direction], h_c, True)
        ys_lat.append(y_l)
        if need_ctx:
            ys_ctx.append(y_c)
    d_h = prm['m2_d'][None, :, None]

    def finish(ys, xs, zz):
        y = (ys[0] + ys[1] + d_h * xs).reshape(zz.shape[0], gw)
        return _gated_rmsnorm(y, zz, prm['m2_norm_w'])

    y_lat = finish(ys_lat, lat_in[0], z[t_ctx:])
    if need_ctx:
        return jnp.concatenate([finish(ys_ctx, ctx_in[0], z[:t_ctx]), y_lat], axis=0)
    return y_lat


def _forward_loss(dp, consts, dims):
    depth, t_ctx, d_model = dims['depth'], dims['t_ctx'], dims['d_model']
    gw = d_model // 4
    alpha = (2 * depth) ** 0.25
    in_sizes = (gw, gw, gw, gw, gw, gw + 2 * M2_GROUPS * M2_STATE, 2 * (gw // M2_HEAD_DIM))
    in_offs = np.concatenate([[0], np.cumsum(in_sizes)])
    ml = [[dp['mod_lat'][i, k][None, :] for k in range(6)] for i in range(depth)]
    mc = [[dp['mod_ctx'][i, k][None, :] for k in range(6)] for i in range(depth)]
    zero_row = jnp.zeros((1, d_model), F32)
    h = jnp.concatenate([consts['ctx'], dp['x']], axis=0)
    hm = jnp.concatenate([_modulate(consts['ctx'], mc[0][0], mc[0][1]), _modulate(dp['x'], ml[0][0], ml[0][1])],
                         axis=0)
    for i in range(depth):
        need_ctx = i < depth - 1
        prm = {k: v[i] for k, v in dp['small'].items()}
        p = _linear_tap(hm, consts['w_in'][i], dp['taps']['w_in'][i], f'in{i}')
        seg = [_take_cols(p, dims['in_per_dev'], dims['in_per_dev_pad'], int(in_offs[s]), int(in_offs[s + 1]))
               for s in range(7)]
        ya = _s5_mixer(seg[0], t_ctx, prm, f's5_{i}')
        yb = _chunk_gating(seg[1], seg[2], prm)
        yc = jnp.concatenate([_pool_mixer(seg[3][:t_ctx], prm), _pool_mixer(seg[3][t_ctx:], prm)], axis=0)
        yd = _mamba2_mixer(seg[4], seg[5], seg[6], t_ctx, prm, need_ctx)
        if need_ctx:
            mix_in = jnp.concatenate([ya, yb, yc, yd], axis=1)
        else:
            mix_in = jnp.concatenate([ya[t_ctx:], yb[t_ctx:], yc[t_ctx:], yd], axis=1)
        mix = _linear_tap(mix_in, consts['w_out'][i], dp['taps']['w_out'][i], f'out{i}')
        t_c = t_ctx if need_ctx else 0
        if not need_ctx and h.shape[0] != mix.shape[0]:
            h = h[t_ctx:]
        mods = jnp.concatenate([mc[i][2], mc[i][3], mc[i][4], ml[i][2], ml[i][3], ml[i][4],
                                prm['ln1_g'][None], prm['ln1_b'][None]], axis=0)
        h, hm = _res_ln_mod(h, mix, mods, t_c, alpha, True, f'ln1_{i}')

        up = _linear_tap(hm, consts['ffn_w_up'][i], dp['taps']['ffn_w_up'][i], f'up{i}')
        f_hidden = dims['ffn_hidden']
        if dims['up_per_dev'] == dims['up_per_dev_pad']:
            gv = up
        else:
            gv = jnp.concatenate([_take_cols(up, dims['up_per_dev'], dims['up_per_dev_pad'], 0, f_hidden),
                                  _take_cols(up, dims['up_per_dev'], dims['up_per_dev_pad'], f_hidden, 2 * f_hidden)],
                                 axis=1)
        f_out = _ffn_tail(gv, prm['ffn_conv_w'], prm['ffn_conv_b'], consts['ffn_w_down'][i],
                          dp['taps']['ffn_w_down'][i], t_c, f'ffn{i}')
        if need_ctx:
            mods = jnp.concatenate([mc[i][5], mc[i + 1][0], mc[i + 1][1], ml[i][5], ml[i + 1][0], ml[i + 1][1],
                                    prm['ln2_g'][None], prm['ln2_b'][None]], axis=0)
            h, hm = _res_ln_mod(h, f_out, mods, t_c, alpha, True, f'ln2_{i}')
        else:
            mods = jnp.concatenate([mc[i][5], zero_row, zero_row, ml[i][5], zero_row, zero_row,
                                    prm['ln2_g'][None], prm['ln2_b'][None]], axis=0)
            h = _res_ln_mod(h, f_out, mods, t_c, alpha, False, f'ln2_{i}')[0]
    err = jnp.square(h - consts['target'])
    return 0.5 * jnp.sum(jnp.mean(err, axis=-1))


def _silu_grad(x):
    s = jax.nn.sigmoid(x)
    return s * (1 + x * (1 - s))


def _step(w, m, v, x, c, ctx, target):
    depth, d_model, ada_cols = w['w_ada'].shape
    t_ctx, t_lat = ctx.shape[1], x.shape[1]
    me = _my_index()
    in_per_dev = w['w_in'].shape[2]
    in_pad = _round_up(in_per_dev, LANES)
    up_per_dev = w['ffn_w_up'].shape[2]
    up_pad = _round_up(up_per_dev, LANES)
    f_hidden = w['ffn_w_down'].shape[1] * N_DEV
    dims = dict(depth=depth, t_ctx=t_ctx, d_model=d_model, in_per_dev=in_per_dev, in_per_dev_pad=in_pad,
                up_per_dev=up_per_dev, up_per_dev_pad=up_pad, ffn_hidden=f_hidden)

    rows16 = 2 * SUBLANES
    silu_c_all = _all_gather(jnp.pad(jax.nn.silu(c), ((0, SUBLANES - 1), (0, 0))), 0, 'ag_c')
    silu_c_all = silu_c_all.reshape(N_DEV, SUBLANES, d_model)[:, 0]
    silu_cc = jax.nn.silu(w['c_ctx'])
    ada_in = jnp.concatenate([silu_c_all, silu_cc[None], jnp.zeros((rows16 - N_DEV - 1, d_model), F32)], axis=0)
    mod_loc = jnp.concatenate([_matmul(ada_in, w['w_ada'][i], 'nn', 'ada_fwd') for i in range(depth)], axis=0)
    mod_all = _all_gather(mod_loc, 1, 'ag_mod').reshape(depth, rows16, 6 * d_model)
    mod_all = mod_all + w['b_ada'][:, None, :]
    mod_lat = lax.dynamic_index_in_dim(mod_all, me, axis=1, keepdims=False).reshape(depth, 6, d_model)
    mod_ctx = mod_all[:, N_DEV].reshape(depth, 6, d_model)

    def pad_cols(a, to):
        return jnp.pad(a, ((0, 0), (0, to - a.shape[1])))

    gathered = {
        'w_in': [_all_gather(pad_cols(w['w_in'][i], in_pad).astype(BF16), 1, 'ag_w_in') for i in range(depth)],
        'ffn_w_up': [_all_gather(pad_cols(w['ffn_w_up'][i], up_pad).astype(BF16), 1, 'ag_w_up')
                     for i in range(depth)],
        'w_out': [_all_gather(w['w_out'][i].astype(BF16), 0, 'ag_w_out') for i in range(depth)],
        'ffn_w_down': [_all_gather(w['ffn_w_down'][i].astype(BF16), 0, 'ag_w_down') for i in range(depth)],
    }
    sharded_small = [w[n] for n in COL_SHARDED_SMALL + ROW_SHARDED_SMALL]
    small_slab = _pack(sharded_small, PACK_ROWS)
    slab_rows = small_slab.shape[0]
    small_all = _all_gather(small_slab, 0, 'ag_small').reshape(N_DEV, slab_rows, LANES)
    per_dev = [_unpack(small_all[d], [a.shape for a in sharded_small]) for d in range(N_DEV)]
    small_full = {}
    for k, n in enumerate(COL_SHARDED_SMALL):
        small_full[n] = jnp.concatenate([per_dev[d][k] for d in range(N_DEV)], axis=-1)
    for k, n in enumerate(ROW_SHARDED_SMALL):
        small_full[n] = jnp.concatenate([per_dev[d][len(COL_SHARDED_SMALL) + k] for d in range(N_DEV)], axis=1)

    small = {n: w[n] for n in REPLICATED}
    small.update(small_full)
    taps = {n: [jnp.zeros(gathered[n][i].shape, F32) for i in range(depth)] for n in BIG}
    dp = dict(x=x[0], small=small, mod_lat=mod_lat, mod_ctx=mod_ctx, taps=taps)
    consts = dict(ctx=ctx[0], target=target[0], **gathered)
    loss_local, grads = jax.value_and_grad(functools.partial(_forward_loss, consts=consts, dims=dims))(dp)

    out_g, out_d, out_m, out_v = {}, {}, {}, {}

    def finish_big(name, i, rows, cols, axis, keep_cols):
        slots = _reduce_scatter_slots(grads['taps'][name][i], rows, cols, axis, BF16, f'rs_{name}')
        shard = [pad_cols(t[name][i], cols) for t in (w, m, v)]
        res = _sum_adamw(slots, *shard, f'adamw_{name}')
        return [r[:, :keep_cols] for r in res]

    big_specs = {
        'w_in': (d_model, in_pad, 1, in_per_dev),
        'ffn_w_up': (d_model, up_pad, 1, up_per_dev),
        'w_out': (w['w_out'].shape[1], d_model, 0, d_model),
        'ffn_w_down': (w['ffn_w_down'].shape[1], d_model, 0, d_model),
    }
    for name, (rows, cols, axis, keep) in big_specs.items():
        per_layer = [finish_big(name, i, rows, cols, axis, keep) for i in range(depth)]
        for k, dst in enumerate((out_g, out_d, out_m, out_v)):
            dst[name] = jnp.stack([per_layer[i][k] for i in range(depth)])

    d_lat = grads['mod_lat'].reshape(depth, 6 * d_model)
    d_ctx = grads['mod_ctx'].reshape(depth, 6 * d_model)
    d_rows = jnp.concatenate([d_lat, d_ctx, jnp.zeros((SUBLANES - 2 * depth, 6 * d_model), F32)], axis=0)
    d_all = _all_gather(d_rows, 0, 'ag_dmod').reshape(N_DEV, SUBLANES, 6 * d_model)
    d_lat_all = d_all[:, :depth]
    d_ctx_sum = d_all[0, depth:2 * depth]
    for d in range(1, N_DEV):
        d_ctx_sum = d_ctx_sum + d_all[d, depth:2 * depth]
    g_b_ada = d_ctx_sum
    for d in range(N_DEV):
        g_b_ada = g_b_ada + d_lat_all[d]
    g_w_ada, c_ctx_part = [], jnp.zeros((d_model,), F32)
    for i in range(depth):
        d_mat = jnp.concatenate([d_lat_all[:, i], d_ctx_sum[i][None],
                                 jnp.zeros((rows16 - N_DEV - 1, 6 * d_model), F32)], axis=0)
        d_mine = lax.dynamic_slice_in_dim(d_mat, me * ada_cols, ada_cols, axis=1)
        g_w_ada.append(_matmul(ada_in, d_mine, 'tn', 'ada_dw'))
        back = _matmul(d_mine, w['w_ada'][i], 'nt', 'ada_dx')
        c_ctx_part = c_ctx_part + back[N_DEV]
    c_ctx_part = c_ctx_part * _silu_grad(w['c_ctx'])
    g_w_ada = jnp.stack(g_w_ada).reshape(1, depth * d_model, ada_cols)
    res = _sum_adamw(g_w_ada, *[t['w_ada'].reshape(depth * d_model, ada_cols) for t in (w, m, v)], 'adamw_w_ada')
    for k, dst in enumerate((out_g, out_d, out_m, out_v)):
        dst['w_ada'] = res[k].reshape(depth, d_model, ada_cols)

    reduced_names = REPLICATED[:]
    reduced_names.remove('b_ada')
    reduced_names += list(COL_SHARDED_SMALL + ROW_SHARDED_SMALL)
    to_reduce = [grads['small'][n] for n in reduced_names] + [c_ctx_part, loss_local]
    slab = _pack(to_reduce, N_DEV * PACK_ROWS)
    chunk_rows = slab.shape[0] // N_DEV
    slots = _reduce_scatter_slots(slab, chunk_rows, LANES, 0, F32, 'rs_small')
    mine = _sum_slots(slots, 'sum_small')
    summed = _all_gather(mine, 0, 'ag_small_sum')
    parts = _unpack(summed, [a.shape for a in to_reduce])
    g_small = dict(zip(reduced_names, parts[:len(reduced_names)]))
    g_small['c_ctx'] = parts[-2]
    g_small['b_ada'] = g_b_ada
    loss = parts[-1]

    def my_shard(name, full):
        if name in COL_SHARDED_SMALL:
            n = full.shape[-1] // N_DEV
            return lax.dynamic_slice_in_dim(full, me * n, n, axis=full.ndim - 1)
        if name in ROW_SHARDED_SMALL:
            n = full.shape[1] // N_DEV
            return lax.dynamic_slice_in_dim(full, me * n, n, axis=1)
        return full

    small_names = [n for n in WEIGHTS if n not in BIG + ('w_ada',)]
    g_list = [my_shard(n, g_small[n]) for n in small_names]
    g_slab = _pack(g_list, PACK_ROWS)
    res = _sum_adamw(g_slab[None], *[_pack([t[n] for n in small_names], PACK_ROWS) for t in (w, m, v)],
                     'adamw_small')
    shapes = [w[n].shape for n in small_names]
    for k, dst in enumerate((out_g, out_d, out_m, out_v)):
        if k == 0:
            dst.update(dict(zip(small_names, g_list)))
        else:
            dst.update(dict(zip(small_names, _unpack(res[k], shapes))))

    grad_x = grads['x'][None]
    return (loss, grad_x, *[out_g[n] for n in WEIGHTS], *[out_d[n] for n in WEIGHTS],
            *[out_m[n] for n in WEIGHTS], *[out_v[n] for n in WEIGHTS])


def kernel(x, c, ctx, c_ctx, w_ada, b_ada, w_in, w_out, ln1_g, ln1_b, ln2_g, ln2_b, s5_a_re, s5_a_im, s5_b_re, s5_b_im, s5_c_re, s5_c_im, s5_log_step, s5_d, s5_glu_w, s5_glu_b, sg_ln_g, sg_ln_b, sg_w, sg_b, pool_w, pool_b, pool_scale, m2_conv_w, m2_conv_b, m2_dt_bias, m2_a_log, m2_d, m2_norm_w, ffn_w_up, ffn_conv_w, ffn_conv_b, ffn_w_down, loss_target, m_c_ctx, m_w_ada, m_b_ada, m_w_in, m_w_out, m_ln1_g, m_ln1_b, m_ln2_g, m_ln2_b, m_s5_a_re, m_s5_a_im, m_s5_b_re, m_s5_b_im, m_s5_c_re, m_s5_c_im, m_s5_log_step, m_s5_d, m_s5_glu_w, m_s5_glu_b, m_sg_ln_g, m_sg_ln_b, m_sg_w, m_sg_b, m_pool_w, m_pool_b, m_pool_scale, m_m2_conv_w, m_m2_conv_b, m_m2_dt_bias, m_m2_a_log, m_m2_d, m_m2_norm_w, m_ffn_w_up, m_ffn_conv_w, m_ffn_conv_b, m_ffn_w_down, v_c_ctx, v_w_ada, v_b_ada, v_w_in, v_w_out, v_ln1_g, v_ln1_b, v_ln2_g, v_ln2_b, v_s5_a_re, v_s5_a_im, v_s5_b_re, v_s5_b_im, v_s5_c_re, v_s5_c_im, v_s5_log_step, v_s5_d, v_s5_glu_w, v_s5_glu_b, v_sg_ln_g, v_sg_ln_b, v_sg_w, v_sg_b, v_pool_w, v_pool_b, v_pool_scale, v_m2_conv_w, v_m2_conv_b, v_m2_dt_bias, v_m2_a_log, v_m2_d, v_m2_norm_w, v_ffn_w_up, v_ffn_conv_w, v_ffn_conv_b, v_ffn_w_down):
    given = dict(locals())
    w = {n: given[n] for n in WEIGHTS}
    m = {n: given['m_' + n] for n in WEIGHTS}
    v = {n: given['v_' + n] for n in WEIGHTS}
    return _step(w, m, v, x, c, ctx, loss_target)
```

```python
import functools

import jax
import jax.numpy as jnp
import numpy as np
from jax import lax
from jax.experimental import pallas as pl
from jax.experimental.pallas import tpu as pltpu

F32 = jnp.float32
BF16 = jnp.bfloat16
MESH = pl.DeviceIdType.MESH
ANY = pl.BlockSpec(memory_space=pl.ANY)

N_DEV = 8
N_CHIP = 4
LANES = 128
SUBLANES = 8
VMEM_LIMIT_BYTES = 48 * 1024 * 1024
MATMUL_VMEM_BUDGET = 36 * 1024 * 1024
PACK_ROWS = 512

GRID_W = 64
S5_CH = 16
S5_STATE = 64
SG_HEADS = 4
SG_CHUNK = 128
POOL_WINDOWS = (2, 4, 8, 16)
M2_HEAD_DIM = 64
M2_STATE = 128
M2_GROUPS = 2
M2_CONV = 4
M2_PAD = (M2_CONV // 2, M2_CONV - 1 - M2_CONV // 2)
M2_CHUNK = 128
FFN_CONV = 3
LN_EPS = 1e-5
RMS_EPS = 1e-5
ADAM_LR = 0.001
ADAM_B1 = 0.9
ADAM_B2 = 0.999
ADAM_EPS = 1e-08
ADAM_WD = 0.01
ADAM_STEP = 10

WEIGHTS = ['c_ctx', 'w_ada', 'b_ada', 'w_in', 'w_out', 'ln1_g', 'ln1_b', 'ln2_g', 'ln2_b', 's5_a_re', 's5_a_im',
           's5_b_re', 's5_b_im', 's5_c_re', 's5_c_im', 's5_log_step', 's5_d', 's5_glu_w', 's5_glu_b', 'sg_ln_g',
           'sg_ln_b', 'sg_w', 'sg_b', 'pool_w', 'pool_b', 'pool_scale', 'm2_conv_w', 'm2_conv_b', 'm2_dt_bias',
           'm2_a_log', 'm2_d', 'm2_norm_w', 'ffn_w_up', 'ffn_conv_w', 'ffn_conv_b', 'ffn_w_down']
BIG = ('w_in', 'w_out', 'ffn_w_up', 'ffn_w_down')
COL_SHARDED_SMALL = ('m2_conv_w', 'ffn_conv_w')
ROW_SHARDED_SMALL = ('s5_glu_w',)
REPLICATED = [n for n in WEIGHTS if n not in BIG + COL_SHARDED_SMALL + ROW_SHARDED_SMALL + ('w_ada', 'c_ctx')]


def _round_up(n, m):
    return (n + m - 1) // m * m


def _my_coords():
    return lax.axis_index('x'), lax.axis_index('y'), lax.axis_index('c')


def _my_index():
    x, y, c = _my_coords()
    return 4 * x + 2 * y + c


def _first_divisor(n, cands):
    for c in cands:
        if n % c == 0:
            return c
    return n


def _matmul_tiles(mode, m, n, k, a_bytes, b_bytes):
    lane_c = (1024, 512, 384, 256, 128)
    sub_c = (1088, 1024, 544, 512, 256, 128, 64, 32, 16)
    tm = _first_divisor(m, lane_c if mode == 'tn' else sub_c)
    tn = _first_divisor(n, lane_c)
    k_c = [c for c in (5632, 4096, 2048, 1408, 1088, 1024, 544, 512, 256, 128) if k % c == 0] or [k]
    if mode == 'tn':
        k_c = [c for c in k_c if c <= 1088] or [k_c[-1]]
    for tk in k_c:
        use = 2 * (tm * tk * a_bytes + tk * tn * b_bytes) + 3 * tm * tn * 4
        if use <= MATMUL_VMEM_BUDGET:
            return tm, tn, tk
    return tm, tn, k_c[-1]


def _matmul(a, b, mode, name, out_dtype=F32):
    if mode == 'nn':
        (m, k), (k2, n) = a.shape, b.shape
    elif mode == 'nt':
        (m, k), (n, k2) = a.shape, b.shape
    else:
        (k, m), (k2, n) = a.shape, b.shape
    assert k == k2, (mode, a.shape, b.shape)
    tm, tn, tk = _matmul_tiles(mode, m, n, k, a.dtype.itemsize, b.dtype.itemsize)
    nk = k // tk
    if mode == 'nn':
        a_spec = pl.BlockSpec((tm, tk), lambda i, j, kk: (i, kk))
        b_spec = pl.BlockSpec((tk, tn), lambda i, j, kk: (kk, j))
        dims = (((1,), (0,)), ((), ()))
    elif mode == 'nt':
        a_spec = pl.BlockSpec((tm, tk), lambda i, j, kk: (i, kk))
        b_spec = pl.BlockSpec((tn, tk), lambda i, j, kk: (j, kk))
        dims = (((1,), (1,)), ((), ()))
    else:
        a_spec = pl.BlockSpec((tk, tm), lambda i, j, kk: (kk, i))
        b_spec = pl.BlockSpec((tk, tn), lambda i, j, kk: (kk, j))
        dims = (((0,), (0,)), ((), ()))

    def body(a_ref, b_ref, o_ref, acc_ref):
        kk = pl.program_id(2)

        @pl.when(kk == 0)
        def _():
            acc_ref[...] = jnp.zeros_like(acc_ref)

        acc_ref[...] += lax.dot_general(a_ref[...].astype(BF16), b_ref[...].astype(BF16), dims,
                                        preferred_element_type=F32)

        @pl.when(kk == nk - 1)
        def _():
            o_ref[...] = acc_ref[...].astype(o_ref.dtype)

    return pl.pallas_call(
        body,
        name=name,
        grid=(m // tm, n // tn, nk),
        in_specs=[a_spec, b_spec],
        out_specs=pl.BlockSpec((tm, tn), lambda i, j, kk: (i, j)),
        out_shape=jax.ShapeDtypeStruct((m, n), out_dtype),
        scratch_shapes=[pltpu.VMEM((tm, tn), F32)],
        compiler_params=pltpu.CompilerParams(dimension_semantics=('parallel', 'parallel', 'arbitrary'),
                                             vmem_limit_bytes=VMEM_LIMIT_BYTES),
    )(a, b)


@functools.partial(jax.custom_vjp, nondiff_argnums=(2,))
def _linear(x, w, name):
    return _matmul(x, w, 'nn', name + '_fwd')


def _linear_fwd(x, w, name):
    return _matmul(x, w, 'nn', name + '_fwd'), (x, w)


def _linear_bwd(name, res, dy):
    x, w = res
    return _matmul(dy, w, 'nt', name + '_dx'), _matmul(x, dy, 'tn', name + '_dw')


_linear.defvjp(_linear_fwd, _linear_bwd)


@functools.partial(jax.custom_vjp, nondiff_argnums=(3,))
def _linear_tap(x, w, tap, name):
    del tap
    return _matmul(x, w, 'nn', name + '_fwd')


def _linear_tap_fwd(x, w, tap, name):
    del tap
    return _matmul(x, w, 'nn', name + '_fwd'), (x, w)


def _linear_tap_bwd(name, res, dy):
    x, w = res
    return _matmul(dy, w, 'nt', name + '_dx'), jnp.zeros_like(w), _matmul(x, dy, 'tn', name + '_dw')


_linear_tap.defvjp(_linear_tap_fwd, _linear_tap_bwd)


SCAN_ROWS = 128
SCAN_LANES = 512


def _scan_lanes(n):
    return SCAN_LANES if n % SCAN_LANES == 0 else n


def _interleave(re, im):
    r, n = re.shape
    lanes = _scan_lanes(n)
    return jnp.stack([re.reshape(r, n // lanes, lanes), im.reshape(r, n // lanes, lanes)], axis=2).reshape(r, 2 * n)


def _scan_call(a, x, h, n_ctx_blocks, ctx_first, reverse_rows, name):
    t, n2 = x.shape
    n = n2 // 2
    nblk = t // SCAN_ROWS
    nlat = nblk - n_ctx_blocks
    lanes = _scan_lanes(n)
    with_da = h is not None

    def block_of(i):
        if ctx_first:
            first_n, first_0, second_0, second_n = n_ctx_blocks, 0, n_ctx_blocks, nlat
        else:
            first_n, first_0, second_0, second_n = nlat, n_ctx_blocks, 0, n_ctx_blocks
        if reverse_rows:
            in_first = first_0 + first_n - 1 - i
            in_second = second_0 + second_n - 1 - (i - first_n)
        else:
            in_first = first_0 + i
            in_second = second_0 + (i - first_n)
        return jnp.where(i < first_n, in_first, in_second)

    groups = SCAN_ROWS // SUBLANES
    first_row = SUBLANES - 1 if reverse_rows else 0
    to_previous = SUBLANES - 1 if reverse_rows else 1

    def body(*refs):
        if with_da:
            a_ref, x_ref, h_ref, o_ref, da_ref, st_ref = refs
        else:
            a_ref, x_ref, o_ref, st_ref = refs

        @pl.when(pl.program_id(1) == 0)
        def _():
            st_ref[...] = jnp.zeros_like(st_ref)
            if with_da:
                da_ref[...] = jnp.zeros_like(da_ref)

        row_id = lax.broadcasted_iota(jnp.int32, (SUBLANES, lanes), 0)
        behind = (SUBLANES - 1 - row_id) if reverse_rows else row_id

        def cmul(pr, pi, qr, qi):
            return pr * qr - pi * qi, pr * qi + pi * qr

        a1 = (jnp.broadcast_to(a_ref[0:1, :], (SUBLANES, lanes)), jnp.broadcast_to(a_ref[1:2, :], (SUBLANES, lanes)))
        a2 = cmul(*a1, *a1)
        a4 = cmul(*a2, *a2)
        pw = a1
        for bit, ak in ((1, a1), (2, a2), (4, a4)):
            nxt = cmul(*pw, *ak)
            pw = (jnp.where((behind & bit) != 0, nxt[0], pw[0]), jnp.where((behind & bit) != 0, nxt[1], pw[1]))

        def group(g, carry):
            in_r, in_i = carry
            gi = (groups - 1 - g) if reverse_rows else g
            start = pl.multiple_of(gi * SUBLANES, SUBLANES)
            xr = x_ref[pl.ds(start, SUBLANES), pl.ds(0, lanes)]
            xi = x_ref[pl.ds(start, SUBLANES), pl.ds(lanes, lanes)]
            for k, ak in ((1, a1), (2, a2), (4, a4)):
                shift = (SUBLANES - k) if reverse_rows else k
                pr = jnp.where(behind >= k, pltpu.roll(xr, shift, axis=0), 0.0)
                pi = jnp.where(behind >= k, pltpu.roll(xi, shift, axis=0), 0.0)
                qr, qi = cmul(*ak, pr, pi)
                xr, xi = xr + qr, xi + qi
            cr, ci = cmul(*pw, in_r, in_i)
            out_r, out_i = xr + cr, xi + ci
            last = 0 if reverse_rows else SUBLANES - 1
            sr = jnp.broadcast_to(out_r[last:last + 1, :], (SUBLANES, lanes))
            si = jnp.broadcast_to(out_i[last:last + 1, :], (SUBLANES, lanes))
            o_ref[pl.ds(start, SUBLANES), pl.ds(0, lanes)] = out_r
            o_ref[pl.ds(start, SUBLANES), pl.ds(lanes, lanes)] = out_i
            if with_da:
                pr = jnp.where(row_id == first_row, in_r, pltpu.roll(out_r, to_previous, axis=0))
                pi = jnp.where(row_id == first_row, in_i, pltpu.roll(out_i, to_previous, axis=0))
                hr = h_ref[pl.ds(start, SUBLANES), pl.ds(0, lanes)]
                hi = h_ref[pl.ds(start, SUBLANES), pl.ds(lanes, lanes)]
                da_ref[0:SUBLANES, :] += hr * pr + hi * pi
                da_ref[SUBLANES:2 * SUBLANES, :] += hr * pi - hi * pr
            return sr, si

        sr, si = lax.fori_loop(0, groups, group, (st_ref[0], st_ref[1]))
        st_ref[0] = sr
        st_ref[1] = si

    row_spec = pl.BlockSpec((SCAN_ROWS, 2 * lanes), lambda j, i: (block_of(i), j))
    in_specs = [pl.BlockSpec((2, lanes), lambda j, i: (0, j)), row_spec]
    out_specs = [row_spec]
    out_shape = [jax.ShapeDtypeStruct((t, n2), F32)]
    operands = [a, x]
    if with_da:
        in_specs.append(row_spec)
        operands.append(h)
        out_specs.append(pl.BlockSpec((2 * SUBLANES, lanes), lambda j, i: (0, j)))
        out_shape.append(jax.ShapeDtypeStruct((2 * SUBLANES, n), F32))
    return pl.pallas_call(
        body,
        name=name,
        grid=(n // lanes, nblk),
        in_specs=in_specs,
        out_specs=out_specs,
        out_shape=out_shape,
        scratch_shapes=[pltpu.VMEM((2, SUBLANES, lanes), F32)],
        compiler_params=pltpu.CompilerParams(dimension_semantics=('parallel', 'arbitrary')),
    )(*operands)


@functools.partial(jax.custom_vjp, nondiff_argnums=(2, 3, 4))
def _s5_scan(a, bu, t_ctx, reverse, name):
    return _scan_call(a, bu, None, t_ctx // SCAN_ROWS, True, reverse, name + '_fwd')[0]


def _s5_scan_fwd(a, bu, t_ctx, reverse, name):
    h = _scan_call(a, bu, None, t_ctx // SCAN_ROWS, True, reverse, name + '_fwd')[0]
    return h, (a, h)


def _s5_scan_bwd(t_ctx, reverse, name, res, dh):
    a, h = res
    a_conj = a * jnp.array([[1.0], [-1.0]], F32)
    g, da = _scan_call(a_conj, dh, h, t_ctx // SCAN_ROWS, False, not reverse, name + '_bwd')
    da = jnp.stack([jnp.sum(da[:SUBLANES], axis=0), jnp.sum(da[SUBLANES:], axis=0)])
    return da, g


_s5_scan.defvjp(_s5_scan_fwd, _s5_scan_bwd)


FFN_LANES = 128
CONV_TAPS = FFN_CONV * FFN_CONV
CONV_W_ROWS = 16
GELU_K = 0.7978845608028654
GELU_C = 0.044715


def _gelu_and_slope(x):
    x2 = x * x
    th = jnp.tanh(GELU_K * (x + GELU_C * x * x2))
    cdf = 0.5 * (1.0 + th)
    slope = cdf + 0.5 * x * (1.0 - th * th) * (GELU_K * (1.0 + 3.0 * GELU_C * x2))
    return x * cdf, slope


def _band_rows(t_lat):
    return _first_divisor(t_lat // GRID_W, (8, 4, 2, 1)) * GRID_W


def _shifted_ctx(x):
    n = x.shape[0]
    row = lax.broadcasted_iota(jnp.int32, x.shape, 0)
    left = jnp.where(row == 0, 0.0, pltpu.roll(x, 1, axis=0))
    right = jnp.where(row == n - 1, 0.0, pltpu.roll(x, n - 1, axis=0))
    return left, x, right


def _shifted_band(src_ref, bidx, n_bands, band, t_ctx, t):
    start = pl.multiple_of(t_ctx + bidx * band, GRID_W)
    top_start = pl.multiple_of(jnp.maximum(start - GRID_W, 0), SUBLANES)
    bot_start = pl.multiple_of(jnp.minimum(start + band, t - GRID_W), SUBLANES)
    top = jnp.where(bidx > 0, src_ref[pl.ds(top_start, GRID_W), :], 0.0)
    bot = jnp.where(bidx < n_bands - 1, src_ref[pl.ds(bot_start, GRID_W), :], 0.0)
    ext = jnp.concatenate([top, src_ref[pl.ds(start, band), :], bot], axis=0)
    n_ext = band + 2 * GRID_W
    col = lax.broadcasted_iota(jnp.int32, ext.shape, 0) & (GRID_W - 1)
    left = jnp.where(col == 0, 0.0, pltpu.roll(ext, 1, axis=0))
    right = jnp.where(col == GRID_W - 1, 0.0, pltpu.roll(ext, n_ext - 1, axis=0))
    return left, ext, right


def _conv_sum(shifted, w_rows, band, flip):
    acc = None
    for i in range(FFN_CONV):
        for j in range(FFN_CONV):
            w = w_rows[(FFN_CONV - 1 - i) * FFN_CONV + (FFN_CONV - 1 - j)] if flip else w_rows[i * FFN_CONV + j]
            term = shifted[j][i * GRID_W:i * GRID_W + band] * w
            acc = term if acc is None else acc + term
    return acc


def _conv_operand(conv_w, conv_b):
    f = conv_b.shape[0]
    return jnp.concatenate([conv_w.reshape(CONV_TAPS, f), conv_b[None],
                            jnp.zeros((CONV_W_ROWS - CONV_TAPS - 1, f), F32)], axis=0)


def _ffn_mid_fwd(gv, w16, t_ctx, name):
    t, f2 = gv.shape
    f = f2 // 2
    lanes = FFN_LANES
    nf = f // lanes
    band = _band_rows(t - t_ctx)
    n_bands = (t - t_ctx) // band
    mid = (FFN_CONV // 2) * FFN_CONV

    def body(g_ref, v_ref, w_ref, o_ref):
        w_rows = [w_ref[k:k + 1, :] for k in range(CONV_TAPS)]
        bias = w_ref[CONV_TAPS:CONV_TAPS + 1, :]
        if t_ctx:
            sh = _shifted_ctx(g_ref[0:t_ctx, :])
            pre = sh[0] * w_rows[mid] + sh[1] * w_rows[mid + 1] + sh[2] * w_rows[mid + 2] + bias
            o_ref[0:t_ctx, :] = (_gelu_and_slope(pre)[0] * v_ref[0:t_ctx, :]).astype(o_ref.dtype)

        def one_band(b, carry):
            start = pl.multiple_of(t_ctx + b * band, GRID_W)
            pre = _conv_sum(_shifted_band(g_ref, b, n_bands, band, t_ctx, t), w_rows, band, False) + bias
            o_ref[pl.ds(start, band), :] = (_gelu_and_slope(pre)[0] * v_ref[pl.ds(start, band), :]).astype(o_ref.dtype)
            return carry

        lax.fori_loop(0, n_bands, one_band, 0)

    return pl.pallas_call(
        body, name=name,
        grid=(nf,),
        in_specs=[pl.BlockSpec((t, lanes), lambda j: (0, j)),
                  pl.BlockSpec((t, lanes), lambda j: (0, nf + j)),
                  pl.BlockSpec((CONV_W_ROWS, lanes), lambda j: (0, j))],
        out_specs=pl.BlockSpec((t, lanes), lambda j: (0, j)),
        out_shape=jax.ShapeDtypeStruct((t, f), BF16),
        compiler_params=pltpu.CompilerParams(dimension_semantics=('parallel',), vmem_limit_bytes=VMEM_LIMIT_BYTES),
    )(gv, gv, w16)


def _ffn_mid_bwd(d_act, gv, w16, t_ctx, name):
    t, f2 = gv.shape
    f = f2 // 2
    lanes = FFN_LANES
    nf = f // lanes
    band = _band_rows(t - t_ctx)
    n_bands = (t - t_ctx) // band
    mid = (FFN_CONV // 2) * FFN_CONV
    n_acc = CONV_TAPS + 1

    def tile_sum(x):
        return jnp.sum(x.reshape(x.shape[0] // SUBLANES, SUBLANES, lanes), axis=0)

    def body(da_ref, g_ref, v_ref, w_ref, dg_ref, dv_ref, dw_ref, dp_ref):
        w_rows = [w_ref[k:k + 1, :] for k in range(CONV_TAPS)]
        bias = w_ref[CONV_TAPS:CONV_TAPS + 1, :]
        acc = [jnp.zeros((SUBLANES, lanes), F32) for _ in range(n_acc)]
        if t_ctx:
            sh = _shifted_ctx(g_ref[0:t_ctx, :])
            pre = sh[0] * w_rows[mid] + sh[1] * w_rows[mid + 1] + sh[2] * w_rows[mid + 2] + bias
            ge, slope = _gelu_and_slope(pre)
            da = da_ref[0:t_ctx, :]
            dv_ref[0:t_ctx, :] = (da * ge).astype(dv_ref.dtype)
            dpre = da * v_ref[0:t_ctx, :] * slope
            dp_ref[0:t_ctx, :] = dpre
            for j in range(FFN_CONV):
                acc[mid + j] = acc[mid + j] + tile_sum(sh[j] * dpre)
            acc[CONV_TAPS] = acc[CONV_TAPS] + tile_sum(dpre)
            back = _shifted_ctx(dpre)
            dg_ref[0:t_ctx, :] = (back[2] * w_rows[mid] + back[1] * w_rows[mid + 1]
                                  + back[0] * w_rows[mid + 2]).astype(dg_ref.dtype)

        def first_pass(b, acc):
            acc = list(acc)
            start = pl.multiple_of(t_ctx + b * band, GRID_W)
            sh = _shifted_band(g_ref, b, n_bands, band, t_ctx, t)
            ge, slope = _gelu_and_slope(_conv_sum(sh, w_rows, band, False) + bias)
            da = da_ref[pl.ds(start, band), :]
            dv_ref[pl.ds(start, band), :] = (da * ge).astype(dv_ref.dtype)
            dpre = da * v_ref[pl.ds(start, band), :] * slope
            dp_ref[pl.ds(start, band), :] = dpre
            for i in range(FFN_CONV):
                for j in range(FFN_CONV):
                    k = i * FFN_CONV + j
                    acc[k] = acc[k] + tile_sum(sh[j][i * GRID_W:i * GRID_W + band] * dpre)
            acc[CONV_TAPS] = acc[CONV_TAPS] + tile_sum(dpre)
            return tuple(acc)

        acc = lax.fori_loop(0, n_bands, first_pass, tuple(acc))
        for k in range(n_acc):
            dw_ref[k * SUBLANES:(k + 1) * SUBLANES, :] = acc[k]

        def second_pass(b, carry):
            start = pl.multiple_of(t_ctx + b * band, GRID_W)
            dg_ref[pl.ds(start, band), :] = _conv_sum(_shifted_band(dp_ref, b, n_bands, band, t_ctx, t),
                                                      w_rows, band, True).astype(dg_ref.dtype)
            return carry

        lax.fori_loop(0, n_bands, second_pass, 0)

    col = pl.BlockSpec((t, lanes), lambda j: (0, j))
    return pl.pallas_call(
        body, name=name,
        grid=(nf,),
        in_specs=[col, col, pl.BlockSpec((t, lanes), lambda j: (0, nf + j)),
                  pl.BlockSpec((CONV_W_ROWS, lanes), lambda j: (0, j))],
        out_specs=[col, col, pl.BlockSpec((n_acc * SUBLANES, lanes), lambda j: (0, j))],
        out_shape=[jax.ShapeDtypeStruct((t, f), BF16), jax.ShapeDtypeStruct((t, f), BF16),
                   jax.ShapeDtypeStruct((n_acc * SUBLANES, f), F32)],
        scratch_shapes=[pltpu.VMEM((t, lanes), F32)],
        compiler_params=pltpu.CompilerParams(dimension_semantics=('parallel',), vmem_limit_bytes=VMEM_LIMIT_BYTES),
    )(d_act, gv, gv, w16)


def _gate_value(up, layout):
    per_dev, per_dev_pad, f = layout
    if per_dev == per_dev_pad:
        return up
    return jnp.concatenate([_take_cols(up, per_dev, per_dev_pad, 0, f), _take_cols(up, per_dev, per_dev_pad, f, 2 * f)],
                           axis=1)


def _ffn_forward(hm, w_up, conv_w, conv_b, w_down, t_ctx, layout, name):
    up = _matmul(hm, w_up, 'nn', name + '_up_fwd')
    gv = _gate_value(up, layout)
    w16 = _conv_operand(conv_w, conv_b)
    act = _ffn_mid_fwd(gv, w16, t_ctx, name + '_mid_fwd')
    return _matmul(act, w_down, 'nn', name + '_down_fwd'), (hm, w_up, gv, w16, w_down, act)


@functools.partial(jax.custom_vjp, nondiff_argnums=(7, 8, 9))
def _ffn_block(hm, w_up, tap_up, conv_w, conv_b, w_down, tap_down, t_ctx, layout, name):
    del tap_up, tap_down
    return _ffn_forward(hm, w_up, conv_w, conv_b, w_down, t_ctx, layout, name)[0]


def _ffn_block_fwd(hm, w_up, tap_up, conv_w, conv_b, w_down, tap_down, t_ctx, layout, name):
    del tap_up, tap_down
    return _ffn_forward(hm, w_up, conv_w, conv_b, w_down, t_ctx, layout, name)


def _ffn_block_bwd(t_ctx, layout, name, res, d_out):
    hm, w_up, gv, w16, w_down, act = res
    f = act.shape[1]
    d_act = _matmul(d_out, w_down, 'nt', name + '_down_dx')
    d_w_down = _matmul(act, d_out, 'tn', name + '_down_dw')
    d_gate, d_val, d_w = _ffn_mid_bwd(d_act, gv, w16, t_ctx, name + '_mid_bwd')
    d_gv = jnp.concatenate([d_gate, d_val], axis=1)
    if layout[0] == layout[1]:
        d_up = d_gv
    else:
        d_up = jax.vjp(lambda u: _gate_value(u, layout), jnp.zeros((gv.shape[0], w_up.shape[1]), d_gv.dtype))[1](d_gv)[0]
    d_hm = _matmul(d_up, w_up, 'nt', name + '_up_dx')
    d_w_up = _matmul(hm, d_up, 'tn', name + '_up_dw')
    d_w = jnp.sum(d_w.reshape(CONV_TAPS + 1, SUBLANES, f), axis=1)
    return (d_hm, jnp.zeros_like(w_up), d_w_up, d_w[:CONV_TAPS].reshape(FFN_CONV, FFN_CONV, f), d_w[CONV_TAPS],
            jnp.zeros_like(w_down), d_w_down)


_ffn_block.defvjp(_ffn_block_fwd, _ffn_block_bwd)


LN_ROWS = 128
MOD_ROWS = 8


def _normalised(h, y, gate, alpha):
    r = alpha * h + gate * y
    xc = r - jnp.mean(r, axis=-1, keepdims=True)
    rstd = lax.rsqrt(jnp.mean(xc * xc, axis=-1, keepdims=True) + LN_EPS)
    return xc * rstd, rstd


def _mod_row(m_ref, k, is_ctx):
    lat = m_ref[3 + k:4 + k, :]
    return lat if is_ctx is None else jnp.where(is_ctx, m_ref[k:k + 1, :], lat)


def _res_ln_fwd_call(h, y, mods, t_ctx, alpha, with_mod, name):
    t, d = h.shape
    n_ctx_tiles = t_ctx // LN_ROWS

    def body(h_ref, y_ref, m_ref, hn_ref, *rest):
        is_ctx = (pl.program_id(0) < n_ctx_tiles) if n_ctx_tiles else None
        xhat, _ = _normalised(h_ref[...], y_ref[...], _mod_row(m_ref, 0, is_ctx), alpha)
        hn = xhat * m_ref[6:7, :] + m_ref[7:8, :]
        hn_ref[...] = hn
        if with_mod:
            rest[0][...] = hn * (1.0 + _mod_row(m_ref, 2, is_ctx)) + _mod_row(m_ref, 1, is_ctx)

    blk = pl.BlockSpec((LN_ROWS, d), lambda i: (i, 0))
    n_out = 2 if with_mod else 1
    return pl.pallas_call(
        body, name=name,
        grid=(t // LN_ROWS,),
        in_specs=[blk, blk, pl.BlockSpec((MOD_ROWS, d), lambda i: (0, 0))],
        out_specs=[blk] * n_out,
        out_shape=[jax.ShapeDtypeStruct((t, d), F32)] * n_out,
        compiler_params=pltpu.CompilerParams(dimension_semantics=('parallel',)),
    )(h, y, mods)


def _res_ln_bwd_call(h, y, mods, d_hn, d_hm, t_ctx, alpha, name):
    t, d = h.shape
    n_ctx_tiles = t_ctx // LN_ROWS
    with_mod = d_hm is not None

    def tile_sum(x):
        return jnp.sum(x.reshape(LN_ROWS // SUBLANES, SUBLANES, d), axis=0)

    def body(*refs):
        if with_mod:
            h_ref, y_ref, m_ref, dhn_ref, dhm_ref, dh_ref, dy_ref, acc_ref = refs
        else:
            h_ref, y_ref, m_ref, dhn_ref, dh_ref, dy_ref, acc_ref = refs

        @pl.when(pl.program_id(0) == 0)
        def _():
            acc_ref[...] = jnp.zeros_like(acc_ref)

        is_ctx = (pl.program_id(0) < n_ctx_tiles) if n_ctx_tiles else None
        gate = _mod_row(m_ref, 0, is_ctx)
        y = y_ref[...]
        xhat, rstd = _normalised(h_ref[...], y, gate, alpha)
        ln_g = m_ref[6:7, :]
        dhn = dhn_ref[...]
        base = 3 * SUBLANES if is_ctx is None else jnp.where(is_ctx, 0, 3 * SUBLANES)

        def add_to(row, part):
            if isinstance(row, int):
                acc_ref[row:row + SUBLANES, :] += part
            else:
                acc_ref[pl.ds(pl.multiple_of(row, SUBLANES), SUBLANES), :] += part

        if with_mod:
            dhm = dhm_ref[...]
            hn = xhat * ln_g + m_ref[7:8, :]
            add_to(base + SUBLANES, tile_sum(dhm))
            add_to(base + 2 * SUBLANES, tile_sum(dhm * hn))
            dhn = dhn + dhm * (1.0 + _mod_row(m_ref, 2, is_ctx))
        add_to(6 * SUBLANES, tile_sum(dhn * xhat))
        add_to(7 * SUBLANES, tile_sum(dhn))
        dx = dhn * ln_g
        dr = rstd * (dx - jnp.mean(dx, axis=-1, keepdims=True) - xhat * jnp.mean(dx * xhat, axis=-1, keepdims=True))
        dh_ref[...] = alpha * dr
        dy_ref[...] = gate * dr
        add_to(base, tile_sum(dr * y))

    blk = pl.BlockSpec((LN_ROWS, d), lambda i: (i, 0))
    operands = [h, y, mods, d_hn] + ([d_hm] if with_mod else [])
    return pl.pallas_call(
        body, name=name,
        grid=(t // LN_ROWS,),
        in_specs=[blk, blk, pl.BlockSpec((MOD_ROWS, d), lambda i: (0, 0)), blk] + ([blk] if with_mod else []),
        out_specs=[blk, blk, pl.BlockSpec((MOD_ROWS * SUBLANES, d), lambda i: (0, 0))],
        out_shape=[jax.ShapeDtypeStruct((t, d), F32), jax.ShapeDtypeStruct((t, d), F32),
                   jax.ShapeDtypeStruct((MOD_ROWS * SUBLANES, d), F32)],
        compiler_params=pltpu.CompilerParams(dimension_semantics=('arbitrary',)),
    )(*operands)


@functools.partial(jax.custom_vjp, nondiff_argnums=(3, 4, 5, 6))
def _res_ln_mod(h, y, mods, t_ctx, alpha, with_mod, name):
    return tuple(_res_ln_fwd_call(h, y, mods, t_ctx, alpha, with_mod, name + '_fwd'))


def _res_ln_mod_fwd(h, y, mods, t_ctx, alpha, with_mod, name):
    return tuple(_res_ln_fwd_call(h, y, mods, t_ctx, alpha, with_mod, name + '_fwd')), (h, y, mods)


def _res_ln_mod_bwd(t_ctx, alpha, with_mod, name, res, cts):
    h, y, mods = res
    d_h, d_y, acc = _res_ln_bwd_call(h, y, mods, cts[0], cts[1] if with_mod else None, t_ctx, alpha, name + '_bwd')
    return d_h, d_y, jnp.sum(acc.reshape(MOD_ROWS, SUBLANES, h.shape[1]), axis=1)


_res_ln_mod.defvjp(_res_ln_mod_fwd, _res_ln_mod_bwd)


def _window(ref, j, rows, cols, axis):
    if axis == 0:
        return ref.at[pl.ds(j * rows, rows), :]
    return ref.at[:, pl.ds(j * cols, cols)]


def _all_gather(block, axis, name):
    rows, cols = block.shape

    def body(x_ref, out_ref, send_sems, recv_sems, local_sem):
        x, y, c = _my_coords()
        me, sibling = (x, y, c), (x, y, 1 - c)
        chips = [(1 - x, y), (x, 1 - y), (1 - x, 1 - y)]

        def win(px, py, pc):
            return _window(out_ref, 4 * px + 2 * py + pc, rows, cols, axis)

        def copy(k, blk, to, src=None):
            return pltpu.make_async_remote_copy(
                src_ref=win(*blk) if src is None else src, dst_ref=win(*blk),
                send_sem=send_sems.at[k], recv_sem=recv_sems.at[k], device_id=to, device_id_type=MESH)

        mine = pltpu.make_async_copy(x_ref, win(*me), local_sem)
        mine.start()
        first = [copy(0, me, sibling, src=x_ref)]
        first += [copy(1 + j, me, (*chip, c), src=x_ref) for j, chip in enumerate(chips)]
        for cp in first:
            cp.start()
        passed = [copy(4 + j, (*chip, c), sibling) for j, chip in enumerate(chips)]
        for j, chip in enumerate(chips):
            copy(1 + j, (*chip, c), me).wait_recv()
            passed[j].start()
        copy(0, sibling, me).wait_recv()
        for j, chip in enumerate(chips):
            copy(4 + j, (*chip, 1 - c), me).wait_recv()
        for cp in first + passed:
            cp.wait_send()
        mine.wait()

    out_shape = (N_DEV * rows, cols) if axis == 0 else (rows, N_DEV * cols)
    return pl.pallas_call(
        body, name=name,
        out_shape=jax.ShapeDtypeStruct(out_shape, block.dtype),
        in_specs=[ANY], out_specs=ANY,
        scratch_shapes=[pltpu.SemaphoreType.DMA((7,)), pltpu.SemaphoreType.DMA((7,)), pltpu.SemaphoreType.DMA],
    )(block)


def _exchange_pair(full, rows, cols, axis, name):
    def body(g_ref, land_ref, send_sems, recv_sems):
        x, y, c = _my_coords()
        copies = []
        for k in range(N_CHIP):
            j = 2 * k + (1 - c)
            copies.append(pltpu.make_async_remote_copy(
                src_ref=_window(g_ref, j, rows, cols, axis), dst_ref=land_ref.at[k],
                send_sem=send_sems.at[k], recv_sem=recv_sems.at[k], device_id=(x, y, 1 - c), device_id_type=MESH))
        for cp in copies:
            cp.start()
        for cp in copies:
            cp.wait()

    return pl.pallas_call(
        body, name=name,
        out_shape=jax.ShapeDtypeStruct((N_CHIP, rows, cols), full.dtype),
        in_specs=[ANY], out_specs=ANY,
        scratch_shapes=[pltpu.SemaphoreType.DMA((N_CHIP,)), pltpu.SemaphoreType.DMA((N_CHIP,))],
    )(full)


def _elementwise_tiles(rows, cols):
    tc = _first_divisor(cols, (1024, 512, 384, 256, 128))
    tr = _first_divisor(rows, (512, 256, 128, 64, 32, 16, 8))
    return tr, tc


def _pair_add(full, land, rows, cols, axis, wire_dtype, name):
    tr, tc = _elementwise_tiles(rows, cols)
    c_arr = jnp.reshape(lax.axis_index('c'), (1,)).astype(jnp.int32)

    def full_map(k, i, j, c_ref):
        blk = 2 * k + c_ref[0]
        if axis == 0:
            return (blk * (rows // tr) + i, j)
        return (i, blk * (cols // tc) + j)

    def body(c_ref, g_ref, l_ref, o_ref):
        del c_ref
        o_ref[0] = (g_ref[...] + l_ref[0]).astype(o_ref.dtype)

    return pl.pallas_call(
        body, name=name,
        grid_spec=pltpu.PrefetchScalarGridSpec(
            num_scalar_prefetch=1,
            grid=(N_CHIP, rows // tr, cols // tc),
            in_specs=[pl.BlockSpec((tr, tc), full_map),
                      pl.BlockSpec((1, tr, tc), lambda k, i, j, c_ref: (k, i, j))],
            out_specs=pl.BlockSpec((1, tr, tc), lambda k, i, j, c_ref: (k, i, j)),
        ),
        out_shape=jax.ShapeDtypeStruct((N_CHIP, rows, cols), wire_dtype),
        compiler_params=pltpu.CompilerParams(dimension_semantics=('parallel', 'parallel', 'parallel')),
    )(c_arr, full, land)


def _exchange_chips(sums, name):
    _, rows, cols = sums.shape

    def body(s_ref, land_ref, send_sems, recv_sems, local_sem):
        x, y, c = _my_coords()
        my_chip = 2 * x + y
        mine = pltpu.make_async_copy(s_ref.at[my_chip], land_ref.at[my_chip], local_sem)
        mine.start()
        chips = [(1 - x, y), (x, 1 - y), (1 - x, 1 - y)]
        sends = []
        for t, (px, py) in enumerate(chips):
            sends.append(pltpu.make_async_remote_copy(
                src_ref=s_ref.at[2 * px + py], dst_ref=land_ref.at[my_chip],
                send_sem=send_sems.at[t], recv_sem=recv_sems.at[t], device_id=(px, py, c), device_id_type=MESH))
        for cp in sends:
            cp.start()
        for t, (px, py) in enumerate(chips):
            pltpu.make_async_remote_copy(
                src_ref=s_ref.at[my_chip], dst_ref=land_ref.at[2 * px + py],
                send_sem=send_sems.at[t], recv_sem=recv_sems.at[t], device_id=(px, py, c),
                device_id_type=MESH).wait_recv()
        for cp in sends:
            cp.wait_send()
        mine.wait()

    return pl.pallas_call(
        body, name=name,
        out_shape=jax.ShapeDtypeStruct((N_CHIP, rows, cols), sums.dtype),
        in_specs=[ANY], out_specs=ANY,
        scratch_shapes=[pltpu.SemaphoreType.DMA((3,)), pltpu.SemaphoreType.DMA((3,)), pltpu.SemaphoreType.DMA],
    )(sums)


def _reduce_scatter_slots(full, rows, cols, axis, wire_dtype, name):
    land = _exchange_pair(full, rows, cols, axis, name + '_pair')
    sums = _pair_add(full, land, rows, cols, axis, wire_dtype, name + '_add')
    return _exchange_chips(sums, name + '_chips')


def _sum_slots(slots, name):
    n_slots, rows, cols = slots.shape
    tr, tc = _elementwise_tiles(rows, cols)

    def body(s_ref, o_ref):
        g = s_ref[0]
        for s in range(1, n_slots):
            g = g + s_ref[s]
        o_ref[...] = g

    return pl.pallas_call(
        body, name=name,
        grid=(rows // tr, cols // tc),
        in_specs=[pl.BlockSpec((n_slots, tr, tc), lambda i, j: (0, i, j))],
        out_specs=pl.BlockSpec((tr, tc), lambda i, j: (i, j)),
        out_shape=jax.ShapeDtypeStruct((rows, cols), F32),
        compiler_params=pltpu.CompilerParams(dimension_semantics=('parallel', 'parallel')),
    )(slots)


def _sum_adamw(slots, w, m, v, name):
    n_slots, rows, cols = slots.shape
    tr, tc = _elementwise_tiles(rows, cols)
    c1 = 1.0 - ADAM_B1 ** ADAM_STEP
    c2 = 1.0 - ADAM_B2 ** ADAM_STEP

    def body(s_ref, w_ref, m_ref, v_ref, g_out, d_out, m_out, v_out):
        g = s_ref[0].astype(F32)
        for s in range(1, n_slots):
            g = g + s_ref[s].astype(F32)
        m_new = ADAM_B1 * m_ref[...] + (1.0 - ADAM_B1) * g
        v_new = ADAM_B2 * v_ref[...] + (1.0 - ADAM_B2) * (g * g)
        m_hat = m_new / c1
        v_hat = v_new / c2
        g_out[...] = g
        d_out[...] = -ADAM_LR * (m_hat / (jnp.sqrt(v_hat) + ADAM_EPS) + ADAM_WD * w_ref[...])
        m_out[...] = m_new
        v_out[...] = v_new

    blk = pl.BlockSpec((tr, tc), lambda i, j: (i, j))
    shape = jax.ShapeDtypeStruct((rows, cols), F32)
    return pl.pallas_call(
        body, name=name,
        grid=(rows // tr, cols // tc),
        in_specs=[pl.BlockSpec((n_slots, tr, tc), lambda i, j: (0, i, j)), blk, blk, blk],
        out_specs=[blk, blk, blk, blk],
        out_shape=[shape, shape, shape, shape],
        compiler_params=pltpu.CompilerParams(dimension_semantics=('parallel', 'parallel')),
    )(slots, w, m, v)


def _rows_of(shape):
    size = int(np.prod(shape)) if len(shape) else 1
    return _round_up(_round_up(size, LANES) // LANES, SUBLANES)


def _pack(arrays, rows_multiple):
    parts = []
    for arr in arrays:
        flat = jnp.ravel(arr).astype(F32)
        rows = _rows_of(arr.shape)
        parts.append(jnp.pad(flat, (0, rows * LANES - flat.shape[0])).reshape(rows, LANES))
    total = sum(p.shape[0] for p in parts)
    pad = _round_up(total, rows_multiple) - total
    if pad:
        parts.append(jnp.zeros((pad, LANES), F32))
    return jnp.concatenate(parts, axis=0)


def _unpack(slab, shapes):
    out, r0 = [], 0
    for s in shapes:
        size = int(np.prod(s)) if len(s) else 1
        rows = _rows_of(s)
        out.append(slab[r0:r0 + rows].reshape(-1)[:size].reshape(s))
        r0 += rows
    return out


def _layer_norm(x, g, b):
    mu = jnp.mean(x, -1, keepdims=True)
    var = jnp.mean(jnp.square(x - mu), -1, keepdims=True)
    return (x - mu) * lax.rsqrt(var + LN_EPS) * g + b


def _modulate(x, shift, scale):
    return x * (1 + scale) + shift


def _take_cols(p, per_dev, per_dev_padded, lo, hi):
    parts = []
    while lo < hi:
        dev, off = divmod(lo, per_dev)
        n = min(hi - lo, per_dev - off)
        parts.append(p[:, dev * per_dev_padded + off: dev * per_dev_padded + off + n])
        lo += n
    return parts[0] if len(parts) == 1 else jnp.concatenate(parts, axis=1)


def _block_diag(blocks):
    g, a, b = blocks.shape
    eye = jnp.eye(g, dtype=blocks.dtype)
    return (blocks[:, :, None, :] * eye[:, None, :, None]).reshape(g * a, g * b)


def _s5_mixer(u, t_ctx, prm, name):
    t, gw = u.shape
    groups = gw // S5_CH
    ys = []
    for direction, reverse in enumerate((False, True)):
        lam = lax.complex(prm['s5_a_re'][direction], prm['s5_a_im'][direction])
        step = jnp.exp(prm['s5_log_step'][direction])[:, None]
        a_bar = jnp.exp(lam * step)
        b_bar = ((a_bar - 1.0) / lam)[..., None] * lax.complex(prm['s5_b_re'][direction],
                                                               prm['s5_b_im'][direction])
        c_mat = lax.complex(prm['s5_c_re'][direction], prm['s5_c_im'][direction])
        b_t = jnp.swapaxes(b_bar, 1, 2)
        b_mat = _interleave(_block_diag(jnp.real(b_t)), _block_diag(jnp.imag(b_t)))
        c_t = jnp.swapaxes(c_mat, 1, 2)
        c_blk = _interleave(_block_diag(jnp.real(c_t)).T, -_block_diag(jnp.imag(c_t)).T).T
        a_rows = jnp.stack([jnp.real(a_bar).reshape(-1), jnp.imag(a_bar).reshape(-1)])
        bu = _linear(u, b_mat, f'{name}_bu{direction}')
        h = _s5_scan(a_rows, bu, t_ctx, reverse, f'{name}_scan{direction}')
        ys.append(_linear(h, c_blk, f'{name}_y{direction}'))
    y = ys[0] + ys[1] + prm['s5_d'][None, :] * u
    z = jax.nn.gelu(y)
    return z * jax.nn.sigmoid(_linear(z, prm['s5_glu_w'], f'{name}_glu') + prm['s5_glu_b'])


def _chunk_gating(u, v, prm):
    t, gw = u.shape
    hd = gw // SG_HEADS
    u = jax.nn.gelu(u)
    v = jax.nn.gelu(v).reshape(t // SG_CHUNK, SG_CHUNK, SG_HEADS, hd)
    v = _layer_norm(v, prm['sg_ln_g'].reshape(SG_HEADS, hd), prm['sg_ln_b'].reshape(SG_HEADS, hd))
    s = jnp.einsum('hij,cjhd->cihd', prm['sg_w'], v) + prm['sg_b'].T[None, :, :, None]
    return u * s.reshape(t, gw)


def _pool_mixer(p, prm):
    l, gw = p.shape
    pd = gw // len(POOL_WINDOWS)
    t = np.arange(l)
    outs = []
    for g, win in enumerate(POOL_WINDOWS):
        lo = np.clip(t - win // 2, 0, l - 1)
        hi = np.clip(t + win // 2 - 1, 0, l - 1)
        pg = p[:, g * pd:(g + 1) * pd]
        padded = jnp.pad(pg, ((win // 2, win // 2), (0, 0)))
        total = padded[0:l]
        for d in range(1, win):
            total = total + padded[d:d + l]
        mean = total / jnp.asarray((hi - lo + 1).astype(np.float32))[:, None]
        outs.append(jnp.einsum('lc,cd->ld', mean - pg, prm['pool_w'][g]))
    y = jnp.concatenate(outs, axis=-1) + prm['pool_b']
    return y * prm['pool_scale']


def _dw_conv1d(x, w, b, pad):
    l = x.shape[0]
    xp = jnp.pad(x, (pad, (0, 0)))
    y = xp[0:l] * w[0]
    for k in range(1, w.shape[0]):
        y = y + xp[k:k + l] * w[k]
    return y + b


def _m2_prepare(xbc, dt_raw, prm, gw):
    heads = gw // M2_HEAD_DIM
    xbc = jax.nn.silu(_dw_conv1d(xbc, prm['m2_conv_w'], prm['m2_conv_b'], M2_PAD))
    l = xbc.shape[0]
    n_bc = M2_GROUPS * M2_STATE
    rep = heads // M2_GROUPS
    xs = xbc[:, :gw].reshape(l, heads, M2_HEAD_DIM)
    bm = jnp.repeat(xbc[:, gw:gw + n_bc].reshape(l, M2_GROUPS, M2_STATE), rep, axis=1)
    cm = jnp.repeat(xbc[:, gw + n_bc:].reshape(l, M2_GROUPS, M2_STATE), rep, axis=1)
    dt = jax.nn.softplus(dt_raw.reshape(l, 2, heads) + prm['m2_dt_bias'])
    return xs, bm, cm, dt


def _ssd_scan(xs, dt, a, bm, cm, h0, need_y):
    l, nh, hp = xs.shape
    nc = l // M2_CHUNK

    def chunks(t):
        return t.reshape((nc, M2_CHUNK) + t.shape[1:])

    xd = chunks(xs * dt[..., None])
    bc, cc = chunks(bm), chunks(cm)
    a_cum = jnp.cumsum(chunks(dt * a), axis=1)
    a_tot = a_cum[:, -1]
    decay_end = jnp.exp(a_tot[:, None] - a_cum)
    chunk_states = jnp.einsum('cqhn,cqh,cqhp->chpn', bc, decay_end, xd)

    cum = jnp.cumsum(a_tot, axis=0)
    before = cum - a_tot
    earlier = jnp.tril(jnp.ones((nc, nc), bool), -1)[:, :, None]
    carry_w = jnp.exp(jnp.where(earlier, before[:, None, :] - cum[None, :, :], -jnp.inf))
    h_final = (jnp.exp(cum[-1])[:, None, None] * h0
               + jnp.einsum('dh,dhpn->hpn', jnp.exp(cum[-1][None, :] - cum), chunk_states,
                            precision=lax.Precision.HIGHEST))
    if not need_y:
        return None, h_final
    h_prev = (jnp.exp(before)[:, :, None, None] * h0[None]
              + jnp.einsum('cdh,dhpn->chpn', carry_w, chunk_states, precision=lax.Precision.HIGHEST))
    seg = a_cum[:, :, None, :] - a_cum[:, None, :, :]
    lower = jnp.tril(jnp.ones((M2_CHUNK, M2_CHUNK), bool))[None, :, :, None]
    decay = jnp.exp(jnp.where(lower, seg, -jnp.inf))
    scores = jnp.einsum('cihn,cjhn->cijh', cc, bc) * decay
    y = (jnp.einsum('cijh,cjhp->cihp', scores, xd)
         + jnp.einsum('cihn,chpn->cihp', cc, h_prev) * jnp.exp(a_cum)[..., None])
    return y.reshape(l, nh, hp), h_final


def _ssd_direction(inputs, direction, a, h0, need_y):
    xs, bm, cm, dt = inputs
    dt = dt[:, direction]
    if direction == 1:
        xs, bm, cm, dt = (jnp.flip(t, 0) for t in (xs, bm, cm, dt))
    y, h_final = _ssd_scan(xs, dt, a, bm, cm, h0, need_y)
    if direction == 1 and y is not None:
        y = jnp.flip(y, 0)
    return y, h_final


def _gated_rmsnorm(y, z, w):
    l, gw = z.shape
    g = (y * jax.nn.silu(z)).reshape(l, M2_GROUPS, gw // M2_GROUPS)
    g = g * lax.rsqrt(jnp.mean(jnp.square(g), -1, keepdims=True) + RMS_EPS)
    return g.reshape(l, gw) * w


def _mamba2_mixer(z, xbc, dt_raw, t_ctx, prm, need_ctx):
    gw = z.shape[1]
    heads = gw // M2_HEAD_DIM
    ctx_in = _m2_prepare(xbc[:t_ctx], dt_raw[:t_ctx], prm, gw)
    lat_in = _m2_prepare(xbc[t_ctx:], dt_raw[t_ctx:], prm, gw)
    a = -jnp.exp(prm['m2_a_log'])
    ys_ctx, ys_lat = [], []
    for direction in range(2):
        h0 = jnp.zeros((heads, M2_HEAD_DIM, M2_STATE), F32)
        y_c, h_c = _ssd_direction(ctx_in, direction, a[direction], h0, need_ctx)
        y_l, _ = _ssd_direction(lat_in, direction, a[direction], h_c, True)
        ys_lat.append(y_l)
        if need_ctx:
            ys_ctx.append(y_c)
    d_h = prm['m2_d'][None, :, None]

    def finish(ys, xs, zz):
        y = (ys[0] + ys[1] + d_h * xs).reshape(zz.shape[0], gw)
        return _gated_rmsnorm(y, zz, prm['m2_norm_w'])

    y_lat = finish(ys_lat, lat_in[0], z[t_ctx:])
    if need_ctx:
        return jnp.concatenate([finish(ys_ctx, ctx_in[0], z[:t_ctx]), y_lat], axis=0)
    return y_lat


def _forward_loss(dp, consts, dims):
    depth, t_ctx, d_model = dims['depth'], dims['t_ctx'], dims['d_model']
    gw = d_model // 4
    alpha = (2 * depth) ** 0.25
    in_sizes = (gw, gw, gw, gw, gw, gw + 2 * M2_GROUPS * M2_STATE, 2 * (gw // M2_HEAD_DIM))
    in_offs = np.concatenate([[0], np.cumsum(in_sizes)])
    ml = [[dp['mod_lat'][i, k][None, :] for k in range(6)] for i in range(depth)]
    mc = [[dp['mod_ctx'][i, k][None, :] for k in range(6)] for i in range(depth)]
    zero_row = jnp.zeros((1, d_model), F32)
    h = jnp.concatenate([consts['ctx'], dp['x']], axis=0)
    hm = jnp.concatenate([_modulate(consts['ctx'], mc[0][0], mc[0][1]), _modulate(dp['x'], ml[0][0], ml[0][1])],
                         axis=0)
    for i in range(depth):
        need_ctx = i < depth - 1
        prm = {k: v[i] for k, v in dp['small'].items()}
        p = _linear_tap(hm, consts['w_in'][i], dp['taps']['w_in'][i], f'in{i}')
        seg = [_take_cols(p, dims['in_per_dev'], dims['in_per_dev_pad'], int(in_offs[s]), int(in_offs[s + 1]))
               for s in range(7)]
        ya = _s5_mixer(seg[0], t_ctx, prm, f's5_{i}')
        yb = _chunk_gating(seg[1], seg[2], prm)
        yc = jnp.concatenate([_pool_mixer(seg[3][:t_ctx], prm), _pool_mixer(seg[3][t_ctx:], prm)], axis=0)
        yd = _mamba2_mixer(seg[4], seg[5], seg[6], t_ctx, prm, need_ctx)
        if need_ctx:
            mix_in = jnp.concatenate([ya, yb, yc, yd], axis=1)
        else:
            mix_in = jnp.concatenate([ya[t_ctx:], yb[t_ctx:], yc[t_ctx:], yd], axis=1)
        mix = _linear_tap(mix_in, consts['w_out'][i], dp['taps']['w_out'][i], f'out{i}')
        t_c = t_ctx if need_ctx else 0
        if not need_ctx and h.shape[0] != mix.shape[0]:
            h = h[t_ctx:]
        mods = jnp.concatenate([mc[i][2], mc[i][3], mc[i][4], ml[i][2], ml[i][3], ml[i][4],
                                prm['ln1_g'][None], prm['ln1_b'][None]], axis=0)
        h, hm = _res_ln_mod(h, mix, mods, t_c, alpha, True, f'ln1_{i}')

        f_out = _ffn_block(hm, consts['ffn_w_up'][i], dp['taps']['ffn_w_up'][i], prm['ffn_conv_w'], prm['ffn_conv_b'],
                           consts['ffn_w_down'][i], dp['taps']['ffn_w_down'][i], t_c,
                           (dims['up_per_dev'], dims['up_per_dev_pad'], dims['ffn_hidden']), f'ffn{i}')
        if need_ctx:
            mods = jnp.concatenate([mc[i][5], mc[i + 1][0], mc[i + 1][1], ml[i][5], ml[i + 1][0], ml[i + 1][1],
                                    prm['ln2_g'][None], prm['ln2_b'][None]], axis=0)
            h, hm = _res_ln_mod(h, f_out, mods, t_c, alpha, True, f'ln2_{i}')
        else:
            mods = jnp.concatenate([mc[i][5], zero_row, zero_row, ml[i][5], zero_row, zero_row,
                                    prm['ln2_g'][None], prm['ln2_b'][None]], axis=0)
            h = _res_ln_mod(h, f_out, mods, t_c, alpha, False, f'ln2_{i}')[0]
    err = jnp.square(h - consts['target'])
    return 0.5 * jnp.sum(jnp.mean(err, axis=-1))


def _silu_grad(x):
    s = jax.nn.sigmoid(x)
    return s * (1 + x * (1 - s))


def _step(w, m, v, x, c, ctx, target):
    depth, d_model, ada_cols = w['w_ada'].shape
    t_ctx, t_lat = ctx.shape[1], x.shape[1]
    me = _my_index()
    in_per_dev = w['w_in'].shape[2]
    in_pad = _round_up(in_per_dev, LANES)
    up_per_dev = w['ffn_w_up'].shape[2]
    up_pad = _round_up(up_per_dev, LANES)
    f_hidden = w['ffn_w_down'].shape[1] * N_DEV
    dims = dict(depth=depth, t_ctx=t_ctx, d_model=d_model, in_per_dev=in_per_dev, in_per_dev_pad=in_pad,
                up_per_dev=up_per_dev, up_per_dev_pad=up_pad, ffn_hidden=f_hidden)

    rows16 = 2 * SUBLANES
    silu_c_all = _all_gather(jnp.pad(jax.nn.silu(c), ((0, SUBLANES - 1), (0, 0))), 0, 'ag_c')
    silu_c_all = silu_c_all.reshape(N_DEV, SUBLANES, d_model)[:, 0]
    silu_cc = jax.nn.silu(w['c_ctx'])
    ada_in = jnp.concatenate([silu_c_all, silu_cc[None], jnp.zeros((rows16 - N_DEV - 1, d_model), F32)], axis=0)
    mod_loc = jnp.concatenate([_matmul(ada_in, w['w_ada'][i], 'nn', 'ada_fwd') for i in range(depth)], axis=0)
    mod_all = _all_gather(mod_loc, 1, 'ag_mod').reshape(depth, rows16, 6 * d_model)
    mod_all = mod_all + w['b_ada'][:, None, :]
    mod_lat = lax.dynamic_index_in_dim(mod_all, me, axis=1, keepdims=False).reshape(depth, 6, d_model)
    mod_ctx = mod_all[:, N_DEV].reshape(depth, 6, d_model)

    def pad_cols(a, to):
        return jnp.pad(a, ((0, 0), (0, to - a.shape[1])))

    gathered = {
        'w_in': [_all_gather(pad_cols(w['w_in'][i], in_pad).astype(BF16), 1, 'ag_w_in') for i in range(depth)],
        'ffn_w_up': [_all_gather(pad_cols(w['ffn_w_up'][i], up_pad).astype(BF16), 1, 'ag_w_up')
                     for i in range(depth)],
        'w_out': [_all_gather(w['w_out'][i].astype(BF16), 0, 'ag_w_out') for i in range(depth)],
        'ffn_w_down': [_all_gather(w['ffn_w_down'][i].astype(BF16), 0, 'ag_w_down') for i in range(depth)],
    }
    sharded_small = [w[n] for n in COL_SHARDED_SMALL + ROW_SHARDED_SMALL]
    small_slab = _pack(sharded_small, PACK_ROWS)
    slab_rows = small_slab.shape[0]
    small_all = _all_gather(small_slab, 0, 'ag_small').reshape(N_DEV, slab_rows, LANES)
    per_dev = [_unpack(small_all[d], [a.shape for a in sharded_small]) for d in range(N_DEV)]
    small_full = {}
    for k, n in enumerate(COL_SHARDED_SMALL):
        small_full[n] = jnp.concatenate([per_dev[d][k] for d in range(N_DEV)], axis=-1)
    for k, n in enumerate(ROW_SHARDED_SMALL):
        small_full[n] = jnp.concatenate([per_dev[d][len(COL_SHARDED_SMALL) + k] for d in range(N_DEV)], axis=1)

    small = {n: w[n] for n in REPLICATED}
    small.update(small_full)
    taps = {n: [jnp.zeros(gathered[n][i].shape, F32) for i in range(depth)] for n in BIG}
    dp = dict(x=x[0], small=small, mod_lat=mod_lat, mod_ctx=mod_ctx, taps=taps)
    consts = dict(ctx=ctx[0], target=target[0], **gathered)
    loss_local, grads = jax.value_and_grad(functools.partial(_forward_loss, consts=consts, dims=dims))(dp)

    out_g, out_d, out_m, out_v = {}, {}, {}, {}

    def finish_big(name, i, rows, cols, axis, keep_cols):
        slots = _reduce_scatter_slots(grads['taps'][name][i], rows, cols, axis, BF16, f'rs_{name}')
        shard = [pad_cols(t[name][i], cols) for t in (w, m, v)]
        res = _sum_adamw(slots, *shard, f'adamw_{name}')
        return [r[:, :keep_cols] for r in res]

    big_specs = {
        'w_in': (d_model, in_pad, 1, in_per_dev),
        'ffn_w_up': (d_model, up_pad, 1, up_per_dev),
        'w_out': (w['w_out'].shape[1], d_model, 0, d_model),
        'ffn_w_down': (w['ffn_w_down'].shape[1], d_model, 0, d_model),
    }
    for name, (rows, cols, axis, keep) in big_specs.items():
        per_layer = [finish_big(name, i, rows, cols, axis, keep) for i in range(depth)]
        for k, dst in enumerate((out_g, out_d, out_m, out_v)):
            dst[name] = jnp.stack([per_layer[i][k] for i in range(depth)])

    d_lat = grads['mod_lat'].reshape(depth, 6 * d_model)
    d_ctx = grads['mod_ctx'].reshape(depth, 6 * d_model)
    d_rows = jnp.concatenate([d_lat, d_ctx, jnp.zeros((SUBLANES - 2 * depth, 6 * d_model), F32)], axis=0)
    d_all = _all_gather(d_rows, 0, 'ag_dmod').reshape(N_DEV, SUBLANES, 6 * d_model)
    d_lat_all = d_all[:, :depth]
    d_ctx_sum = d_all[0, depth:2 * depth]
    for d in range(1, N_DEV):
        d_ctx_sum = d_ctx_sum + d_all[d, depth:2 * depth]
    g_b_ada = d_ctx_sum
    for d in range(N_DEV):
        g_b_ada = g_b_ada + d_lat_all[d]
    g_w_ada, c_ctx_part = [], jnp.zeros((d_model,), F32)
    for i in range(depth):
        d_mat = jnp.concatenate([d_lat_all[:, i], d_ctx_sum[i][None],
                                 jnp.zeros((rows16 - N_DEV - 1, 6 * d_model), F32)], axis=0)
        d_mine = lax.dynamic_slice_in_dim(d_mat, me * ada_cols, ada_cols, axis=1)
        g_w_ada.append(_matmul(ada_in, d_mine, 'tn', 'ada_dw'))
        back = _matmul(d_mine, w['w_ada'][i], 'nt', 'ada_dx')
        c_ctx_part = c_ctx_part + back[N_DEV]
    c_ctx_part = c_ctx_part * _silu_grad(w['c_ctx'])
    g_w_ada = jnp.stack(g_w_ada).reshape(1, depth * d_model, ada_cols)
    res = _sum_adamw(g_w_ada, *[t['w_ada'].reshape(depth * d_model, ada_cols) for t in (w, m, v)], 'adamw_w_ada')
    for k, dst in enumerate((out_g, out_d, out_m, out_v)):
        dst['w_ada'] = res[k].reshape(depth, d_model, ada_cols)

    reduced_names = REPLICATED[:]
    reduced_names.remove('b_ada')
    reduced_names += list(COL_SHARDED_SMALL + ROW_SHARDED_SMALL)
    to_reduce = [grads['small'][n] for n in reduced_names] + [c_ctx_part, loss_local]
    slab = _pack(to_reduce, N_DEV * PACK_ROWS)
    chunk_rows = slab.shape[0] // N_DEV
    slots = _reduce_scatter_slots(slab, chunk_rows, LANES, 0, F32, 'rs_small')
    mine = _sum_slots(slots, 'sum_small')
    summed = _all_gather(mine, 0, 'ag_small_sum')
    parts = _unpack(summed, [a.shape for a in to_reduce])
    g_small = dict(zip(reduced_names, parts[:len(reduced_names)]))
    g_small['c_ctx'] = parts[-2]
    g_small['b_ada'] = g_b_ada
    loss = parts[-1]

    def my_shard(name, full):
        if name in COL_SHARDED_SMALL:
            n = full.shape[-1] // N_DEV
            return lax.dynamic_slice_in_dim(full, me * n, n, axis=full.ndim - 1)
        if name in ROW_SHARDED_SMALL:
            n = full.shape[1] // N_DEV
            return lax.dynamic_slice_in_dim(full, me * n, n, axis=1)
        return full

    small_names = [n for n in WEIGHTS if n not in BIG + ('w_ada',)]
    g_list = [my_shard(n, g_small[n]) for n in small_names]
    g_slab = _pack(g_list, PACK_ROWS)
    res = _sum_adamw(g_slab[None], *[_pack([t[n] for n in small_names], PACK_ROWS) for t in (w, m, v)],
                     'adamw_small')
    shapes = [w[n].shape for n in small_names]
    for k, dst in enumerate((out_g, out_d, out_m, out_v)):
        if k == 0:
            dst.update(dict(zip(small_names, g_list)))
        else:
            dst.update(dict(zip(small_names, _unpack(res[k], shapes))))

    grad_x = grads['x'][None]
    return (loss, grad_x, *[out_g[n] for n in WEIGHTS], *[out_d[n] for n in WEIGHTS],
            *[out_m[n] for n in WEIGHTS], *[out_v[n] for n in WEIGHTS])


def kernel(x, c, ctx, c_ctx, w_ada, b_ada, w_in, w_out, ln1_g, ln1_b, ln2_g, ln2_b, s5_a_re, s5_a_im, s5_b_re, s5_b_im, s5_c_re, s5_c_im, s5_log_step, s5_d, s5_glu_w, s5_glu_b, sg_ln_g, sg_ln_b, sg_w, sg_b, pool_w, pool_b, pool_scale, m2_conv_w, m2_conv_b, m2_dt_bias, m2_a_log, m2_d, m2_norm_w, ffn_w_up, ffn_conv_w, ffn_conv_b, ffn_w_down, loss_target, m_c_ctx, m_w_ada, m_b_ada, m_w_in, m_w_out, m_ln1_g, m_ln1_b, m_ln2_g, m_ln2_b, m_s5_a_re, m_s5_a_im, m_s5_b_re, m_s5_b_im, m_s5_c_re, m_s5_c_im, m_s5_log_step, m_s5_d, m_s5_glu_w, m_s5_glu_b, m_sg_ln_g, m_sg_ln_b, m_sg_w, m_sg_b, m_pool_w, m_pool_b, m_pool_scale, m_m2_conv_w, m_m2_conv_b, m_m2_dt_bias, m_m2_a_log, m_m2_d, m_m2_norm_w, m_ffn_w_up, m_ffn_conv_w, m_ffn_conv_b, m_ffn_w_down, v_c_ctx, v_w_ada, v_b_ada, v_w_in, v_w_out, v_ln1_g, v_ln1_b, v_ln2_g, v_ln2_b, v_s5_a_re, v_s5_a_im, v_s5_b_re, v_s5_b_im, v_s5_c_re, v_s5_c_im, v_s5_log_step, v_s5_d, v_s5_glu_w, v_s5_glu_b, v_sg_ln_g, v_sg_ln_b, v_sg_w, v_sg_b, v_pool_w, v_pool_b, v_pool_scale, v_m2_conv_w, v_m2_conv_b, v_m2_dt_bias, v_m2_a_log, v_m2_d, v_m2_norm_w, v_ffn_w_up, v_ffn_conv_w, v_ffn_conv_b, v_ffn_w_down):
    given = dict(locals())
    w = {n: given[n] for n in WEIGHTS}
    m = {n: given['m_' + n] for n in WEIGHTS}
    v = {n: given['v_' + n] for n in WEIGHTS}
    return _step(w, m, v, x, c, ctx, loss_target)
```

```python
import functools

import jax
import jax.numpy as jnp
import numpy as np
from jax import lax
from jax.experimental import pallas as pl
from jax.experimental.pallas import tpu as pltpu
from jax.experimental.pallas import tpu_sc as plsc

F32 = jnp.float32
BF16 = jnp.bfloat16
MESH = pl.DeviceIdType.MESH
ANY = pl.BlockSpec(memory_space=pl.ANY)

N_DEV = 8
N_CHIP = 4
LANES = 128
SUBLANES = 8
VMEM_LIMIT_BYTES = 48 * 1024 * 1024
MATMUL_VMEM_BUDGET = 36 * 1024 * 1024
PACK_ROWS = 512

GRID_W = 64
S5_CH = 16
S5_STATE = 64
SG_HEADS = 4
SG_CHUNK = 128
POOL_WINDOWS = (2, 4, 8, 16)
M2_HEAD_DIM = 64
M2_STATE = 128
M2_GROUPS = 2
M2_CONV = 4
M2_PAD = (M2_CONV // 2, M2_CONV - 1 - M2_CONV // 2)
M2_CHUNK = 128
FFN_CONV = 3
LN_EPS = 1e-5
RMS_EPS = 1e-5
ADAM_LR = 0.001
ADAM_B1 = 0.9
ADAM_B2 = 0.999
ADAM_EPS = 1e-08
ADAM_WD = 0.01
ADAM_STEP = 10

WEIGHTS = ['c_ctx', 'w_ada', 'b_ada', 'w_in', 'w_out', 'ln1_g', 'ln1_b', 'ln2_g', 'ln2_b', 's5_a_re', 's5_a_im',
           's5_b_re', 's5_b_im', 's5_c_re', 's5_c_im', 's5_log_step', 's5_d', 's5_glu_w', 's5_glu_b', 'sg_ln_g',
           'sg_ln_b', 'sg_w', 'sg_b', 'pool_w', 'pool_b', 'pool_scale', 'm2_conv_w', 'm2_conv_b', 'm2_dt_bias',
           'm2_a_log', 'm2_d', 'm2_norm_w', 'ffn_w_up', 'ffn_conv_w', 'ffn_conv_b', 'ffn_w_down']
BIG = ('w_in', 'w_out', 'ffn_w_up', 'ffn_w_down')
COL_SHARDED_SMALL = ('m2_conv_w', 'ffn_conv_w')
ROW_SHARDED_SMALL = ('s5_glu_w',)
REPLICATED = [n for n in WEIGHTS if n not in BIG + COL_SHARDED_SMALL + ROW_SHARDED_SMALL + ('w_ada', 'c_ctx')]


def _round_up(n, m):
    return (n + m - 1) // m * m


def _my_coords():
    return lax.axis_index('x'), lax.axis_index('y'), lax.axis_index('c')


def _my_index():
    x, y, c = _my_coords()
    return 4 * x + 2 * y + c


def _first_divisor(n, cands):
    for c in cands:
        if n % c == 0:
            return c
    return n


def _matmul_tiles(mode, m, n, k, a_bytes, b_bytes):
    lane_c = (1024, 512, 384, 256, 128)
    sub_c = (1088, 1024, 544, 512, 256, 128, 64, 32, 16)
    tm = _first_divisor(m, lane_c if mode == 'tn' else sub_c)
    tn = _first_divisor(n, lane_c)
    k_c = [c for c in (5632, 4096, 2048, 1408, 1088, 1024, 544, 512, 256, 128) if k % c == 0] or [k]
    if mode == 'tn':
        k_c = [c for c in k_c if c <= 1088] or [k_c[-1]]
    for tk in k_c:
        use = 2 * (tm * tk * a_bytes + tk * tn * b_bytes) + 3 * tm * tn * 4
        if use <= MATMUL_VMEM_BUDGET:
            return tm, tn, tk
    return tm, tn, k_c[-1]


def _matmul(a, b, mode, name, out_dtype=F32):
    if mode == 'nn':
        (m, k), (k2, n) = a.shape, b.shape
    elif mode == 'nt':
        (m, k), (n, k2) = a.shape, b.shape
    else:
        (k, m), (k2, n) = a.shape, b.shape
    assert k == k2, (mode, a.shape, b.shape)
    tm, tn, tk = _matmul_tiles(mode, m, n, k, a.dtype.itemsize, b.dtype.itemsize)
    nk = k // tk
    if mode == 'nn':
        a_spec = pl.BlockSpec((tm, tk), lambda i, j, kk: (i, kk))
        b_spec = pl.BlockSpec((tk, tn), lambda i, j, kk: (kk, j))
        dims = (((1,), (0,)), ((), ()))
    elif mode == 'nt':
        a_spec = pl.BlockSpec((tm, tk), lambda i, j, kk: (i, kk))
        b_spec = pl.BlockSpec((tn, tk), lambda i, j, kk: (j, kk))
        dims = (((1,), (1,)), ((), ()))
    else:
        a_spec = pl.BlockSpec((tk, tm), lambda i, j, kk: (kk, i))
        b_spec = pl.BlockSpec((tk, tn), lambda i, j, kk: (kk, j))
        dims = (((0,), (0,)), ((), ()))

    def body(a_ref, b_ref, o_ref, acc_ref):
        kk = pl.program_id(2)

        @pl.when(kk == 0)
        def _():
            acc_ref[...] = jnp.zeros_like(acc_ref)

        acc_ref[...] += lax.dot_general(a_ref[...].astype(BF16), b_ref[...].astype(BF16), dims,
                                        preferred_element_type=F32)

        @pl.when(kk == nk - 1)
        def _():
            o_ref[...] = acc_ref[...].astype(o_ref.dtype)

    return pl.pallas_call(
        body,
        name=name,
        grid=(m // tm, n // tn, nk),
        in_specs=[a_spec, b_spec],
        out_specs=pl.BlockSpec((tm, tn), lambda i, j, kk: (i, j)),
        out_shape=jax.ShapeDtypeStruct((m, n), out_dtype),
        scratch_shapes=[pltpu.VMEM((tm, tn), F32)],
        compiler_params=pltpu.CompilerParams(dimension_semantics=('parallel', 'parallel', 'arbitrary'),
                                             vmem_limit_bytes=VMEM_LIMIT_BYTES),
    )(a, b)


@functools.partial(jax.custom_vjp, nondiff_argnums=(2,))
def _linear(x, w, name):
    return _matmul(x, w, 'nn', name + '_fwd')


def _linear_fwd(x, w, name):
    return _matmul(x, w, 'nn', name + '_fwd'), (x, w)


def _linear_bwd(name, res, dy):
    x, w = res
    return _matmul(dy, w, 'nt', name + '_dx'), _matmul(x, dy, 'tn', name + '_dw')


_linear.defvjp(_linear_fwd, _linear_bwd)


@functools.partial(jax.custom_vjp, nondiff_argnums=(3,))
def _linear_tap(x, w, tap, name):
    del tap
    return _matmul(x, w, 'nn', name + '_fwd')


def _linear_tap_fwd(x, w, tap, name):
    del tap
    return _matmul(x, w, 'nn', name + '_fwd'), (x, w)


def _linear_tap_bwd(name, res, dy):
    x, w = res
    return _matmul(dy, w, 'nt', name + '_dx'), jnp.zeros_like(w), _matmul(x, dy, 'tn', name + '_dw')


_linear_tap.defvjp(_linear_tap_fwd, _linear_tap_bwd)


SCAN_ROWS = 128
SCAN_LANES = 512


def _scan_lanes(n):
    return SCAN_LANES if n % SCAN_LANES == 0 else n


def _interleave(re, im):
    r, n = re.shape
    lanes = _scan_lanes(n)
    return jnp.stack([re.reshape(r, n // lanes, lanes), im.reshape(r, n // lanes, lanes)], axis=2).reshape(r, 2 * n)


def _scan_call(a, x, h, n_ctx_blocks, ctx_first, reverse_rows, name):
    t, n2 = x.shape
    n = n2 // 2
    nblk = t // SCAN_ROWS
    nlat = nblk - n_ctx_blocks
    lanes = _scan_lanes(n)
    with_da = h is not None

    def block_of(i):
        if ctx_first:
            first_n, first_0, second_0, second_n = n_ctx_blocks, 0, n_ctx_blocks, nlat
        else:
            first_n, first_0, second_0, second_n = nlat, n_ctx_blocks, 0, n_ctx_blocks
        if reverse_rows:
            in_first = first_0 + first_n - 1 - i
            in_second = second_0 + second_n - 1 - (i - first_n)
        else:
            in_first = first_0 + i
            in_second = second_0 + (i - first_n)
        return jnp.where(i < first_n, in_first, in_second)

    groups = SCAN_ROWS // SUBLANES
    first_row = SUBLANES - 1 if reverse_rows else 0
    to_previous = SUBLANES - 1 if reverse_rows else 1

    def body(*refs):
        if with_da:
            a_ref, x_ref, h_ref, o_ref, da_ref, st_ref = refs
        else:
            a_ref, x_ref, o_ref, st_ref = refs

        @pl.when(pl.program_id(1) == 0)
        def _():
            st_ref[...] = jnp.zeros_like(st_ref)
            if with_da:
                da_ref[...] = jnp.zeros_like(da_ref)

        row_id = lax.broadcasted_iota(jnp.int32, (SUBLANES, lanes), 0)
        behind = (SUBLANES - 1 - row_id) if reverse_rows else row_id

        def cmul(pr, pi, qr, qi):
            return pr * qr - pi * qi, pr * qi + pi * qr

        a1 = (jnp.broadcast_to(a_ref[0:1, :], (SUBLANES, lanes)), jnp.broadcast_to(a_ref[1:2, :], (SUBLANES, lanes)))
        a2 = cmul(*a1, *a1)
        a4 = cmul(*a2, *a2)
        pw = a1
        for bit, ak in ((1, a1), (2, a2), (4, a4)):
            nxt = cmul(*pw, *ak)
            pw = (jnp.where((behind & bit) != 0, nxt[0], pw[0]), jnp.where((behind & bit) != 0, nxt[1], pw[1]))

        def group(g, carry):
            in_r, in_i = carry
            gi = (groups - 1 - g) if reverse_rows else g
            start = pl.multiple_of(gi * SUBLANES, SUBLANES)
            xr = x_ref[pl.ds(start, SUBLANES), pl.ds(0, lanes)]
            xi = x_ref[pl.ds(start, SUBLANES), pl.ds(lanes, lanes)]
            for k, ak in ((1, a1), (2, a2), (4, a4)):
                shift = (SUBLANES - k) if reverse_rows else k
                pr = jnp.where(behind >= k, pltpu.roll(xr, shift, axis=0), 0.0)
                pi = jnp.where(behind >= k, pltpu.roll(xi, shift, axis=0), 0.0)
                qr, qi = cmul(*ak, pr, pi)
                xr, xi = xr + qr, xi + qi
            cr, ci = cmul(*pw, in_r, in_i)
            out_r, out_i = xr + cr, xi + ci
            last = 0 if reverse_rows else SUBLANES - 1
            sr = jnp.broadcast_to(out_r[last:last + 1, :], (SUBLANES, lanes))
            si = jnp.broadcast_to(out_i[last:last + 1, :], (SUBLANES, lanes))
            o_ref[pl.ds(start, SUBLANES), pl.ds(0, lanes)] = out_r
            o_ref[pl.ds(start, SUBLANES), pl.ds(lanes, lanes)] = out_i
            if with_da:
                pr = jnp.where(row_id == first_row, in_r, pltpu.roll(out_r, to_previous, axis=0))
                pi = jnp.where(row_id == first_row, in_i, pltpu.roll(out_i, to_previous, axis=0))
                hr = h_ref[pl.ds(start, SUBLANES), pl.ds(0, lanes)]
                hi = h_ref[pl.ds(start, SUBLANES), pl.ds(lanes, lanes)]
                da_ref[0:SUBLANES, :] += hr * pr + hi * pi
                da_ref[SUBLANES:2 * SUBLANES, :] += hr * pi - hi * pr
            return sr, si

        sr, si = lax.fori_loop(0, groups, group, (st_ref[0], st_ref[1]))
        st_ref[0] = sr
        st_ref[1] = si

    row_spec = pl.BlockSpec((SCAN_ROWS, 2 * lanes), lambda j, i: (block_of(i), j))
    in_specs = [pl.BlockSpec((2, lanes), lambda j, i: (0, j)), row_spec]
    out_specs = [row_spec]
    out_shape = [jax.ShapeDtypeStruct((t, n2), F32)]
    operands = [a, x]
    if with_da:
        in_specs.append(row_spec)
        operands.append(h)
        out_specs.append(pl.BlockSpec((2 * SUBLANES, lanes), lambda j, i: (0, j)))
        out_shape.append(jax.ShapeDtypeStruct((2 * SUBLANES, n), F32))
    return pl.pallas_call(
        body,
        name=name,
        grid=(n // lanes, nblk),
        in_specs=in_specs,
        out_specs=out_specs,
        out_shape=out_shape,
        scratch_shapes=[pltpu.VMEM((2, SUBLANES, lanes), F32)],
        compiler_params=pltpu.CompilerParams(dimension_semantics=('parallel', 'arbitrary')),
    )(*operands)


@functools.partial(jax.custom_vjp, nondiff_argnums=(2, 3, 4))
def _s5_scan(a, bu, t_ctx, reverse, name):
    return _scan_call(a, bu, None, t_ctx // SCAN_ROWS, True, reverse, name + '_fwd')[0]


def _s5_scan_fwd(a, bu, t_ctx, reverse, name):
    h = _scan_call(a, bu, None, t_ctx // SCAN_ROWS, True, reverse, name + '_fwd')[0]
    return h, (a, h)


def _s5_scan_bwd(t_ctx, reverse, name, res, dh):
    a, h = res
    a_conj = a * jnp.array([[1.0], [-1.0]], F32)
    g, da = _scan_call(a_conj, dh, h, t_ctx // SCAN_ROWS, False, not reverse, name + '_bwd')
    da = jnp.stack([jnp.sum(da[:SUBLANES], axis=0), jnp.sum(da[SUBLANES:], axis=0)])
    return da, g


_s5_scan.defvjp(_s5_scan_fwd, _s5_scan_bwd)


FFN_LANES = 128
CONV_TAPS = FFN_CONV * FFN_CONV
CONV_W_ROWS = 16
GELU_K = 0.7978845608028654
GELU_C = 0.044715


def _gelu_and_slope(x):
    x2 = x * x
    th = jnp.tanh(GELU_K * (x + GELU_C * x * x2))
    cdf = 0.5 * (1.0 + th)
    slope = cdf + 0.5 * x * (1.0 - th * th) * (GELU_K * (1.0 + 3.0 * GELU_C * x2))
    return x * cdf, slope


def _band_rows(t_lat):
    return _first_divisor(t_lat // GRID_W, (8, 4, 2, 1)) * GRID_W


def _shifted_ctx(x):
    n = x.shape[0]
    row = lax.broadcasted_iota(jnp.int32, x.shape, 0)
    left = jnp.where(row == 0, 0.0, pltpu.roll(x, 1, axis=0))
    right = jnp.where(row == n - 1, 0.0, pltpu.roll(x, n - 1, axis=0))
    return left, x, right


def _shifted_band(src_ref, bidx, n_bands, band, t_ctx, t):
    start = pl.multiple_of(t_ctx + bidx * band, GRID_W)
    top_start = pl.multiple_of(jnp.maximum(start - GRID_W, 0), SUBLANES)
    bot_start = pl.multiple_of(jnp.minimum(start + band, t - GRID_W), SUBLANES)
    top = jnp.where(bidx > 0, src_ref[pl.ds(top_start, GRID_W), :], 0.0)
    bot = jnp.where(bidx < n_bands - 1, src_ref[pl.ds(bot_start, GRID_W), :], 0.0)
    ext = jnp.concatenate([top, src_ref[pl.ds(start, band), :], bot], axis=0)
    n_ext = band + 2 * GRID_W
    col = lax.broadcasted_iota(jnp.int32, ext.shape, 0) & (GRID_W - 1)
    left = jnp.where(col == 0, 0.0, pltpu.roll(ext, 1, axis=0))
    right = jnp.where(col == GRID_W - 1, 0.0, pltpu.roll(ext, n_ext - 1, axis=0))
    return left, ext, right


def _conv_sum(shifted, w_rows, band, flip):
    acc = None
    for i in range(FFN_CONV):
        for j in range(FFN_CONV):
            w = w_rows[(FFN_CONV - 1 - i) * FFN_CONV + (FFN_CONV - 1 - j)] if flip else w_rows[i * FFN_CONV + j]
            term = shifted[j][i * GRID_W:i * GRID_W + band] * w
            acc = term if acc is None else acc + term
    return acc


def _conv_operand(conv_w, conv_b):
    f = conv_b.shape[0]
    return jnp.concatenate([conv_w.reshape(CONV_TAPS, f), conv_b[None],
                            jnp.zeros((CONV_W_ROWS - CONV_TAPS - 1, f), F32)], axis=0)


def _ffn_mid_fwd(gv, w16, t_ctx, name):
    t, f2 = gv.shape
    f = f2 // 2
    lanes = FFN_LANES
    nf = f // lanes
    band = _band_rows(t - t_ctx)
    n_bands = (t - t_ctx) // band
    mid = (FFN_CONV // 2) * FFN_CONV

    def body(g_ref, v_ref, w_ref, o_ref):
        w_rows = [w_ref[k:k + 1, :] for k in range(CONV_TAPS)]
        bias = w_ref[CONV_TAPS:CONV_TAPS + 1, :]
        if t_ctx:
            sh = _shifted_ctx(g_ref[0:t_ctx, :])
            pre = sh[0] * w_rows[mid] + sh[1] * w_rows[mid + 1] + sh[2] * w_rows[mid + 2] + bias
            o_ref[0:t_ctx, :] = (_gelu_and_slope(pre)[0] * v_ref[0:t_ctx, :]).astype(o_ref.dtype)

        def one_band(b, carry):
            start = pl.multiple_of(t_ctx + b * band, GRID_W)
            pre = _conv_sum(_shifted_band(g_ref, b, n_bands, band, t_ctx, t), w_rows, band, False) + bias
            o_ref[pl.ds(start, band), :] = (_gelu_and_slope(pre)[0] * v_ref[pl.ds(start, band), :]).astype(o_ref.dtype)
            return carry

        lax.fori_loop(0, n_bands, one_band, 0)

    return pl.pallas_call(
        body, name=name,
        grid=(nf,),
        in_specs=[pl.BlockSpec((t, lanes), lambda j: (0, j)),
                  pl.BlockSpec((t, lanes), lambda j: (0, nf + j)),
                  pl.BlockSpec((CONV_W_ROWS, lanes), lambda j: (0, j))],
        out_specs=pl.BlockSpec((t, lanes), lambda j: (0, j)),
        out_shape=jax.ShapeDtypeStruct((t, f), BF16),
        compiler_params=pltpu.CompilerParams(dimension_semantics=('parallel',), vmem_limit_bytes=VMEM_LIMIT_BYTES),
    )(gv, gv, w16)


def _ffn_mid_bwd(d_act, gv, w16, t_ctx, name):
    t, f2 = gv.shape
    f = f2 // 2
    lanes = FFN_LANES
    nf = f // lanes
    band = _band_rows(t - t_ctx)
    n_bands = (t - t_ctx) // band
    mid = (FFN_CONV // 2) * FFN_CONV
    n_acc = CONV_TAPS + 1

    def tile_sum(x):
        return jnp.sum(x.reshape(x.shape[0] // SUBLANES, SUBLANES, lanes), axis=0)

    def body(da_ref, g_ref, v_ref, w_ref, dg_ref, dv_ref, dw_ref, dp_ref):
        w_rows = [w_ref[k:k + 1, :] for k in range(CONV_TAPS)]
        bias = w_ref[CONV_TAPS:CONV_TAPS + 1, :]
        acc = [jnp.zeros((SUBLANES, lanes), F32) for _ in range(n_acc)]
        if t_ctx:
            sh = _shifted_ctx(g_ref[0:t_ctx, :])
            pre = sh[0] * w_rows[mid] + sh[1] * w_rows[mid + 1] + sh[2] * w_rows[mid + 2] + bias
            ge, slope = _gelu_and_slope(pre)
            da = da_ref[0:t_ctx, :]
            dv_ref[0:t_ctx, :] = (da * ge).astype(dv_ref.dtype)
            dpre = da * v_ref[0:t_ctx, :] * slope
            dp_ref[0:t_ctx, :] = dpre
            for j in range(FFN_CONV):
                acc[mid + j] = acc[mid + j] + tile_sum(sh[j] * dpre)
            acc[CONV_TAPS] = acc[CONV_TAPS] + tile_sum(dpre)
            back = _shifted_ctx(dpre)
            dg_ref[0:t_ctx, :] = (back[2] * w_rows[mid] + back[1] * w_rows[mid + 1]
                                  + back[0] * w_rows[mid + 2]).astype(dg_ref.dtype)

        def first_pass(b, acc):
            acc = list(acc)
            start = pl.multiple_of(t_ctx + b * band, GRID_W)
            sh = _shifted_band(g_ref, b, n_bands, band, t_ctx, t)
            ge, slope = _gelu_and_slope(_conv_sum(sh, w_rows, band, False) + bias)
            da = da_ref[pl.ds(start, band), :]
            dv_ref[pl.ds(start, band), :] = (da * ge).astype(dv_ref.dtype)
            dpre = da * v_ref[pl.ds(start, band), :] * slope
            dp_ref[pl.ds(start, band), :] = dpre
            for i in range(FFN_CONV):
                for j in range(FFN_CONV):
                    k = i * FFN_CONV + j
                    acc[k] = acc[k] + tile_sum(sh[j][i * GRID_W:i * GRID_W + band] * dpre)
            acc[CONV_TAPS] = acc[CONV_TAPS] + tile_sum(dpre)
            return tuple(acc)

        acc = lax.fori_loop(0, n_bands, first_pass, tuple(acc))
        for k in range(n_acc):
            dw_ref[k * SUBLANES:(k + 1) * SUBLANES, :] = acc[k]

        def second_pass(b, carry):
            start = pl.multiple_of(t_ctx + b * band, GRID_W)
            dg_ref[pl.ds(start, band), :] = _conv_sum(_shifted_band(dp_ref, b, n_bands, band, t_ctx, t),
                                                      w_rows, band, True).astype(dg_ref.dtype)
            return carry

        lax.fori_loop(0, n_bands, second_pass, 0)

    col = pl.BlockSpec((t, lanes), lambda j: (0, j))
    return pl.pallas_call(
        body, name=name,
        grid=(nf,),
        in_specs=[col, col, pl.BlockSpec((t, lanes), lambda j: (0, nf + j)),
                  pl.BlockSpec((CONV_W_ROWS, lanes), lambda j: (0, j))],
        out_specs=[col, col, pl.BlockSpec((n_acc * SUBLANES, lanes), lambda j: (0, j))],
        out_shape=[jax.ShapeDtypeStruct((t, f), BF16), jax.ShapeDtypeStruct((t, f), BF16),
                   jax.ShapeDtypeStruct((n_acc * SUBLANES, f), F32)],
        scratch_shapes=[pltpu.VMEM((t, lanes), F32)],
        compiler_params=pltpu.CompilerParams(dimension_semantics=('parallel',), vmem_limit_bytes=VMEM_LIMIT_BYTES),
    )(d_act, gv, gv, w16)


def _gate_value(up, layout):
    per_dev, per_dev_pad, f = layout
    if per_dev == per_dev_pad:
        return up
    return jnp.concatenate([_take_cols(up, per_dev, per_dev_pad, 0, f), _take_cols(up, per_dev, per_dev_pad, f, 2 * f)],
                           axis=1)


def _ffn_forward(hm, w_up, conv_w, conv_b, w_down, t_ctx, layout, name):
    up = _matmul(hm, w_up, 'nn', name + '_up_fwd')
    gv = _gate_value(up, layout)
    w16 = _conv_operand(conv_w, conv_b)
    act = _ffn_mid_fwd(gv, w16, t_ctx, name + '_mid_fwd')
    return _matmul(act, w_down, 'nn', name + '_down_fwd'), (hm, w_up, gv, w16, w_down, act)


@functools.partial(jax.custom_vjp, nondiff_argnums=(7, 8, 9))
def _ffn_block(hm, w_up, tap_up, conv_w, conv_b, w_down, tap_down, t_ctx, layout, name):
    del tap_up, tap_down
    return _ffn_forward(hm, w_up, conv_w, conv_b, w_down, t_ctx, layout, name)[0]


def _ffn_block_fwd(hm, w_up, tap_up, conv_w, conv_b, w_down, tap_down, t_ctx, layout, name):
    del tap_up, tap_down
    return _ffn_forward(hm, w_up, conv_w, conv_b, w_down, t_ctx, layout, name)


def _ffn_block_bwd(t_ctx, layout, name, res, d_out):
    hm, w_up, gv, w16, w_down, act = res
    f = act.shape[1]
    d_act = _matmul(d_out, w_down, 'nt', name + '_down_dx')
    d_w_down = _matmul(act, d_out, 'tn', name + '_down_dw')
    d_gate, d_val, d_w = _ffn_mid_bwd(d_act, gv, w16, t_ctx, name + '_mid_bwd')
    d_gv = jnp.concatenate([d_gate, d_val], axis=1)
    if layout[0] == layout[1]:
        d_up = d_gv
    else:
        d_up = jax.vjp(lambda u: _gate_value(u, layout), jnp.zeros((gv.shape[0], w_up.shape[1]), d_gv.dtype))[1](d_gv)[0]
    d_hm = _matmul(d_up, w_up, 'nt', name + '_up_dx')
    d_w_up = _matmul(hm, d_up, 'tn', name + '_up_dw')
    d_w = jnp.sum(d_w.reshape(CONV_TAPS + 1, SUBLANES, f), axis=1)
    return (d_hm, jnp.zeros_like(w_up), d_w_up, d_w[:CONV_TAPS].reshape(FFN_CONV, FFN_CONV, f), d_w[CONV_TAPS],
            jnp.zeros_like(w_down), d_w_down)


_ffn_block.defvjp(_ffn_block_fwd, _ffn_block_bwd)


LN_ROWS = 128
MOD_ROWS = 8


def _normalised(h, y, gate, alpha):
    r = alpha * h + gate * y
    xc = r - jnp.mean(r, axis=-1, keepdims=True)
    rstd = lax.rsqrt(jnp.mean(xc * xc, axis=-1, keepdims=True) + LN_EPS)
    return xc * rstd, rstd


def _mod_row(m_ref, k, is_ctx):
    lat = m_ref[3 + k:4 + k, :]
    return lat if is_ctx is None else jnp.where(is_ctx, m_ref[k:k + 1, :], lat)


def _res_ln_fwd_call(h, y, mods, t_ctx, alpha, with_mod, name):
    t, d = h.shape
    n_ctx_tiles = t_ctx // LN_ROWS

    def body(h_ref, y_ref, m_ref, hn_ref, *rest):
        is_ctx = (pl.program_id(0) < n_ctx_tiles) if n_ctx_tiles else None
        xhat, _ = _normalised(h_ref[...], y_ref[...], _mod_row(m_ref, 0, is_ctx), alpha)
        hn = xhat * m_ref[6:7, :] + m_ref[7:8, :]
        hn_ref[...] = hn
        if with_mod:
            rest[0][...] = hn * (1.0 + _mod_row(m_ref, 2, is_ctx)) + _mod_row(m_ref, 1, is_ctx)

    blk = pl.BlockSpec((LN_ROWS, d), lambda i: (i, 0))
    n_out = 2 if with_mod else 1
    return pl.pallas_call(
        body, name=name,
        grid=(t // LN_ROWS,),
        in_specs=[blk, blk, pl.BlockSpec((MOD_ROWS, d), lambda i: (0, 0))],
        out_specs=[blk] * n_out,
        out_shape=[jax.ShapeDtypeStruct((t, d), F32)] * n_out,
        compiler_params=pltpu.CompilerParams(dimension_semantics=('parallel',)),
    )(h, y, mods)


def _res_ln_bwd_call(h, y, mods, d_hn, d_hm, t_ctx, alpha, name):
    t, d = h.shape
    n_ctx_tiles = t_ctx // LN_ROWS
    with_mod = d_hm is not None

    def tile_sum(x):
        return jnp.sum(x.reshape(LN_ROWS // SUBLANES, SUBLANES, d), axis=0)

    def body(*refs):
        if with_mod:
            h_ref, y_ref, m_ref, dhn_ref, dhm_ref, dh_ref, dy_ref, acc_ref = refs
        else:
            h_ref, y_ref, m_ref, dhn_ref, dh_ref, dy_ref, acc_ref = refs

        @pl.when(pl.program_id(0) == 0)
        def _():
            acc_ref[...] = jnp.zeros_like(acc_ref)

        is_ctx = (pl.program_id(0) < n_ctx_tiles) if n_ctx_tiles else None
        gate = _mod_row(m_ref, 0, is_ctx)
        y = y_ref[...]
        xhat, rstd = _normalised(h_ref[...], y, gate, alpha)
        ln_g = m_ref[6:7, :]
        dhn = dhn_ref[...]
        base = 3 * SUBLANES if is_ctx is None else jnp.where(is_ctx, 0, 3 * SUBLANES)

        def add_to(row, part):
            if isinstance(row, int):
                acc_ref[row:row + SUBLANES, :] += part
            else:
                acc_ref[pl.ds(pl.multiple_of(row, SUBLANES), SUBLANES), :] += part

        if with_mod:
            dhm = dhm_ref[...]
            hn = xhat * ln_g + m_ref[7:8, :]
            add_to(base + SUBLANES, tile_sum(dhm))
            add_to(base + 2 * SUBLANES, tile_sum(dhm * hn))
            dhn = dhn + dhm * (1.0 + _mod_row(m_ref, 2, is_ctx))
        add_to(6 * SUBLANES, tile_sum(dhn * xhat))
        add_to(7 * SUBLANES, tile_sum(dhn))
        dx = dhn * ln_g
        dr = rstd * (dx - jnp.mean(dx, axis=-1, keepdims=True) - xhat * jnp.mean(dx * xhat, axis=-1, keepdims=True))
        dh_ref[...] = alpha * dr
        dy_ref[...] = gate * dr
        add_to(base, tile_sum(dr * y))

    blk = pl.BlockSpec((LN_ROWS, d), lambda i: (i, 0))
    operands = [h, y, mods, d_hn] + ([d_hm] if with_mod else [])
    return pl.pallas_call(
        body, name=name,
        grid=(t // LN_ROWS,),
        in_specs=[blk, blk, pl.BlockSpec((MOD_ROWS, d), lambda i: (0, 0)), blk] + ([blk] if with_mod else []),
        out_specs=[blk, blk, pl.BlockSpec((MOD_ROWS * SUBLANES, d), lambda i: (0, 0))],
        out_shape=[jax.ShapeDtypeStruct((t, d), F32), jax.ShapeDtypeStruct((t, d), F32),
                   jax.ShapeDtypeStruct((MOD_ROWS * SUBLANES, d), F32)],
        compiler_params=pltpu.CompilerParams(dimension_semantics=('arbitrary',)),
    )(*operands)


@functools.partial(jax.custom_vjp, nondiff_argnums=(3, 4, 5, 6))
def _res_ln_mod(h, y, mods, t_ctx, alpha, with_mod, name):
    return tuple(_res_ln_fwd_call(h, y, mods, t_ctx, alpha, with_mod, name + '_fwd'))


def _res_ln_mod_fwd(h, y, mods, t_ctx, alpha, with_mod, name):
    return tuple(_res_ln_fwd_call(h, y, mods, t_ctx, alpha, with_mod, name + '_fwd')), (h, y, mods)


def _res_ln_mod_bwd(t_ctx, alpha, with_mod, name, res, cts):
    h, y, mods = res
    d_h, d_y, acc = _res_ln_bwd_call(h, y, mods, cts[0], cts[1] if with_mod else None, t_ctx, alpha, name + '_bwd')
    return d_h, d_y, jnp.sum(acc.reshape(MOD_ROWS, SUBLANES, h.shape[1]), axis=1)


_res_ln_mod.defvjp(_res_ln_mod_fwd, _res_ln_mod_bwd)


def _window(ref, j, rows, cols, axis):
    if axis == 0:
        return ref.at[pl.ds(j * rows, rows), :]
    return ref.at[:, pl.ds(j * cols, cols)]


def _all_gather(block, axis, name):
    rows, cols = block.shape

    def body(x_ref, out_ref, send_sems, recv_sems, local_sem):
        x, y, c = _my_coords()
        me, sibling = (x, y, c), (x, y, 1 - c)
        chips = [(1 - x, y), (x, 1 - y), (1 - x, 1 - y)]

        def win(px, py, pc):
            return _window(out_ref, 4 * px + 2 * py + pc, rows, cols, axis)

        def copy(k, blk, to, src=None):
            return pltpu.make_async_remote_copy(
                src_ref=win(*blk) if src is None else src, dst_ref=win(*blk),
                send_sem=send_sems.at[k], recv_sem=recv_sems.at[k], device_id=to, device_id_type=MESH)

        mine = pltpu.make_async_copy(x_ref, win(*me), local_sem)
        mine.start()
        first = [copy(0, me, sibling, src=x_ref)]
        first += [copy(1 + j, me, (*chip, c), src=x_ref) for j, chip in enumerate(chips)]
        for cp in first:
            cp.start()
        passed = [copy(4 + j, (*chip, c), sibling) for j, chip in enumerate(chips)]
        for j, chip in enumerate(chips):
            copy(1 + j, (*chip, c), me).wait_recv()
            passed[j].start()
        copy(0, sibling, me).wait_recv()
        for j, chip in enumerate(chips):
            copy(4 + j, (*chip, 1 - c), me).wait_recv()
        for cp in first + passed:
            cp.wait_send()
        mine.wait()

    out_shape = (N_DEV * rows, cols) if axis == 0 else (rows, N_DEV * cols)
    return pl.pallas_call(
        body, name=name,
        out_shape=jax.ShapeDtypeStruct(out_shape, block.dtype),
        in_specs=[ANY], out_specs=ANY,
        scratch_shapes=[pltpu.SemaphoreType.DMA((7,)), pltpu.SemaphoreType.DMA((7,)), pltpu.SemaphoreType.DMA],
    )(block)


AG_COLLECTIVE_ID = 1


def _all_gather_sequencer(block, axis, name):
    rows, cols = block.shape
    out_shape = (N_DEV * rows, cols) if axis == 0 else (rows, N_DEV * cols)
    x_ref = jax.new_ref(block, memory_space=pltpu.MemorySpace.HBM)
    out_ref = jax.empty_ref(jax.ShapeDtypeStruct(out_shape, block.dtype), memory_space=pltpu.MemorySpace.HBM)

    @pl.kernel(mesh=plsc.ScalarSubcoreMesh(axis_name='sequencer', num_cores=1), name=name,
               scratch_types=(pltpu.SemaphoreType.DMA((7,)), pltpu.SemaphoreType.DMA((7,)), pltpu.SemaphoreType.DMA),
               compiler_params=pltpu.CompilerParams(collective_id=AG_COLLECTIVE_ID))
    def launch(send_sems, recv_sems, local_sem):
        x, y, c = _my_coords()
        me, sibling = (x, y, c), (x, y, 1 - c)
        chips = [(1 - x, y), (x, 1 - y), (1 - x, 1 - y)]
        barrier = pltpu.get_barrier_semaphore()
        for peer in [sibling] + [(*chip, c) for chip in chips]:
            pl.semaphore_signal(barrier, inc=1, device_id=peer, device_id_type=MESH)
        pl.semaphore_wait(barrier, 1 + len(chips))

        def win(px, py, pc):
            return _window(out_ref, 4 * px + 2 * py + pc, rows, cols, axis)

        def copy(k, blk, to, src=None):
            return pltpu.make_async_remote_copy(
                src_ref=win(*blk) if src is None else src, dst_ref=win(*blk),
                send_sem=send_sems.at[k], recv_sem=recv_sems.at[k], device_id=to, device_id_type=MESH)

        mine = pltpu.make_async_copy(x_ref, win(*me), local_sem)
        mine.start()
        first = [copy(0, me, sibling, src=x_ref)]
        first += [copy(1 + j, me, (*chip, c), src=x_ref) for j, chip in enumerate(chips)]
        for cp in first:
            cp.start()
        passed = [copy(4 + j, (*chip, c), sibling) for j, chip in enumerate(chips)]
        for j, chip in enumerate(chips):
            copy(1 + j, (*chip, c), me).wait_recv()
            passed[j].start()
        copy(0, sibling, me).wait_recv()
        for j, chip in enumerate(chips):
            copy(4 + j, (*chip, 1 - c), me).wait_recv()
        for cp in first + passed:
            cp.wait_send()
        mine.wait()

    launch()
    return out_ref[...]


PAIR_COLLECTIVE_ID = 2
CHIPS_COLLECTIVE_ID = 3


def _handshake(peers):
    barrier = pltpu.get_barrier_semaphore()
    for peer in peers:
        pl.semaphore_signal(barrier, inc=1, device_id=peer, device_id_type=MESH)
    pl.semaphore_wait(barrier, len(peers))


def _launch_on_sequencer(body, operand, result, scratch_types, collective_id, name):
    operand_ref = jax.new_ref(operand, memory_space=pltpu.MemorySpace.HBM)
    result_ref = jax.empty_ref(result, memory_space=pltpu.MemorySpace.HBM)
    pl.kernel(functools.partial(body, operand_ref, result_ref),
              mesh=plsc.ScalarSubcoreMesh(axis_name='sequencer', num_cores=1), name=name,
              scratch_types=scratch_types, compiler_params=pltpu.CompilerParams(collective_id=collective_id))()
    return result_ref[...]


def _exchange_pair(full, rows, cols, axis, sequencer, name):
    def body(g_ref, land_ref, send_sems, recv_sems):
        x, y, c = _my_coords()
        if sequencer:
            _handshake([(x, y, 1 - c)])
        copies = []
        for k in range(N_CHIP):
            j = 2 * k + (1 - c)
            copies.append(pltpu.make_async_remote_copy(
                src_ref=_window(g_ref, j, rows, cols, axis), dst_ref=land_ref.at[k],
                send_sem=send_sems.at[k], recv_sem=recv_sems.at[k], device_id=(x, y, 1 - c), device_id_type=MESH))
        for cp in copies:
            cp.start()
        for cp in copies:
            cp.wait()

    result = jax.ShapeDtypeStruct((N_CHIP, rows, cols), full.dtype)
    sems = [pltpu.SemaphoreType.DMA((N_CHIP,)), pltpu.SemaphoreType.DMA((N_CHIP,))]
    if sequencer:
        return _launch_on_sequencer(body, full, result, sems, PAIR_COLLECTIVE_ID, name)
    return pl.pallas_call(body, name=name, out_shape=result, in_specs=[ANY], out_specs=ANY, scratch_shapes=sems)(full)


def _elementwise_tiles(rows, cols):
    tc = _first_divisor(cols, (1024, 512, 384, 256, 128))
    tr = _first_divisor(rows, (512, 256, 128, 64, 32, 16, 8))
    return tr, tc


def _pair_add(full, land, rows, cols, axis, wire_dtype, name):
    tr, tc = _elementwise_tiles(rows, cols)
    c_arr = jnp.reshape(lax.axis_index('c'), (1,)).astype(jnp.int32)

    def full_map(k, i, j, c_ref):
        blk = 2 * k + c_ref[0]
        if axis == 0:
            return (blk * (rows // tr) + i, j)
        return (i, blk * (cols // tc) + j)

    def body(c_ref, g_ref, l_ref, o_ref):
        del c_ref
        o_ref[0] = (g_ref[...] + l_ref[0]).astype(o_ref.dtype)

    return pl.pallas_call(
        body, name=name,
        grid_spec=pltpu.PrefetchScalarGridSpec(
            num_scalar_prefetch=1,
            grid=(N_CHIP, rows // tr, cols // tc),
            in_specs=[pl.BlockSpec((tr, tc), full_map),
                      pl.BlockSpec((1, tr, tc), lambda k, i, j, c_ref: (k, i, j))],
            out_specs=pl.BlockSpec((1, tr, tc), lambda k, i, j, c_ref: (k, i, j)),
        ),
        out_shape=jax.ShapeDtypeStruct((N_CHIP, rows, cols), wire_dtype),
        compiler_params=pltpu.CompilerParams(dimension_semantics=('parallel', 'parallel', 'parallel')),
    )(c_arr, full, land)


def _exchange_chips(sums, sequencer, name):
    _, rows, cols = sums.shape

    def body(s_ref, land_ref, send_sems, recv_sems, local_sem):
        x, y, c = _my_coords()
        my_chip = 2 * x + y
        chips = [(1 - x, y), (x, 1 - y), (1 - x, 1 - y)]
        if sequencer:
            _handshake([(px, py, c) for px, py in chips])
        mine = pltpu.make_async_copy(s_ref.at[my_chip], land_ref.at[my_chip], local_sem)
        mine.start()
        sends = []
        for t, (px, py) in enumerate(chips):
            sends.append(pltpu.make_async_remote_copy(
                src_ref=s_ref.at[2 * px + py], dst_ref=land_ref.at[my_chip],
                send_sem=send_sems.at[t], recv_sem=recv_sems.at[t], device_id=(px, py, c), device_id_type=MESH))
        for cp in sends:
            cp.start()
        for t, (px, py) in enumerate(chips):
            pltpu.make_async_remote_copy(
                src_ref=s_ref.at[my_chip], dst_ref=land_ref.at[2 * px + py],
                send_sem=send_sems.at[t], recv_sem=recv_sems.at[t], device_id=(px, py, c),
                device_id_type=MESH).wait_recv()
        for cp in sends:
            cp.wait_send()
        mine.wait()

    result = jax.ShapeDtypeStruct((N_CHIP, rows, cols), sums.dtype)
    sems = [pltpu.SemaphoreType.DMA((3,)), pltpu.SemaphoreType.DMA((3,)), pltpu.SemaphoreType.DMA]
    if sequencer:
        return _launch_on_sequencer(body, sums, result, sems, CHIPS_COLLECTIVE_ID, name)
    return pl.pallas_call(body, name=name, out_shape=result, in_specs=[ANY], out_specs=ANY, scratch_shapes=sems)(sums)


def _reduce_scatter_slots(full, rows, cols, axis, wire_dtype, sequencer, name):
    land = _exchange_pair(full, rows, cols, axis, sequencer, name + '_pair')
    sums = _pair_add(full, land, rows, cols, axis, wire_dtype, name + '_add')
    return _exchange_chips(sums, sequencer, name + '_chips')


def _sum_slots(slots, name):
    n_slots, rows, cols = slots.shape
    tr, tc = _elementwise_tiles(rows, cols)

    def body(s_ref, o_ref):
        g = s_ref[0]
        for s in range(1, n_slots):
            g = g + s_ref[s]
        o_ref[...] = g

    return pl.pallas_call(
        body, name=name,
        grid=(rows // tr, cols // tc),
        in_specs=[pl.BlockSpec((n_slots, tr, tc), lambda i, j: (0, i, j))],
        out_specs=pl.BlockSpec((tr, tc), lambda i, j: (i, j)),
        out_shape=jax.ShapeDtypeStruct((rows, cols), F32),
        compiler_params=pltpu.CompilerParams(dimension_semantics=('parallel', 'parallel')),
    )(slots)


def _sum_adamw(slots, w, m, v, name):
    n_slots, rows, cols = slots.shape
    tr, tc = _elementwise_tiles(rows, cols)
    c1 = 1.0 - ADAM_B1 ** ADAM_STEP
    c2 = 1.0 - ADAM_B2 ** ADAM_STEP

    def body(s_ref, w_ref, m_ref, v_ref, g_out, d_out, m_out, v_out):
        g = s_ref[0].astype(F32)
        for s in range(1, n_slots):
            g = g + s_ref[s].astype(F32)
        m_new = ADAM_B1 * m_ref[...] + (1.0 - ADAM_B1) * g
        v_new = ADAM_B2 * v_ref[...] + (1.0 - ADAM_B2) * (g * g)
        m_hat = m_new / c1
        v_hat = v_new / c2
        g_out[...] = g
        d_out[...] = -ADAM_LR * (m_hat / (jnp.sqrt(v_hat) + ADAM_EPS) + ADAM_WD * w_ref[...])
        m_out[...] = m_new
        v_out[...] = v_new

    blk = pl.BlockSpec((tr, tc), lambda i, j: (i, j))
    shape = jax.ShapeDtypeStruct((rows, cols), F32)
    return pl.pallas_call(
        body, name=name,
        grid=(rows // tr, cols // tc),
        in_specs=[pl.BlockSpec((n_slots, tr, tc), lambda i, j: (0, i, j)), blk, blk, blk],
        out_specs=[blk, blk, blk, blk],
        out_shape=[shape, shape, shape, shape],
        compiler_params=pltpu.CompilerParams(dimension_semantics=('parallel', 'parallel')),
    )(slots, w, m, v)


def _rows_of(shape):
    size = int(np.prod(shape)) if len(shape) else 1
    return _round_up(_round_up(size, LANES) // LANES, SUBLANES)


def _pack(arrays, rows_multiple):
    parts = []
    for arr in arrays:
        flat = jnp.ravel(arr).astype(F32)
        rows = _rows_of(arr.shape)
        parts.append(jnp.pad(flat, (0, rows * LANES - flat.shape[0])).reshape(rows, LANES))
    total = sum(p.shape[0] for p in parts)
    pad = _round_up(total, rows_multiple) - total
    if pad:
        parts.append(jnp.zeros((pad, LANES), F32))
    return jnp.concatenate(parts, axis=0)


def _unpack(slab, shapes):
    out, r0 = [], 0
    for s in shapes:
        size = int(np.prod(s)) if len(s) else 1
        rows = _rows_of(s)
        out.append(slab[r0:r0 + rows].reshape(-1)[:size].reshape(s))
        r0 += rows
    return out


def _layer_norm(x, g, b):
    mu = jnp.mean(x, -1, keepdims=True)
    var = jnp.mean(jnp.square(x - mu), -1, keepdims=True)
    return (x - mu) * lax.rsqrt(var + LN_EPS) * g + b


def _modulate(x, shift, scale):
    return x * (1 + scale) + shift


def _take_cols(p, per_dev, per_dev_padded, lo, hi):
    parts = []
    while lo < hi:
        dev, off = divmod(lo, per_dev)
        n = min(hi - lo, per_dev - off)
        parts.append(p[:, dev * per_dev_padded + off: dev * per_dev_padded + off + n])
        lo += n
    return parts[0] if len(parts) == 1 else jnp.concatenate(parts, axis=1)


def _block_diag(blocks):
    g, a, b = blocks.shape
    eye = jnp.eye(g, dtype=blocks.dtype)
    return (blocks[:, :, None, :] * eye[:, None, :, None]).reshape(g * a, g * b)


def _s5_mixer(u, t_ctx, prm, name):
    t, gw = u.shape
    groups = gw // S5_CH
    ys = []
    for direction, reverse in enumerate((False, True)):
        lam = lax.complex(prm['s5_a_re'][direction], prm['s5_a_im'][direction])
        step = jnp.exp(prm['s5_log_step'][direction])[:, None]
        a_bar = jnp.exp(lam * step)
        b_bar = ((a_bar - 1.0) / lam)[..., None] * lax.complex(prm['s5_b_re'][direction],
                                                               prm['s5_b_im'][direction])
        c_mat = lax.complex(prm['s5_c_re'][direction], prm['s5_c_im'][direction])
        b_t = jnp.swapaxes(b_bar, 1, 2)
        b_mat = _interleave(_block_diag(jnp.real(b_t)), _block_diag(jnp.imag(b_t)))
        c_t = jnp.swapaxes(c_mat, 1, 2)
        c_blk = _interleave(_block_diag(jnp.real(c_t)).T, -_block_diag(jnp.imag(c_t)).T).T
        a_rows = jnp.stack([jnp.real(a_bar).reshape(-1), jnp.imag(a_bar).reshape(-1)])
        bu = _linear(u, b_mat, f'{name}_bu{direction}')
        h = _s5_scan(a_rows, bu, t_ctx, reverse, f'{name}_scan{direction}')
        ys.append(_linear(h, c_blk, f'{name}_y{direction}'))
    y = ys[0] + ys[1] + prm['s5_d'][None, :] * u
    z = jax.nn.gelu(y)
    return z * jax.nn.sigmoid(_linear(z, prm['s5_glu_w'], f'{name}_glu') + prm['s5_glu_b'])


def _chunk_gating(u, v, prm):
    t, gw = u.shape
    hd = gw // SG_HEADS
    u = jax.nn.gelu(u)
    v = jax.nn.gelu(v).reshape(t // SG_CHUNK, SG_CHUNK, SG_HEADS, hd)
    v = _layer_norm(v, prm['sg_ln_g'].reshape(SG_HEADS, hd), prm['sg_ln_b'].reshape(SG_HEADS, hd))
    s = jnp.einsum('hij,cjhd->cihd', prm['sg_w'], v) + prm['sg_b'].T[None, :, :, None]
    return u * s.reshape(t, gw)


def _pool_mixer(p, prm):
    l, gw = p.shape
    pd = gw // len(POOL_WINDOWS)
    t = np.arange(l)
    outs = []
    for g, win in enumerate(POOL_WINDOWS):
        lo = np.clip(t - win // 2, 0, l - 1)
        hi = np.clip(t + win // 2 - 1, 0, l - 1)
        pg = p[:, g * pd:(g + 1) * pd]
        padded = jnp.pad(pg, ((win // 2, win // 2), (0, 0)))
        total = padded[0:l]
        for d in range(1, win):
            total = total + padded[d:d + l]
        mean = total / jnp.asarray((hi - lo + 1).astype(np.float32))[:, None]
        outs.append(jnp.einsum('lc,cd->ld', mean - pg, prm['pool_w'][g]))
    y = jnp.concatenate(outs, axis=-1) + prm['pool_b']
    return y * prm['pool_scale']


def _dw_conv1d(x, w, b, pad):
    l = x.shape[0]
    xp = jnp.pad(x, (pad, (0, 0)))
    y = xp[0:l] * w[0]
    for k in range(1, w.shape[0]):
        y = y + xp[k:k + l] * w[k]
    return y + b


def _m2_prepare(xbc, dt_raw, prm, gw):
    heads = gw // M2_HEAD_DIM
    xbc = jax.nn.silu(_dw_conv1d(xbc, prm['m2_conv_w'], prm['m2_conv_b'], M2_PAD))
    l = xbc.shape[0]
    n_bc = M2_GROUPS * M2_STATE
    rep = heads // M2_GROUPS
    xs = xbc[:, :gw].reshape(l, heads, M2_HEAD_DIM)
    bm = jnp.repeat(xbc[:, gw:gw + n_bc].reshape(l, M2_GROUPS, M2_STATE), rep, axis=1)
    cm = jnp.repeat(xbc[:, gw + n_bc:].reshape(l, M2_GROUPS, M2_STATE), rep, axis=1)
    dt = jax.nn.softplus(dt_raw.reshape(l, 2, heads) + prm['m2_dt_bias'])
    return xs, bm, cm, dt


def _ssd_scan(xs, dt, a, bm, cm, h0, need_y):
    l, nh, hp = xs.shape
    nc = l // M2_CHUNK

    def chunks(t):
        return t.reshape((nc, M2_CHUNK) + t.shape[1:])

    xd = chunks(xs * dt[..., None])
    bc, cc = chunks(bm), chunks(cm)
    a_cum = jnp.cumsum(chunks(dt * a), axis=1)
    a_tot = a_cum[:, -1]
    decay_end = jnp.exp(a_tot[:, None] - a_cum)
    chunk_states = jnp.einsum('cqhn,cqh,cqhp->chpn', bc, decay_end, xd)

    cum = jnp.cumsum(a_tot, axis=0)
    before = cum - a_tot
    earlier = jnp.tril(jnp.ones((nc, nc), bool), -1)[:, :, None]
    carry_w = jnp.exp(jnp.where(earlier, before[:, None, :] - cum[None, :, :], -jnp.inf))
    h_final = (jnp.exp(cum[-1])[:, None, None] * h0
               + jnp.einsum('dh,dhpn->hpn', jnp.exp(cum[-1][None, :] - cum), chunk_states,
                            precision=lax.Precision.HIGHEST))
    if not need_y:
        return None, h_final
    h_prev = (jnp.exp(before)[:, :, None, None] * h0[None]
              + jnp.einsum('cdh,dhpn->chpn', carry_w, chunk_states, precision=lax.Precision.HIGHEST))
    seg = a_cum[:, :, None, :] - a_cum[:, None, :, :]
    lower = jnp.tril(jnp.ones((M2_CHUNK, M2_CHUNK), bool))[None, :, :, None]
    decay = jnp.exp(jnp.where(lower, seg, -jnp.inf))
    scores = jnp.einsum('cihn,cjhn->cijh', cc, bc) * decay
    y = (jnp.einsum('cijh,cjhp->cihp', scores, xd)
         + jnp.einsum('cihn,chpn->cihp', cc, h_prev) * jnp.exp(a_cum)[..., None])
    return y.reshape(l, nh, hp), h_final


def _ssd_direction(inputs, direction, a, h0, need_y):
    xs, bm, cm, dt = inputs
    dt = dt[:, direction]
    if direction == 1:
        xs, bm, cm, dt = (jnp.flip(t, 0) for t in (xs, bm, cm, dt))
    y, h_final = _ssd_scan(xs, dt, a, bm, cm, h0, need_y)
    if direction == 1 and y is not None:
        y = jnp.flip(y, 0)
    return y, h_final


def _gated_rmsnorm(y, z, w):
    l, gw = z.shape
    g = (y * jax.nn.silu(z)).reshape(l, M2_GROUPS, gw // M2_GROUPS)
    g = g * lax.rsqrt(jnp.mean(jnp.square(g), -1, keepdims=True) + RMS_EPS)
    return g.reshape(l, gw) * w


def _mamba2_mixer(z, xbc, dt_raw, t_ctx, prm, need_ctx):
    gw = z.shape[1]
    heads = gw // M2_HEAD_DIM
    ctx_in = _m2_prepare(xbc[:t_ctx], dt_raw[:t_ctx], prm, gw)
    lat_in = _m2_prepare(xbc[t_ctx:], dt_raw[t_ctx:], prm, gw)
    a = -jnp.exp(prm['m2_a_log'])
    ys_ctx, ys_lat = [], []
    for direction in range(2):
        h0 = jnp.zeros((heads, M2_HEAD_DIM, M2_STATE), F32)
        y_c, h_c = _ssd_direction(ctx_in, direction, a[direction], h0, need_ctx)
        y_l, _ = _ssd_direction(lat_in, direction, a[direction], h_c, True)
        ys_lat.append(y_l)
        if need_ctx:
            ys_ctx.append(y_c)
    d_h = prm['m2_d'][None, :, None]

    def finish(ys, xs, zz):
        y = (ys[0] + ys[1] + d_h * xs).reshape(zz.shape[0], gw)
        return _gated_rmsnorm(y, zz, prm['m2_norm_w'])

    y_lat = finish(ys_lat, lat_in[0], z[t_ctx:])
    if need_ctx:
        return jnp.concatenate([finish(ys_ctx, ctx_in[0], z[:t_ctx]), y_lat], axis=0)
    return y_lat


def _forward_loss(dp, consts, dims):
    depth, t_ctx, d_model = dims['depth'], dims['t_ctx'], dims['d_model']
    gw = d_model // 4
    alpha = (2 * depth) ** 0.25
    in_sizes = (gw, gw, gw, gw, gw, gw + 2 * M2_GROUPS * M2_STATE, 2 * (gw // M2_HEAD_DIM))
    in_offs = np.concatenate([[0], np.cumsum(in_sizes)])
    ml = [[dp['mod_lat'][i, k][None, :] for k in range(6)] for i in range(depth)]
    mc = [[dp['mod_ctx'][i, k][None, :] for k in range(6)] for i in range(depth)]
    zero_row = jnp.zeros((1, d_model), F32)
    h = jnp.concatenate([consts['ctx'], dp['x']], axis=0)
    hm = jnp.concatenate([_modulate(consts['ctx'], mc[0][0], mc[0][1]), _modulate(dp['x'], ml[0][0], ml[0][1])],
                         axis=0)
    for i in range(depth):
        need_ctx = i < depth - 1
        prm = {k: v[i] for k, v in dp['small'].items()}
        p = _linear_tap(hm, consts['w_in'][i], dp['taps']['w_in'][i], f'in{i}')
        seg = [_take_cols(p, dims['in_per_dev'], dims['in_per_dev_pad'], int(in_offs[s]), int(in_offs[s + 1]))
               for s in range(7)]
        ya = _s5_mixer(seg[0], t_ctx, prm, f's5_{i}')
        yb = _chunk_gating(seg[1], seg[2], prm)
        yc = jnp.concatenate([_pool_mixer(seg[3][:t_ctx], prm), _pool_mixer(seg[3][t_ctx:], prm)], axis=0)
        yd = _mamba2_mixer(seg[4], seg[5], seg[6], t_ctx, prm, need_ctx)
        if need_ctx:
            mix_in = jnp.concatenate([ya, yb, yc, yd], axis=1)
        else:
            mix_in = jnp.concatenate([ya[t_ctx:], yb[t_ctx:], yc[t_ctx:], yd], axis=1)
        mix = _linear_tap(mix_in, consts['w_out'][i], dp['taps']['w_out'][i], f'out{i}')
        t_c = t_ctx if need_ctx else 0
        if not need_ctx and h.shape[0] != mix.shape[0]:
            h = h[t_ctx:]
        mods = jnp.concatenate([mc[i][2], mc[i][3], mc[i][4], ml[i][2], ml[i][3], ml[i][4],
                                prm['ln1_g'][None], prm['ln1_b'][None]], axis=0)
        h, hm = _res_ln_mod(h, mix, mods, t_c, alpha, True, f'ln1_{i}')

        f_out = _ffn_block(hm, consts['ffn_w_up'][i], dp['taps']['ffn_w_up'][i], prm['ffn_conv_w'], prm['ffn_conv_b'],
                           consts['ffn_w_down'][i], dp['taps']['ffn_w_down'][i], t_c,
                           (dims['up_per_dev'], dims['up_per_dev_pad'], dims['ffn_hidden']), f'ffn{i}')
        if need_ctx:
            mods = jnp.concatenate([mc[i][5], mc[i + 1][0], mc[i + 1][1], ml[i][5], ml[i + 1][0], ml[i + 1][1],
                                    prm['ln2_g'][None], prm['ln2_b'][None]], axis=0)
            h, hm = _res_ln_mod(h, f_out, mods, t_c, alpha, True, f'ln2_{i}')
        else:
            mods = jnp.concatenate([mc[i][5], zero_row, zero_row, ml[i][5], zero_row, zero_row,
                                    prm['ln2_g'][None], prm['ln2_b'][None]], axis=0)
            h = _res_ln_mod(h, f_out, mods, t_c, alpha, False, f'ln2_{i}')[0]
    err = jnp.square(h - consts['target'])
    return 0.5 * jnp.sum(jnp.mean(err, axis=-1))


def _silu_grad(x):
    s = jax.nn.sigmoid(x)
    return s * (1 + x * (1 - s))


def _step(w, m, v, x, c, ctx, target):
    depth, d_model, ada_cols = w['w_ada'].shape
    t_ctx, t_lat = ctx.shape[1], x.shape[1]
    me = _my_index()
    in_per_dev = w['w_in'].shape[2]
    in_pad = _round_up(in_per_dev, LANES)
    up_per_dev = w['ffn_w_up'].shape[2]
    up_pad = _round_up(up_per_dev, LANES)
    f_hidden = w['ffn_w_down'].shape[1] * N_DEV
    dims = dict(depth=depth, t_ctx=t_ctx, d_model=d_model, in_per_dev=in_per_dev, in_per_dev_pad=in_pad,
                up_per_dev=up_per_dev, up_per_dev_pad=up_pad, ffn_hidden=f_hidden)

    rows16 = 2 * SUBLANES
    silu_c_all = _all_gather(jnp.pad(jax.nn.silu(c), ((0, SUBLANES - 1), (0, 0))), 0, 'ag_c')
    silu_c_all = silu_c_all.reshape(N_DEV, SUBLANES, d_model)[:, 0]
    silu_cc = jax.nn.silu(w['c_ctx'])
    ada_in = jnp.concatenate([silu_c_all, silu_cc[None], jnp.zeros((rows16 - N_DEV - 1, d_model), F32)], axis=0)
    mod_loc = jnp.concatenate([_matmul(ada_in, w['w_ada'][i], 'nn', 'ada_fwd') for i in range(depth)], axis=0)
    mod_all = _all_gather(mod_loc, 1, 'ag_mod').reshape(depth, rows16, 6 * d_model)
    mod_all = mod_all + w['b_ada'][:, None, :]
    mod_lat = lax.dynamic_index_in_dim(mod_all, me, axis=1, keepdims=False).reshape(depth, 6, d_model)
    mod_ctx = mod_all[:, N_DEV].reshape(depth, 6, d_model)

    def pad_cols(a, to):
        return jnp.pad(a, ((0, 0), (0, to - a.shape[1])))

    gathered = {n: [] for n in BIG}
    for i in range(depth):
        gathered['w_in'].append(_all_gather_sequencer(pad_cols(w['w_in'][i], in_pad).astype(BF16), 1, f'ag_w_in{i}'))
        gathered['w_out'].append(_all_gather_sequencer(w['w_out'][i].astype(BF16), 0, f'ag_w_out{i}'))
        gathered['ffn_w_up'].append(
            _all_gather_sequencer(pad_cols(w['ffn_w_up'][i], up_pad).astype(BF16), 1, f'ag_w_up{i}'))
        gathered['ffn_w_down'].append(_all_gather_sequencer(w['ffn_w_down'][i].astype(BF16), 0, f'ag_w_down{i}'))
    sharded_small = [w[n] for n in COL_SHARDED_SMALL + ROW_SHARDED_SMALL]
    small_slab = _pack(sharded_small, PACK_ROWS)
    slab_rows = small_slab.shape[0]
    small_all = _all_gather(small_slab, 0, 'ag_small').reshape(N_DEV, slab_rows, LANES)
    per_dev = [_unpack(small_all[d], [a.shape for a in sharded_small]) for d in range(N_DEV)]
    small_full = {}
    for k, n in enumerate(COL_SHARDED_SMALL):
        small_full[n] = jnp.concatenate([per_dev[d][k] for d in range(N_DEV)], axis=-1)
    for k, n in enumerate(ROW_SHARDED_SMALL):
        small_full[n] = jnp.concatenate([per_dev[d][len(COL_SHARDED_SMALL) + k] for d in range(N_DEV)], axis=1)

    small = {n: w[n] for n in REPLICATED}
    small.update(small_full)
    taps = {n: [jnp.zeros(gathered[n][i].shape, F32) for i in range(depth)] for n in BIG}
    dp = dict(x=x[0], small=small, mod_lat=mod_lat, mod_ctx=mod_ctx, taps=taps)
    consts = dict(ctx=ctx[0], target=target[0], **gathered)
    loss_local, grads = jax.value_and_grad(functools.partial(_forward_loss, consts=consts, dims=dims))(dp)

    out_g, out_d, out_m, out_v = {}, {}, {}, {}

    def finish_big(name, i, rows, cols, axis, keep_cols):
        slots = _reduce_scatter_slots(grads['taps'][name][i], rows, cols, axis, BF16, True, f'rs_{name}{i}')
        shard = [pad_cols(t[name][i], cols) for t in (w, m, v)]
        res = _sum_adamw(slots, *shard, f'adamw_{name}')
        return [r[:, :keep_cols] for r in res]

    big_specs = {
        'ffn_w_down': (w['ffn_w_down'].shape[1], d_model, 0, d_model),
        'ffn_w_up': (d_model, up_pad, 1, up_per_dev),
        'w_out': (w['w_out'].shape[1], d_model, 0, d_model),
        'w_in': (d_model, in_pad, 1, in_per_dev),
    }
    per_layer = {name: [None] * depth for name in big_specs}
    for i in reversed(range(depth)):
        for name, (rows, cols, axis, keep) in big_specs.items():
            per_layer[name][i] = finish_big(name, i, rows, cols, axis, keep)
    for name in big_specs:
        for k, dst in enumerate((out_g, out_d, out_m, out_v)):
            dst[name] = jnp.stack([per_layer[name][i][k] for i in range(depth)])

    d_lat = grads['mod_lat'].reshape(depth, 6 * d_model)
    d_ctx = grads['mod_ctx'].reshape(depth, 6 * d_model)
    d_rows = jnp.concatenate([d_lat, d_ctx, jnp.zeros((SUBLANES - 2 * depth, 6 * d_model), F32)], axis=0)
    d_all = _all_gather(d_rows, 0, 'ag_dmod').reshape(N_DEV, SUBLANES, 6 * d_model)
    d_lat_all = d_all[:, :depth]
    d_ctx_sum = d_all[0, depth:2 * depth]
    for d in range(1, N_DEV):
        d_ctx_sum = d_ctx_sum + d_all[d, depth:2 * depth]
    g_b_ada = d_ctx_sum
    for d in range(N_DEV):
        g_b_ada = g_b_ada + d_lat_all[d]
    g_w_ada, c_ctx_part = [], jnp.zeros((d_model,), F32)
    for i in range(depth):
        d_mat = jnp.concatenate([d_lat_all[:, i], d_ctx_sum[i][None],
                                 jnp.zeros((rows16 - N_DEV - 1, 6 * d_model), F32)], axis=0)
        d_mine = lax.dynamic_slice_in_dim(d_mat, me * ada_cols, ada_cols, axis=1)
        g_w_ada.append(_matmul(ada_in, d_mine, 'tn', 'ada_dw'))
        back = _matmul(d_mine, w['w_ada'][i], 'nt', 'ada_dx')
        c_ctx_part = c_ctx_part + back[N_DEV]
    c_ctx_part = c_ctx_part * _silu_grad(w['c_ctx'])
    g_w_ada = jnp.stack(g_w_ada).reshape(1, depth * d_model, ada_cols)
    res = _sum_adamw(g_w_ada, *[t['w_ada'].reshape(depth * d_model, ada_cols) for t in (w, m, v)], 'adamw_w_ada')
    for k, dst in enumerate((out_g, out_d, out_m, out_v)):
        dst['w_ada'] = res[k].reshape(depth, d_model, ada_cols)

    reduced_names = REPLICATED[:]
    reduced_names.remove('b_ada')
    reduced_names += list(COL_SHARDED_SMALL + ROW_SHARDED_SMALL)
    to_reduce = [grads['small'][n] for n in reduced_names] + [c_ctx_part, loss_local]
    slab = _pack(to_reduce, N_DEV * PACK_ROWS)
    chunk_rows = slab.shape[0] // N_DEV
    slots = _reduce_scatter_slots(slab, chunk_rows, LANES, 0, F32, False, 'rs_small')
    mine = _sum_slots(slots, 'sum_small')
    summed = _all_gather(mine, 0, 'ag_small_sum')
    parts = _unpack(summed, [a.shape for a in to_reduce])
    g_small = dict(zip(reduced_names, parts[:len(reduced_names)]))
    g_small['c_ctx'] = parts[-2]
    g_small['b_ada'] = g_b_ada
    loss = parts[-1]

    def my_shard(name, full):
        if name in COL_SHARDED_SMALL:
            n = full.shape[-1] // N_DEV
            return lax.dynamic_slice_in_dim(full, me * n, n, axis=full.ndim - 1)
        if name in ROW_SHARDED_SMALL:
            n = full.shape[1] // N_DEV
            return lax.dynamic_slice_in_dim(full, me * n, n, axis=1)
        return full

    small_names = [n for n in WEIGHTS if n not in BIG + ('w_ada',)]
    g_list = [my_shard(n, g_small[n]) for n in small_names]
    g_slab = _pack(g_list, PACK_ROWS)
    res = _sum_adamw(g_slab[None], *[_pack([t[n] for n in small_names], PACK_ROWS) for t in (w, m, v)],
                     'adamw_small')
    shapes = [w[n].shape for n in small_names]
    for k, dst in enumerate((out_g, out_d, out_m, out_v)):
        if k == 0:
            dst.update(dict(zip(small_names, g_list)))
        else:
            dst.update(dict(zip(small_names, _unpack(res[k], shapes))))

    grad_x = grads['x'][None]
    return (loss, grad_x, *[out_g[n] for n in WEIGHTS], *[out_d[n] for n in WEIGHTS],
            *[out_m[n] for n in WEIGHTS], *[out_v[n] for n in WEIGHTS])


def kernel(x, c, ctx, c_ctx, w_ada, b_ada, w_in, w_out, ln1_g, ln1_b, ln2_g, ln2_b, s5_a_re, s5_a_im, s5_b_re, s5_b_im, s5_c_re, s5_c_im, s5_log_step, s5_d, s5_glu_w, s5_glu_b, sg_ln_g, sg_ln_b, sg_w, sg_b, pool_w, pool_b, pool_scale, m2_conv_w, m2_conv_b, m2_dt_bias, m2_a_log, m2_d, m2_norm_w, ffn_w_up, ffn_conv_w, ffn_conv_b, ffn_w_down, loss_target, m_c_ctx, m_w_ada, m_b_ada, m_w_in, m_w_out, m_ln1_g, m_ln1_b, m_ln2_g, m_ln2_b, m_s5_a_re, m_s5_a_im, m_s5_b_re, m_s5_b_im, m_s5_c_re, m_s5_c_im, m_s5_log_step, m_s5_d, m_s5_glu_w, m_s5_glu_b, m_sg_ln_g, m_sg_ln_b, m_sg_w, m_sg_b, m_pool_w, m_pool_b, m_pool_scale, m_m2_conv_w, m_m2_conv_b, m_m2_dt_bias, m_m2_a_log, m_m2_d, m_m2_norm_w, m_ffn_w_up, m_ffn_conv_w, m_ffn_conv_b, m_ffn_w_down, v_c_ctx, v_w_ada, v_b_ada, v_w_in, v_w_out, v_ln1_g, v_ln1_b, v_ln2_g, v_ln2_b, v_s5_a_re, v_s5_a_im, v_s5_b_re, v_s5_b_im, v_s5_c_re, v_s5_c_im, v_s5_log_step, v_s5_d, v_s5_glu_w, v_s5_glu_b, v_sg_ln_g, v_sg_ln_b, v_sg_w, v_sg_b, v_pool_w, v_pool_b, v_pool_scale, v_m2_conv_w, v_m2_conv_b, v_m2_dt_bias, v_m2_a_log, v_m2_d, v_m2_norm_w, v_ffn_w_up, v_ffn_conv_w, v_ffn_conv_b, v_ffn_w_down):
    given = dict(locals())
    w = {n: given[n] for n in WEIGHTS}
    m = {n: given['m_' + n] for n in WEIGHTS}
    v = {n: given['v_' + n] for n in WEIGHTS}
    return _step(w, m, v, x, c, ctx, loss_target)
```

```python
import functools

import jax
import jax.numpy as jnp
import numpy as np
from jax import lax
from jax.experimental import pallas as pl
from jax.experimental.pallas import tpu as pltpu
from jax.experimental.pallas import tpu_sc as plsc

F32 = jnp.float32
BF16 = jnp.bfloat16
MESH = pl.DeviceIdType.MESH
ANY = pl.BlockSpec(memory_space=pl.ANY)

N_DEV = 8
N_CHIP = 4
LANES = 128
SUBLANES = 8
VMEM_LIMIT_BYTES = 48 * 1024 * 1024
MATMUL_VMEM_BUDGET = 36 * 1024 * 1024
PACK_ROWS = 512

GRID_W = 64
S5_CH = 16
S5_STATE = 64
SG_HEADS = 4
SG_CHUNK = 128
POOL_WINDOWS = (2, 4, 8, 16)
M2_HEAD_DIM = 64
M2_STATE = 128
M2_GROUPS = 2
M2_CONV = 4
M2_PAD = (M2_CONV // 2, M2_CONV - 1 - M2_CONV // 2)
M2_CHUNK = 128
FFN_CONV = 3
LN_EPS = 1e-5
RMS_EPS = 1e-5
ADAM_LR = 0.001
ADAM_B1 = 0.9
ADAM_B2 = 0.999
ADAM_EPS = 1e-08
ADAM_WD = 0.01
ADAM_STEP = 10

WEIGHTS = ['c_ctx', 'w_ada', 'b_ada', 'w_in', 'w_out', 'ln1_g', 'ln1_b', 'ln2_g', 'ln2_b', 's5_a_re', 's5_a_im',
           's5_b_re', 's5_b_im', 's5_c_re', 's5_c_im', 's5_log_step', 's5_d', 's5_glu_w', 's5_glu_b', 'sg_ln_g',
           'sg_ln_b', 'sg_w', 'sg_b', 'pool_w', 'pool_b', 'pool_scale', 'm2_conv_w', 'm2_conv_b', 'm2_dt_bias',
           'm2_a_log', 'm2_d', 'm2_norm_w', 'ffn_w_up', 'ffn_conv_w', 'ffn_conv_b', 'ffn_w_down']
BIG = ('w_in', 'w_out', 'ffn_w_up', 'ffn_w_down')
COL_SHARDED_SMALL = ('m2_conv_w', 'ffn_conv_w')
ROW_SHARDED_SMALL = ('s5_glu_w',)
REPLICATED = [n for n in WEIGHTS if n not in BIG + COL_SHARDED_SMALL + ROW_SHARDED_SMALL + ('w_ada', 'c_ctx')]


def _round_up(n, m):
    return (n + m - 1) // m * m


def _my_coords():
    return lax.axis_index('x'), lax.axis_index('y'), lax.axis_index('c')


def _my_index():
    x, y, c = _my_coords()
    return 4 * x + 2 * y + c


def _first_divisor(n, cands):
    for c in cands:
        if n % c == 0:
            return c
    return n


def _matmul_tiles(mode, m, n, k, a_bytes, b_bytes):
    lane_c = (1024, 512, 384, 256, 128)
    sub_c = (1088, 1024, 544, 512, 256, 128, 64, 32, 16)
    tm = _first_divisor(m, lane_c if mode == 'tn' else sub_c)
    tn = _first_divisor(n, lane_c)
    k_c = [c for c in (5632, 4096, 2048, 1408, 1088, 1024, 544, 512, 256, 128) if k % c == 0] or [k]
    if mode == 'tn':
        k_c = [c for c in k_c if c <= 1088] or [k_c[-1]]
    for tk in k_c:
        use = 2 * (tm * tk * a_bytes + tk * tn * b_bytes) + 3 * tm * tn * 4
        if use <= MATMUL_VMEM_BUDGET:
            return tm, tn, tk
    return tm, tn, k_c[-1]


def _matmul(a, b, mode, name, out_dtype=F32):
    if mode == 'nn':
        (m, k), (k2, n) = a.shape, b.shape
    elif mode == 'nt':
        (m, k), (n, k2) = a.shape, b.shape
    else:
        (k, m), (k2, n) = a.shape, b.shape
    assert k == k2, (mode, a.shape, b.shape)
    tm, tn, tk = _matmul_tiles(mode, m, n, k, a.dtype.itemsize, b.dtype.itemsize)
    nk = k // tk
    if mode == 'nn':
        a_spec = pl.BlockSpec((tm, tk), lambda i, j, kk: (i, kk))
        b_spec = pl.BlockSpec((tk, tn), lambda i, j, kk: (kk, j))
        dims = (((1,), (0,)), ((), ()))
    elif mode == 'nt':
        a_spec = pl.BlockSpec((tm, tk), lambda i, j, kk: (i, kk))
        b_spec = pl.BlockSpec((tn, tk), lambda i, j, kk: (j, kk))
        dims = (((1,), (1,)), ((), ()))
    else:
        a_spec = pl.BlockSpec((tk, tm), lambda i, j, kk: (kk, i))
        b_spec = pl.BlockSpec((tk, tn), lambda i, j, kk: (kk, j))
        dims = (((0,), (0,)), ((), ()))

    def body(a_ref, b_ref, o_ref, acc_ref):
        kk = pl.program_id(2)

        @pl.when(kk == 0)
        def _():
            acc_ref[...] = jnp.zeros_like(acc_ref)

        acc_ref[...] += lax.dot_general(a_ref[...].astype(BF16), b_ref[...].astype(BF16), dims,
                                        preferred_element_type=F32)

        @pl.when(kk == nk - 1)
        def _():
            o_ref[...] = acc_ref[...].astype(o_ref.dtype)

    return pl.pallas_call(
        body,
        name=name,
        grid=(m // tm, n // tn, nk),
        in_specs=[a_spec, b_spec],
        out_specs=pl.BlockSpec((tm, tn), lambda i, j, kk: (i, j)),
        out_shape=jax.ShapeDtypeStruct((m, n), out_dtype),
        scratch_shapes=[pltpu.VMEM((tm, tn), F32)],
        compiler_params=pltpu.CompilerParams(dimension_semantics=('parallel', 'parallel', 'arbitrary'),
                                             vmem_limit_bytes=VMEM_LIMIT_BYTES),
    )(a, b)


@functools.partial(jax.custom_vjp, nondiff_argnums=(2,))
def _linear(x, w, name):
    return _matmul(x, w, 'nn', name + '_fwd')


def _linear_fwd(x, w, name):
    return _matmul(x, w, 'nn', name + '_fwd'), (x, w)


def _linear_bwd(name, res, dy):
    x, w = res
    return _matmul(dy, w, 'nt', name + '_dx'), _matmul(x, dy, 'tn', name + '_dw')


_linear.defvjp(_linear_fwd, _linear_bwd)


@functools.partial(jax.custom_vjp, nondiff_argnums=(3,))
def _linear_tap(x, w, tap, name):
    del tap
    return _matmul(x, w, 'nn', name + '_fwd')


def _linear_tap_fwd(x, w, tap, name):
    del tap
    return _matmul(x, w, 'nn', name + '_fwd'), (x, w)


def _linear_tap_bwd(name, res, dy):
    x, w = res
    return _matmul(dy, w, 'nt', name + '_dx'), jnp.zeros_like(w), _matmul(x, dy, 'tn', name + '_dw')


_linear_tap.defvjp(_linear_tap_fwd, _linear_tap_bwd)


def _blocked_matmul(a, b, mode, name):
    t = a.shape[0]
    nb, k, n = b.shape
    tm = _first_divisor(t, (1088, 1024, 544, 512, 256, 128, 64, 32, 16))
    cin, cout = (k, n) if mode == 'nn' else (n, k)
    dims = (((1,), (0,)), ((), ())) if mode == 'nn' else (((1,), (1,)), ((), ()))

    def body(a_ref, b_ref, o_ref):
        o_ref[...] = lax.dot_general(a_ref[...].astype(BF16), b_ref[0].astype(BF16), dims,
                                     preferred_element_type=F32)

    return pl.pallas_call(
        body, name=name, grid=(t // tm, nb),
        in_specs=[pl.BlockSpec((tm, cin), lambda i, j: (i, j)), pl.BlockSpec((1, k, n), lambda i, j: (j, 0, 0))],
        out_specs=pl.BlockSpec((tm, cout), lambda i, j: (i, j)),
        out_shape=jax.ShapeDtypeStruct((t, nb * cout), F32),
        compiler_params=pltpu.CompilerParams(vmem_limit_bytes=VMEM_LIMIT_BYTES,
                                             dimension_semantics=('parallel', 'parallel')),
    )(a, b)


def _blocked_weight_grad(x, dy, nb, name):
    t = x.shape[0]
    k, n = x.shape[1] // nb, dy.shape[1] // nb
    tk = _first_divisor(t, (1088, 1024, 544, 512, 256, 128, 64, 32, 16))
    steps = t // tk

    def body(x_ref, dy_ref, o_ref):
        @pl.when(pl.program_id(1) == 0)
        def _():
            o_ref[...] = jnp.zeros_like(o_ref)

        o_ref[0] += lax.dot_general(x_ref[...].astype(BF16), dy_ref[...].astype(BF16), (((0,), (0,)), ((), ())),
                                    preferred_element_type=F32)

    return pl.pallas_call(
        body, name=name, grid=(nb, steps),
        in_specs=[pl.BlockSpec((tk, k), lambda j, s: (s, j)), pl.BlockSpec((tk, n), lambda j, s: (s, j))],
        out_specs=pl.BlockSpec((1, k, n), lambda j, s: (j, 0, 0)),
        out_shape=jax.ShapeDtypeStruct((nb, k, n), F32),
        compiler_params=pltpu.CompilerParams(vmem_limit_bytes=VMEM_LIMIT_BYTES,
                                             dimension_semantics=('parallel', 'arbitrary')),
    )(x, dy)


@functools.partial(jax.custom_vjp, nondiff_argnums=(2,))
def _blocked_linear(x, w, name):
    return _blocked_matmul(x, w, 'nn', name + '_fwd')


def _blocked_linear_fwd(x, w, name):
    return _blocked_matmul(x, w, 'nn', name + '_fwd'), (x, w)


def _blocked_linear_bwd(name, res, dy):
    x, w = res
    return _blocked_matmul(dy, w, 'nt', name + '_dx'), _blocked_weight_grad(x, dy, w.shape[0], name + '_dw')


_blocked_linear.defvjp(_blocked_linear_fwd, _blocked_linear_bwd)


SCAN_ROWS = 128
SCAN_LANES = 512


def _scan_lanes(n):
    return SCAN_LANES if n % SCAN_LANES == 0 else n


def _scan_call(a, x, h, n_ctx_blocks, ctx_first, reverse_rows, name):
    t, n2 = x.shape
    n = n2 // 2
    nblk = t // SCAN_ROWS
    nlat = nblk - n_ctx_blocks
    lanes = _scan_lanes(n)
    with_da = h is not None

    def block_of(i):
        if ctx_first:
            first_n, first_0, second_0, second_n = n_ctx_blocks, 0, n_ctx_blocks, nlat
        else:
            first_n, first_0, second_0, second_n = nlat, n_ctx_blocks, 0, n_ctx_blocks
        if reverse_rows:
            in_first = first_0 + first_n - 1 - i
            in_second = second_0 + second_n - 1 - (i - first_n)
        else:
            in_first = first_0 + i
            in_second = second_0 + (i - first_n)
        return jnp.where(i < first_n, in_first, in_second)

    groups = SCAN_ROWS // SUBLANES
    first_row = SUBLANES - 1 if reverse_rows else 0
    to_previous = SUBLANES - 1 if reverse_rows else 1

    def body(*refs):
        if with_da:
            a_ref, x_ref, h_ref, o_ref, da_ref, st_ref = refs
        else:
            a_ref, x_ref, o_ref, st_ref = refs

        @pl.when(pl.program_id(1) == 0)
        def _():
            st_ref[...] = jnp.zeros_like(st_ref)
            if with_da:
                da_ref[...] = jnp.zeros_like(da_ref)

        row_id = lax.broadcasted_iota(jnp.int32, (SUBLANES, lanes), 0)
        behind = (SUBLANES - 1 - row_id) if reverse_rows else row_id

        def cmul(pr, pi, qr, qi):
            return pr * qr - pi * qi, pr * qi + pi * qr

        a1 = (jnp.broadcast_to(a_ref[0:1, :], (SUBLANES, lanes)), jnp.broadcast_to(a_ref[1:2, :], (SUBLANES, lanes)))
        a2 = cmul(*a1, *a1)
        a4 = cmul(*a2, *a2)
        pw = a1
        for bit, ak in ((1, a1), (2, a2), (4, a4)):
            nxt = cmul(*pw, *ak)
            pw = (jnp.where((behind & bit) != 0, nxt[0], pw[0]), jnp.where((behind & bit) != 0, nxt[1], pw[1]))

        def group(g, carry):
            in_r, in_i = carry
            gi = (groups - 1 - g) if reverse_rows else g
            start = pl.multiple_of(gi * SUBLANES, SUBLANES)
            xr = x_ref[pl.ds(start, SUBLANES), pl.ds(0, lanes)]
            xi = x_ref[pl.ds(start, SUBLANES), pl.ds(lanes, lanes)]
            for k, ak in ((1, a1), (2, a2), (4, a4)):
                shift = (SUBLANES - k) if reverse_rows else k
                pr = jnp.where(behind >= k, pltpu.roll(xr, shift, axis=0), 0.0)
                pi = jnp.where(behind >= k, pltpu.roll(xi, shift, axis=0), 0.0)
                qr, qi = cmul(*ak, pr, pi)
                xr, xi = xr + qr, xi + qi
            cr, ci = cmul(*pw, in_r, in_i)
            out_r, out_i = xr + cr, xi + ci
            last = 0 if reverse_rows else SUBLANES - 1
            sr = jnp.broadcast_to(out_r[last:last + 1, :], (SUBLANES, lanes))
            si = jnp.broadcast_to(out_i[last:last + 1, :], (SUBLANES, lanes))
            o_ref[pl.ds(start, SUBLANES), pl.ds(0, lanes)] = out_r
            o_ref[pl.ds(start, SUBLANES), pl.ds(lanes, lanes)] = out_i
            if with_da:
                pr = jnp.where(row_id == first_row, in_r, pltpu.roll(out_r, to_previous, axis=0))
                pi = jnp.where(row_id == first_row, in_i, pltpu.roll(out_i, to_previous, axis=0))
                hr = h_ref[pl.ds(start, SUBLANES), pl.ds(0, lanes)]
                hi = h_ref[pl.ds(start, SUBLANES), pl.ds(lanes, lanes)]
                da_ref[0:SUBLANES, :] += hr * pr + hi * pi
                da_ref[SUBLANES:2 * SUBLANES, :] += hr * pi - hi * pr
            return sr, si

        sr, si = lax.fori_loop(0, groups, group, (st_ref[0], st_ref[1]))
        st_ref[0] = sr
        st_ref[1] = si

    row_spec = pl.BlockSpec((SCAN_ROWS, 2 * lanes), lambda j, i: (block_of(i), j))
    in_specs = [pl.BlockSpec((2, lanes), lambda j, i: (0, j)), row_spec]
    out_specs = [row_spec]
    out_shape = [jax.ShapeDtypeStruct((t, n2), F32)]
    operands = [a, x]
    if with_da:
        in_specs.append(row_spec)
        operands.append(h)
        out_specs.append(pl.BlockSpec((2 * SUBLANES, lanes), lambda j, i: (0, j)))
        out_shape.append(jax.ShapeDtypeStruct((2 * SUBLANES, n), F32))
    return pl.pallas_call(
        body,
        name=name,
        grid=(n // lanes, nblk),
        in_specs=in_specs,
        out_specs=out_specs,
        out_shape=out_shape,
        scratch_shapes=[pltpu.VMEM((2, SUBLANES, lanes), F32)],
        compiler_params=pltpu.CompilerParams(dimension_semantics=('parallel', 'arbitrary')),
    )(*operands)


@functools.partial(jax.custom_vjp, nondiff_argnums=(2, 3, 4))
def _s5_scan(a, bu, t_ctx, reverse, name):
    return _scan_call(a, bu, None, t_ctx // SCAN_ROWS, True, reverse, name + '_fwd')[0]


def _s5_scan_fwd(a, bu, t_ctx, reverse, name):
    h = _scan_call(a, bu, None, t_ctx // SCAN_ROWS, True, reverse, name + '_fwd')[0]
    return h, (a, h)


def _s5_scan_bwd(t_ctx, reverse, name, res, dh):
    a, h = res
    a_conj = a * jnp.array([[1.0], [-1.0]], F32)
    g, da = _scan_call(a_conj, dh, h, t_ctx // SCAN_ROWS, False, not reverse, name + '_bwd')
    da = jnp.stack([jnp.sum(da[:SUBLANES], axis=0), jnp.sum(da[SUBLANES:], axis=0)])
    return da, g


_s5_scan.defvjp(_s5_scan_fwd, _s5_scan_bwd)


FFN_LANES = 128
CONV_TAPS = FFN_CONV * FFN_CONV
CONV_W_ROWS = 16
GELU_K = 0.7978845608028654
GELU_C = 0.044715


def _gelu_and_slope(x):
    x2 = x * x
    th = jnp.tanh(GELU_K * (x + GELU_C * x * x2))
    cdf = 0.5 * (1.0 + th)
    slope = cdf + 0.5 * x * (1.0 - th * th) * (GELU_K * (1.0 + 3.0 * GELU_C * x2))
    return x * cdf, slope


def _band_rows(t_lat):
    return _first_divisor(t_lat // GRID_W, (8, 4, 2, 1)) * GRID_W


def _shifted_ctx(x):
    n = x.shape[0]
    row = lax.broadcasted_iota(jnp.int32, x.shape, 0)
    left = jnp.where(row == 0, 0.0, pltpu.roll(x, 1, axis=0))
    right = jnp.where(row == n - 1, 0.0, pltpu.roll(x, n - 1, axis=0))
    return left, x, right


def _shifted_band(src_ref, bidx, n_bands, band, t_ctx, t):
    start = pl.multiple_of(t_ctx + bidx * band, GRID_W)
    top_start = pl.multiple_of(jnp.maximum(start - GRID_W, 0), SUBLANES)
    bot_start = pl.multiple_of(jnp.minimum(start + band, t - GRID_W), SUBLANES)
    top = jnp.where(bidx > 0, src_ref[pl.ds(top_start, GRID_W), :], 0.0)
    bot = jnp.where(bidx < n_bands - 1, src_ref[pl.ds(bot_start, GRID_W), :], 0.0)
    ext = jnp.concatenate([top, src_ref[pl.ds(start, band), :], bot], axis=0)
    n_ext = band + 2 * GRID_W
    col = lax.broadcasted_iota(jnp.int32, ext.shape, 0) & (GRID_W - 1)
    left = jnp.where(col == 0, 0.0, pltpu.roll(ext, 1, axis=0))
    right = jnp.where(col == GRID_W - 1, 0.0, pltpu.roll(ext, n_ext - 1, axis=0))
    return left, ext, right


def _conv_sum(shifted, w_rows, band, flip):
    acc = None
    for i in range(FFN_CONV):
        for j in range(FFN_CONV):
            w = w_rows[(FFN_CONV - 1 - i) * FFN_CONV + (FFN_CONV - 1 - j)] if flip else w_rows[i * FFN_CONV + j]
            term = shifted[j][i * GRID_W:i * GRID_W + band] * w
            acc = term if acc is None else acc + term
    return acc


def _conv_operand(conv_w, conv_b):
    f = conv_b.shape[0]
    return jnp.concatenate([conv_w.reshape(CONV_TAPS, f), conv_b[None],
                            jnp.zeros((CONV_W_ROWS - CONV_TAPS - 1, f), F32)], axis=0)


def _ffn_mid_fwd(gv, w16, t_ctx, name):
    t, f2 = gv.shape
    f = f2 // 2
    lanes = FFN_LANES
    nf = f // lanes
    band = _band_rows(t - t_ctx)
    n_bands = (t - t_ctx) // band
    mid = (FFN_CONV // 2) * FFN_CONV

    def body(g_ref, v_ref, w_ref, o_ref):
        w_rows = [w_ref[k:k + 1, :] for k in range(CONV_TAPS)]
        bias = w_ref[CONV_TAPS:CONV_TAPS + 1, :]
        if t_ctx:
            sh = _shifted_ctx(g_ref[0:t_ctx, :])
            pre = sh[0] * w_rows[mid] + sh[1] * w_rows[mid + 1] + sh[2] * w_rows[mid + 2] + bias
            o_ref[0:t_ctx, :] = (_gelu_and_slope(pre)[0] * v_ref[0:t_ctx, :]).astype(o_ref.dtype)

        def one_band(b, carry):
            start = pl.multiple_of(t_ctx + b * band, GRID_W)
            pre = _conv_sum(_shifted_band(g_ref, b, n_bands, band, t_ctx, t), w_rows, band, False) + bias
            o_ref[pl.ds(start, band), :] = (_gelu_and_slope(pre)[0] * v_ref[pl.ds(start, band), :]).astype(o_ref.dtype)
            return carry

        lax.fori_loop(0, n_bands, one_band, 0)

    return pl.pallas_call(
        body, name=name,
        grid=(nf,),
        in_specs=[pl.BlockSpec((t, lanes), lambda j: (0, j)),
                  pl.BlockSpec((t, lanes), lambda j: (0, nf + j)),
                  pl.BlockSpec((CONV_W_ROWS, lanes), lambda j: (0, j))],
        out_specs=pl.BlockSpec((t, lanes), lambda j: (0, j)),
        out_shape=jax.ShapeDtypeStruct((t, f), BF16),
        compiler_params=pltpu.CompilerParams(dimension_semantics=('parallel',), vmem_limit_bytes=VMEM_LIMIT_BYTES),
    )(gv, gv, w16)


def _ffn_mid_bwd(d_act, gv, w16, t_ctx, name):
    t, f2 = gv.shape
    f = f2 // 2
    lanes = FFN_LANES
    nf = f // lanes
    band = _band_rows(t - t_ctx)
    n_bands = (t - t_ctx) // band
    mid = (FFN_CONV // 2) * FFN_CONV
    n_acc = CONV_TAPS + 1

    def tile_sum(x):
        return jnp.sum(x.reshape(x.shape[0] // SUBLANES, SUBLANES, lanes), axis=0)

    def body(da_ref, g_ref, v_ref, w_ref, dg_ref, dv_ref, dw_ref, dp_ref):
        w_rows = [w_ref[k:k + 1, :] for k in range(CONV_TAPS)]
        bias = w_ref[CONV_TAPS:CONV_TAPS + 1, :]
        acc = [jnp.zeros((SUBLANES, lanes), F32) for _ in range(n_acc)]
        if t_ctx:
            sh = _shifted_ctx(g_ref[0:t_ctx, :])
            pre = sh[0] * w_rows[mid] + sh[1] * w_rows[mid + 1] + sh[2] * w_rows[mid + 2] + bias
            ge, slope = _gelu_and_slope(pre)
            da = da_ref[0:t_ctx, :]
            dv_ref[0:t_ctx, :] = (da * ge).astype(dv_ref.dtype)
            dpre = da * v_ref[0:t_ctx, :] * slope
            dp_ref[0:t_ctx, :] = dpre
            for j in range(FFN_CONV):
                acc[mid + j] = acc[mid + j] + tile_sum(sh[j] * dpre)
            acc[CONV_TAPS] = acc[CONV_TAPS] + tile_sum(dpre)
            back = _shifted_ctx(dpre)
            dg_ref[0:t_ctx, :] = (back[2] * w_rows[mid] + back[1] * w_rows[mid + 1]
                                  + back[0] * w_rows[mid + 2]).astype(dg_ref.dtype)

        def first_pass(b, acc):
            acc = list(acc)
            start = pl.multiple_of(t_ctx + b * band, GRID_W)
            sh = _shifted_band(g_ref, b, n_bands, band, t_ctx, t)
            ge, slope = _gelu_and_slope(_conv_sum(sh, w_rows, band, False) + bias)
            da = da_ref[pl.ds(start, band), :]
            dv_ref[pl.ds(start, band), :] = (da * ge).astype(dv_ref.dtype)
            dpre = da * v_ref[pl.ds(start, band), :] * slope
            dp_ref[pl.ds(start, band), :] = dpre
            for i in range(FFN_CONV):
                for j in range(FFN_CONV):
                    k = i * FFN_CONV + j
                    acc[k] = acc[k] + tile_sum(sh[j][i * GRID_W:i * GRID_W + band] * dpre)
            acc[CONV_TAPS] = acc[CONV_TAPS] + tile_sum(dpre)
            return tuple(acc)

        acc = lax.fori_loop(0, n_bands, first_pass, tuple(acc))
        for k in range(n_acc):
            dw_ref[k * SUBLANES:(k + 1) * SUBLANES, :] = acc[k]

        def second_pass(b, carry):
            start = pl.multiple_of(t_ctx + b * band, GRID_W)
            dg_ref[pl.ds(start, band), :] = _conv_sum(_shifted_band(dp_ref, b, n_bands, band, t_ctx, t),
                                                      w_rows, band, True).astype(dg_ref.dtype)
            return carry

        lax.fori_loop(0, n_bands, second_pass, 0)

    col = pl.BlockSpec((t, lanes), lambda j: (0, j))
    return pl.pallas_call(
        body, name=name,
        grid=(nf,),
        in_specs=[col, col, pl.BlockSpec((t, lanes), lambda j: (0, nf + j)),
                  pl.BlockSpec((CONV_W_ROWS, lanes), lambda j: (0, j))],
        out_specs=[col, col, pl.BlockSpec((n_acc * SUBLANES, lanes), lambda j: (0, j))],
        out_shape=[jax.ShapeDtypeStruct((t, f), BF16), jax.ShapeDtypeStruct((t, f), BF16),
                   jax.ShapeDtypeStruct((n_acc * SUBLANES, f), F32)],
        scratch_shapes=[pltpu.VMEM((t, lanes), F32)],
        compiler_params=pltpu.CompilerParams(dimension_semantics=('parallel',), vmem_limit_bytes=VMEM_LIMIT_BYTES),
    )(d_act, gv, gv, w16)


def _gate_value(up, layout):
    per_dev, per_dev_pad, f = layout
    if per_dev == per_dev_pad:
        return up
    return jnp.concatenate([_take_cols(up, per_dev, per_dev_pad, 0, f), _take_cols(up, per_dev, per_dev_pad, f, 2 * f)],
                           axis=1)


def _ffn_forward(hm, w_up, conv_w, conv_b, w_down, t_ctx, layout, name):
    up = _matmul(hm, w_up, 'nn', name + '_up_fwd')
    gv = _gate_value(up, layout)
    w16 = _conv_operand(conv_w, conv_b)
    act = _ffn_mid_fwd(gv, w16, t_ctx, name + '_mid_fwd')
    return _matmul(act, w_down, 'nn', name + '_down_fwd'), (hm, w_up, gv, w16, w_down, act)


@functools.partial(jax.custom_vjp, nondiff_argnums=(7, 8, 9))
def _ffn_block(hm, w_up, tap_up, conv_w, conv_b, w_down, tap_down, t_ctx, layout, name):
    del tap_up, tap_down
    return _ffn_forward(hm, w_up, conv_w, conv_b, w_down, t_ctx, layout, name)[0]


def _ffn_block_fwd(hm, w_up, tap_up, conv_w, conv_b, w_down, tap_down, t_ctx, layout, name):
    del tap_up, tap_down
    return _ffn_forward(hm, w_up, conv_w, conv_b, w_down, t_ctx, layout, name)


def _ffn_block_bwd(t_ctx, layout, name, res, d_out):
    hm, w_up, gv, w16, w_down, act = res
    f = act.shape[1]
    d_act = _matmul(d_out, w_down, 'nt', name + '_down_dx')
    d_w_down = _matmul(act, d_out, 'tn', name + '_down_dw')
    d_gate, d_val, d_w = _ffn_mid_bwd(d_act, gv, w16, t_ctx, name + '_mid_bwd')
    d_gv = jnp.concatenate([d_gate, d_val], axis=1)
    if layout[0] == layout[1]:
        d_up = d_gv
    else:
        d_up = jax.vjp(lambda u: _gate_value(u, layout), jnp.zeros((gv.shape[0], w_up.shape[1]), d_gv.dtype))[1](d_gv)[0]
    d_hm = _matmul(d_up, w_up, 'nt', name + '_up_dx')
    d_w_up = _matmul(hm, d_up, 'tn', name + '_up_dw')
    d_w = jnp.sum(d_w.reshape(CONV_TAPS + 1, SUBLANES, f), axis=1)
    return (d_hm, jnp.zeros_like(w_up), d_w_up, d_w[:CONV_TAPS].reshape(FFN_CONV, FFN_CONV, f), d_w[CONV_TAPS],
            jnp.zeros_like(w_down), d_w_down)


_ffn_block.defvjp(_ffn_block_fwd, _ffn_block_bwd)


LN_ROWS = 128
MOD_ROWS = 8


def _normalised(h, y, gate, alpha):
    r = alpha * h + gate * y
    xc = r - jnp.mean(r, axis=-1, keepdims=True)
    rstd = lax.rsqrt(jnp.mean(xc * xc, axis=-1, keepdims=True) + LN_EPS)
    return xc * rstd, rstd


def _mod_row(m_ref, k, is_ctx):
    lat = m_ref[3 + k:4 + k, :]
    return lat if is_ctx is None else jnp.where(is_ctx, m_ref[k:k + 1, :], lat)


def _res_ln_fwd_call(h, y, mods, t_ctx, alpha, with_mod, name):
    t, d = h.shape
    n_ctx_tiles = t_ctx // LN_ROWS

    def body(h_ref, y_ref, m_ref, hn_ref, *rest):
        is_ctx = (pl.program_id(0) < n_ctx_tiles) if n_ctx_tiles else None
        xhat, _ = _normalised(h_ref[...], y_ref[...], _mod_row(m_ref, 0, is_ctx), alpha)
        hn = xhat * m_ref[6:7, :] + m_ref[7:8, :]
        hn_ref[...] = hn
        if with_mod:
            rest[0][...] = hn * (1.0 + _mod_row(m_ref, 2, is_ctx)) + _mod_row(m_ref, 1, is_ctx)

    blk = pl.BlockSpec((LN_ROWS, d), lambda i: (i, 0))
    n_out = 2 if with_mod else 1
    return pl.pallas_call(
        body, name=name,
        grid=(t // LN_ROWS,),
        in_specs=[blk, blk, pl.BlockSpec((MOD_ROWS, d), lambda i: (0, 0))],
        out_specs=[blk] * n_out,
        out_shape=[jax.ShapeDtypeStruct((t, d), F32)] * n_out,
        compiler_params=pltpu.CompilerParams(dimension_semantics=('parallel',)),
    )(h, y, mods)


def _res_ln_bwd_call(h, y, mods, d_hn, d_hm, t_ctx, alpha, name):
    t, d = h.shape
    n_ctx_tiles = t_ctx // LN_ROWS
    with_mod = d_hm is not None

    def tile_sum(x):
        return jnp.sum(x.reshape(LN_ROWS // SUBLANES, SUBLANES, d), axis=0)

    def body(*refs):
        if with_mod:
            h_ref, y_ref, m_ref, dhn_ref, dhm_ref, dh_ref, dy_ref, acc_ref = refs
        else:
            h_ref, y_ref, m_ref, dhn_ref, dh_ref, dy_ref, acc_ref = refs

        @pl.when(pl.program_id(0) == 0)
        def _():
            acc_ref[...] = jnp.zeros_like(acc_ref)

        is_ctx = (pl.program_id(0) < n_ctx_tiles) if n_ctx_tiles else None
        gate = _mod_row(m_ref, 0, is_ctx)
        y = y_ref[...]
        xhat, rstd = _normalised(h_ref[...], y, gate, alpha)
        ln_g = m_ref[6:7, :]
        dhn = dhn_ref[...]
        base = 3 * SUBLANES if is_ctx is None else jnp.where(is_ctx, 0, 3 * SUBLANES)

        def add_to(row, part):
            if isinstance(row, int):
                acc_ref[row:row + SUBLANES, :] += part
            else:
                acc_ref[pl.ds(pl.multiple_of(row, SUBLANES), SUBLANES), :] += part

        if with_mod:
            dhm = dhm_ref[...]
            hn = xhat * ln_g + m_ref[7:8, :]
            add_to(base + SUBLANES, tile_sum(dhm))
            add_to(base + 2 * SUBLANES, tile_sum(dhm * hn))
            dhn = dhn + dhm * (1.0 + _mod_row(m_ref, 2, is_ctx))
        add_to(6 * SUBLANES, tile_sum(dhn * xhat))
        add_to(7 * SUBLANES, tile_sum(dhn))
        dx = dhn * ln_g
        dr = rstd * (dx - jnp.mean(dx, axis=-1, keepdims=True) - xhat * jnp.mean(dx * xhat, axis=-1, keepdims=True))
        dh_ref[...] = alpha * dr
        dy_ref[...] = gate * dr
        add_to(base, tile_sum(dr * y))

    blk = pl.BlockSpec((LN_ROWS, d), lambda i: (i, 0))
    operands = [h, y, mods, d_hn] + ([d_hm] if with_mod else [])
    return pl.pallas_call(
        body, name=name,
        grid=(t // LN_ROWS,),
        in_specs=[blk, blk, pl.BlockSpec((MOD_ROWS, d), lambda i: (0, 0)), blk] + ([blk] if with_mod else []),
        out_specs=[blk, blk, pl.BlockSpec((MOD_ROWS * SUBLANES, d), lambda i: (0, 0))],
        out_shape=[jax.ShapeDtypeStruct((t, d), F32), jax.ShapeDtypeStruct((t, d), F32),
                   jax.ShapeDtypeStruct((MOD_ROWS * SUBLANES, d), F32)],
        compiler_params=pltpu.CompilerParams(dimension_semantics=('arbitrary',)),
    )(*operands)


@functools.partial(jax.custom_vjp, nondiff_argnums=(3, 4, 5, 6))
def _res_ln_mod(h, y, mods, t_ctx, alpha, with_mod, name):
    return tuple(_res_ln_fwd_call(h, y, mods, t_ctx, alpha, with_mod, name + '_fwd'))


def _res_ln_mod_fwd(h, y, mods, t_ctx, alpha, with_mod, name):
    return tuple(_res_ln_fwd_call(h, y, mods, t_ctx, alpha, with_mod, name + '_fwd')), (h, y, mods)


def _res_ln_mod_bwd(t_ctx, alpha, with_mod, name, res, cts):
    h, y, mods = res
    d_h, d_y, acc = _res_ln_bwd_call(h, y, mods, cts[0], cts[1] if with_mod else None, t_ctx, alpha, name + '_bwd')
    return d_h, d_y, jnp.sum(acc.reshape(MOD_ROWS, SUBLANES, h.shape[1]), axis=1)


_res_ln_mod.defvjp(_res_ln_mod_fwd, _res_ln_mod_bwd)


def _window(ref, j, rows, cols, axis):
    if axis == 0:
        return ref.at[pl.ds(j * rows, rows), :]
    return ref.at[:, pl.ds(j * cols, cols)]


def _all_gather(block, axis, name):
    rows, cols = block.shape

    def body(x_ref, out_ref, send_sems, recv_sems, local_sem):
        x, y, c = _my_coords()
        me, sibling = (x, y, c), (x, y, 1 - c)
        chips = [(1 - x, y), (x, 1 - y), (1 - x, 1 - y)]

        def win(px, py, pc):
            return _window(out_ref, 4 * px + 2 * py + pc, rows, cols, axis)

        def copy(k, blk, to, src=None):
            return pltpu.make_async_remote_copy(
                src_ref=win(*blk) if src is None else src, dst_ref=win(*blk),
                send_sem=send_sems.at[k], recv_sem=recv_sems.at[k], device_id=to, device_id_type=MESH)

        mine = pltpu.make_async_copy(x_ref, win(*me), local_sem)
        mine.start()
        first = [copy(0, me, sibling, src=x_ref)]
        first += [copy(1 + j, me, (*chip, c), src=x_ref) for j, chip in enumerate(chips)]
        for cp in first:
            cp.start()
        passed = [copy(4 + j, (*chip, c), sibling) for j, chip in enumerate(chips)]
        for j, chip in enumerate(chips):
            copy(1 + j, (*chip, c), me).wait_recv()
            passed[j].start()
        copy(0, sibling, me).wait_recv()
        for j, chip in enumerate(chips):
            copy(4 + j, (*chip, 1 - c), me).wait_recv()
        for cp in first + passed:
            cp.wait_send()
        mine.wait()

    out_shape = (N_DEV * rows, cols) if axis == 0 else (rows, N_DEV * cols)
    return pl.pallas_call(
        body, name=name,
        out_shape=jax.ShapeDtypeStruct(out_shape, block.dtype),
        in_specs=[ANY], out_specs=ANY,
        scratch_shapes=[pltpu.SemaphoreType.DMA((7,)), pltpu.SemaphoreType.DMA((7,)), pltpu.SemaphoreType.DMA],
    )(block)


AG_COLLECTIVE_ID = 1


def _all_gather_sequencer(block, axis, name):
    rows, cols = block.shape
    out_shape = (N_DEV * rows, cols) if axis == 0 else (rows, N_DEV * cols)
    x_ref = jax.new_ref(block, memory_space=pltpu.MemorySpace.HBM)
    out_ref = jax.empty_ref(jax.ShapeDtypeStruct(out_shape, block.dtype), memory_space=pltpu.MemorySpace.HBM)

    @pl.kernel(mesh=plsc.ScalarSubcoreMesh(axis_name='sequencer', num_cores=1), name=name,
               scratch_types=(pltpu.SemaphoreType.DMA((7,)), pltpu.SemaphoreType.DMA((7,)), pltpu.SemaphoreType.DMA),
               compiler_params=pltpu.CompilerParams(collective_id=AG_COLLECTIVE_ID))
    def launch(send_sems, recv_sems, local_sem):
        x, y, c = _my_coords()
        me, sibling = (x, y, c), (x, y, 1 - c)
        chips = [(1 - x, y), (x, 1 - y), (1 - x, 1 - y)]
        barrier = pltpu.get_barrier_semaphore()
        for peer in [sibling] + [(*chip, c) for chip in chips]:
            pl.semaphore_signal(barrier, inc=1, device_id=peer, device_id_type=MESH)
        pl.semaphore_wait(barrier, 1 + len(chips))

        def win(px, py, pc):
            return _window(out_ref, 4 * px + 2 * py + pc, rows, cols, axis)

        def copy(k, blk, to, src=None):
            return pltpu.make_async_remote_copy(
                src_ref=win(*blk) if src is None else src, dst_ref=win(*blk),
                send_sem=send_sems.at[k], recv_sem=recv_sems.at[k], device_id=to, device_id_type=MESH)

        mine = pltpu.make_async_copy(x_ref, win(*me), local_sem)
        mine.start()
        first = [copy(0, me, sibling, src=x_ref)]
        first += [copy(1 + j, me, (*chip, c), src=x_ref) for j, chip in enumerate(chips)]
        for cp in first:
            cp.start()
        passed = [copy(4 + j, (*chip, c), sibling) for j, chip in enumerate(chips)]
        for j, chip in enumerate(chips):
            copy(1 + j, (*chip, c), me).wait_recv()
            passed[j].start()
        copy(0, sibling, me).wait_recv()
        for j, chip in enumerate(chips):
            copy(4 + j, (*chip, 1 - c), me).wait_recv()
        for cp in first + passed:
            cp.wait_send()
        mine.wait()

    launch()
    return out_ref[...]


PAIR_COLLECTIVE_ID = 2
CHIPS_COLLECTIVE_ID = 3


def _handshake(peers):
    barrier = pltpu.get_barrier_semaphore()
    for peer in peers:
        pl.semaphore_signal(barrier, inc=1, device_id=peer, device_id_type=MESH)
    pl.semaphore_wait(barrier, len(peers))


def _launch_on_sequencer(body, operand, result, scratch_types, collective_id, name):
    operand_ref = jax.new_ref(operand, memory_space=pltpu.MemorySpace.HBM)
    result_ref = jax.empty_ref(result, memory_space=pltpu.MemorySpace.HBM)
    pl.kernel(functools.partial(body, operand_ref, result_ref),
              mesh=plsc.ScalarSubcoreMesh(axis_name='sequencer', num_cores=1), name=name,
              scratch_types=scratch_types, compiler_params=pltpu.CompilerParams(collective_id=collective_id))()
    return result_ref[...]


def _exchange_pair(full, rows, cols, axis, sequencer, name):
    def body(g_ref, land_ref, send_sems, recv_sems):
        x, y, c = _my_coords()
        if sequencer:
            _handshake([(x, y, 1 - c)])
        copies = []
        for k in range(N_CHIP):
            j = 2 * k + (1 - c)
            copies.append(pltpu.make_async_remote_copy(
                src_ref=_window(g_ref, j, rows, cols, axis), dst_ref=land_ref.at[k],
                send_sem=send_sems.at[k], recv_sem=recv_sems.at[k], device_id=(x, y, 1 - c), device_id_type=MESH))
        for cp in copies:
            cp.start()
        for cp in copies:
            cp.wait()

    result = jax.ShapeDtypeStruct((N_CHIP, rows, cols), full.dtype)
    sems = [pltpu.SemaphoreType.DMA((N_CHIP,)), pltpu.SemaphoreType.DMA((N_CHIP,))]
    if sequencer:
        return _launch_on_sequencer(body, full, result, sems, PAIR_COLLECTIVE_ID, name)
    return pl.pallas_call(body, name=name, out_shape=result, in_specs=[ANY], out_specs=ANY, scratch_shapes=sems)(full)


def _elementwise_tiles(rows, cols):
    tc = _first_divisor(cols, (1024, 512, 384, 256, 128))
    tr = _first_divisor(rows, (512, 256, 128, 64, 32, 16, 8))
    return tr, tc


def _pair_add(full, land, rows, cols, axis, wire_dtype, name):
    tr, tc = _elementwise_tiles(rows, cols)
    c_arr = jnp.reshape(lax.axis_index('c'), (1,)).astype(jnp.int32)

    def full_map(k, i, j, c_ref):
        blk = 2 * k + c_ref[0]
        if axis == 0:
            return (blk * (rows // tr) + i, j)
        return (i, blk * (cols // tc) + j)

    def body(c_ref, g_ref, l_ref, o_ref):
        del c_ref
        o_ref[0] = (g_ref[...] + l_ref[0]).astype(o_ref.dtype)

    return pl.pallas_call(
        body, name=name,
        grid_spec=pltpu.PrefetchScalarGridSpec(
            num_scalar_prefetch=1,
            grid=(N_CHIP, rows // tr, cols // tc),
            in_specs=[pl.BlockSpec((tr, tc), full_map),
                      pl.BlockSpec((1, tr, tc), lambda k, i, j, c_ref: (k, i, j))],
            out_specs=pl.BlockSpec((1, tr, tc), lambda k, i, j, c_ref: (k, i, j)),
        ),
        out_shape=jax.ShapeDtypeStruct((N_CHIP, rows, cols), wire_dtype),
        compiler_params=pltpu.CompilerParams(dimension_semantics=('parallel', 'parallel', 'parallel')),
    )(c_arr, full, land)


def _exchange_chips(sums, sequencer, name):
    _, rows, cols = sums.shape

    def body(s_ref, land_ref, send_sems, recv_sems, local_sem):
        x, y, c = _my_coords()
        my_chip = 2 * x + y
        chips = [(1 - x, y), (x, 1 - y), (1 - x, 1 - y)]
        if sequencer:
            _handshake([(px, py, c) for px, py in chips])
        mine = pltpu.make_async_copy(s_ref.at[my_chip], land_ref.at[my_chip], local_sem)
        mine.start()
        sends = []
        for t, (px, py) in enumerate(chips):
            sends.append(pltpu.make_async_remote_copy(
                src_ref=s_ref.at[2 * px + py], dst_ref=land_ref.at[my_chip],
                send_sem=send_sems.at[t], recv_sem=recv_sems.at[t], device_id=(px, py, c), device_id_type=MESH))
        for cp in sends:
            cp.start()
        for t, (px, py) in enumerate(chips):
            pltpu.make_async_remote_copy(
                src_ref=s_ref.at[my_chip], dst_ref=land_ref.at[2 * px + py],
                send_sem=send_sems.at[t], recv_sem=recv_sems.at[t], device_id=(px, py, c),
                device_id_type=MESH).wait_recv()
        for cp in sends:
            cp.wait_send()
        mine.wait()

    result = jax.ShapeDtypeStruct((N_CHIP, rows, cols), sums.dtype)
    sems = [pltpu.SemaphoreType.DMA((3,)), pltpu.SemaphoreType.DMA((3,)), pltpu.SemaphoreType.DMA]
    if sequencer:
        return _launch_on_sequencer(body, sums, result, sems, CHIPS_COLLECTIVE_ID, name)
    return pl.pallas_call(body, name=name, out_shape=result, in_specs=[ANY], out_specs=ANY, scratch_shapes=sems)(sums)


def _reduce_scatter_slots(full, rows, cols, axis, wire_dtype, sequencer, name):
    land = _exchange_pair(full, rows, cols, axis, sequencer, name + '_pair')
    sums = _pair_add(full, land, rows, cols, axis, wire_dtype, name + '_add')
    return _exchange_chips(sums, sequencer, name + '_chips')


def _sum_slots(slots, name):
    n_slots, rows, cols = slots.shape
    tr, tc = _elementwise_tiles(rows, cols)

    def body(s_ref, o_ref):
        g = s_ref[0]
        for s in range(1, n_slots):
            g = g + s_ref[s]
        o_ref[...] = g

    return pl.pallas_call(
        body, name=name,
        grid=(rows // tr, cols // tc),
        in_specs=[pl.BlockSpec((n_slots, tr, tc), lambda i, j: (0, i, j))],
        out_specs=pl.BlockSpec((tr, tc), lambda i, j: (i, j)),
        out_shape=jax.ShapeDtypeStruct((rows, cols), F32),
        compiler_params=pltpu.CompilerParams(dimension_semantics=('parallel', 'parallel')),
    )(slots)


def _sum_adamw(slots, w, m, v, name):
    n_slots, rows, cols = slots.shape
    tr, tc = _elementwise_tiles(rows, cols)
    c1 = 1.0 - ADAM_B1 ** ADAM_STEP
    c2 = 1.0 - ADAM_B2 ** ADAM_STEP

    def body(s_ref, w_ref, m_ref, v_ref, g_out, d_out, m_out, v_out):
        g = s_ref[0].astype(F32)
        for s in range(1, n_slots):
            g = g + s_ref[s].astype(F32)
        m_new = ADAM_B1 * m_ref[...] + (1.0 - ADAM_B1) * g
        v_new = ADAM_B2 * v_ref[...] + (1.0 - ADAM_B2) * (g * g)
        m_hat = m_new / c1
        v_hat = v_new / c2
        g_out[...] = g
        d_out[...] = -ADAM_LR * (m_hat / (jnp.sqrt(v_hat) + ADAM_EPS) + ADAM_WD * w_ref[...])
        m_out[...] = m_new
        v_out[...] = v_new

    blk = pl.BlockSpec((tr, tc), lambda i, j: (i, j))
    shape = jax.ShapeDtypeStruct((rows, cols), F32)
    return pl.pallas_call(
        body, name=name,
        grid=(rows // tr, cols // tc),
        in_specs=[pl.BlockSpec((n_slots, tr, tc), lambda i, j: (0, i, j)), blk, blk, blk],
        out_specs=[blk, blk, blk, blk],
        out_shape=[shape, shape, shape, shape],
        compiler_params=pltpu.CompilerParams(dimension_semantics=('parallel', 'parallel')),
    )(slots, w, m, v)


def _rows_of(shape):
    size = int(np.prod(shape)) if len(shape) else 1
    return _round_up(_round_up(size, LANES) // LANES, SUBLANES)


def _pack(arrays, rows_multiple):
    parts = []
    for arr in arrays:
        flat = jnp.ravel(arr).astype(F32)
        rows = _rows_of(arr.shape)
        parts.append(jnp.pad(flat, (0, rows * LANES - flat.shape[0])).reshape(rows, LANES))
    total = sum(p.shape[0] for p in parts)
    pad = _round_up(total, rows_multiple) - total
    if pad:
        parts.append(jnp.zeros((pad, LANES), F32))
    return jnp.concatenate(parts, axis=0)


def _unpack(slab, shapes):
    out, r0 = [], 0
    for s in shapes:
        size = int(np.prod(s)) if len(s) else 1
        rows = _rows_of(s)
        out.append(slab[r0:r0 + rows].reshape(-1)[:size].reshape(s))
        r0 += rows
    return out


def _layer_norm(x, g, b):
    mu = jnp.mean(x, -1, keepdims=True)
    var = jnp.mean(jnp.square(x - mu), -1, keepdims=True)
    return (x - mu) * lax.rsqrt(var + LN_EPS) * g + b


def _modulate(x, shift, scale):
    return x * (1 + scale) + shift


def _take_cols(p, per_dev, per_dev_padded, lo, hi):
    parts = []
    while lo < hi:
        dev, off = divmod(lo, per_dev)
        n = min(hi - lo, per_dev - off)
        parts.append(p[:, dev * per_dev_padded + off: dev * per_dev_padded + off + n])
        lo += n
    return parts[0] if len(parts) == 1 else jnp.concatenate(parts, axis=1)


def _block_diag(blocks, nb):
    g, a, b = blocks.shape
    per = g // nb
    eye = jnp.eye(per, dtype=blocks.dtype)
    return (blocks.reshape(nb, per, a, 1, b) * eye[None, :, None, :, None]).reshape(nb, per * a, per * b)


def _s5_mixer(u, t_ctx, prm, name):
    t, gw = u.shape
    groups = gw // S5_CH
    ys = []
    for direction, reverse in enumerate((False, True)):
        lam = lax.complex(prm['s5_a_re'][direction], prm['s5_a_im'][direction])
        step = jnp.exp(prm['s5_log_step'][direction])[:, None]
        a_bar = jnp.exp(lam * step)
        b_bar = ((a_bar - 1.0) / lam)[..., None] * lax.complex(prm['s5_b_re'][direction],
                                                               prm['s5_b_im'][direction])
        c_mat = lax.complex(prm['s5_c_re'][direction], prm['s5_c_im'][direction])
        nb = (groups * S5_STATE) // _scan_lanes(groups * S5_STATE)
        b_t = jnp.swapaxes(b_bar, 1, 2)
        b_blocks = jnp.concatenate([_block_diag(jnp.real(b_t), nb), _block_diag(jnp.imag(b_t), nb)], axis=2)
        c_t = jnp.swapaxes(c_mat, 1, 2)
        c_blocks = jnp.concatenate([_block_diag(jnp.real(c_t), nb), -_block_diag(jnp.imag(c_t), nb)], axis=1)
        a_rows = jnp.stack([jnp.real(a_bar).reshape(-1), jnp.imag(a_bar).reshape(-1)])
        bu = _blocked_linear(u, b_blocks, f'{name}_bu{direction}')
        h = _s5_scan(a_rows, bu, t_ctx, reverse, f'{name}_scan{direction}')
        ys.append(_blocked_linear(h, c_blocks, f'{name}_y{direction}'))
    y = ys[0] + ys[1] + prm['s5_d'][None, :] * u
    z = jax.nn.gelu(y)
    return z * jax.nn.sigmoid(_linear(z, prm['s5_glu_w'], f'{name}_glu') + prm['s5_glu_b'])


def _chunk_gating(u, v, prm):
    t, gw = u.shape
    hd = gw // SG_HEADS
    u = jax.nn.gelu(u)
    v = jax.nn.gelu(v).reshape(t // SG_CHUNK, SG_CHUNK, SG_HEADS, hd)
    v = _layer_norm(v, prm['sg_ln_g'].reshape(SG_HEADS, hd), prm['sg_ln_b'].reshape(SG_HEADS, hd))
    s = jnp.einsum('hij,cjhd->cihd', prm['sg_w'], v) + prm['sg_b'].T[None, :, :, None]
    return u * s.reshape(t, gw)


def _pool_mixer(p, prm):
    l, gw = p.shape
    pd = gw // len(POOL_WINDOWS)
    t = np.arange(l)
    outs = []
    for g, win in enumerate(POOL_WINDOWS):
        lo = np.clip(t - win // 2, 0, l - 1)
        hi = np.clip(t + win // 2 - 1, 0, l - 1)
        pg = p[:, g * pd:(g + 1) * pd]
        padded = jnp.pad(pg, ((win // 2, win // 2), (0, 0)))
        total = padded[0:l]
        for d in range(1, win):
            total = total + padded[d:d + l]
        mean = total / jnp.asarray((hi - lo + 1).astype(np.float32))[:, None]
        outs.append(jnp.einsum('lc,cd->ld', mean - pg, prm['pool_w'][g]))
    y = jnp.concatenate(outs, axis=-1) + prm['pool_b']
    return y * prm['pool_scale']


def _dw_conv1d(x, w, b, pad):
    l = x.shape[0]
    xp = jnp.pad(x, (pad, (0, 0)))
    y = xp[0:l] * w[0]
    for k in range(1, w.shape[0]):
        y = y + xp[k:k + l] * w[k]
    return y + b


def _m2_prepare(xbc, dt_raw, prm, gw):
    heads = gw // M2_HEAD_DIM
    xbc = jax.nn.silu(_dw_conv1d(xbc, prm['m2_conv_w'], prm['m2_conv_b'], M2_PAD))
    l = xbc.shape[0]
    n_bc = M2_GROUPS * M2_STATE
    rep = heads // M2_GROUPS
    xs = xbc[:, :gw].reshape(l, heads, M2_HEAD_DIM)
    bm = jnp.repeat(xbc[:, gw:gw + n_bc].reshape(l, M2_GROUPS, M2_STATE), rep, axis=1)
    cm = jnp.repeat(xbc[:, gw + n_bc:].reshape(l, M2_GROUPS, M2_STATE), rep, axis=1)
    dt = jax.nn.softplus(dt_raw.reshape(l, 2, heads) + prm['m2_dt_bias'])
    return xs, bm, cm, dt


def _walk_cumsum(x, axis, reverse):
    return jnp.flip(jnp.cumsum(jnp.flip(x, axis), axis), axis) if reverse else jnp.cumsum(x, axis)


def _ssd_scan(xs, dt, a, bm, cm, h0, need_y, reverse):
    l, nh, hp = xs.shape
    nc = l // M2_CHUNK

    def chunks(t):
        return t.reshape((nc, M2_CHUNK) + t.shape[1:])

    def visited_before(n, strictly):
        ones = jnp.ones((n, n), bool)
        return jnp.triu(ones, 1 if strictly else 0) if reverse else jnp.tril(ones, -1 if strictly else 0)

    xd = chunks(xs * dt[..., None])
    bc, cc = chunks(bm), chunks(cm)
    a_cum = _walk_cumsum(chunks(dt * a), 1, reverse)
    a_tot = a_cum[:, 0] if reverse else a_cum[:, -1]
    decay_end = jnp.exp(a_tot[:, None] - a_cum)
    chunk_states = jnp.einsum('cqhn,cqh,cqhp->chpn', bc, decay_end, xd)

    cum = _walk_cumsum(a_tot, 0, reverse)
    before = cum - a_tot
    carry_w = jnp.exp(jnp.where(visited_before(nc, True)[:, :, None], before[:, None, :] - cum[None, :, :],
                                -jnp.inf))
    whole = cum[0] if reverse else cum[-1]
    h_final = (jnp.exp(whole)[:, None, None] * h0
               + jnp.einsum('dh,dhpn->hpn', jnp.exp(whole[None, :] - cum), chunk_states,
                            precision=lax.Precision.HIGHEST))
    if not need_y:
        return None, h_final
    h_prev = (jnp.exp(before)[:, :, None, None] * h0[None]
              + jnp.einsum('cdh,dhpn->chpn', carry_w, chunk_states, precision=lax.Precision.HIGHEST))
    seg = a_cum[:, :, None, :] - a_cum[:, None, :, :]
    decay = jnp.exp(jnp.where(visited_before(M2_CHUNK, False)[None, :, :, None], seg, -jnp.inf))
    scores = jnp.einsum('cihn,cjhn->cijh', cc, bc) * decay
    y = (jnp.einsum('cijh,cjhp->cihp', scores, xd)
         + jnp.einsum('cihn,chpn->cihp', cc, h_prev) * jnp.exp(a_cum)[..., None])
    return y.reshape(l, nh, hp), h_final


def _ssd_direction(inputs, direction, a, h0, need_y):
    xs, bm, cm, dt = inputs
    return _ssd_scan(xs, dt[:, direction], a, bm, cm, h0, need_y, direction == 1)


def _gated_rmsnorm(y, z, w):
    l, gw = z.shape
    g = (y * jax.nn.silu(z)).reshape(l, M2_GROUPS, gw // M2_GROUPS)
    g = g * lax.rsqrt(jnp.mean(jnp.square(g), -1, keepdims=True) + RMS_EPS)
    return g.reshape(l, gw) * w


def _mamba2_mixer(z, xbc, dt_raw, t_ctx, prm, need_ctx):
    gw = z.shape[1]
    heads = gw // M2_HEAD_DIM
    ctx_in = _m2_prepare(xbc[:t_ctx], dt_raw[:t_ctx], prm, gw)
    lat_in = _m2_prepare(xbc[t_ctx:], dt_raw[t_ctx:], prm, gw)
    a = -jnp.exp(prm['m2_a_log'])
    ys_ctx, ys_lat = [], []
    for direction in range(2):
        h0 = jnp.zeros((heads, M2_HEAD_DIM, M2_STATE), F32)
        y_c, h_c = _ssd_direction(ctx_in, direction, a[direction], h0, need_ctx)
        y_l, _ = _ssd_direction(lat_in, direction, a[direction], h_c, True)
        ys_lat.append(y_l)
        if need_ctx:
            ys_ctx.append(y_c)
    d_h = prm['m2_d'][None, :, None]

    def finish(ys, xs, zz):
        y = (ys[0] + ys[1] + d_h * xs).reshape(zz.shape[0], gw)
        return _gated_rmsnorm(y, zz, prm['m2_norm_w'])

    y_lat = finish(ys_lat, lat_in[0], z[t_ctx:])
    if need_ctx:
        return jnp.concatenate([finish(ys_ctx, ctx_in[0], z[:t_ctx]), y_lat], axis=0)
    return y_lat


def _forward_loss(dp, consts, dims):
    depth, t_ctx, d_model = dims['depth'], dims['t_ctx'], dims['d_model']
    gw = d_model // 4
    alpha = (2 * depth) ** 0.25
    in_sizes = (gw, gw, gw, gw, gw, gw + 2 * M2_GROUPS * M2_STATE, 2 * (gw // M2_HEAD_DIM))
    in_offs = np.concatenate([[0], np.cumsum(in_sizes)])
    ml = [[dp['mod_lat'][i, k][None, :] for k in range(6)] for i in range(depth)]
    mc = [[dp['mod_ctx'][i, k][None, :] for k in range(6)] for i in range(depth)]
    zero_row = jnp.zeros((1, d_model), F32)
    h = jnp.concatenate([consts['ctx'], dp['x']], axis=0)
    hm = jnp.concatenate([_modulate(consts['ctx'], mc[0][0], mc[0][1]), _modulate(dp['x'], ml[0][0], ml[0][1])],
                         axis=0)
    for i in range(depth):
        need_ctx = i < depth - 1
        prm = {k: v[i] for k, v in dp['small'].items()}
        p = _linear_tap(hm, consts['w_in'][i], dp['taps']['w_in'][i], f'in{i}')
        seg = [_take_cols(p, dims['in_per_dev'], dims['in_per_dev_pad'], int(in_offs[s]), int(in_offs[s + 1]))
               for s in range(7)]
        ya = _s5_mixer(seg[0], t_ctx, prm, f's5_{i}')
        yb = _chunk_gating(seg[1], seg[2], prm)
        yc = jnp.concatenate([_pool_mixer(seg[3][:t_ctx], prm), _pool_mixer(seg[3][t_ctx:], prm)], axis=0)
        yd = _mamba2_mixer(seg[4], seg[5], seg[6], t_ctx, prm, need_ctx)
        if need_ctx:
            mix_in = jnp.concatenate([ya, yb, yc, yd], axis=1)
        else:
            mix_in = jnp.concatenate([ya[t_ctx:], yb[t_ctx:], yc[t_ctx:], yd], axis=1)
        mix = _linear_tap(mix_in, consts['w_out'][i], dp['taps']['w_out'][i], f'out{i}')
        t_c = t_ctx if need_ctx else 0
        if not need_ctx and h.shape[0] != mix.shape[0]:
            h = h[t_ctx:]
        mods = jnp.concatenate([mc[i][2], mc[i][3], mc[i][4], ml[i][2], ml[i][3], ml[i][4],
                                prm['ln1_g'][None], prm['ln1_b'][None]], axis=0)
        h, hm = _res_ln_mod(h, mix, mods, t_c, alpha, True, f'ln1_{i}')

        f_out = _ffn_block(hm, consts['ffn_w_up'][i], dp['taps']['ffn_w_up'][i], prm['ffn_conv_w'], prm['ffn_conv_b'],
                           consts['ffn_w_down'][i], dp['taps']['ffn_w_down'][i], t_c,
                           (dims['up_per_dev'], dims['up_per_dev_pad'], dims['ffn_hidden']), f'ffn{i}')
        if need_ctx:
            mods = jnp.concatenate([mc[i][5], mc[i + 1][0], mc[i + 1][1], ml[i][5], ml[i + 1][0], ml[i + 1][1],
                                    prm['ln2_g'][None], prm['ln2_b'][None]], axis=0)
            h, hm = _res_ln_mod(h, f_out, mods, t_c, alpha, True, f'ln2_{i}')
        else:
            mods = jnp.concatenate([mc[i][5], zero_row, zero_row, ml[i][5], zero_row, zero_row,
                                    prm['ln2_g'][None], prm['ln2_b'][None]], axis=0)
            h = _res_ln_mod(h, f_out, mods, t_c, alpha, False, f'ln2_{i}')[0]
    err = jnp.square(h - consts['target'])
    return 0.5 * jnp.sum(jnp.mean(err, axis=-1))


def _silu_grad(x):
    s = jax.nn.sigmoid(x)
    return s * (1 + x * (1 - s))


def _step(w, m, v, x, c, ctx, target):
    depth, d_model, ada_cols = w['w_ada'].shape
    t_ctx, t_lat = ctx.shape[1], x.shape[1]
    me = _my_index()
    in_per_dev = w['w_in'].shape[2]
    in_pad = _round_up(in_per_dev, LANES)
    up_per_dev = w['ffn_w_up'].shape[2]
    up_pad = _round_up(up_per_dev, LANES)
    f_hidden = w['ffn_w_down'].shape[1] * N_DEV
    dims = dict(depth=depth, t_ctx=t_ctx, d_model=d_model, in_per_dev=in_per_dev, in_per_dev_pad=in_pad,
                up_per_dev=up_per_dev, up_per_dev_pad=up_pad, ffn_hidden=f_hidden)

    rows16 = 2 * SUBLANES
    silu_c_all = _all_gather(jnp.pad(jax.nn.silu(c), ((0, SUBLANES - 1), (0, 0))), 0, 'ag_c')
    silu_c_all = silu_c_all.reshape(N_DEV, SUBLANES, d_model)[:, 0]
    silu_cc = jax.nn.silu(w['c_ctx'])
    ada_in = jnp.concatenate([silu_c_all, silu_cc[None], jnp.zeros((rows16 - N_DEV - 1, d_model), F32)], axis=0)
    mod_loc = jnp.concatenate([_matmul(ada_in, w['w_ada'][i], 'nn', 'ada_fwd') for i in range(depth)], axis=0)
    mod_all = _all_gather(mod_loc, 1, 'ag_mod').reshape(depth, rows16, 6 * d_model)
    mod_all = mod_all + w['b_ada'][:, None, :]
    mod_lat = lax.dynamic_index_in_dim(mod_all, me, axis=1, keepdims=False).reshape(depth, 6, d_model)
    mod_ctx = mod_all[:, N_DEV].reshape(depth, 6, d_model)

    def pad_cols(a, to):
        return jnp.pad(a, ((0, 0), (0, to - a.shape[1])))

    gathered = {n: [] for n in BIG}
    for i in range(depth):
        gathered['w_in'].append(_all_gather_sequencer(pad_cols(w['w_in'][i], in_pad).astype(BF16), 1, f'ag_w_in{i}'))
        gathered['w_out'].append(_all_gather_sequencer(w['w_out'][i].astype(BF16), 0, f'ag_w_out{i}'))
        gathered['ffn_w_up'].append(
            _all_gather_sequencer(pad_cols(w['ffn_w_up'][i], up_pad).astype(BF16), 1, f'ag_w_up{i}'))
        gathered['ffn_w_down'].append(_all_gather_sequencer(w['ffn_w_down'][i].astype(BF16), 0, f'ag_w_down{i}'))
    sharded_small = [w[n] for n in COL_SHARDED_SMALL + ROW_SHARDED_SMALL]
    small_slab = _pack(sharded_small, PACK_ROWS)
    slab_rows = small_slab.shape[0]
    small_all = _all_gather(small_slab, 0, 'ag_small').reshape(N_DEV, slab_rows, LANES)
    per_dev = [_unpack(small_all[d], [a.shape for a in sharded_small]) for d in range(N_DEV)]
    small_full = {}
    for k, n in enumerate(COL_SHARDED_SMALL):
        small_full[n] = jnp.concatenate([per_dev[d][k] for d in range(N_DEV)], axis=-1)
    for k, n in enumerate(ROW_SHARDED_SMALL):
        small_full[n] = jnp.concatenate([per_dev[d][len(COL_SHARDED_SMALL) + k] for d in range(N_DEV)], axis=1)

    small = {n: w[n] for n in REPLICATED}
    small.update(small_full)
    taps = {n: [jnp.zeros(gathered[n][i].shape, F32) for i in range(depth)] for n in BIG}
    dp = dict(x=x[0], small=small, mod_lat=mod_lat, mod_ctx=mod_ctx, taps=taps)
    consts = dict(ctx=ctx[0], target=target[0], **gathered)
    loss_local, grads = jax.value_and_grad(functools.partial(_forward_loss, consts=consts, dims=dims))(dp)

    out_g, out_d, out_m, out_v = {}, {}, {}, {}

    def finish_big(name, i, rows, cols, axis, keep_cols):
        slots = _reduce_scatter_slots(grads['taps'][name][i], rows, cols, axis, BF16, True, f'rs_{name}{i}')
        shard = [pad_cols(t[name][i], cols) for t in (w, m, v)]
        res = _sum_adamw(slots, *shard, f'adamw_{name}')
        return [r[:, :keep_cols] for r in res]

    big_specs = {
        'ffn_w_down': (w['ffn_w_down'].shape[1], d_model, 0, d_model),
        'ffn_w_up': (d_model, up_pad, 1, up_per_dev),
        'w_out': (w['w_out'].shape[1], d_model, 0, d_model),
        'w_in': (d_model, in_pad, 1, in_per_dev),
    }
    per_layer = {name: [None] * depth for name in big_specs}
    for i in reversed(range(depth)):
        for name, (rows, cols, axis, keep) in big_specs.items():
            per_layer[name][i] = finish_big(name, i, rows, cols, axis, keep)
    for name in big_specs:
        for k, dst in enumerate((out_g, out_d, out_m, out_v)):
            dst[name] = jnp.stack([per_layer[name][i][k] for i in range(depth)])

    d_lat = grads['mod_lat'].reshape(depth, 6 * d_model)
    d_ctx = grads['mod_ctx'].reshape(depth, 6 * d_model)
    d_rows = jnp.concatenate([d_lat, d_ctx, jnp.zeros((SUBLANES - 2 * depth, 6 * d_model), F32)], axis=0)
    d_all = _all_gather(d_rows, 0, 'ag_dmod').reshape(N_DEV, SUBLANES, 6 * d_model)
    d_lat_all = d_all[:, :depth]
    d_ctx_sum = d_all[0, depth:2 * depth]
    for d in range(1, N_DEV):
        d_ctx_sum = d_ctx_sum + d_all[d, depth:2 * depth]
    g_b_ada = d_ctx_sum
    for d in range(N_DEV):
        g_b_ada = g_b_ada + d_lat_all[d]
    g_w_ada, c_ctx_part = [], jnp.zeros((d_model,), F32)
    for i in range(depth):
        d_mat = jnp.concatenate([d_lat_all[:, i], d_ctx_sum[i][None],
                                 jnp.zeros((rows16 - N_DEV - 1, 6 * d_model), F32)], axis=0)
        d_mine = lax.dynamic_slice_in_dim(d_mat, me * ada_cols, ada_cols, axis=1)
        g_w_ada.append(_matmul(ada_in, d_mine, 'tn', 'ada_dw'))
        back = _matmul(d_mine, w['w_ada'][i], 'nt', 'ada_dx')
        c_ctx_part = c_ctx_part + back[N_DEV]
    c_ctx_part = c_ctx_part * _silu_grad(w['c_ctx'])
    g_w_ada = jnp.stack(g_w_ada).reshape(1, depth * d_model, ada_cols)
    res = _sum_adamw(g_w_ada, *[t['w_ada'].reshape(depth * d_model, ada_cols) for t in (w, m, v)], 'adamw_w_ada')
    for k, dst in enumerate((out_g, out_d, out_m, out_v)):
        dst['w_ada'] = res[k].reshape(depth, d_model, ada_cols)

    reduced_names = REPLICATED[:]
    reduced_names.remove('b_ada')
    reduced_names += list(COL_SHARDED_SMALL + ROW_SHARDED_SMALL)
    to_reduce = [grads['small'][n] for n in reduced_names] + [c_ctx_part, loss_local]
    slab = _pack(to_reduce, N_DEV * PACK_ROWS)
    chunk_rows = slab.shape[0] // N_DEV
    slots = _reduce_scatter_slots(slab, chunk_rows, LANES, 0, F32, False, 'rs_small')
    mine = _sum_slots(slots, 'sum_small')
    summed = _all_gather(mine, 0, 'ag_small_sum')
    parts = _unpack(summed, [a.shape for a in to_reduce])
    g_small = dict(zip(reduced_names, parts[:len(reduced_names)]))
    g_small['c_ctx'] = parts[-2]
    g_small['b_ada'] = g_b_ada
    loss = parts[-1]

    def my_shard(name, full):
        if name in COL_SHARDED_SMALL:
            n = full.shape[-1] // N_DEV
            return lax.dynamic_slice_in_dim(full, me * n, n, axis=full.ndim - 1)
        if name in ROW_SHARDED_SMALL:
            n = full.shape[1] // N_DEV
            return lax.dynamic_slice_in_dim(full, me * n, n, axis=1)
        return full

    small_names = [n for n in WEIGHTS if n not in BIG + ('w_ada',)]
    g_list = [my_shard(n, g_small[n]) for n in small_names]
    g_slab = _pack(g_list, PACK_ROWS)
    res = _sum_adamw(g_slab[None], *[_pack([t[n] for n in small_names], PACK_ROWS) for t in (w, m, v)],
                     'adamw_small')
    shapes = [w[n].shape for n in small_names]
    for k, dst in enumerate((out_g, out_d, out_m, out_v)):
        if k == 0:
            dst.update(dict(zip(small_names, g_list)))
        else:
            dst.update(dict(zip(small_names, _unpack(res[k], shapes))))

    grad_x = grads['x'][None]
    return (loss, grad_x, *[out_g[n] for n in WEIGHTS], *[out_d[n] for n in WEIGHTS],
            *[out_m[n] for n in WEIGHTS], *[out_v[n] for n in WEIGHTS])


def kernel(x, c, ctx, c_ctx, w_ada, b_ada, w_in, w_out, ln1_g, ln1_b, ln2_g, ln2_b, s5_a_re, s5_a_im, s5_b_re, s5_b_im, s5_c_re, s5_c_im, s5_log_step, s5_d, s5_glu_w, s5_glu_b, sg_ln_g, sg_ln_b, sg_w, sg_b, pool_w, pool_b, pool_scale, m2_conv_w, m2_conv_b, m2_dt_bias, m2_a_log, m2_d, m2_norm_w, ffn_w_up, ffn_conv_w, ffn_conv_b, ffn_w_down, loss_target, m_c_ctx, m_w_ada, m_b_ada, m_w_in, m_w_out, m_ln1_g, m_ln1_b, m_ln2_g, m_ln2_b, m_s5_a_re, m_s5_a_im, m_s5_b_re, m_s5_b_im, m_s5_c_re, m_s5_c_im, m_s5_log_step, m_s5_d, m_s5_glu_w, m_s5_glu_b, m_sg_ln_g, m_sg_ln_b, m_sg_w, m_sg_b, m_pool_w, m_pool_b, m_pool_scale, m_m2_conv_w, m_m2_conv_b, m_m2_dt_bias, m_m2_a_log, m_m2_d, m_m2_norm_w, m_ffn_w_up, m_ffn_conv_w, m_ffn_conv_b, m_ffn_w_down, v_c_ctx, v_w_ada, v_b_ada, v_w_in, v_w_out, v_ln1_g, v_ln1_b, v_ln2_g, v_ln2_b, v_s5_a_re, v_s5_a_im, v_s5_b_re, v_s5_b_im, v_s5_c_re, v_s5_c_im, v_s5_log_step, v_s5_d, v_s5_glu_w, v_s5_glu_b, v_sg_ln_g, v_sg_ln_b, v_sg_w, v_sg_b, v_pool_w, v_pool_b, v_pool_scale, v_m2_conv_w, v_m2_conv_b, v_m2_dt_bias, v_m2_a_log, v_m2_d, v_m2_norm_w, v_ffn_w_up, v_ffn_conv_w, v_ffn_conv_b, v_ffn_w_down):
    given = dict(locals())
    w = {n: given[n] for n in WEIGHTS}
    m = {n: given['m_' + n] for n in WEIGHTS}
    v = {n: given['v_' + n] for n in WEIGHTS}
    return _step(w, m, v, x, c, ctx, loss_target)
```

```python
import functools

import jax
import jax.numpy as jnp
import numpy as np
from jax import lax
from jax.experimental import pallas as pl
from jax.experimental.pallas import tpu as pltpu
from jax.experimental.pallas import tpu_sc as plsc

F32 = jnp.float32
BF16 = jnp.bfloat16
MESH = pl.DeviceIdType.MESH
ANY = pl.BlockSpec(memory_space=pl.ANY)

N_DEV = 8
N_CHIP = 4
LANES = 128
SUBLANES = 8
VMEM_LIMIT_BYTES = 48 * 1024 * 1024
MATMUL_VMEM_BUDGET = 36 * 1024 * 1024
PACK_ROWS = 512

GRID_W = 64
S5_CH = 16
S5_STATE = 64
SG_HEADS = 4
SG_CHUNK = 128
POOL_WINDOWS = (2, 4, 8, 16)
M2_HEAD_DIM = 64
M2_STATE = 128
M2_GROUPS = 2
M2_CONV = 4
M2_PAD = (M2_CONV // 2, M2_CONV - 1 - M2_CONV // 2)
M2_CHUNK = 128
FFN_CONV = 3
LN_EPS = 1e-5
RMS_EPS = 1e-5
ADAM_LR = 0.001
ADAM_B1 = 0.9
ADAM_B2 = 0.999
ADAM_EPS = 1e-08
ADAM_WD = 0.01
ADAM_STEP = 10

WEIGHTS = ['c_ctx', 'w_ada', 'b_ada', 'w_in', 'w_out', 'ln1_g', 'ln1_b', 'ln2_g', 'ln2_b', 's5_a_re', 's5_a_im',
           's5_b_re', 's5_b_im', 's5_c_re', 's5_c_im', 's5_log_step', 's5_d', 's5_glu_w', 's5_glu_b', 'sg_ln_g',
           'sg_ln_b', 'sg_w', 'sg_b', 'pool_w', 'pool_b', 'pool_scale', 'm2_conv_w', 'm2_conv_b', 'm2_dt_bias',
           'm2_a_log', 'm2_d', 'm2_norm_w', 'ffn_w_up', 'ffn_conv_w', 'ffn_conv_b', 'ffn_w_down']
BIG = ('w_in', 'w_out', 'ffn_w_up', 'ffn_w_down')
COL_SHARDED_SMALL = ('m2_conv_w', 'ffn_conv_w')
ROW_SHARDED_SMALL = ('s5_glu_w',)
REPLICATED = [n for n in WEIGHTS if n not in BIG + COL_SHARDED_SMALL + ROW_SHARDED_SMALL + ('w_ada', 'c_ctx')]


def _round_up(n, m):
    return (n + m - 1) // m * m


def _my_coords():
    return lax.axis_index('x'), lax.axis_index('y'), lax.axis_index('c')


def _my_index():
    x, y, c = _my_coords()
    return 4 * x + 2 * y + c


def _first_divisor(n, cands):
    for c in cands:
        if n % c == 0:
            return c
    return n


def _matmul_tiles(mode, m, n, k, a_bytes, b_bytes):
    lane_c = (1024, 512, 384, 256, 128)
    sub_c = (1088, 1024, 544, 512, 256, 128, 64, 32, 16)
    tm = _first_divisor(m, lane_c if mode == 'tn' else sub_c)
    tn = _first_divisor(n, lane_c)
    k_c = [c for c in (5632, 4096, 2048, 1408, 1088, 1024, 544, 512, 256, 128) if k % c == 0] or [k]
    if mode == 'tn':
        k_c = [c for c in k_c if c <= 1088] or [k_c[-1]]
    for tk in k_c:
        use = 2 * (tm * tk * a_bytes + tk * tn * b_bytes) + 3 * tm * tn * 4
        if use <= MATMUL_VMEM_BUDGET:
            return tm, tn, tk
    return tm, tn, k_c[-1]


def _matmul(a, b, mode, name, out_dtype=F32):
    if mode == 'nn':
        (m, k), (k2, n) = a.shape, b.shape
    elif mode == 'nt':
        (m, k), (n, k2) = a.shape, b.shape
    else:
        (k, m), (k2, n) = a.shape, b.shape
    assert k == k2, (mode, a.shape, b.shape)
    tm, tn, tk = _matmul_tiles(mode, m, n, k, a.dtype.itemsize, b.dtype.itemsize)
    nk = k // tk
    if mode == 'nn':
        a_spec = pl.BlockSpec((tm, tk), lambda i, j, kk: (i, kk))
        b_spec = pl.BlockSpec((tk, tn), lambda i, j, kk: (kk, j))
        dims = (((1,), (0,)), ((), ()))
    elif mode == 'nt':
        a_spec = pl.BlockSpec((tm, tk), lambda i, j, kk: (i, kk))
        b_spec = pl.BlockSpec((tn, tk), lambda i, j, kk: (j, kk))
        dims = (((1,), (1,)), ((), ()))
    else:
        a_spec = pl.BlockSpec((tk, tm), lambda i, j, kk: (kk, i))
        b_spec = pl.BlockSpec((tk, tn), lambda i, j, kk: (kk, j))
        dims = (((0,), (0,)), ((), ()))

    def body(a_ref, b_ref, o_ref, acc_ref):
        kk = pl.program_id(2)

        @pl.when(kk == 0)
        def _():
            acc_ref[...] = jnp.zeros_like(acc_ref)

        acc_ref[...] += lax.dot_general(a_ref[...].astype(BF16), b_ref[...].astype(BF16), dims,
                                        preferred_element_type=F32)

        @pl.when(kk == nk - 1)
        def _():
            o_ref[...] = acc_ref[...].astype(o_ref.dtype)

    return pl.pallas_call(
        body,
        name=name,
        grid=(m // tm, n // tn, nk),
        in_specs=[a_spec, b_spec],
        out_specs=pl.BlockSpec((tm, tn), lambda i, j, kk: (i, j)),
        out_shape=jax.ShapeDtypeStruct((m, n), out_dtype),
        scratch_shapes=[pltpu.VMEM((tm, tn), F32)],
        compiler_params=pltpu.CompilerParams(dimension_semantics=('parallel', 'parallel', 'arbitrary'),
                                             vmem_limit_bytes=VMEM_LIMIT_BYTES),
    )(a, b)


@functools.partial(jax.custom_vjp, nondiff_argnums=(2,))
def _linear(x, w, name):
    return _matmul(x, w, 'nn', name + '_fwd')


def _linear_fwd(x, w, name):
    return _matmul(x, w, 'nn', name + '_fwd'), (x, w)


def _linear_bwd(name, res, dy):
    x, w = res
    return _matmul(dy, w, 'nt', name + '_dx'), _matmul(x, dy, 'tn', name + '_dw')


_linear.defvjp(_linear_fwd, _linear_bwd)


@functools.partial(jax.custom_vjp, nondiff_argnums=(3,))
def _linear_tap(x, w, tap, name):
    del tap
    return _matmul(x, w, 'nn', name + '_fwd')


def _linear_tap_fwd(x, w, tap, name):
    del tap
    return _matmul(x, w, 'nn', name + '_fwd'), (x, w)


def _linear_tap_bwd(name, res, dy):
    x, w = res
    return _matmul(dy, w, 'nt', name + '_dx'), jnp.zeros_like(w), _matmul(x, dy, 'tn', name + '_dw')


_linear_tap.defvjp(_linear_tap_fwd, _linear_tap_bwd)


@functools.partial(jax.custom_vjp, nondiff_argnums=(4,))
def _linear_tap_carried(x, carrier, w, tap, name):
    del carrier, tap
    return _matmul(x, w, 'nn', name + '_fwd')


def _linear_tap_carried_fwd(x, carrier, w, tap, name):
    del carrier, tap
    return _matmul(x, w, 'nn', name + '_fwd'), (x, w)


def _linear_tap_carried_bwd(name, res, dy):
    x, w = res
    return (jnp.zeros_like(x), _matmul(dy, w, 'nt', name + '_dx'), jnp.zeros_like(w),
            _matmul(x, dy, 'tn', name + '_dw'))


_linear_tap_carried.defvjp(_linear_tap_carried_fwd, _linear_tap_carried_bwd)


def _blocked_matmul(a, b, mode, name):
    t = a.shape[0]
    nb, k, n = b.shape
    tm = _first_divisor(t, (1088, 1024, 544, 512, 256, 128, 64, 32, 16))
    cin, cout = (k, n) if mode == 'nn' else (n, k)
    dims = (((1,), (0,)), ((), ())) if mode == 'nn' else (((1,), (1,)), ((), ()))

    def body(a_ref, b_ref, o_ref):
        o_ref[...] = lax.dot_general(a_ref[...].astype(BF16), b_ref[0].astype(BF16), dims,
                                     preferred_element_type=F32)

    return pl.pallas_call(
        body, name=name, grid=(t // tm, nb),
        in_specs=[pl.BlockSpec((tm, cin), lambda i, j: (i, j)), pl.BlockSpec((1, k, n), lambda i, j: (j, 0, 0))],
        out_specs=pl.BlockSpec((tm, cout), lambda i, j: (i, j)),
        out_shape=jax.ShapeDtypeStruct((t, nb * cout), F32),
        compiler_params=pltpu.CompilerParams(vmem_limit_bytes=VMEM_LIMIT_BYTES,
                                             dimension_semantics=('parallel', 'parallel')),
    )(a, b)


def _blocked_weight_grad(x, dy, nb, name):
    t = x.shape[0]
    k, n = x.shape[1] // nb, dy.shape[1] // nb
    tk = _first_divisor(t, (1088, 1024, 544, 512, 256, 128, 64, 32, 16))
    steps = t // tk

    def body(x_ref, dy_ref, o_ref):
        @pl.when(pl.program_id(1) == 0)
        def _():
            o_ref[...] = jnp.zeros_like(o_ref)

        o_ref[0] += lax.dot_general(x_ref[...].astype(BF16), dy_ref[...].astype(BF16), (((0,), (0,)), ((), ())),
                                    preferred_element_type=F32)

    return pl.pallas_call(
        body, name=name, grid=(nb, steps),
        in_specs=[pl.BlockSpec((tk, k), lambda j, s: (s, j)), pl.BlockSpec((tk, n), lambda j, s: (s, j))],
        out_specs=pl.BlockSpec((1, k, n), lambda j, s: (j, 0, 0)),
        out_shape=jax.ShapeDtypeStruct((nb, k, n), F32),
        compiler_params=pltpu.CompilerParams(vmem_limit_bytes=VMEM_LIMIT_BYTES,
                                             dimension_semantics=('parallel', 'arbitrary')),
    )(x, dy)


@functools.partial(jax.custom_vjp, nondiff_argnums=(2,))
def _blocked_linear(x, w, name):
    return _blocked_matmul(x, w, 'nn', name + '_fwd')


def _blocked_linear_fwd(x, w, name):
    return _blocked_matmul(x, w, 'nn', name + '_fwd'), (x, w)


def _blocked_linear_bwd(name, res, dy):
    x, w = res
    return _blocked_matmul(dy, w, 'nt', name + '_dx'), _blocked_weight_grad(x, dy, w.shape[0], name + '_dw')


_blocked_linear.defvjp(_blocked_linear_fwd, _blocked_linear_bwd)


SCAN_ROWS = 128
SCAN_LANES = 512


def _scan_lanes(n):
    return SCAN_LANES if n % SCAN_LANES == 0 else n


def _scan_call(a, x, h, n_ctx_blocks, ctx_first, reverse_rows, name):
    t, n2 = x.shape
    n = n2 // 2
    nblk = t // SCAN_ROWS
    nlat = nblk - n_ctx_blocks
    lanes = _scan_lanes(n)
    with_da = h is not None

    def block_of(i):
        if ctx_first:
            first_n, first_0, second_0, second_n = n_ctx_blocks, 0, n_ctx_blocks, nlat
        else:
            first_n, first_0, second_0, second_n = nlat, n_ctx_blocks, 0, n_ctx_blocks
        if reverse_rows:
            in_first = first_0 + first_n - 1 - i
            in_second = second_0 + second_n - 1 - (i - first_n)
        else:
            in_first = first_0 + i
            in_second = second_0 + (i - first_n)
        return jnp.where(i < first_n, in_first, in_second)

    groups = SCAN_ROWS // SUBLANES
    first_row = SUBLANES - 1 if reverse_rows else 0
    to_previous = SUBLANES - 1 if reverse_rows else 1

    def body(*refs):
        if with_da:
            a_ref, x_ref, h_ref, o_ref, da_ref, st_ref = refs
        else:
            a_ref, x_ref, o_ref, st_ref = refs

        @pl.when(pl.program_id(1) == 0)
        def _():
            st_ref[...] = jnp.zeros_like(st_ref)
            if with_da:
                da_ref[...] = jnp.zeros_like(da_ref)

        row_id = lax.broadcasted_iota(jnp.int32, (SUBLANES, lanes), 0)
        behind = (SUBLANES - 1 - row_id) if reverse_rows else row_id

        def cmul(pr, pi, qr, qi):
            return pr * qr - pi * qi, pr * qi + pi * qr

        a1 = (jnp.broadcast_to(a_ref[0:1, :], (SUBLANES, lanes)), jnp.broadcast_to(a_ref[1:2, :], (SUBLANES, lanes)))
        a2 = cmul(*a1, *a1)
        a4 = cmul(*a2, *a2)
        pw = a1
        for bit, ak in ((1, a1), (2, a2), (4, a4)):
            nxt = cmul(*pw, *ak)
            pw = (jnp.where((behind & bit) != 0, nxt[0], pw[0]), jnp.where((behind & bit) != 0, nxt[1], pw[1]))

        def group(g, carry):
            in_r, in_i = carry
            gi = (groups - 1 - g) if reverse_rows else g
            start = pl.multiple_of(gi * SUBLANES, SUBLANES)
            xr = x_ref[pl.ds(start, SUBLANES), pl.ds(0, lanes)]
            xi = x_ref[pl.ds(start, SUBLANES), pl.ds(lanes, lanes)]
            for k, ak in ((1, a1), (2, a2), (4, a4)):
                shift = (SUBLANES - k) if reverse_rows else k
                pr = jnp.where(behind >= k, pltpu.roll(xr, shift, axis=0), 0.0)
                pi = jnp.where(behind >= k, pltpu.roll(xi, shift, axis=0), 0.0)
                qr, qi = cmul(*ak, pr, pi)
                xr, xi = xr + qr, xi + qi
            cr, ci = cmul(*pw, in_r, in_i)
            out_r, out_i = xr + cr, xi + ci
            last = 0 if reverse_rows else SUBLANES - 1
            sr = jnp.broadcast_to(out_r[last:last + 1, :], (SUBLANES, lanes))
            si = jnp.broadcast_to(out_i[last:last + 1, :], (SUBLANES, lanes))
            o_ref[pl.ds(start, SUBLANES), pl.ds(0, lanes)] = out_r
            o_ref[pl.ds(start, SUBLANES), pl.ds(lanes, lanes)] = out_i
            if with_da:
                pr = jnp.where(row_id == first_row, in_r, pltpu.roll(out_r, to_previous, axis=0))
                pi = jnp.where(row_id == first_row, in_i, pltpu.roll(out_i, to_previous, axis=0))
                hr = h_ref[pl.ds(start, SUBLANES), pl.ds(0, lanes)]
                hi = h_ref[pl.ds(start, SUBLANES), pl.ds(lanes, lanes)]
                da_ref[0:SUBLANES, :] += hr * pr + hi * pi
                da_ref[SUBLANES:2 * SUBLANES, :] += hr * pi - hi * pr
            return sr, si

        sr, si = lax.fori_loop(0, groups, group, (st_ref[0], st_ref[1]))
        st_ref[0] = sr
        st_ref[1] = si

    row_spec = pl.BlockSpec((SCAN_ROWS, 2 * lanes), lambda j, i: (block_of(i), j))
    in_specs = [pl.BlockSpec((2, lanes), lambda j, i: (0, j)), row_spec]
    out_specs = [row_spec]
    out_shape = [jax.ShapeDtypeStruct((t, n2), F32)]
    operands = [a, x]
    if with_da:
        in_specs.append(row_spec)
        operands.append(h)
        out_specs.append(pl.BlockSpec((2 * SUBLANES, lanes), lambda j, i: (0, j)))
        out_shape.append(jax.ShapeDtypeStruct((2 * SUBLANES, n), F32))
    return pl.pallas_call(
        body,
        name=name,
        grid=(n // lanes, nblk),
        in_specs=in_specs,
        out_specs=out_specs,
        out_shape=out_shape,
        scratch_shapes=[pltpu.VMEM((2, SUBLANES, lanes), F32)],
        compiler_params=pltpu.CompilerParams(dimension_semantics=('parallel', 'arbitrary')),
    )(*operands)


@functools.partial(jax.custom_vjp, nondiff_argnums=(2, 3, 4))
def _s5_scan(a, bu, t_ctx, reverse, name):
    return _scan_call(a, bu, None, t_ctx // SCAN_ROWS, True, reverse, name + '_fwd')[0]


def _s5_scan_fwd(a, bu, t_ctx, reverse, name):
    h = _scan_call(a, bu, None, t_ctx // SCAN_ROWS, True, reverse, name + '_fwd')[0]
    return h, (a, h)


def _s5_scan_bwd(t_ctx, reverse, name, res, dh):
    a, h = res
    a_conj = a * jnp.array([[1.0], [-1.0]], F32)
    g, da = _scan_call(a_conj, dh, h, t_ctx // SCAN_ROWS, False, not reverse, name + '_bwd')
    da = jnp.stack([jnp.sum(da[:SUBLANES], axis=0), jnp.sum(da[SUBLANES:], axis=0)])
    return da, g


_s5_scan.defvjp(_s5_scan_fwd, _s5_scan_bwd)


FFN_LANES = 128
CONV_TAPS = FFN_CONV * FFN_CONV
CONV_W_ROWS = 16
GELU_K = 0.7978845608028654
GELU_C = 0.044715


def _gelu_and_slope(x):
    x2 = x * x
    th = jnp.tanh(GELU_K * (x + GELU_C * x * x2))
    cdf = 0.5 * (1.0 + th)
    slope = cdf + 0.5 * x * (1.0 - th * th) * (GELU_K * (1.0 + 3.0 * GELU_C * x2))
    return x * cdf, slope


def _band_rows(t_lat):
    return _first_divisor(t_lat // GRID_W, (8, 4, 2, 1)) * GRID_W


def _shifted_ctx(x):
    n = x.shape[0]
    row = lax.broadcasted_iota(jnp.int32, x.shape, 0)
    left = jnp.where(row == 0, 0.0, pltpu.roll(x, 1, axis=0))
    right = jnp.where(row == n - 1, 0.0, pltpu.roll(x, n - 1, axis=0))
    return left, x, right


def _shifted_band(src_ref, bidx, n_bands, band, t_ctx, t):
    start = pl.multiple_of(t_ctx + bidx * band, GRID_W)
    top_start = pl.multiple_of(jnp.maximum(start - GRID_W, 0), SUBLANES)
    bot_start = pl.multiple_of(jnp.minimum(start + band, t - GRID_W), SUBLANES)
    top = jnp.where(bidx > 0, src_ref[pl.ds(top_start, GRID_W), :], 0.0)
    bot = jnp.where(bidx < n_bands - 1, src_ref[pl.ds(bot_start, GRID_W), :], 0.0)
    ext = jnp.concatenate([top, src_ref[pl.ds(start, band), :], bot], axis=0)
    n_ext = band + 2 * GRID_W
    col = lax.broadcasted_iota(jnp.int32, ext.shape, 0) & (GRID_W - 1)
    left = jnp.where(col == 0, 0.0, pltpu.roll(ext, 1, axis=0))
    right = jnp.where(col == GRID_W - 1, 0.0, pltpu.roll(ext, n_ext - 1, axis=0))
    return left, ext, right


def _conv_sum(shifted, w_rows, band, flip):
    acc = None
    for i in range(FFN_CONV):
        for j in range(FFN_CONV):
            w = w_rows[(FFN_CONV - 1 - i) * FFN_CONV + (FFN_CONV - 1 - j)] if flip else w_rows[i * FFN_CONV + j]
            term = shifted[j][i * GRID_W:i * GRID_W + band] * w
            acc = term if acc is None else acc + term
    return acc


def _conv_operand(conv_w, conv_b):
    f = conv_b.shape[0]
    return jnp.concatenate([conv_w.reshape(CONV_TAPS, f), conv_b[None],
                            jnp.zeros((CONV_W_ROWS - CONV_TAPS - 1, f), F32)], axis=0)


def _ffn_mid_fwd(gv, w16, t_ctx, name):
    t, f2 = gv.shape
    f = f2 // 2
    lanes = FFN_LANES
    nf = f // lanes
    band = _band_rows(t - t_ctx)
    n_bands = (t - t_ctx) // band
    mid = (FFN_CONV // 2) * FFN_CONV

    def body(g_ref, v_ref, w_ref, o_ref):
        w_rows = [w_ref[k:k + 1, :] for k in range(CONV_TAPS)]
        bias = w_ref[CONV_TAPS:CONV_TAPS + 1, :]
        if t_ctx:
            sh = _shifted_ctx(g_ref[0:t_ctx, :])
            pre = sh[0] * w_rows[mid] + sh[1] * w_rows[mid + 1] + sh[2] * w_rows[mid + 2] + bias
            o_ref[0:t_ctx, :] = (_gelu_and_slope(pre)[0] * v_ref[0:t_ctx, :]).astype(o_ref.dtype)

        def one_band(b, carry):
            start = pl.multiple_of(t_ctx + b * band, GRID_W)
            pre = _conv_sum(_shifted_band(g_ref, b, n_bands, band, t_ctx, t), w_rows, band, False) + bias
            o_ref[pl.ds(start, band), :] = (_gelu_and_slope(pre)[0] * v_ref[pl.ds(start, band), :]).astype(o_ref.dtype)
            return carry

        lax.fori_loop(0, n_bands, one_band, 0)

    return pl.pallas_call(
        body, name=name,
        grid=(nf,),
        in_specs=[pl.BlockSpec((t, lanes), lambda j: (0, j)),
                  pl.BlockSpec((t, lanes), lambda j: (0, nf + j)),
                  pl.BlockSpec((CONV_W_ROWS, lanes), lambda j: (0, j))],
        out_specs=pl.BlockSpec((t, lanes), lambda j: (0, j)),
        out_shape=jax.ShapeDtypeStruct((t, f), BF16),
        compiler_params=pltpu.CompilerParams(dimension_semantics=('parallel',), vmem_limit_bytes=VMEM_LIMIT_BYTES),
    )(gv, gv, w16)


def _ffn_mid_bwd(d_act, gv, w16, t_ctx, name):
    t, f2 = gv.shape
    f = f2 // 2
    lanes = FFN_LANES
    nf = f // lanes
    band = _band_rows(t - t_ctx)
    n_bands = (t - t_ctx) // band
    mid = (FFN_CONV // 2) * FFN_CONV
    n_acc = CONV_TAPS + 1

    def tile_sum(x):
        return jnp.sum(x.reshape(x.shape[0] // SUBLANES, SUBLANES, lanes), axis=0)

    def body(da_ref, g_ref, v_ref, w_ref, dg_ref, dv_ref, dw_ref, dp_ref):
        w_rows = [w_ref[k:k + 1, :] for k in range(CONV_TAPS)]
        bias = w_ref[CONV_TAPS:CONV_TAPS + 1, :]
        acc = [jnp.zeros((SUBLANES, lanes), F32) for _ in range(n_acc)]
        if t_ctx:
            sh = _shifted_ctx(g_ref[0:t_ctx, :])
            pre = sh[0] * w_rows[mid] + sh[1] * w_rows[mid + 1] + sh[2] * w_rows[mid + 2] + bias
            ge, slope = _gelu_and_slope(pre)
            da = da_ref[0:t_ctx, :]
            dv_ref[0:t_ctx, :] = (da * ge).astype(dv_ref.dtype)
            dpre = da * v_ref[0:t_ctx, :] * slope
            dp_ref[0:t_ctx, :] = dpre
            for j in range(FFN_CONV):
                acc[mid + j] = acc[mid + j] + tile_sum(sh[j] * dpre)
            acc[CONV_TAPS] = acc[CONV_TAPS] + tile_sum(dpre)
            back = _shifted_ctx(dpre)
            dg_ref[0:t_ctx, :] = (back[2] * w_rows[mid] + back[1] * w_rows[mid + 1]
                                  + back[0] * w_rows[mid + 2]).astype(dg_ref.dtype)

        def first_pass(b, acc):
            acc = list(acc)
            start = pl.multiple_of(t_ctx + b * band, GRID_W)
            sh = _shifted_band(g_ref, b, n_bands, band, t_ctx, t)
            ge, slope = _gelu_and_slope(_conv_sum(sh, w_rows, band, False) + bias)
            da = da_ref[pl.ds(start, band), :]
            dv_ref[pl.ds(start, band), :] = (da * ge).astype(dv_ref.dtype)
            dpre = da * v_ref[pl.ds(start, band), :] * slope
            dp_ref[pl.ds(start, band), :] = dpre
            for i in range(FFN_CONV):
                for j in range(FFN_CONV):
                    k = i * FFN_CONV + j
                    acc[k] = acc[k] + tile_sum(sh[j][i * GRID_W:i * GRID_W + band] * dpre)
            acc[CONV_TAPS] = acc[CONV_TAPS] + tile_sum(dpre)
            return tuple(acc)

        acc = lax.fori_loop(0, n_bands, first_pass, tuple(acc))
        for k in range(n_acc):
            dw_ref[k * SUBLANES:(k + 1) * SUBLANES, :] = acc[k]

        def second_pass(b, carry):
            start = pl.multiple_of(t_ctx + b * band, GRID_W)
            dg_ref[pl.ds(start, band), :] = _conv_sum(_shifted_band(dp_ref, b, n_bands, band, t_ctx, t),
                                                      w_rows, band, True).astype(dg_ref.dtype)
            return carry

        lax.fori_loop(0, n_bands, second_pass, 0)

    col = pl.BlockSpec((t, lanes), lambda j: (0, j))
    return pl.pallas_call(
        body, name=name,
        grid=(nf,),
        in_specs=[col, col, pl.BlockSpec((t, lanes), lambda j: (0, nf + j)),
                  pl.BlockSpec((CONV_W_ROWS, lanes), lambda j: (0, j))],
        out_specs=[col, col, pl.BlockSpec((n_acc * SUBLANES, lanes), lambda j: (0, j))],
        out_shape=[jax.ShapeDtypeStruct((t, f), BF16), jax.ShapeDtypeStruct((t, f), BF16),
                   jax.ShapeDtypeStruct((n_acc * SUBLANES, f), F32)],
        scratch_shapes=[pltpu.VMEM((t, lanes), F32)],
        compiler_params=pltpu.CompilerParams(dimension_semantics=('parallel',), vmem_limit_bytes=VMEM_LIMIT_BYTES),
    )(d_act, gv, gv, w16)


def _gate_value(up, layout):
    per_dev, per_dev_pad, f = layout
    if per_dev == per_dev_pad:
        return up
    return jnp.concatenate([_take_cols(up, per_dev, per_dev_pad, 0, f), _take_cols(up, per_dev, per_dev_pad, f, 2 * f)],
                           axis=1)


def _ffn_forward(hm, w_up, conv_w, conv_b, w_down, t_ctx, layout, name):
    up = _matmul(hm, w_up, 'nn', name + '_up_fwd')
    gv = _gate_value(up, layout)
    w16 = _conv_operand(conv_w, conv_b)
    act = _ffn_mid_fwd(gv, w16, t_ctx, name + '_mid_fwd')
    return _matmul(act, w_down, 'nn', name + '_down_fwd'), (hm, w_up, gv, w16, w_down, act)


@functools.partial(jax.custom_vjp, nondiff_argnums=(8, 9, 10))
def _ffn_block(hm, carrier, w_up, tap_up, conv_w, conv_b, w_down, tap_down, t_ctx, layout, name):
    del carrier, tap_up, tap_down
    return _ffn_forward(hm, w_up, conv_w, conv_b, w_down, t_ctx, layout, name)[0]


def _ffn_block_fwd(hm, carrier, w_up, tap_up, conv_w, conv_b, w_down, tap_down, t_ctx, layout, name):
    del carrier, tap_up, tap_down
    return _ffn_forward(hm, w_up, conv_w, conv_b, w_down, t_ctx, layout, name)


def _ffn_block_bwd(t_ctx, layout, name, res, d_out):
    hm, w_up, gv, w16, w_down, act = res
    f = act.shape[1]
    d_act = _matmul(d_out, w_down, 'nt', name + '_down_dx')
    d_w_down = _matmul(act, d_out, 'tn', name + '_down_dw')
    d_gate, d_val, d_w = _ffn_mid_bwd(d_act, gv, w16, t_ctx, name + '_mid_bwd')
    d_gv = jnp.concatenate([d_gate, d_val], axis=1)
    if layout[0] == layout[1]:
        d_up = d_gv
    else:
        d_up = jax.vjp(lambda u: _gate_value(u, layout), jnp.zeros((gv.shape[0], w_up.shape[1]), d_gv.dtype))[1](d_gv)[0]
    d_hm = _matmul(d_up, w_up, 'nt', name + '_up_dx')
    d_w_up = _matmul(hm, d_up, 'tn', name + '_up_dw')
    d_w = jnp.sum(d_w.reshape(CONV_TAPS + 1, SUBLANES, f), axis=1)
    return (jnp.zeros_like(hm), d_hm, jnp.zeros_like(w_up), d_w_up, d_w[:CONV_TAPS].reshape(FFN_CONV, FFN_CONV, f),
            d_w[CONV_TAPS], jnp.zeros_like(w_down), d_w_down)


_ffn_block.defvjp(_ffn_block_fwd, _ffn_block_bwd)


LN_ROWS = 128
MOD_ROWS = 8


def _normalised(h, y, gate, alpha):
    r = alpha * h + gate * y
    xc = r - jnp.mean(r, axis=-1, keepdims=True)
    rstd = lax.rsqrt(jnp.mean(xc * xc, axis=-1, keepdims=True) + LN_EPS)
    return xc * rstd, rstd


def _mod_row(m_ref, k, is_ctx):
    lat = m_ref[3 + k:4 + k, :]
    return lat if is_ctx is None else jnp.where(is_ctx, m_ref[k:k + 1, :], lat)


def _res_ln_fwd_call(h, y, mods, t_ctx, alpha, with_mod, name):
    t, d = h.shape
    n_ctx_tiles = t_ctx // LN_ROWS

    def body(h_ref, y_ref, m_ref, hn_ref, *rest):
        is_ctx = (pl.program_id(0) < n_ctx_tiles) if n_ctx_tiles else None
        xhat, _ = _normalised(h_ref[...], y_ref[...], _mod_row(m_ref, 0, is_ctx), alpha)
        hn = xhat * m_ref[6:7, :] + m_ref[7:8, :]
        hn_ref[...] = hn
        if with_mod:
            hm = hn * (1.0 + _mod_row(m_ref, 2, is_ctx)) + _mod_row(m_ref, 1, is_ctx)
            rest[0][...] = hm.astype(rest[0].dtype)

    blk = pl.BlockSpec((LN_ROWS, d), lambda i: (i, 0))
    out_shape = [jax.ShapeDtypeStruct((t, d), F32)] + ([jax.ShapeDtypeStruct((t, d), BF16)] if with_mod else [])
    return pl.pallas_call(
        body, name=name,
        grid=(t // LN_ROWS,),
        in_specs=[blk, blk, pl.BlockSpec((MOD_ROWS, d), lambda i: (0, 0))],
        out_specs=[blk] * len(out_shape),
        out_shape=out_shape,
        compiler_params=pltpu.CompilerParams(dimension_semantics=('parallel',)),
    )(h, y, mods)


def _res_ln_bwd_call(h, y, mods, d_hn, d_hm, t_ctx, alpha, name):
    t, d = h.shape
    n_ctx_tiles = t_ctx // LN_ROWS
    with_mod = d_hm is not None

    def tile_sum(x):
        return jnp.sum(x.reshape(LN_ROWS // SUBLANES, SUBLANES, d), axis=0)

    def body(*refs):
        if with_mod:
            h_ref, y_ref, m_ref, dhn_ref, dhm_ref, dh_ref, dy_ref, acc_ref = refs
        else:
            h_ref, y_ref, m_ref, dhn_ref, dh_ref, dy_ref, acc_ref = refs

        @pl.when(pl.program_id(0) == 0)
        def _():
            acc_ref[...] = jnp.zeros_like(acc_ref)

        is_ctx = (pl.program_id(0) < n_ctx_tiles) if n_ctx_tiles else None
        gate = _mod_row(m_ref, 0, is_ctx)
        y = y_ref[...]
        xhat, rstd = _normalised(h_ref[...], y, gate, alpha)
        ln_g = m_ref[6:7, :]
        dhn = dhn_ref[...]
        base = 3 * SUBLANES if is_ctx is None else jnp.where(is_ctx, 0, 3 * SUBLANES)

        def add_to(row, part):
            if isinstance(row, int):
                acc_ref[row:row + SUBLANES, :] += part
            else:
                acc_ref[pl.ds(pl.multiple_of(row, SUBLANES), SUBLANES), :] += part

        if with_mod:
            dhm = dhm_ref[...]
            hn = xhat * ln_g + m_ref[7:8, :]
            add_to(base + SUBLANES, tile_sum(dhm))
            add_to(base + 2 * SUBLANES, tile_sum(dhm * hn))
            dhn = dhn + dhm * (1.0 + _mod_row(m_ref, 2, is_ctx))
        add_to(6 * SUBLANES, tile_sum(dhn * xhat))
        add_to(7 * SUBLANES, tile_sum(dhn))
        dx = dhn * ln_g
        dr = rstd * (dx - jnp.mean(dx, axis=-1, keepdims=True) - xhat * jnp.mean(dx * xhat, axis=-1, keepdims=True))
        dh_ref[...] = alpha * dr
        dy_ref[...] = gate * dr
        add_to(base, tile_sum(dr * y))

    blk = pl.BlockSpec((LN_ROWS, d), lambda i: (i, 0))
    operands = [h, y, mods, d_hn] + ([d_hm] if with_mod else [])
    return pl.pallas_call(
        body, name=name,
        grid=(t // LN_ROWS,),
        in_specs=[blk, blk, pl.BlockSpec((MOD_ROWS, d), lambda i: (0, 0)), blk] + ([blk] if with_mod else []),
        out_specs=[blk, blk, pl.BlockSpec((MOD_ROWS * SUBLANES, d), lambda i: (0, 0))],
        out_shape=[jax.ShapeDtypeStruct((t, d), F32), jax.ShapeDtypeStruct((t, d), F32),
                   jax.ShapeDtypeStruct((MOD_ROWS * SUBLANES, d), F32)],
        compiler_params=pltpu.CompilerParams(dimension_semantics=('arbitrary',)),
    )(*operands)


@functools.partial(jax.custom_vjp, nondiff_argnums=(4, 5, 6, 7))
def _res_ln_mod(h, y, mods, carrier, t_ctx, alpha, with_mod, name):
    out = tuple(_res_ln_fwd_call(h, y, mods, t_ctx, alpha, with_mod, name + '_fwd'))
    return out + (carrier,) if with_mod else out


def _res_ln_mod_fwd(h, y, mods, carrier, t_ctx, alpha, with_mod, name):
    out = tuple(_res_ln_fwd_call(h, y, mods, t_ctx, alpha, with_mod, name + '_fwd'))
    return (out + (carrier,) if with_mod else out), (h, y, mods, carrier)


def _res_ln_mod_bwd(t_ctx, alpha, with_mod, name, res, cts):
    h, y, mods, carrier = res
    d_h, d_y, acc = _res_ln_bwd_call(h, y, mods, cts[0], cts[2] if with_mod else None, t_ctx, alpha, name + '_bwd')
    return d_h, d_y, jnp.sum(acc.reshape(MOD_ROWS, SUBLANES, h.shape[1]), axis=1), jnp.zeros_like(carrier)


_res_ln_mod.defvjp(_res_ln_mod_fwd, _res_ln_mod_bwd)


def _window(ref, j, rows, cols, axis):
    if axis == 0:
        return ref.at[pl.ds(j * rows, rows), :]
    return ref.at[:, pl.ds(j * cols, cols)]


def _all_gather(block, axis, name):
    rows, cols = block.shape

    def body(x_ref, out_ref, send_sems, recv_sems, local_sem):
        x, y, c = _my_coords()
        me, sibling = (x, y, c), (x, y, 1 - c)
        chips = [(1 - x, y), (x, 1 - y), (1 - x, 1 - y)]

        def win(px, py, pc):
            return _window(out_ref, 4 * px + 2 * py + pc, rows, cols, axis)

        def copy(k, blk, to, src=None):
            return pltpu.make_async_remote_copy(
                src_ref=win(*blk) if src is None else src, dst_ref=win(*blk),
                send_sem=send_sems.at[k], recv_sem=recv_sems.at[k], device_id=to, device_id_type=MESH)

        mine = pltpu.make_async_copy(x_ref, win(*me), local_sem)
        mine.start()
        first = [copy(0, me, sibling, src=x_ref)]
        first += [copy(1 + j, me, (*chip, c), src=x_ref) for j, chip in enumerate(chips)]
        for cp in first:
            cp.start()
        passed = [copy(4 + j, (*chip, c), sibling) for j, chip in enumerate(chips)]
        for j, chip in enumerate(chips):
            copy(1 + j, (*chip, c), me).wait_recv()
            passed[j].start()
        copy(0, sibling, me).wait_recv()
        for j, chip in enumerate(chips):
            copy(4 + j, (*chip, 1 - c), me).wait_recv()
        for cp in first + passed:
            cp.wait_send()
        mine.wait()

    out_shape = (N_DEV * rows, cols) if axis == 0 else (rows, N_DEV * cols)
    return pl.pallas_call(
        body, name=name,
        out_shape=jax.ShapeDtypeStruct(out_shape, block.dtype),
        in_specs=[ANY], out_specs=ANY,
        scratch_shapes=[pltpu.SemaphoreType.DMA((7,)), pltpu.SemaphoreType.DMA((7,)), pltpu.SemaphoreType.DMA],
    )(block)


AG_COLLECTIVE_ID = 1


def _all_gather_sequencer(block, axis, name):
    rows, cols = block.shape
    out_shape = (N_DEV * rows, cols) if axis == 0 else (rows, N_DEV * cols)
    x_ref = jax.new_ref(block, memory_space=pltpu.MemorySpace.HBM)
    out_ref = jax.empty_ref(jax.ShapeDtypeStruct(out_shape, block.dtype), memory_space=pltpu.MemorySpace.HBM)

    @pl.kernel(mesh=plsc.ScalarSubcoreMesh(axis_name='sequencer', num_cores=1), name=name,
               scratch_types=(pltpu.SemaphoreType.DMA((7,)), pltpu.SemaphoreType.DMA((7,)), pltpu.SemaphoreType.DMA),
               compiler_params=pltpu.CompilerParams(collective_id=AG_COLLECTIVE_ID))
    def launch(send_sems, recv_sems, local_sem):
        x, y, c = _my_coords()
        me, sibling = (x, y, c), (x, y, 1 - c)
        chips = [(1 - x, y), (x, 1 - y), (1 - x, 1 - y)]
        barrier = pltpu.get_barrier_semaphore()
        for peer in [sibling] + [(*chip, c) for chip in chips]:
            pl.semaphore_signal(barrier, inc=1, device_id=peer, device_id_type=MESH)
        pl.semaphore_wait(barrier, 1 + len(chips))

        def win(px, py, pc):
            return _window(out_ref, 4 * px + 2 * py + pc, rows, cols, axis)

        def copy(k, blk, to, src=None):
            return pltpu.make_async_remote_copy(
                src_ref=win(*blk) if src is None else src, dst_ref=win(*blk),
                send_sem=send_sems.at[k], recv_sem=recv_sems.at[k], device_id=to, device_id_type=MESH)

        mine = pltpu.make_async_copy(x_ref, win(*me), local_sem)
        mine.start()
        first = [copy(0, me, sibling, src=x_ref)]
        first += [copy(1 + j, me, (*chip, c), src=x_ref) for j, chip in enumerate(chips)]
        for cp in first:
            cp.start()
        passed = [copy(4 + j, (*chip, c), sibling) for j, chip in enumerate(chips)]
        for j, chip in enumerate(chips):
            copy(1 + j, (*chip, c), me).wait_recv()
            passed[j].start()
        copy(0, sibling, me).wait_recv()
        for j, chip in enumerate(chips):
            copy(4 + j, (*chip, 1 - c), me).wait_recv()
        for cp in first + passed:
            cp.wait_send()
        mine.wait()

    launch()
    return out_ref[...]


PAIR_COLLECTIVE_ID = 2
CHIPS_COLLECTIVE_ID = 3


def _handshake(peers):
    barrier = pltpu.get_barrier_semaphore()
    for peer in peers:
        pl.semaphore_signal(barrier, inc=1, device_id=peer, device_id_type=MESH)
    pl.semaphore_wait(barrier, len(peers))


def _launch_on_sequencer(body, operand, result, scratch_types, collective_id, name):
    operand_ref = jax.new_ref(operand, memory_space=pltpu.MemorySpace.HBM)
    result_ref = jax.empty_ref(result, memory_space=pltpu.MemorySpace.HBM)
    pl.kernel(functools.partial(body, operand_ref, result_ref),
              mesh=plsc.ScalarSubcoreMesh(axis_name='sequencer', num_cores=1), name=name,
              scratch_types=scratch_types, compiler_params=pltpu.CompilerParams(collective_id=collective_id))()
    return result_ref[...]


def _exchange_pair(full, rows, cols, axis, sequencer, name):
    def body(g_ref, land_ref, send_sems, recv_sems):
        x, y, c = _my_coords()
        if sequencer:
            _handshake([(x, y, 1 - c)])
        copies = []
        for k in range(N_CHIP):
            j = 2 * k + (1 - c)
            copies.append(pltpu.make_async_remote_copy(
                src_ref=_window(g_ref, j, rows, cols, axis), dst_ref=land_ref.at[k],
                send_sem=send_sems.at[k], recv_sem=recv_sems.at[k], device_id=(x, y, 1 - c), device_id_type=MESH))
        for cp in copies:
            cp.start()
        for cp in copies:
            cp.wait()

    result = jax.ShapeDtypeStruct((N_CHIP, rows, cols), full.dtype)
    sems = [pltpu.SemaphoreType.DMA((N_CHIP,)), pltpu.SemaphoreType.DMA((N_CHIP,))]
    if sequencer:
        return _launch_on_sequencer(body, full, result, sems, PAIR_COLLECTIVE_ID, name)
    return pl.pallas_call(body, name=name, out_shape=result, in_specs=[ANY], out_specs=ANY, scratch_shapes=sems)(full)


def _elementwise_tiles(rows, cols):
    tc = _first_divisor(cols, (1024, 512, 384, 256, 128))
    tr = _first_divisor(rows, (512, 256, 128, 64, 32, 16, 8))
    return tr, tc


def _pair_add(full, land, rows, cols, axis, wire_dtype, name):
    tr, tc = _elementwise_tiles(rows, cols)
    c_arr = jnp.reshape(lax.axis_index('c'), (1,)).astype(jnp.int32)

    def full_map(k, i, j, c_ref):
        blk = 2 * k + c_ref[0]
        if axis == 0:
            return (blk * (rows // tr) + i, j)
        return (i, blk * (cols // tc) + j)

    def body(c_ref, g_ref, l_ref, o_ref):
        del c_ref
        o_ref[0] = (g_ref[...] + l_ref[0]).astype(o_ref.dtype)

    return pl.pallas_call(
        body, name=name,
        grid_spec=pltpu.PrefetchScalarGridSpec(
            num_scalar_prefetch=1,
            grid=(N_CHIP, rows // tr, cols // tc),
            in_specs=[pl.BlockSpec((tr, tc), full_map),
                      pl.BlockSpec((1, tr, tc), lambda k, i, j, c_ref: (k, i, j))],
            out_specs=pl.BlockSpec((1, tr, tc), lambda k, i, j, c_ref: (k, i, j)),
        ),
        out_shape=jax.ShapeDtypeStruct((N_CHIP, rows, cols), wire_dtype),
        compiler_params=pltpu.CompilerParams(dimension_semantics=('parallel', 'parallel', 'parallel')),
    )(c_arr, full, land)


def _exchange_chips(sums, sequencer, name):
    _, rows, cols = sums.shape

    def body(s_ref, land_ref, send_sems, recv_sems, local_sem):
        x, y, c = _my_coords()
        my_chip = 2 * x + y
        chips = [(1 - x, y), (x, 1 - y), (1 - x, 1 - y)]
        if sequencer:
            _handshake([(px, py, c) for px, py in chips])
        mine = pltpu.make_async_copy(s_ref.at[my_chip], land_ref.at[my_chip], local_sem)
        mine.start()
        sends = []
        for t, (px, py) in enumerate(chips):
            sends.append(pltpu.make_async_remote_copy(
                src_ref=s_ref.at[2 * px + py], dst_ref=land_ref.at[my_chip],
                send_sem=send_sems.at[t], recv_sem=recv_sems.at[t], device_id=(px, py, c), device_id_type=MESH))
        for cp in sends:
            cp.start()
        for t, (px, py) in enumerate(chips):
            pltpu.make_async_remote_copy(
                src_ref=s_ref.at[my_chip], dst_ref=land_ref.at[2 * px + py],
                send_sem=send_sems.at[t], recv_sem=recv_sems.at[t], device_id=(px, py, c),
                device_id_type=MESH).wait_recv()
        for cp in sends:
            cp.wait_send()
        mine.wait()

    result = jax.ShapeDtypeStruct((N_CHIP, rows, cols), sums.dtype)
    sems = [pltpu.SemaphoreType.DMA((3,)), pltpu.SemaphoreType.DMA((3,)), pltpu.SemaphoreType.DMA]
    if sequencer:
        return _launch_on_sequencer(body, sums, result, sems, CHIPS_COLLECTIVE_ID, name)
    return pl.pallas_call(body, name=name, out_shape=result, in_specs=[ANY], out_specs=ANY, scratch_shapes=sems)(sums)


def _reduce_scatter_slots(full, rows, cols, axis, wire_dtype, sequencer, name):
    land = _exchange_pair(full, rows, cols, axis, sequencer, name + '_pair')
    sums = _pair_add(full, land, rows, cols, axis, wire_dtype, name + '_add')
    return _exchange_chips(sums, sequencer, name + '_chips')


def _sum_slots(slots, name):
    n_slots, rows, cols = slots.shape
    tr, tc = _elementwise_tiles(rows, cols)

    def body(s_ref, o_ref):
        g = s_ref[0]
        for s in range(1, n_slots):
            g = g + s_ref[s]
        o_ref[...] = g

    return pl.pallas_call(
        body, name=name,
        grid=(rows // tr, cols // tc),
        in_specs=[pl.BlockSpec((n_slots, tr, tc), lambda i, j: (0, i, j))],
        out_specs=pl.BlockSpec((tr, tc), lambda i, j: (i, j)),
        out_shape=jax.ShapeDtypeStruct((rows, cols), F32),
        compiler_params=pltpu.CompilerParams(dimension_semantics=('parallel', 'parallel')),
    )(slots)


def _sum_adamw(slots, w, m, v, name):
    n_slots, rows, cols = slots.shape
    tr, tc = _elementwise_tiles(rows, cols)
    c1 = 1.0 - ADAM_B1 ** ADAM_STEP
    c2 = 1.0 - ADAM_B2 ** ADAM_STEP

    def body(s_ref, w_ref, m_ref, v_ref, g_out, d_out, m_out, v_out):
        g = s_ref[0].astype(F32)
        for s in range(1, n_slots):
            g = g + s_ref[s].astype(F32)
        m_new = ADAM_B1 * m_ref[...] + (1.0 - ADAM_B1) * g
        v_new = ADAM_B2 * v_ref[...] + (1.0 - ADAM_B2) * (g * g)
        m_hat = m_new / c1
        v_hat = v_new / c2
        g_out[...] = g
        d_out[...] = -ADAM_LR * (m_hat / (jnp.sqrt(v_hat) + ADAM_EPS) + ADAM_WD * w_ref[...])
        m_out[...] = m_new
        v_out[...] = v_new

    blk = pl.BlockSpec((tr, tc), lambda i, j: (i, j))
    shape = jax.ShapeDtypeStruct((rows, cols), F32)
    return pl.pallas_call(
        body, name=name,
        grid=(rows // tr, cols // tc),
        in_specs=[pl.BlockSpec((n_slots, tr, tc), lambda i, j: (0, i, j)), blk, blk, blk],
        out_specs=[blk, blk, blk, blk],
        out_shape=[shape, shape, shape, shape],
        compiler_params=pltpu.CompilerParams(dimension_semantics=('parallel', 'parallel')),
    )(slots, w, m, v)


def _rows_of(shape):
    size = int(np.prod(shape)) if len(shape) else 1
    return _round_up(_round_up(size, LANES) // LANES, SUBLANES)


def _pack(arrays, rows_multiple):
    parts = []
    for arr in arrays:
        flat = jnp.ravel(arr).astype(F32)
        rows = _rows_of(arr.shape)
        parts.append(jnp.pad(flat, (0, rows * LANES - flat.shape[0])).reshape(rows, LANES))
    total = sum(p.shape[0] for p in parts)
    pad = _round_up(total, rows_multiple) - total
    if pad:
        parts.append(jnp.zeros((pad, LANES), F32))
    return jnp.concatenate(parts, axis=0)


def _unpack(slab, shapes):
    out, r0 = [], 0
    for s in shapes:
        size = int(np.prod(s)) if len(s) else 1
        rows = _rows_of(s)
        out.append(slab[r0:r0 + rows].reshape(-1)[:size].reshape(s))
        r0 += rows
    return out


def _layer_norm(x, g, b):
    mu = jnp.mean(x, -1, keepdims=True)
    var = jnp.mean(jnp.square(x - mu), -1, keepdims=True)
    return (x - mu) * lax.rsqrt(var + LN_EPS) * g + b


def _modulate(x, shift, scale):
    return x * (1 + scale) + shift


def _take_cols(p, per_dev, per_dev_padded, lo, hi):
    parts = []
    while lo < hi:
        dev, off = divmod(lo, per_dev)
        n = min(hi - lo, per_dev - off)
        parts.append(p[:, dev * per_dev_padded + off: dev * per_dev_padded + off + n])
        lo += n
    return parts[0] if len(parts) == 1 else jnp.concatenate(parts, axis=1)


def _block_diag(blocks, nb):
    g, a, b = blocks.shape
    per = g // nb
    eye = jnp.eye(per, dtype=blocks.dtype)
    return (blocks.reshape(nb, per, a, 1, b) * eye[None, :, None, :, None]).reshape(nb, per * a, per * b)


def _s5_mixer(u, t_ctx, prm, name):
    t, gw = u.shape
    groups = gw // S5_CH
    ys = []
    for direction, reverse in enumerate((False, True)):
        lam = lax.complex(prm['s5_a_re'][direction], prm['s5_a_im'][direction])
        step = jnp.exp(prm['s5_log_step'][direction])[:, None]
        a_bar = jnp.exp(lam * step)
        b_bar = ((a_bar - 1.0) / lam)[..., None] * lax.complex(prm['s5_b_re'][direction],
                                                               prm['s5_b_im'][direction])
        c_mat = lax.complex(prm['s5_c_re'][direction], prm['s5_c_im'][direction])
        nb = (groups * S5_STATE) // _scan_lanes(groups * S5_STATE)
        b_t = jnp.swapaxes(b_bar, 1, 2)
        b_blocks = jnp.concatenate([_block_diag(jnp.real(b_t), nb), _block_diag(jnp.imag(b_t), nb)], axis=2)
        c_t = jnp.swapaxes(c_mat, 1, 2)
        c_blocks = jnp.concatenate([_block_diag(jnp.real(c_t), nb), -_block_diag(jnp.imag(c_t), nb)], axis=1)
        a_rows = jnp.stack([jnp.real(a_bar).reshape(-1), jnp.imag(a_bar).reshape(-1)])
        bu = _blocked_linear(u, b_blocks, f'{name}_bu{direction}')
        h = _s5_scan(a_rows, bu, t_ctx, reverse, f'{name}_scan{direction}')
        ys.append(_blocked_linear(h, c_blocks, f'{name}_y{direction}'))
    y = ys[0] + ys[1] + prm['s5_d'][None, :] * u
    z = jax.nn.gelu(y)
    return z * jax.nn.sigmoid(_linear(z, prm['s5_glu_w'], f'{name}_glu') + prm['s5_glu_b'])


def _chunk_gating(u, v, prm):
    t, gw = u.shape
    hd = gw // SG_HEADS
    u = jax.nn.gelu(u)
    v = jax.nn.gelu(v).reshape(t // SG_CHUNK, SG_CHUNK, SG_HEADS, hd)
    v = _layer_norm(v, prm['sg_ln_g'].reshape(SG_HEADS, hd), prm['sg_ln_b'].reshape(SG_HEADS, hd))
    s = jnp.einsum('hij,cjhd->cihd', prm['sg_w'], v) + prm['sg_b'].T[None, :, :, None]
    return u * s.reshape(t, gw)


def _pool_mixer(p, prm):
    l, gw = p.shape
    pd = gw // len(POOL_WINDOWS)
    t = np.arange(l)
    outs = []
    for g, win in enumerate(POOL_WINDOWS):
        lo = np.clip(t - win // 2, 0, l - 1)
        hi = np.clip(t + win // 2 - 1, 0, l - 1)
        pg = p[:, g * pd:(g + 1) * pd]
        padded = jnp.pad(pg, ((win // 2, win // 2), (0, 0)))
        total = padded[0:l]
        for d in range(1, win):
            total = total + padded[d:d + l]
        mean = total / jnp.asarray((hi - lo + 1).astype(np.float32))[:, None]
        outs.append(jnp.einsum('lc,cd->ld', mean - pg, prm['pool_w'][g]))
    y = jnp.concatenate(outs, axis=-1) + prm['pool_b']
    return y * prm['pool_scale']


def _dw_conv1d(x, w, b, pad):
    l = x.shape[0]
    xp = jnp.pad(x, (pad, (0, 0)))
    y = xp[0:l] * w[0]
    for k in range(1, w.shape[0]):
        y = y + xp[k:k + l] * w[k]
    return y + b


def _m2_prepare(xbc, dt_raw, prm, gw):
    heads = gw // M2_HEAD_DIM
    xbc = jax.nn.silu(_dw_conv1d(xbc, prm['m2_conv_w'], prm['m2_conv_b'], M2_PAD))
    l = xbc.shape[0]
    n_bc = M2_GROUPS * M2_STATE
    rep = heads // M2_GROUPS
    xs = xbc[:, :gw].reshape(l, heads, M2_HEAD_DIM)
    bm = jnp.repeat(xbc[:, gw:gw + n_bc].reshape(l, M2_GROUPS, M2_STATE), rep, axis=1)
    cm = jnp.repeat(xbc[:, gw + n_bc:].reshape(l, M2_GROUPS, M2_STATE), rep, axis=1)
    dt = jax.nn.softplus(dt_raw.reshape(l, 2, heads) + prm['m2_dt_bias'])
    return xs, bm, cm, dt


def _walk_cumsum(x, axis, reverse):
    return jnp.flip(jnp.cumsum(jnp.flip(x, axis), axis), axis) if reverse else jnp.cumsum(x, axis)


def _ssd_scan(xs, dt, a, bm, cm, h0, need_y, reverse):
    l, nh, hp = xs.shape
    nc = l // M2_CHUNK

    def chunks(t):
        return t.reshape((nc, M2_CHUNK) + t.shape[1:])

    def visited_before(n, strictly):
        ones = jnp.ones((n, n), bool)
        return jnp.triu(ones, 1 if strictly else 0) if reverse else jnp.tril(ones, -1 if strictly else 0)

    xd = chunks(xs * dt[..., None])
    bc, cc = chunks(bm), chunks(cm)
    a_cum = _walk_cumsum(chunks(dt * a), 1, reverse)
    a_tot = a_cum[:, 0] if reverse else a_cum[:, -1]
    decay_end = jnp.exp(a_tot[:, None] - a_cum)
    chunk_states = jnp.einsum('cqhn,cqh,cqhp->chpn', bc, decay_end, xd)

    cum = _walk_cumsum(a_tot, 0, reverse)
    before = cum - a_tot
    carry_w = jnp.exp(jnp.where(visited_before(nc, True)[:, :, None], before[:, None, :] - cum[None, :, :],
                                -jnp.inf))
    whole = cum[0] if reverse else cum[-1]
    h_final = (jnp.exp(whole)[:, None, None] * h0
               + jnp.einsum('dh,dhpn->hpn', jnp.exp(whole[None, :] - cum), chunk_states,
                            precision=lax.Precision.HIGHEST))
    if not need_y:
        return None, h_final
    h_prev = (jnp.exp(before)[:, :, None, None] * h0[None]
              + jnp.einsum('cdh,dhpn->chpn', carry_w, chunk_states, precision=lax.Precision.HIGHEST))
    seg = a_cum[:, :, None, :] - a_cum[:, None, :, :]
    decay = jnp.exp(jnp.where(visited_before(M2_CHUNK, False)[None, :, :, None], seg, -jnp.inf))
    scores = jnp.einsum('cihn,cjhn->cijh', cc, bc) * decay
    y = (jnp.einsum('cijh,cjhp->cihp', scores, xd)
         + jnp.einsum('cihn,chpn->cihp', cc, h_prev) * jnp.exp(a_cum)[..., None])
    return y.reshape(l, nh, hp), h_final


def _ssd_direction(inputs, direction, a, h0, need_y):
    xs, bm, cm, dt = inputs
    return _ssd_scan(xs, dt[:, direction], a, bm, cm, h0, need_y, direction == 1)


def _gated_rmsnorm(y, z, w):
    l, gw = z.shape
    g = (y * jax.nn.silu(z)).reshape(l, M2_GROUPS, gw // M2_GROUPS)
    g = g * lax.rsqrt(jnp.mean(jnp.square(g), -1, keepdims=True) + RMS_EPS)
    return g.reshape(l, gw) * w


def _mamba2_mixer(z, xbc, dt_raw, t_ctx, prm, need_ctx):
    gw = z.shape[1]
    heads = gw // M2_HEAD_DIM
    ctx_in = _m2_prepare(xbc[:t_ctx], dt_raw[:t_ctx], prm, gw)
    lat_in = _m2_prepare(xbc[t_ctx:], dt_raw[t_ctx:], prm, gw)
    a = -jnp.exp(prm['m2_a_log'])
    ys_ctx, ys_lat = [], []
    for direction in range(2):
        h0 = jnp.zeros((heads, M2_HEAD_DIM, M2_STATE), F32)
        y_c, h_c = _ssd_direction(ctx_in, direction, a[direction], h0, need_ctx)
        y_l, _ = _ssd_direction(lat_in, direction, a[direction], h_c, True)
        ys_lat.append(y_l)
        if need_ctx:
            ys_ctx.append(y_c)
    d_h = prm['m2_d'][None, :, None]

    def finish(ys, xs, zz):
        y = (ys[0] + ys[1] + d_h * xs).reshape(zz.shape[0], gw)
        return _gated_rmsnorm(y, zz, prm['m2_norm_w'])

    y_lat = finish(ys_lat, lat_in[0], z[t_ctx:])
    if need_ctx:
        return jnp.concatenate([finish(ys_ctx, ctx_in[0], z[:t_ctx]), y_lat], axis=0)
    return y_lat


def _forward_loss(dp, consts, dims):
    depth, t_ctx, d_model = dims['depth'], dims['t_ctx'], dims['d_model']
    gw = d_model // 4
    alpha = (2 * depth) ** 0.25
    in_sizes = (gw, gw, gw, gw, gw, gw + 2 * M2_GROUPS * M2_STATE, 2 * (gw // M2_HEAD_DIM))
    in_offs = np.concatenate([[0], np.cumsum(in_sizes)])
    ml = [[dp['mod_lat'][i, k][None, :] for k in range(6)] for i in range(depth)]
    mc = [[dp['mod_ctx'][i, k][None, :] for k in range(6)] for i in range(depth)]
    zero_row = jnp.zeros((1, d_model), F32)
    h = jnp.concatenate([consts['ctx'], dp['x']], axis=0)
    hm32 = jnp.concatenate([_modulate(consts['ctx'], mc[0][0], mc[0][1]), _modulate(dp['x'], ml[0][0], ml[0][1])],
                           axis=0)
    hm, carrier = hm32.astype(BF16), hm32 - lax.stop_gradient(hm32)
    for i in range(depth):
        need_ctx = i < depth - 1
        prm = {k: v[i] for k, v in dp['small'].items()}
        p = _linear_tap_carried(hm, carrier, consts['w_in'][i], dp['taps']['w_in'][i], f'in{i}')
        seg = [p[:, int(in_offs[s]):int(in_offs[s + 1])] for s in range(7)]
        ya = _s5_mixer(seg[0], t_ctx, prm, f's5_{i}')
        yb = _chunk_gating(seg[1], seg[2], prm)
        yc = jnp.concatenate([_pool_mixer(seg[3][:t_ctx], prm), _pool_mixer(seg[3][t_ctx:], prm)], axis=0)
        yd = _mamba2_mixer(seg[4], seg[5], seg[6], t_ctx, prm, need_ctx)
        if need_ctx:
            mix_in = jnp.concatenate([ya, yb, yc, yd], axis=1)
        else:
            mix_in = jnp.concatenate([ya[t_ctx:], yb[t_ctx:], yc[t_ctx:], yd], axis=1)
        mix = _linear_tap(mix_in, consts['w_out'][i], dp['taps']['w_out'][i], f'out{i}')
        t_c = t_ctx if need_ctx else 0
        if not need_ctx and h.shape[0] != mix.shape[0]:
            h = h[t_ctx:]
        mods = jnp.concatenate([mc[i][2], mc[i][3], mc[i][4], ml[i][2], ml[i][3], ml[i][4],
                                prm['ln1_g'][None], prm['ln1_b'][None]], axis=0)
        h, hm, carrier = _res_ln_mod(h, mix, mods, jnp.zeros(h.shape, F32), t_c, alpha, True, f'ln1_{i}')

        f_out = _ffn_block(hm, carrier, consts['ffn_w_up'][i], dp['taps']['ffn_w_up'][i], prm['ffn_conv_w'],
                           prm['ffn_conv_b'], consts['ffn_w_down'][i], dp['taps']['ffn_w_down'][i], t_c,
                           (dims['up_per_dev'], dims['up_per_dev_pad'], dims['ffn_hidden']), f'ffn{i}')
        if need_ctx:
            mods = jnp.concatenate([mc[i][5], mc[i + 1][0], mc[i + 1][1], ml[i][5], ml[i + 1][0], ml[i + 1][1],
                                    prm['ln2_g'][None], prm['ln2_b'][None]], axis=0)
            h, hm, carrier = _res_ln_mod(h, f_out, mods, jnp.zeros(h.shape, F32), t_c, alpha, True, f'ln2_{i}')
        else:
            mods = jnp.concatenate([mc[i][5], zero_row, zero_row, ml[i][5], zero_row, zero_row,
                                    prm['ln2_g'][None], prm['ln2_b'][None]], axis=0)
            h = _res_ln_mod(h, f_out, mods, jnp.zeros(h.shape, F32), t_c, alpha, False, f'ln2_{i}')[0]
    err = jnp.square(h - consts['target'])
    return 0.5 * jnp.sum(jnp.mean(err, axis=-1))


def _silu_grad(x):
    s = jax.nn.sigmoid(x)
    return s * (1 + x * (1 - s))


def _step(w, m, v, x, c, ctx, target):
    depth, d_model, ada_cols = w['w_ada'].shape
    t_ctx, t_lat = ctx.shape[1], x.shape[1]
    me = _my_index()
    in_per_dev = w['w_in'].shape[2]
    in_pad = _round_up(in_per_dev, LANES)
    in_width_pad = _round_up(N_DEV * in_per_dev, 4 * LANES)
    up_per_dev = w['ffn_w_up'].shape[2]
    up_pad = _round_up(up_per_dev, LANES)
    f_hidden = w['ffn_w_down'].shape[1] * N_DEV
    dims = dict(depth=depth, t_ctx=t_ctx, d_model=d_model, in_per_dev=in_per_dev, in_per_dev_pad=in_pad,
                up_per_dev=up_per_dev, up_per_dev_pad=up_pad, ffn_hidden=f_hidden)

    rows16 = 2 * SUBLANES
    silu_c_all = _all_gather(jnp.pad(jax.nn.silu(c), ((0, SUBLANES - 1), (0, 0))), 0, 'ag_c')
    silu_c_all = silu_c_all.reshape(N_DEV, SUBLANES, d_model)[:, 0]
    silu_cc = jax.nn.silu(w['c_ctx'])
    ada_in = jnp.concatenate([silu_c_all, silu_cc[None], jnp.zeros((rows16 - N_DEV - 1, d_model), F32)], axis=0)
    mod_loc = jnp.concatenate([_matmul(ada_in, w['w_ada'][i], 'nn', 'ada_fwd') for i in range(depth)], axis=0)
    mod_all = _all_gather(mod_loc, 1, 'ag_mod').reshape(depth, rows16, 6 * d_model)
    mod_all = mod_all + w['b_ada'][:, None, :]
    mod_lat = lax.dynamic_index_in_dim(mod_all, me, axis=1, keepdims=False).reshape(depth, 6, d_model)
    mod_ctx = mod_all[:, N_DEV].reshape(depth, 6, d_model)

    def pad_cols(a, to):
        return jnp.pad(a, ((0, 0), (0, to - a.shape[1])))

    gathered = {n: [] for n in BIG}
    for i in range(depth):
        w_in_dm = _all_gather_sequencer(pad_cols(w['w_in'][i], in_pad).astype(BF16), 1, f'ag_w_in{i}')
        gathered['w_in'].append(pad_cols(jnp.concatenate(
            [w_in_dm[:, j * in_pad:j * in_pad + in_per_dev] for j in range(N_DEV)], axis=1), in_width_pad))
        gathered['w_out'].append(_all_gather_sequencer(w['w_out'][i].astype(BF16), 0, f'ag_w_out{i}'))
        gathered['ffn_w_up'].append(
            _all_gather_sequencer(pad_cols(w['ffn_w_up'][i], up_pad).astype(BF16), 1, f'ag_w_up{i}'))
        gathered['ffn_w_down'].append(_all_gather_sequencer(w['ffn_w_down'][i].astype(BF16), 0, f'ag_w_down{i}'))
    sharded_small = [w[n] for n in COL_SHARDED_SMALL + ROW_SHARDED_SMALL]
    small_slab = _pack(sharded_small, PACK_ROWS)
    slab_rows = small_slab.shape[0]
    small_all = _all_gather(small_slab, 0, 'ag_small').reshape(N_DEV, slab_rows, LANES)
    per_dev = [_unpack(small_all[d], [a.shape for a in sharded_small]) for d in range(N_DEV)]
    small_full = {}
    for k, n in enumerate(COL_SHARDED_SMALL):
        small_full[n] = jnp.concatenate([per_dev[d][k] for d in range(N_DEV)], axis=-1)
    for k, n in enumerate(ROW_SHARDED_SMALL):
        small_full[n] = jnp.concatenate([per_dev[d][len(COL_SHARDED_SMALL) + k] for d in range(N_DEV)], axis=1)

    small = {n: w[n] for n in REPLICATED}
    small.update(small_full)
    taps = {n: [jnp.zeros(gathered[n][i].shape, F32) for i in range(depth)] for n in BIG}
    dp = dict(x=x[0], small=small, mod_lat=mod_lat, mod_ctx=mod_ctx, taps=taps)
    consts = dict(ctx=ctx[0], target=target[0], **gathered)
    loss_local, grads = jax.value_and_grad(functools.partial(_forward_loss, consts=consts, dims=dims))(dp)

    out_g, out_d, out_m, out_v = {}, {}, {}, {}

    def finish_big(name, i, rows, cols, axis, keep_cols):
        full = grads['taps'][name][i]
        if name == 'w_in':
            full = jnp.concatenate([pad_cols(full[:, j * in_per_dev:(j + 1) * in_per_dev], in_pad)
                                    for j in range(N_DEV)], axis=1)
        slots = _reduce_scatter_slots(full, rows, cols, axis, BF16, True, f'rs_{name}{i}')
        shard = [pad_cols(t[name][i], cols) for t in (w, m, v)]
        res = _sum_adamw(slots, *shard, f'adamw_{name}')
        return [r[:, :keep_cols] for r in res]

    big_specs = {
        'ffn_w_down': (w['ffn_w_down'].shape[1], d_model, 0, d_model),
        'ffn_w_up': (d_model, up_pad, 1, up_per_dev),
        'w_out': (w['w_out'].shape[1], d_model, 0, d_model),
        'w_in': (d_model, in_pad, 1, in_per_dev),
    }
    per_layer = {name: [None] * depth for name in big_specs}
    for i in reversed(range(depth)):
        for name, (rows, cols, axis, keep) in big_specs.items():
            per_layer[name][i] = finish_big(name, i, rows, cols, axis, keep)
    for name in big_specs:
        for k, dst in enumerate((out_g, out_d, out_m, out_v)):
            dst[name] = jnp.stack([per_layer[name][i][k] for i in range(depth)])

    d_lat = grads['mod_lat'].reshape(depth, 6 * d_model)
    d_ctx = grads['mod_ctx'].reshape(depth, 6 * d_model)
    d_rows = jnp.concatenate([d_lat, d_ctx, jnp.zeros((SUBLANES - 2 * depth, 6 * d_model), F32)], axis=0)
    d_all = _all_gather(d_rows, 0, 'ag_dmod').reshape(N_DEV, SUBLANES, 6 * d_model)
    d_lat_all = d_all[:, :depth]
    d_ctx_sum = d_all[0, depth:2 * depth]
    for d in range(1, N_DEV):
        d_ctx_sum = d_ctx_sum + d_all[d, depth:2 * depth]
    g_b_ada = d_ctx_sum
    for d in range(N_DEV):
        g_b_ada = g_b_ada + d_lat_all[d]
    g_w_ada, c_ctx_part = [], jnp.zeros((d_model,), F32)
    for i in range(depth):
        d_mat = jnp.concatenate([d_lat_all[:, i], d_ctx_sum[i][None],
                                 jnp.zeros((rows16 - N_DEV - 1, 6 * d_model), F32)], axis=0)
        d_mine = lax.dynamic_slice_in_dim(d_mat, me * ada_cols, ada_cols, axis=1)
        g_w_ada.append(_matmul(ada_in, d_mine, 'tn', 'ada_dw'))
        back = _matmul(d_mine, w['w_ada'][i], 'nt', 'ada_dx')
        c_ctx_part = c_ctx_part + back[N_DEV]
    c_ctx_part = c_ctx_part * _silu_grad(w['c_ctx'])
    g_w_ada = jnp.stack(g_w_ada).reshape(1, depth * d_model, ada_cols)
    res = _sum_adamw(g_w_ada, *[t['w_ada'].reshape(depth * d_model, ada_cols) for t in (w, m, v)], 'adamw_w_ada')
    for k, dst in enumerate((out_g, out_d, out_m, out_v)):
        dst['w_ada'] = res[k].reshape(depth, d_model, ada_cols)

    reduced_names = REPLICATED[:]
    reduced_names.remove('b_ada')
    reduced_names += list(COL_SHARDED_SMALL + ROW_SHARDED_SMALL)
    to_reduce = [grads['small'][n] for n in reduced_names] + [c_ctx_part, loss_local]
    slab = _pack(to_reduce, N_DEV * PACK_ROWS)
    chunk_rows = slab.shape[0] // N_DEV
    slots = _reduce_scatter_slots(slab, chunk_rows, LANES, 0, F32, False, 'rs_small')
    mine = _sum_slots(slots, 'sum_small')
    summed = _all_gather(mine, 0, 'ag_small_sum')
    parts = _unpack(summed, [a.shape for a in to_reduce])
    g_small = dict(zip(reduced_names, parts[:len(reduced_names)]))
    g_small['c_ctx'] = parts[-2]
    g_small['b_ada'] = g_b_ada
    loss = parts[-1]

    def my_shard(name, full):
        if name in COL_SHARDED_SMALL:
            n = full.shape[-1] // N_DEV
            return lax.dynamic_slice_in_dim(full, me * n, n, axis=full.ndim - 1)
        if name in ROW_SHARDED_SMALL:
            n = full.shape[1] // N_DEV
            return lax.dynamic_slice_in_dim(full, me * n, n, axis=1)
        return full

    small_names = [n for n in WEIGHTS if n not in BIG + ('w_ada',)]
    g_list = [my_shard(n, g_small[n]) for n in small_names]
    g_slab = _pack(g_list, PACK_ROWS)
    res = _sum_adamw(g_slab[None], *[_pack([t[n] for n in small_names], PACK_ROWS) for t in (w, m, v)],
                     'adamw_small')
    shapes = [w[n].shape for n in small_names]
    for k, dst in enumerate((out_g, out_d, out_m, out_v)):
        if k == 0:
            dst.update(dict(zip(small_names, g_list)))
        else:
            dst.update(dict(zip(small_names, _unpack(res[k], shapes))))

    grad_x = grads['x'][None]
    return (loss, grad_x, *[out_g[n] for n in WEIGHTS], *[out_d[n] for n in WEIGHTS],
            *[out_m[n] for n in WEIGHTS], *[out_v[n] for n in WEIGHTS])


def kernel(x, c, ctx, c_ctx, w_ada, b_ada, w_in, w_out, ln1_g, ln1_b, ln2_g, ln2_b, s5_a_re, s5_a_im, s5_b_re, s5_b_im, s5_c_re, s5_c_im, s5_log_step, s5_d, s5_glu_w, s5_glu_b, sg_ln_g, sg_ln_b, sg_w, sg_b, pool_w, pool_b, pool_scale, m2_conv_w, m2_conv_b, m2_dt_bias, m2_a_log, m2_d, m2_norm_w, ffn_w_up, ffn_conv_w, ffn_conv_b, ffn_w_down, loss_target, m_c_ctx, m_w_ada, m_b_ada, m_w_in, m_w_out, m_ln1_g, m_ln1_b, m_ln2_g, m_ln2_b, m_s5_a_re, m_s5_a_im, m_s5_b_re, m_s5_b_im, m_s5_c_re, m_s5_c_im, m_s5_log_step, m_s5_d, m_s5_glu_w, m_s5_glu_b, m_sg_ln_g, m_sg_ln_b, m_sg_w, m_sg_b, m_pool_w, m_pool_b, m_pool_scale, m_m2_conv_w, m_m2_conv_b, m_m2_dt_bias, m_m2_a_log, m_m2_d, m_m2_norm_w, m_ffn_w_up, m_ffn_conv_w, m_ffn_conv_b, m_ffn_w_down, v_c_ctx, v_w_ada, v_b_ada, v_w_in, v_w_out, v_ln1_g, v_ln1_b, v_ln2_g, v_ln2_b, v_s5_a_re, v_s5_a_im, v_s5_b_re, v_s5_b_im, v_s5_c_re, v_s5_c_im, v_s5_log_step, v_s5_d, v_s5_glu_w, v_s5_glu_b, v_sg_ln_g, v_sg_ln_b, v_sg_w, v_sg_b, v_pool_w, v_pool_b, v_pool_scale, v_m2_conv_w, v_m2_conv_b, v_m2_dt_bias, v_m2_a_log, v_m2_d, v_m2_norm_w, v_ffn_w_up, v_ffn_conv_w, v_ffn_conv_b, v_ffn_w_down):
    given = dict(locals())
    w = {n: given[n] for n in WEIGHTS}
    m = {n: given['m_' + n] for n in WEIGHTS}
    v = {n: given['v_' + n] for n in WEIGHTS}
    return _step(w, m, v, x, c, ctx, loss_target)
```

```python
import functools

import jax
import jax.numpy as jnp
import numpy as np
from jax import lax
from jax.experimental import pallas as pl
from jax.experimental.pallas import tpu as pltpu
from jax.experimental.pallas import tpu_sc as plsc

F32 = jnp.float32
BF16 = jnp.bfloat16
MESH = pl.DeviceIdType.MESH
ANY = pl.BlockSpec(memory_space=pl.ANY)

N_DEV = 8
N_CHIP = 4
LANES = 128
SUBLANES = 8
VMEM_LIMIT_BYTES = 48 * 1024 * 1024
MATMUL_VMEM_BUDGET = 36 * 1024 * 1024
PACK_ROWS = 512

GRID_W = 64
S5_CH = 16
S5_STATE = 64
SG_HEADS = 4
SG_CHUNK = 128
POOL_WINDOWS = (2, 4, 8, 16)
M2_HEAD_DIM = 64
M2_STATE = 128
M2_GROUPS = 2
M2_CONV = 4
M2_PAD = (M2_CONV // 2, M2_CONV - 1 - M2_CONV // 2)
M2_CHUNK = 128
FFN_CONV = 3
LN_EPS = 1e-5
RMS_EPS = 1e-5
ADAM_LR = 0.001
ADAM_B1 = 0.9
ADAM_B2 = 0.999
ADAM_EPS = 1e-08
ADAM_WD = 0.01
ADAM_STEP = 10

WEIGHTS = ['c_ctx', 'w_ada', 'b_ada', 'w_in', 'w_out', 'ln1_g', 'ln1_b', 'ln2_g', 'ln2_b', 's5_a_re', 's5_a_im',
           's5_b_re', 's5_b_im', 's5_c_re', 's5_c_im', 's5_log_step', 's5_d', 's5_glu_w', 's5_glu_b', 'sg_ln_g',
           'sg_ln_b', 'sg_w', 'sg_b', 'pool_w', 'pool_b', 'pool_scale', 'm2_conv_w', 'm2_conv_b', 'm2_dt_bias',
           'm2_a_log', 'm2_d', 'm2_norm_w', 'ffn_w_up', 'ffn_conv_w', 'ffn_conv_b', 'ffn_w_down']
BIG = ('w_in', 'w_out', 'ffn_w_up', 'ffn_w_down')
COL_SHARDED_SMALL = ('m2_conv_w', 'ffn_conv_w')
ROW_SHARDED_SMALL = ('s5_glu_w',)
REPLICATED = [n for n in WEIGHTS if n not in BIG + COL_SHARDED_SMALL + ROW_SHARDED_SMALL + ('w_ada', 'c_ctx')]


def _round_up(n, m):
    return (n + m - 1) // m * m


def _my_coords():
    return lax.axis_index('x'), lax.axis_index('y'), lax.axis_index('c')


def _my_index():
    x, y, c = _my_coords()
    return 4 * x + 2 * y + c


def _first_divisor(n, cands):
    for c in cands:
        if n % c == 0:
            return c
    return n


def _matmul_tiles(mode, m, n, k, a_bytes, b_bytes):
    lane_c = (1024, 512, 384, 256, 128)
    sub_c = (1088, 1024, 544, 512, 256, 128, 64, 32, 16)
    tm = _first_divisor(m, lane_c if mode == 'tn' else sub_c)
    tn = _first_divisor(n, lane_c)
    k_c = [c for c in (5632, 4096, 2048, 1408, 1088, 1024, 544, 512, 256, 128) if k % c == 0] or [k]
    if mode == 'tn':
        k_c = [c for c in k_c if c <= 1088] or [k_c[-1]]
    for tk in k_c:
        use = 2 * (tm * tk * a_bytes + tk * tn * b_bytes) + 3 * tm * tn * 4
        if use <= MATMUL_VMEM_BUDGET:
            return tm, tn, tk
    return tm, tn, k_c[-1]


def _matmul(a, b, mode, name, out_dtype=F32):
    if mode == 'nn':
        (m, k), (k2, n) = a.shape, b.shape
    elif mode == 'nt':
        (m, k), (n, k2) = a.shape, b.shape
    else:
        (k, m), (k2, n) = a.shape, b.shape
    assert k == k2, (mode, a.shape, b.shape)
    tm, tn, tk = _matmul_tiles(mode, m, n, k, a.dtype.itemsize, b.dtype.itemsize)
    nk = k // tk
    if mode == 'nn':
        a_spec = pl.BlockSpec((tm, tk), lambda i, j, kk: (i, kk))
        b_spec = pl.BlockSpec((tk, tn), lambda i, j, kk: (kk, j))
        dims = (((1,), (0,)), ((), ()))
    elif mode == 'nt':
        a_spec = pl.BlockSpec((tm, tk), lambda i, j, kk: (i, kk))
        b_spec = pl.BlockSpec((tn, tk), lambda i, j, kk: (j, kk))
        dims = (((1,), (1,)), ((), ()))
    else:
        a_spec = pl.BlockSpec((tk, tm), lambda i, j, kk: (kk, i))
        b_spec = pl.BlockSpec((tk, tn), lambda i, j, kk: (kk, j))
        dims = (((0,), (0,)), ((), ()))

    def body(a_ref, b_ref, o_ref, acc_ref):
        kk = pl.program_id(2)

        @pl.when(kk == 0)
        def _():
            acc_ref[...] = jnp.zeros_like(acc_ref)

        acc_ref[...] += lax.dot_general(a_ref[...].astype(BF16), b_ref[...].astype(BF16), dims,
                                        preferred_element_type=F32)

        @pl.when(kk == nk - 1)
        def _():
            o_ref[...] = acc_ref[...].astype(o_ref.dtype)

    return pl.pallas_call(
        body,
        name=name,
        grid=(m // tm, n // tn, nk),
        in_specs=[a_spec, b_spec],
        out_specs=pl.BlockSpec((tm, tn), lambda i, j, kk: (i, j)),
        out_shape=jax.ShapeDtypeStruct((m, n), out_dtype),
        scratch_shapes=[pltpu.VMEM((tm, tn), F32)],
        compiler_params=pltpu.CompilerParams(dimension_semantics=('parallel', 'parallel', 'arbitrary'),
                                             vmem_limit_bytes=VMEM_LIMIT_BYTES),
    )(a, b)


@functools.partial(jax.custom_vjp, nondiff_argnums=(2,))
def _linear(x, w, name):
    return _matmul(x, w, 'nn', name + '_fwd')


def _linear_fwd(x, w, name):
    return _matmul(x, w, 'nn', name + '_fwd'), (x, w)


def _linear_bwd(name, res, dy):
    x, w = res
    return _matmul(dy, w, 'nt', name + '_dx'), _matmul(x, dy, 'tn', name + '_dw')


_linear.defvjp(_linear_fwd, _linear_bwd)


@functools.partial(jax.custom_vjp, nondiff_argnums=(3,))
def _linear_tap(x, w, tap, name):
    del tap
    return _matmul(x, w, 'nn', name + '_fwd')


def _linear_tap_fwd(x, w, tap, name):
    del tap
    return _matmul(x, w, 'nn', name + '_fwd'), (x, w)


def _linear_tap_bwd(name, res, dy):
    x, w = res
    return _matmul(dy, w, 'nt', name + '_dx'), jnp.zeros_like(w), _matmul(x, dy, 'tn', name + '_dw')


_linear_tap.defvjp(_linear_tap_fwd, _linear_tap_bwd)


@functools.partial(jax.custom_vjp, nondiff_argnums=(4,))
def _linear_tap_carried(x, carrier, w, tap, name):
    del carrier, tap
    return _matmul(x, w, 'nn', name + '_fwd')


def _linear_tap_carried_fwd(x, carrier, w, tap, name):
    del carrier, tap
    return _matmul(x, w, 'nn', name + '_fwd'), (x, w)


def _linear_tap_carried_bwd(name, res, dy):
    x, w = res
    return (jnp.zeros_like(x), _matmul(dy, w, 'nt', name + '_dx'), jnp.zeros_like(w),
            _matmul(x, dy, 'tn', name + '_dw'))


_linear_tap_carried.defvjp(_linear_tap_carried_fwd, _linear_tap_carried_bwd)


def _blocked_matmul(a, b, mode, name):
    t = a.shape[0]
    nb, k, n = b.shape
    tm = _first_divisor(t, (1088, 1024, 544, 512, 256, 128, 64, 32, 16))
    cin, cout = (k, n) if mode == 'nn' else (n, k)
    dims = (((1,), (0,)), ((), ())) if mode == 'nn' else (((1,), (1,)), ((), ()))

    def body(a_ref, b_ref, o_ref):
        o_ref[...] = lax.dot_general(a_ref[...].astype(BF16), b_ref[0].astype(BF16), dims,
                                     preferred_element_type=F32)

    return pl.pallas_call(
        body, name=name, grid=(t // tm, nb),
        in_specs=[pl.BlockSpec((tm, cin), lambda i, j: (i, j)), pl.BlockSpec((1, k, n), lambda i, j: (j, 0, 0))],
        out_specs=pl.BlockSpec((tm, cout), lambda i, j: (i, j)),
        out_shape=jax.ShapeDtypeStruct((t, nb * cout), F32),
        compiler_params=pltpu.CompilerParams(vmem_limit_bytes=VMEM_LIMIT_BYTES,
                                             dimension_semantics=('parallel', 'parallel')),
    )(a, b)


def _blocked_weight_grad(x, dy, nb, name):
    t = x.shape[0]
    k, n = x.shape[1] // nb, dy.shape[1] // nb
    tk = _first_divisor(t, (1088, 1024, 544, 512, 256, 128, 64, 32, 16))
    steps = t // tk

    def body(x_ref, dy_ref, o_ref):
        @pl.when(pl.program_id(1) == 0)
        def _():
            o_ref[...] = jnp.zeros_like(o_ref)

        o_ref[0] += lax.dot_general(x_ref[...].astype(BF16), dy_ref[...].astype(BF16), (((0,), (0,)), ((), ())),
                                    preferred_element_type=F32)

    return pl.pallas_call(
        body, name=name, grid=(nb, steps),
        in_specs=[pl.BlockSpec((tk, k), lambda j, s: (s, j)), pl.BlockSpec((tk, n), lambda j, s: (s, j))],
        out_specs=pl.BlockSpec((1, k, n), lambda j, s: (j, 0, 0)),
        out_shape=jax.ShapeDtypeStruct((nb, k, n), F32),
        compiler_params=pltpu.CompilerParams(vmem_limit_bytes=VMEM_LIMIT_BYTES,
                                             dimension_semantics=('parallel', 'arbitrary')),
    )(x, dy)


@functools.partial(jax.custom_vjp, nondiff_argnums=(2,))
def _blocked_linear(x, w, name):
    return _blocked_matmul(x, w, 'nn', name + '_fwd')


def _blocked_linear_fwd(x, w, name):
    return _blocked_matmul(x, w, 'nn', name + '_fwd'), (x, w)


def _blocked_linear_bwd(name, res, dy):
    x, w = res
    return _blocked_matmul(dy, w, 'nt', name + '_dx'), _blocked_weight_grad(x, dy, w.shape[0], name + '_dw')


_blocked_linear.defvjp(_blocked_linear_fwd, _blocked_linear_bwd)


SCAN_ROWS = (256, 128)
SCAN_LANES = 512


def _scan_lanes(n):
    return SCAN_LANES if n % SCAN_LANES == 0 else n


def _scan_call(a, x, h, t_ctx, ctx_first, reverse_rows, name):
    t, n2 = x.shape
    n = n2 // 2
    rows = next(r for r in SCAN_ROWS if t_ctx % r == 0 and t % r == 0)
    n_ctx_blocks = t_ctx // rows
    nblk = t // rows
    nlat = nblk - n_ctx_blocks
    lanes = _scan_lanes(n)
    with_da = h is not None

    def block_of(i):
        if ctx_first:
            first_n, first_0, second_0, second_n = n_ctx_blocks, 0, n_ctx_blocks, nlat
        else:
            first_n, first_0, second_0, second_n = nlat, n_ctx_blocks, 0, n_ctx_blocks
        if reverse_rows:
            in_first = first_0 + first_n - 1 - i
            in_second = second_0 + second_n - 1 - (i - first_n)
        else:
            in_first = first_0 + i
            in_second = second_0 + (i - first_n)
        return jnp.where(i < first_n, in_first, in_second)

    groups = rows // SUBLANES
    first_row = SUBLANES - 1 if reverse_rows else 0
    to_previous = SUBLANES - 1 if reverse_rows else 1

    def body(*refs):
        if with_da:
            a_ref, x_ref, h_ref, o_ref, da_ref, st_ref = refs
        else:
            a_ref, x_ref, o_ref, st_ref = refs

        @pl.when(pl.program_id(1) == 0)
        def _():
            st_ref[...] = jnp.zeros_like(st_ref)
            if with_da:
                da_ref[...] = jnp.zeros_like(da_ref)

        row_id = lax.broadcasted_iota(jnp.int32, (SUBLANES, lanes), 0)
        behind = (SUBLANES - 1 - row_id) if reverse_rows else row_id

        def cmul(pr, pi, qr, qi):
            return pr * qr - pi * qi, pr * qi + pi * qr

        a1 = (jnp.broadcast_to(a_ref[0:1, :], (SUBLANES, lanes)), jnp.broadcast_to(a_ref[1:2, :], (SUBLANES, lanes)))
        a2 = cmul(*a1, *a1)
        a4 = cmul(*a2, *a2)
        pw = a1
        for bit, ak in ((1, a1), (2, a2), (4, a4)):
            nxt = cmul(*pw, *ak)
            pw = (jnp.where((behind & bit) != 0, nxt[0], pw[0]), jnp.where((behind & bit) != 0, nxt[1], pw[1]))

        def group(g, carry):
            in_r, in_i = carry
            gi = (groups - 1 - g) if reverse_rows else g
            start = pl.multiple_of(gi * SUBLANES, SUBLANES)
            xr = x_ref[pl.ds(start, SUBLANES), pl.ds(0, lanes)]
            xi = x_ref[pl.ds(start, SUBLANES), pl.ds(lanes, lanes)]
            for k, ak in ((1, a1), (2, a2), (4, a4)):
                shift = (SUBLANES - k) if reverse_rows else k
                pr = jnp.where(behind >= k, pltpu.roll(xr, shift, axis=0), 0.0)
                pi = jnp.where(behind >= k, pltpu.roll(xi, shift, axis=0), 0.0)
                qr, qi = cmul(*ak, pr, pi)
                xr, xi = xr + qr, xi + qi
            cr, ci = cmul(*pw, in_r, in_i)
            out_r, out_i = xr + cr, xi + ci
            last = 0 if reverse_rows else SUBLANES - 1
            sr = jnp.broadcast_to(out_r[last:last + 1, :], (SUBLANES, lanes))
            si = jnp.broadcast_to(out_i[last:last + 1, :], (SUBLANES, lanes))
            o_ref[pl.ds(start, SUBLANES), pl.ds(0, lanes)] = out_r
            o_ref[pl.ds(start, SUBLANES), pl.ds(lanes, lanes)] = out_i
            if with_da:
                pr = jnp.where(row_id == first_row, in_r, pltpu.roll(out_r, to_previous, axis=0))
                pi = jnp.where(row_id == first_row, in_i, pltpu.roll(out_i, to_previous, axis=0))
                hr = h_ref[pl.ds(start, SUBLANES), pl.ds(0, lanes)]
                hi = h_ref[pl.ds(start, SUBLANES), pl.ds(lanes, lanes)]
                da_ref[0:SUBLANES, :] += hr * pr + hi * pi
                da_ref[SUBLANES:2 * SUBLANES, :] += hr * pi - hi * pr
            return sr, si

        sr, si = lax.fori_loop(0, groups, group, (st_ref[0], st_ref[1]))
        st_ref[0] = sr
        st_ref[1] = si

    row_spec = pl.BlockSpec((rows, 2 * lanes), lambda j, i: (block_of(i), j))
    in_specs = [pl.BlockSpec((2, lanes), lambda j, i: (0, j)), row_spec]
    out_specs = [row_spec]
    out_shape = [jax.ShapeDtypeStruct((t, n2), F32)]
    operands = [a, x]
    if with_da:
        in_specs.append(row_spec)
        operands.append(h)
        out_specs.append(pl.BlockSpec((2 * SUBLANES, lanes), lambda j, i: (0, j)))
        out_shape.append(jax.ShapeDtypeStruct((2 * SUBLANES, n), F32))
    return pl.pallas_call(
        body,
        name=name,
        grid=(n // lanes, nblk),
        in_specs=in_specs,
        out_specs=out_specs,
        out_shape=out_shape,
        scratch_shapes=[pltpu.VMEM((2, SUBLANES, lanes), F32)],
        compiler_params=pltpu.CompilerParams(dimension_semantics=('parallel', 'arbitrary')),
    )(*operands)


@functools.partial(jax.custom_vjp, nondiff_argnums=(2, 3, 4))
def _s5_scan(a, bu, t_ctx, reverse, name):
    return _scan_call(a, bu, None, t_ctx, True, reverse, name + '_fwd')[0]


def _s5_scan_fwd(a, bu, t_ctx, reverse, name):
    h = _scan_call(a, bu, None, t_ctx, True, reverse, name + '_fwd')[0]
    return h, (a, h)


def _s5_scan_bwd(t_ctx, reverse, name, res, dh):
    a, h = res
    a_conj = a * jnp.array([[1.0], [-1.0]], F32)
    g, da = _scan_call(a_conj, dh, h, t_ctx, False, not reverse, name + '_bwd')
    da = jnp.stack([jnp.sum(da[:SUBLANES], axis=0), jnp.sum(da[SUBLANES:], axis=0)])
    return da, g


_s5_scan.defvjp(_s5_scan_fwd, _s5_scan_bwd)


FFN_LANES = 128
CONV_TAPS = FFN_CONV * FFN_CONV
CONV_W_ROWS = 16
GELU_K = 0.7978845608028654
GELU_C = 0.044715


def _gelu_and_slope(x):
    x2 = x * x
    th = jnp.tanh(GELU_K * (x + GELU_C * x * x2))
    cdf = 0.5 * (1.0 + th)
    slope = cdf + 0.5 * x * (1.0 - th * th) * (GELU_K * (1.0 + 3.0 * GELU_C * x2))
    return x * cdf, slope


def _band_rows(t_lat):
    return _first_divisor(t_lat // GRID_W, (8, 4, 2, 1)) * GRID_W


def _shifted_ctx(x):
    n = x.shape[0]
    row = lax.broadcasted_iota(jnp.int32, x.shape, 0)
    left = jnp.where(row == 0, 0.0, pltpu.roll(x, 1, axis=0))
    right = jnp.where(row == n - 1, 0.0, pltpu.roll(x, n - 1, axis=0))
    return left, x, right


def _shifted_band(src_ref, bidx, n_bands, band, t_ctx, t):
    start = pl.multiple_of(t_ctx + bidx * band, GRID_W)
    top_start = pl.multiple_of(jnp.maximum(start - GRID_W, 0), SUBLANES)
    bot_start = pl.multiple_of(jnp.minimum(start + band, t - GRID_W), SUBLANES)
    top = jnp.where(bidx > 0, src_ref[pl.ds(top_start, GRID_W), :], 0.0)
    bot = jnp.where(bidx < n_bands - 1, src_ref[pl.ds(bot_start, GRID_W), :], 0.0)
    ext = jnp.concatenate([top, src_ref[pl.ds(start, band), :], bot], axis=0)
    n_ext = band + 2 * GRID_W
    col = lax.broadcasted_iota(jnp.int32, ext.shape, 0) & (GRID_W - 1)
    left = jnp.where(col == 0, 0.0, pltpu.roll(ext, 1, axis=0))
    right = jnp.where(col == GRID_W - 1, 0.0, pltpu.roll(ext, n_ext - 1, axis=0))
    return left, ext, right


def _conv_sum(shifted, w_rows, band, flip):
    acc = None
    for i in range(FFN_CONV):
        for j in range(FFN_CONV):
            w = w_rows[(FFN_CONV - 1 - i) * FFN_CONV + (FFN_CONV - 1 - j)] if flip else w_rows[i * FFN_CONV + j]
            term = shifted[j][i * GRID_W:i * GRID_W + band] * w
            acc = term if acc is None else acc + term
    return acc


def _conv_operand(conv_w, conv_b):
    f = conv_b.shape[0]
    return jnp.concatenate([conv_w.reshape(CONV_TAPS, f), conv_b[None],
                            jnp.zeros((CONV_W_ROWS - CONV_TAPS - 1, f), F32)], axis=0)


def _ffn_mid_fwd(gv, w16, t_ctx, name):
    t, f2 = gv.shape
    f = f2 // 2
    lanes = FFN_LANES
    nf = f // lanes
    band = _band_rows(t - t_ctx)
    n_bands = (t - t_ctx) // band
    mid = (FFN_CONV // 2) * FFN_CONV

    def body(g_ref, v_ref, w_ref, o_ref):
        w_rows = [w_ref[k:k + 1, :] for k in range(CONV_TAPS)]
        bias = w_ref[CONV_TAPS:CONV_TAPS + 1, :]
        if t_ctx:
            sh = _shifted_ctx(g_ref[0:t_ctx, :])
            pre = sh[0] * w_rows[mid] + sh[1] * w_rows[mid + 1] + sh[2] * w_rows[mid + 2] + bias
            o_ref[0:t_ctx, :] = (_gelu_and_slope(pre)[0] * v_ref[0:t_ctx, :]).astype(o_ref.dtype)

        def one_band(b, carry):
            start = pl.multiple_of(t_ctx + b * band, GRID_W)
            pre = _conv_sum(_shifted_band(g_ref, b, n_bands, band, t_ctx, t), w_rows, band, False) + bias
            o_ref[pl.ds(start, band), :] = (_gelu_and_slope(pre)[0] * v_ref[pl.ds(start, band), :]).astype(o_ref.dtype)
            return carry

        lax.fori_loop(0, n_bands, one_band, 0)

    return pl.pallas_call(
        body, name=name,
        grid=(nf,),
        in_specs=[pl.BlockSpec((t, lanes), lambda j: (0, j)),
                  pl.BlockSpec((t, lanes), lambda j: (0, nf + j)),
                  pl.BlockSpec((CONV_W_ROWS, lanes), lambda j: (0, j))],
        out_specs=pl.BlockSpec((t, lanes), lambda j: (0, j)),
        out_shape=jax.ShapeDtypeStruct((t, f), BF16),
        compiler_params=pltpu.CompilerParams(dimension_semantics=('parallel',), vmem_limit_bytes=VMEM_LIMIT_BYTES),
    )(gv, gv, w16)


def _ffn_mid_bwd(d_act, gv, w16, t_ctx, name):
    t, f2 = gv.shape
    f = f2 // 2
    lanes = FFN_LANES
    nf = f // lanes
    band = _band_rows(t - t_ctx)
    n_bands = (t - t_ctx) // band
    mid = (FFN_CONV // 2) * FFN_CONV
    n_acc = CONV_TAPS + 1

    def tile_sum(x):
        return jnp.sum(x.reshape(x.shape[0] // SUBLANES, SUBLANES, lanes), axis=0)

    def body(da_ref, g_ref, v_ref, w_ref, dg_ref, dv_ref, dw_ref, dp_ref):
        w_rows = [w_ref[k:k + 1, :] for k in range(CONV_TAPS)]
        bias = w_ref[CONV_TAPS:CONV_TAPS + 1, :]
        acc = [jnp.zeros((SUBLANES, lanes), F32) for _ in range(n_acc)]
        if t_ctx:
            sh = _shifted_ctx(g_ref[0:t_ctx, :])
            pre = sh[0] * w_rows[mid] + sh[1] * w_rows[mid + 1] + sh[2] * w_rows[mid + 2] + bias
            ge, slope = _gelu_and_slope(pre)
            da = da_ref[0:t_ctx, :]
            dv_ref[0:t_ctx, :] = (da * ge).astype(dv_ref.dtype)
            dpre = da * v_ref[0:t_ctx, :] * slope
            dp_ref[0:t_ctx, :] = dpre
            for j in range(FFN_CONV):
                acc[mid + j] = acc[mid + j] + tile_sum(sh[j] * dpre)
            acc[CONV_TAPS] = acc[CONV_TAPS] + tile_sum(dpre)
            back = _shifted_ctx(dpre)
            dg_ref[0:t_ctx, :] = (back[2] * w_rows[mid] + back[1] * w_rows[mid + 1]
                                  + back[0] * w_rows[mid + 2]).astype(dg_ref.dtype)

        def first_pass(b, acc):
            acc = list(acc)
            start = pl.multiple_of(t_ctx + b * band, GRID_W)
            sh = _shifted_band(g_ref, b, n_bands, band, t_ctx, t)
            ge, slope = _gelu_and_slope(_conv_sum(sh, w_rows, band, False) + bias)
            da = da_ref[pl.ds(start, band), :]
            dv_ref[pl.ds(start, band), :] = (da * ge).astype(dv_ref.dtype)
            dpre = da * v_ref[pl.ds(start, band), :] * slope
            dp_ref[pl.ds(start, band), :] = dpre
            for i in range(FFN_CONV):
                for j in range(FFN_CONV):
                    k = i * FFN_CONV + j
                    acc[k] = acc[k] + tile_sum(sh[j][i * GRID_W:i * GRID_W + band] * dpre)
            acc[CONV_TAPS] = acc[CONV_TAPS] + tile_sum(dpre)
            return tuple(acc)

        acc = lax.fori_loop(0, n_bands, first_pass, tuple(acc))
        for k in range(n_acc):
            dw_ref[k * SUBLANES:(k + 1) * SUBLANES, :] = acc[k]

        def second_pass(b, carry):
            start = pl.multiple_of(t_ctx + b * band, GRID_W)
            dg_ref[pl.ds(start, band), :] = _conv_sum(_shifted_band(dp_ref, b, n_bands, band, t_ctx, t),
                                                      w_rows, band, True).astype(dg_ref.dtype)
            return carry

        lax.fori_loop(0, n_bands, second_pass, 0)

    col = pl.BlockSpec((t, lanes), lambda j: (0, j))
    return pl.pallas_call(
        body, name=name,
        grid=(nf,),
        in_specs=[col, col, pl.BlockSpec((t, lanes), lambda j: (0, nf + j)),
                  pl.BlockSpec((CONV_W_ROWS, lanes), lambda j: (0, j))],
        out_specs=[col, col, pl.BlockSpec((n_acc * SUBLANES, lanes), lambda j: (0, j))],
        out_shape=[jax.ShapeDtypeStruct((t, f), BF16), jax.ShapeDtypeStruct((t, f), BF16),
                   jax.ShapeDtypeStruct((n_acc * SUBLANES, f), F32)],
        scratch_shapes=[pltpu.VMEM((t, lanes), F32)],
        compiler_params=pltpu.CompilerParams(dimension_semantics=('parallel',), vmem_limit_bytes=VMEM_LIMIT_BYTES),
    )(d_act, gv, gv, w16)


def _gate_value(up, layout):
    per_dev, per_dev_pad, f = layout
    if per_dev == per_dev_pad:
        return up
    return jnp.concatenate([_take_cols(up, per_dev, per_dev_pad, 0, f), _take_cols(up, per_dev, per_dev_pad, f, 2 * f)],
                           axis=1)


def _ffn_forward(hm, w_up, conv_w, conv_b, w_down, t_ctx, layout, name):
    up = _matmul(hm, w_up, 'nn', name + '_up_fwd')
    gv = _gate_value(up, layout)
    w16 = _conv_operand(conv_w, conv_b)
    act = _ffn_mid_fwd(gv, w16, t_ctx, name + '_mid_fwd')
    return _matmul(act, w_down, 'nn', name + '_down_fwd'), (hm, w_up, gv, w16, w_down, act)


@functools.partial(jax.custom_vjp, nondiff_argnums=(8, 9, 10))
def _ffn_block(hm, carrier, w_up, tap_up, conv_w, conv_b, w_down, tap_down, t_ctx, layout, name):
    del carrier, tap_up, tap_down
    return _ffn_forward(hm, w_up, conv_w, conv_b, w_down, t_ctx, layout, name)[0]


def _ffn_block_fwd(hm, carrier, w_up, tap_up, conv_w, conv_b, w_down, tap_down, t_ctx, layout, name):
    del carrier, tap_up, tap_down
    return _ffn_forward(hm, w_up, conv_w, conv_b, w_down, t_ctx, layout, name)


def _ffn_block_bwd(t_ctx, layout, name, res, d_out):
    hm, w_up, gv, w16, w_down, act = res
    f = act.shape[1]
    d_act = _matmul(d_out, w_down, 'nt', name + '_down_dx')
    d_w_down = _matmul(act, d_out, 'tn', name + '_down_dw')
    d_gate, d_val, d_w = _ffn_mid_bwd(d_act, gv, w16, t_ctx, name + '_mid_bwd')
    d_gv = jnp.concatenate([d_gate, d_val], axis=1)
    if layout[0] == layout[1]:
        d_up = d_gv
    else:
        d_up = jax.vjp(lambda u: _gate_value(u, layout), jnp.zeros((gv.shape[0], w_up.shape[1]), d_gv.dtype))[1](d_gv)[0]
    d_hm = _matmul(d_up, w_up, 'nt', name + '_up_dx')
    d_w_up = _matmul(hm, d_up, 'tn', name + '_up_dw')
    d_w = jnp.sum(d_w.reshape(CONV_TAPS + 1, SUBLANES, f), axis=1)
    return (jnp.zeros_like(hm), d_hm, jnp.zeros_like(w_up), d_w_up, d_w[:CONV_TAPS].reshape(FFN_CONV, FFN_CONV, f),
            d_w[CONV_TAPS], jnp.zeros_like(w_down), d_w_down)


_ffn_block.defvjp(_ffn_block_fwd, _ffn_block_bwd)


LN_ROWS = 128
MOD_ROWS = 8


def _normalised(h, y, gate, alpha):
    r = alpha * h + gate * y
    xc = r - jnp.mean(r, axis=-1, keepdims=True)
    rstd = lax.rsqrt(jnp.mean(xc * xc, axis=-1, keepdims=True) + LN_EPS)
    return xc * rstd, rstd


def _mod_row(m_ref, k, is_ctx):
    lat = m_ref[3 + k:4 + k, :]
    return lat if is_ctx is None else jnp.where(is_ctx, m_ref[k:k + 1, :], lat)


def _res_ln_fwd_call(h, y, mods, t_ctx, alpha, with_mod, name):
    t, d = h.shape
    n_ctx_tiles = t_ctx // LN_ROWS

    def body(h_ref, y_ref, m_ref, hn_ref, *rest):
        is_ctx = (pl.program_id(0) < n_ctx_tiles) if n_ctx_tiles else None
        xhat, _ = _normalised(h_ref[...], y_ref[...], _mod_row(m_ref, 0, is_ctx), alpha)
        hn = xhat * m_ref[6:7, :] + m_ref[7:8, :]
        hn_ref[...] = hn
        if with_mod:
            hm = hn * (1.0 + _mod_row(m_ref, 2, is_ctx)) + _mod_row(m_ref, 1, is_ctx)
            rest[0][...] = hm.astype(rest[0].dtype)

    blk = pl.BlockSpec((LN_ROWS, d), lambda i: (i, 0))
    out_shape = [jax.ShapeDtypeStruct((t, d), F32)] + ([jax.ShapeDtypeStruct((t, d), BF16)] if with_mod else [])
    return pl.pallas_call(
        body, name=name,
        grid=(t // LN_ROWS,),
        in_specs=[blk, blk, pl.BlockSpec((MOD_ROWS, d), lambda i: (0, 0))],
        out_specs=[blk] * len(out_shape),
        out_shape=out_shape,
        compiler_params=pltpu.CompilerParams(dimension_semantics=('parallel',)),
    )(h, y, mods)


def _res_ln_bwd_call(h, y, mods, d_hn, d_hm, t_ctx, alpha, name):
    t, d = h.shape
    n_ctx_tiles = t_ctx // LN_ROWS
    with_mod = d_hm is not None

    def tile_sum(x):
        return jnp.sum(x.reshape(LN_ROWS // SUBLANES, SUBLANES, d), axis=0)

    def body(*refs):
        if with_mod:
            h_ref, y_ref, m_ref, dhn_ref, dhm_ref, dh_ref, dy_ref, acc_ref = refs
        else:
            h_ref, y_ref, m_ref, dhn_ref, dh_ref, dy_ref, acc_ref = refs

        @pl.when(pl.program_id(0) == 0)
        def _():
            acc_ref[...] = jnp.zeros_like(acc_ref)

        is_ctx = (pl.program_id(0) < n_ctx_tiles) if n_ctx_tiles else None
        gate = _mod_row(m_ref, 0, is_ctx)
        y = y_ref[...]
        xhat, rstd = _normalised(h_ref[...], y, gate, alpha)
        ln_g = m_ref[6:7, :]
        dhn = dhn_ref[...]
        base = 3 * SUBLANES if is_ctx is None else jnp.where(is_ctx, 0, 3 * SUBLANES)

        def add_to(row, part):
            if isinstance(row, int):
                acc_ref[row:row + SUBLANES, :] += part
            else:
                acc_ref[pl.ds(pl.multiple_of(row, SUBLANES), SUBLANES), :] += part

        if with_mod:
            dhm = dhm_ref[...]
            hn = xhat * ln_g + m_ref[7:8, :]
            add_to(base + SUBLANES, tile_sum(dhm))
            add_to(base + 2 * SUBLANES, tile_sum(dhm * hn))
            dhn = dhn + dhm * (1.0 + _mod_row(m_ref, 2, is_ctx))
        add_to(6 * SUBLANES, tile_sum(dhn * xhat))
        add_to(7 * SUBLANES, tile_sum(dhn))
        dx = dhn * ln_g
        dr = rstd * (dx - jnp.mean(dx, axis=-1, keepdims=True) - xhat * jnp.mean(dx * xhat, axis=-1, keepdims=True))
        dh_ref[...] = alpha * dr
        dy_ref[...] = gate * dr
        add_to(base, tile_sum(dr * y))

    blk = pl.BlockSpec((LN_ROWS, d), lambda i: (i, 0))
    operands = [h, y, mods, d_hn] + ([d_hm] if with_mod else [])
    return pl.pallas_call(
        body, name=name,
        grid=(t // LN_ROWS,),
        in_specs=[blk, blk, pl.BlockSpec((MOD_ROWS, d), lambda i: (0, 0)), blk] + ([blk] if with_mod else []),
        out_specs=[blk, blk, pl.BlockSpec((MOD_ROWS * SUBLANES, d), lambda i: (0, 0))],
        out_shape=[jax.ShapeDtypeStruct((t, d), F32), jax.ShapeDtypeStruct((t, d), F32),
                   jax.ShapeDtypeStruct((MOD_ROWS * SUBLANES, d), F32)],
        compiler_params=pltpu.CompilerParams(dimension_semantics=('arbitrary',)),
    )(*operands)


@functools.partial(jax.custom_vjp, nondiff_argnums=(4, 5, 6, 7))
def _res_ln_mod(h, y, mods, carrier, t_ctx, alpha, with_mod, name):
    out = tuple(_res_ln_fwd_call(h, y, mods, t_ctx, alpha, with_mod, name + '_fwd'))
    return out + (carrier,) if with_mod else out


def _res_ln_mod_fwd(h, y, mods, carrier, t_ctx, alpha, with_mod, name):
    out = tuple(_res_ln_fwd_call(h, y, mods, t_ctx, alpha, with_mod, name + '_fwd'))
    return (out + (carrier,) if with_mod else out), (h, y, mods, carrier)


def _res_ln_mod_bwd(t_ctx, alpha, with_mod, name, res, cts):
    h, y, mods, carrier = res
    d_h, d_y, acc = _res_ln_bwd_call(h, y, mods, cts[0], cts[2] if with_mod else None, t_ctx, alpha, name + '_bwd')
    return d_h, d_y, jnp.sum(acc.reshape(MOD_ROWS, SUBLANES, h.shape[1]), axis=1), jnp.zeros_like(carrier)


_res_ln_mod.defvjp(_res_ln_mod_fwd, _res_ln_mod_bwd)


def _window(ref, j, rows, cols, axis):
    if axis == 0:
        return ref.at[pl.ds(j * rows, rows), :]
    return ref.at[:, pl.ds(j * cols, cols)]


def _all_gather(block, axis, name):
    rows, cols = block.shape

    def body(x_ref, out_ref, send_sems, recv_sems, local_sem):
        x, y, c = _my_coords()
        me, sibling = (x, y, c), (x, y, 1 - c)
        chips = [(1 - x, y), (x, 1 - y), (1 - x, 1 - y)]

        def win(px, py, pc):
            return _window(out_ref, 4 * px + 2 * py + pc, rows, cols, axis)

        def copy(k, blk, to, src=None):
            return pltpu.make_async_remote_copy(
                src_ref=win(*blk) if src is None else src, dst_ref=win(*blk),
                send_sem=send_sems.at[k], recv_sem=recv_sems.at[k], device_id=to, device_id_type=MESH)

        mine = pltpu.make_async_copy(x_ref, win(*me), local_sem)
        mine.start()
        first = [copy(0, me, sibling, src=x_ref)]
        first += [copy(1 + j, me, (*chip, c), src=x_ref) for j, chip in enumerate(chips)]
        for cp in first:
            cp.start()
        passed = [copy(4 + j, (*chip, c), sibling) for j, chip in enumerate(chips)]
        for j, chip in enumerate(chips):
            copy(1 + j, (*chip, c), me).wait_recv()
            passed[j].start()
        copy(0, sibling, me).wait_recv()
        for j, chip in enumerate(chips):
            copy(4 + j, (*chip, 1 - c), me).wait_recv()
        for cp in first + passed:
            cp.wait_send()
        mine.wait()

    out_shape = (N_DEV * rows, cols) if axis == 0 else (rows, N_DEV * cols)
    return pl.pallas_call(
        body, name=name,
        out_shape=jax.ShapeDtypeStruct(out_shape, block.dtype),
        in_specs=[ANY], out_specs=ANY,
        scratch_shapes=[pltpu.SemaphoreType.DMA((7,)), pltpu.SemaphoreType.DMA((7,)), pltpu.SemaphoreType.DMA],
    )(block)


AG_COLLECTIVE_ID = 1


def _all_gather_sequencer(block, axis, name):
    rows, cols = block.shape
    out_shape = (N_DEV * rows, cols) if axis == 0 else (rows, N_DEV * cols)
    x_ref = jax.new_ref(block, memory_space=pltpu.MemorySpace.HBM)
    out_ref = jax.empty_ref(jax.ShapeDtypeStruct(out_shape, block.dtype), memory_space=pltpu.MemorySpace.HBM)

    @pl.kernel(mesh=plsc.ScalarSubcoreMesh(axis_name='sequencer', num_cores=1), name=name,
               scratch_types=(pltpu.SemaphoreType.DMA((7,)), pltpu.SemaphoreType.DMA((7,)), pltpu.SemaphoreType.DMA),
               compiler_params=pltpu.CompilerParams(collective_id=AG_COLLECTIVE_ID))
    def launch(send_sems, recv_sems, local_sem):
        x, y, c = _my_coords()
        me, sibling = (x, y, c), (x, y, 1 - c)
        chips = [(1 - x, y), (x, 1 - y), (1 - x, 1 - y)]
        barrier = pltpu.get_barrier_semaphore()
        for peer in [sibling] + [(*chip, c) for chip in chips]:
            pl.semaphore_signal(barrier, inc=1, device_id=peer, device_id_type=MESH)
        pl.semaphore_wait(barrier, 1 + len(chips))

        def win(px, py, pc):
            return _window(out_ref, 4 * px + 2 * py + pc, rows, cols, axis)

        def copy(k, blk, to, src=None):
            return pltpu.make_async_remote_copy(
                src_ref=win(*blk) if src is None else src, dst_ref=win(*blk),
                send_sem=send_sems.at[k], recv_sem=recv_sems.at[k], device_id=to, device_id_type=MESH)

        mine = pltpu.make_async_copy(x_ref, win(*me), local_sem)
        mine.start()
        first = [copy(0, me, sibling, src=x_ref)]
        first += [copy(1 + j, me, (*chip, c), src=x_ref) for j, chip in enumerate(chips)]
        for cp in first:
            cp.start()
        passed = [copy(4 + j, (*chip, c), sibling) for j, chip in enumerate(chips)]
        for j, chip in enumerate(chips):
            copy(1 + j, (*chip, c), me).wait_recv()
            passed[j].start()
        copy(0, sibling, me).wait_recv()
        for j, chip in enumerate(chips):
            copy(4 + j, (*chip, 1 - c), me).wait_recv()
        for cp in first + passed:
            cp.wait_send()
        mine.wait()

    launch()
    return out_ref[...]


PAIR_COLLECTIVE_ID = 2
CHIPS_COLLECTIVE_ID = 3


def _handshake(peers):
    barrier = pltpu.get_barrier_semaphore()
    for peer in peers:
        pl.semaphore_signal(barrier, inc=1, device_id=peer, device_id_type=MESH)
    pl.semaphore_wait(barrier, len(peers))


def _launch_on_sequencer(body, operand, result, scratch_types, collective_id, name):
    operand_ref = jax.new_ref(operand, memory_space=pltpu.MemorySpace.HBM)
    result_ref = jax.empty_ref(result, memory_space=pltpu.MemorySpace.HBM)
    pl.kernel(functools.partial(body, operand_ref, result_ref),
              mesh=plsc.ScalarSubcoreMesh(axis_name='sequencer', num_cores=1), name=name,
              scratch_types=scratch_types, compiler_params=pltpu.CompilerParams(collective_id=collective_id))()
    return result_ref[...]


def _exchange_pair(full, rows, cols, axis, sequencer, name):
    def body(g_ref, land_ref, send_sems, recv_sems):
        x, y, c = _my_coords()
        if sequencer:
            _handshake([(x, y, 1 - c)])
        copies = []
        for k in range(N_CHIP):
            j = 2 * k + (1 - c)
            copies.append(pltpu.make_async_remote_copy(
                src_ref=_window(g_ref, j, rows, cols, axis), dst_ref=land_ref.at[k],
                send_sem=send_sems.at[k], recv_sem=recv_sems.at[k], device_id=(x, y, 1 - c), device_id_type=MESH))
        for cp in copies:
            cp.start()
        for cp in copies:
            cp.wait()

    result = jax.ShapeDtypeStruct((N_CHIP, rows, cols), full.dtype)
    sems = [pltpu.SemaphoreType.DMA((N_CHIP,)), pltpu.SemaphoreType.DMA((N_CHIP,))]
    if sequencer:
        return _launch_on_sequencer(body, full, result, sems, PAIR_COLLECTIVE_ID, name)
    return pl.pallas_call(body, name=name, out_shape=result, in_specs=[ANY], out_specs=ANY, scratch_shapes=sems)(full)


def _elementwise_tiles(rows, cols):
    tc = _first_divisor(cols, (1024, 512, 384, 256, 128))
    tr = _first_divisor(rows, (512, 256, 128, 64, 32, 16, 8))
    return tr, tc


def _pair_add(full, land, rows, cols, axis, wire_dtype, name):
    tr, tc = _elementwise_tiles(rows, cols)
    c_arr = jnp.reshape(lax.axis_index('c'), (1,)).astype(jnp.int32)

    def full_map(k, i, j, c_ref):
        blk = 2 * k + c_ref[0]
        if axis == 0:
            return (blk * (rows // tr) + i, j)
        return (i, blk * (cols // tc) + j)

    def body(c_ref, g_ref, l_ref, o_ref):
        del c_ref
        o_ref[0] = (g_ref[...] + l_ref[0]).astype(o_ref.dtype)

    return pl.pallas_call(
        body, name=name,
        grid_spec=pltpu.PrefetchScalarGridSpec(
            num_scalar_prefetch=1,
            grid=(N_CHIP, rows // tr, cols // tc),
            in_specs=[pl.BlockSpec((tr, tc), full_map),
                      pl.BlockSpec((1, tr, tc), lambda k, i, j, c_ref: (k, i, j))],
            out_specs=pl.BlockSpec((1, tr, tc), lambda k, i, j, c_ref: (k, i, j)),
        ),
        out_shape=jax.ShapeDtypeStruct((N_CHIP, rows, cols), wire_dtype),
        compiler_params=pltpu.CompilerParams(dimension_semantics=('parallel', 'parallel', 'parallel')),
    )(c_arr, full, land)


def _exchange_chips(sums, sequencer, name):
    _, rows, cols = sums.shape

    def body(s_ref, land_ref, send_sems, recv_sems, local_sem):
        x, y, c = _my_coords()
        my_chip = 2 * x + y
        chips = [(1 - x, y), (x, 1 - y), (1 - x, 1 - y)]
        if sequencer:
            _handshake([(px, py, c) for px, py in chips])
        mine = pltpu.make_async_copy(s_ref.at[my_chip], land_ref.at[my_chip], local_sem)
        mine.start()
        sends = []
        for t, (px, py) in enumerate(chips):
            sends.append(pltpu.make_async_remote_copy(
                src_ref=s_ref.at[2 * px + py], dst_ref=land_ref.at[my_chip],
                send_sem=send_sems.at[t], recv_sem=recv_sems.at[t], device_id=(px, py, c), device_id_type=MESH))
        for cp in sends:
            cp.start()
        for t, (px, py) in enumerate(chips):
            pltpu.make_async_remote_copy(
                src_ref=s_ref.at[my_chip], dst_ref=land_ref.at[2 * px + py],
                send_sem=send_sems.at[t], recv_sem=recv_sems.at[t], device_id=(px, py, c),
                device_id_type=MESH).wait_recv()
        for cp in sends:
            cp.wait_send()
        mine.wait()

    result = jax.ShapeDtypeStruct((N_CHIP, rows, cols), sums.dtype)
    sems = [pltpu.SemaphoreType.DMA((3,)), pltpu.SemaphoreType.DMA((3,)), pltpu.SemaphoreType.DMA]
    if sequencer:
        return _launch_on_sequencer(body, sums, result, sems, CHIPS_COLLECTIVE_ID, name)
    return pl.pallas_call(body, name=name, out_shape=result, in_specs=[ANY], out_specs=ANY, scratch_shapes=sems)(sums)


def _reduce_scatter_slots(full, rows, cols, axis, wire_dtype, sequencer, name):
    land = _exchange_pair(full, rows, cols, axis, sequencer, name + '_pair')
    sums = _pair_add(full, land, rows, cols, axis, wire_dtype, name + '_add')
    return _exchange_chips(sums, sequencer, name + '_chips')


def _sum_slots(slots, name):
    n_slots, rows, cols = slots.shape
    tr, tc = _elementwise_tiles(rows, cols)

    def body(s_ref, o_ref):
        g = s_ref[0]
        for s in range(1, n_slots):
            g = g + s_ref[s]
        o_ref[...] = g

    return pl.pallas_call(
        body, name=name,
        grid=(rows // tr, cols // tc),
        in_specs=[pl.BlockSpec((n_slots, tr, tc), lambda i, j: (0, i, j))],
        out_specs=pl.BlockSpec((tr, tc), lambda i, j: (i, j)),
        out_shape=jax.ShapeDtypeStruct((rows, cols), F32),
        compiler_params=pltpu.CompilerParams(dimension_semantics=('parallel', 'parallel')),
    )(slots)


def _sum_adamw(slots, w, m, v, name):
    n_slots, rows, cols = slots.shape
    tr, tc = _elementwise_tiles(rows, cols)
    c1 = 1.0 - ADAM_B1 ** ADAM_STEP
    c2 = 1.0 - ADAM_B2 ** ADAM_STEP

    def body(s_ref, w_ref, m_ref, v_ref, g_out, d_out, m_out, v_out):
        g = s_ref[0].astype(F32)
        for s in range(1, n_slots):
            g = g + s_ref[s].astype(F32)
        m_new = ADAM_B1 * m_ref[...] + (1.0 - ADAM_B1) * g
        v_new = ADAM_B2 * v_ref[...] + (1.0 - ADAM_B2) * (g * g)
        m_hat = m_new / c1
        v_hat = v_new / c2
        g_out[...] = g
        d_out[...] = -ADAM_LR * (m_hat / (jnp.sqrt(v_hat) + ADAM_EPS) + ADAM_WD * w_ref[...])
        m_out[...] = m_new
        v_out[...] = v_new

    blk = pl.BlockSpec((tr, tc), lambda i, j: (i, j))
    shape = jax.ShapeDtypeStruct((rows, cols), F32)
    return pl.pallas_call(
        body, name=name,
        grid=(rows // tr, cols // tc),
        in_specs=[pl.BlockSpec((n_slots, tr, tc), lambda i, j: (0, i, j)), blk, blk, blk],
        out_specs=[blk, blk, blk, blk],
        out_shape=[shape, shape, shape, shape],
        compiler_params=pltpu.CompilerParams(dimension_semantics=('parallel', 'parallel')),
    )(slots, w, m, v)


def _rows_of(shape):
    size = int(np.prod(shape)) if len(shape) else 1
    return _round_up(_round_up(size, LANES) // LANES, SUBLANES)


def _pack(arrays, rows_multiple):
    parts = []
    for arr in arrays:
        flat = jnp.ravel(arr).astype(F32)
        rows = _rows_of(arr.shape)
        parts.append(jnp.pad(flat, (0, rows * LANES - flat.shape[0])).reshape(rows, LANES))
    total = sum(p.shape[0] for p in parts)
    pad = _round_up(total, rows_multiple) - total
    if pad:
        parts.append(jnp.zeros((pad, LANES), F32))
    return jnp.concatenate(parts, axis=0)


def _unpack(slab, shapes):
    out, r0 = [], 0
    for s in shapes:
        size = int(np.prod(s)) if len(s) else 1
        rows = _rows_of(s)
        out.append(slab[r0:r0 + rows].reshape(-1)[:size].reshape(s))
        r0 += rows
    return out


def _layer_norm(x, g, b):
    mu = jnp.mean(x, -1, keepdims=True)
    var = jnp.mean(jnp.square(x - mu), -1, keepdims=True)
    return (x - mu) * lax.rsqrt(var + LN_EPS) * g + b


def _modulate(x, shift, scale):
    return x * (1 + scale) + shift


def _take_cols(p, per_dev, per_dev_padded, lo, hi):
    parts = []
    while lo < hi:
        dev, off = divmod(lo, per_dev)
        n = min(hi - lo, per_dev - off)
        parts.append(p[:, dev * per_dev_padded + off: dev * per_dev_padded + off + n])
        lo += n
    return parts[0] if len(parts) == 1 else jnp.concatenate(parts, axis=1)


def _block_diag(blocks, nb):
    g, a, b = blocks.shape
    per = g // nb
    eye = jnp.eye(per, dtype=blocks.dtype)
    return (blocks.reshape(nb, per, a, 1, b) * eye[None, :, None, :, None]).reshape(nb, per * a, per * b)


def _s5_mixer(u, t_ctx, prm, name):
    t, gw = u.shape
    groups = gw // S5_CH
    ys = []
    for direction, reverse in enumerate((False, True)):
        lam = lax.complex(prm['s5_a_re'][direction], prm['s5_a_im'][direction])
        step = jnp.exp(prm['s5_log_step'][direction])[:, None]
        a_bar = jnp.exp(lam * step)
        b_bar = ((a_bar - 1.0) / lam)[..., None] * lax.complex(prm['s5_b_re'][direction],
                                                               prm['s5_b_im'][direction])
        c_mat = lax.complex(prm['s5_c_re'][direction], prm['s5_c_im'][direction])
        nb = (groups * S5_STATE) // _scan_lanes(groups * S5_STATE)
        b_t = jnp.swapaxes(b_bar, 1, 2)
        b_blocks = jnp.concatenate([_block_diag(jnp.real(b_t), nb), _block_diag(jnp.imag(b_t), nb)], axis=2)
        c_t = jnp.swapaxes(c_mat, 1, 2)
        c_blocks = jnp.concatenate([_block_diag(jnp.real(c_t), nb), -_block_diag(jnp.imag(c_t), nb)], axis=1)
        a_rows = jnp.stack([jnp.real(a_bar).reshape(-1), jnp.imag(a_bar).reshape(-1)])
        bu = _blocked_linear(u, b_blocks, f'{name}_bu{direction}')
        h = _s5_scan(a_rows, bu, t_ctx, reverse, f'{name}_scan{direction}')
        ys.append(_blocked_linear(h, c_blocks, f'{name}_y{direction}'))
    y = ys[0] + ys[1] + prm['s5_d'][None, :] * u
    z = jax.nn.gelu(y)
    return z * jax.nn.sigmoid(_linear(z, prm['s5_glu_w'], f'{name}_glu') + prm['s5_glu_b'])


def _chunk_gating(u, v, prm):
    t, gw = u.shape
    hd = gw // SG_HEADS
    u = jax.nn.gelu(u)
    v = jax.nn.gelu(v).reshape(t // SG_CHUNK, SG_CHUNK, SG_HEADS, hd)
    v = _layer_norm(v, prm['sg_ln_g'].reshape(SG_HEADS, hd), prm['sg_ln_b'].reshape(SG_HEADS, hd))
    s = jnp.einsum('hij,cjhd->cihd', prm['sg_w'], v) + prm['sg_b'].T[None, :, :, None]
    return u * s.reshape(t, gw)


def _pool_mixer(p, prm):
    l, gw = p.shape
    pd = gw // len(POOL_WINDOWS)
    t = np.arange(l)
    outs = []
    for g, win in enumerate(POOL_WINDOWS):
        lo = np.clip(t - win // 2, 0, l - 1)
        hi = np.clip(t + win // 2 - 1, 0, l - 1)
        pg = p[:, g * pd:(g + 1) * pd]
        padded = jnp.pad(pg, ((win // 2, win // 2), (0, 0)))
        total = padded[0:l]
        for d in range(1, win):
            total = total + padded[d:d + l]
        mean = total / jnp.asarray((hi - lo + 1).astype(np.float32))[:, None]
        outs.append(jnp.einsum('lc,cd->ld', mean - pg, prm['pool_w'][g]))
    y = jnp.concatenate(outs, axis=-1) + prm['pool_b']
    return y * prm['pool_scale']


def _dw_conv1d(x, w, b, pad):
    l = x.shape[0]
    xp = jnp.pad(x, (pad, (0, 0)))
    y = xp[0:l] * w[0]
    for k in range(1, w.shape[0]):
        y = y + xp[k:k + l] * w[k]
    return y + b


def _m2_prepare(xbc, dt_raw, prm, gw):
    heads = gw // M2_HEAD_DIM
    xbc = jax.nn.silu(_dw_conv1d(xbc, prm['m2_conv_w'], prm['m2_conv_b'], M2_PAD))
    l = xbc.shape[0]
    n_bc = M2_GROUPS * M2_STATE
    rep = heads // M2_GROUPS
    xs = xbc[:, :gw].reshape(l, heads, M2_HEAD_DIM)
    bm = jnp.repeat(xbc[:, gw:gw + n_bc].reshape(l, M2_GROUPS, M2_STATE), rep, axis=1)
    cm = jnp.repeat(xbc[:, gw + n_bc:].reshape(l, M2_GROUPS, M2_STATE), rep, axis=1)
    dt = jax.nn.softplus(dt_raw.reshape(l, 2, heads) + prm['m2_dt_bias'])
    return xs, bm, cm, dt


def _walk_cumsum(x, axis, reverse):
    return jnp.flip(jnp.cumsum(jnp.flip(x, axis), axis), axis) if reverse else jnp.cumsum(x, axis)


def _ssd_scan(xs, dt, a, bm, cm, h0, need_y, reverse):
    l, nh, hp = xs.shape
    nc = l // M2_CHUNK

    def chunks(t):
        return t.reshape((nc, M2_CHUNK) + t.shape[1:])

    def visited_before(n, strictly):
        ones = jnp.ones((n, n), bool)
        return jnp.triu(ones, 1 if strictly else 0) if reverse else jnp.tril(ones, -1 if strictly else 0)

    xd = chunks(xs * dt[..., None])
    bc, cc = chunks(bm), chunks(cm)
    a_cum = _walk_cumsum(chunks(dt * a), 1, reverse)
    a_tot = a_cum[:, 0] if reverse else a_cum[:, -1]
    decay_end = jnp.exp(a_tot[:, None] - a_cum)
    chunk_states = jnp.einsum('cqhn,cqh,cqhp->chpn', bc, decay_end, xd)

    cum = _walk_cumsum(a_tot, 0, reverse)
    before = cum - a_tot
    carry_w = jnp.exp(jnp.where(visited_before(nc, True)[:, :, None], before[:, None, :] - cum[None, :, :],
                                -jnp.inf))
    whole = cum[0] if reverse else cum[-1]
    h_final = (jnp.exp(whole)[:, None, None] * h0
               + jnp.einsum('dh,dhpn->hpn', jnp.exp(whole[None, :] - cum), chunk_states,
                            precision=lax.Precision.HIGHEST))
    if not need_y:
        return None, h_final
    h_prev = (jnp.exp(before)[:, :, None, None] * h0[None]
              + jnp.einsum('cdh,dhpn->chpn', carry_w, chunk_states, precision=lax.Precision.HIGHEST))
    seg = a_cum[:, :, None, :] - a_cum[:, None, :, :]
    decay = jnp.exp(jnp.where(visited_before(M2_CHUNK, False)[None, :, :, None], seg, -jnp.inf))
    scores = jnp.einsum('cihn,cjhn->cijh', cc, bc) * decay
    y = (jnp.einsum('cijh,cjhp->cihp', scores, xd)
         + jnp.einsum('cihn,chpn->cihp', cc, h_prev) * jnp.exp(a_cum)[..., None])
    return y.reshape(l, nh, hp), h_final


def _ssd_direction(inputs, direction, a, h0, need_y):
    xs, bm, cm, dt = inputs
    return _ssd_scan(xs, dt[:, direction], a, bm, cm, h0, need_y, direction == 1)


def _gated_rmsnorm(y, z, w):
    l, gw = z.shape
    g = (y * jax.nn.silu(z)).reshape(l, M2_GROUPS, gw // M2_GROUPS)
    g = g * lax.rsqrt(jnp.mean(jnp.square(g), -1, keepdims=True) + RMS_EPS)
    return g.reshape(l, gw) * w


def _mamba2_mixer(z, xbc, dt_raw, t_ctx, prm, need_ctx):
    gw = z.shape[1]
    heads = gw // M2_HEAD_DIM
    ctx_in = _m2_prepare(xbc[:t_ctx], dt_raw[:t_ctx], prm, gw)
    lat_in = _m2_prepare(xbc[t_ctx:], dt_raw[t_ctx:], prm, gw)
    a = -jnp.exp(prm['m2_a_log'])
    ys_ctx, ys_lat = [], []
    for direction in range(2):
        h0 = jnp.zeros((heads, M2_HEAD_DIM, M2_STATE), F32)
        y_c, h_c = _ssd_direction(ctx_in, direction, a[direction], h0, need_ctx)
        y_l, _ = _ssd_direction(lat_in, direction, a[direction], h_c, True)
        ys_lat.append(y_l)
        if need_ctx:
            ys_ctx.append(y_c)
    d_h = prm['m2_d'][None, :, None]

    def finish(ys, xs, zz):
        y = (ys[0] + ys[1] + d_h * xs).reshape(zz.shape[0], gw)
        return _gated_rmsnorm(y, zz, prm['m2_norm_w'])

    y_lat = finish(ys_lat, lat_in[0], z[t_ctx:])
    if need_ctx:
        return jnp.concatenate([finish(ys_ctx, ctx_in[0], z[:t_ctx]), y_lat], axis=0)
    return y_lat


def _forward_loss(dp, consts, dims):
    depth, t_ctx, d_model = dims['depth'], dims['t_ctx'], dims['d_model']
    gw = d_model // 4
    alpha = (2 * depth) ** 0.25
    in_sizes = (gw, gw, gw, gw, gw, gw + 2 * M2_GROUPS * M2_STATE, 2 * (gw // M2_HEAD_DIM))
    in_offs = np.concatenate([[0], np.cumsum(in_sizes)])
    ml = [[dp['mod_lat'][i, k][None, :] for k in range(6)] for i in range(depth)]
    mc = [[dp['mod_ctx'][i, k][None, :] for k in range(6)] for i in range(depth)]
    zero_row = jnp.zeros((1, d_model), F32)
    h = jnp.concatenate([consts['ctx'], dp['x']], axis=0)
    hm32 = jnp.concatenate([_modulate(consts['ctx'], mc[0][0], mc[0][1]), _modulate(dp['x'], ml[0][0], ml[0][1])],
                           axis=0)
    hm, carrier = hm32.astype(BF16), hm32 - lax.stop_gradient(hm32)
    for i in range(depth):
        need_ctx = i < depth - 1
        prm = {k: v[i] for k, v in dp['small'].items()}
        p = _linear_tap_carried(hm, carrier, consts['w_in'][i], dp['taps']['w_in'][i], f'in{i}')
        seg = [p[:, int(in_offs[s]):int(in_offs[s + 1])] for s in range(7)]
        ya = _s5_mixer(seg[0], t_ctx, prm, f's5_{i}')
        yb = _chunk_gating(seg[1], seg[2], prm)
        yc = jnp.concatenate([_pool_mixer(seg[3][:t_ctx], prm), _pool_mixer(seg[3][t_ctx:], prm)], axis=0)
        yd = _mamba2_mixer(seg[4], seg[5], seg[6], t_ctx, prm, need_ctx)
        if need_ctx:
            mix_in = jnp.concatenate([ya, yb, yc, yd], axis=1)
        else:
            mix_in = jnp.concatenate([ya[t_ctx:], yb[t_ctx:], yc[t_ctx:], yd], axis=1)
        mix = _linear_tap(mix_in, consts['w_out'][i], dp['taps']['w_out'][i], f'out{i}')
        t_c = t_ctx if need_ctx else 0
        if not need_ctx and h.shape[0] != mix.shape[0]:
            h = h[t_ctx:]
        mods = jnp.concatenate([mc[i][2], mc[i][3], mc[i][4], ml[i][2], ml[i][3], ml[i][4],
                                prm['ln1_g'][None], prm['ln1_b'][None]], axis=0)
        h, hm, carrier = _res_ln_mod(h, mix, mods, jnp.zeros(h.shape, F32), t_c, alpha, True, f'ln1_{i}')

        f_out = _ffn_block(hm, carrier, consts['ffn_w_up'][i], dp['taps']['ffn_w_up'][i], prm['ffn_conv_w'],
                           prm['ffn_conv_b'], consts['ffn_w_down'][i], dp['taps']['ffn_w_down'][i], t_c,
                           (dims['up_per_dev'], dims['up_per_dev_pad'], dims['ffn_hidden']), f'ffn{i}')
        if need_ctx:
            mods = jnp.concatenate([mc[i][5], mc[i + 1][0], mc[i + 1][1], ml[i][5], ml[i + 1][0], ml[i + 1][1],
                                    prm['ln2_g'][None], prm['ln2_b'][None]], axis=0)
            h, hm, carrier = _res_ln_mod(h, f_out, mods, jnp.zeros(h.shape, F32), t_c, alpha, True, f'ln2_{i}')
        else:
            mods = jnp.concatenate([mc[i][5], zero_row, zero_row, ml[i][5], zero_row, zero_row,
                                    prm['ln2_g'][None], prm['ln2_b'][None]], axis=0)
            h = _res_ln_mod(h, f_out, mods, jnp.zeros(h.shape, F32), t_c, alpha, False, f'ln2_{i}')[0]
    err = jnp.square(h - consts['target'])
    return 0.5 * jnp.sum(jnp.mean(err, axis=-1))


def _silu_grad(x):
    s = jax.nn.sigmoid(x)
    return s * (1 + x * (1 - s))


def _step(w, m, v, x, c, ctx, target):
    depth, d_model, ada_cols = w['w_ada'].shape
    t_ctx, t_lat = ctx.shape[1], x.shape[1]
    me = _my_index()
    in_per_dev = w['w_in'].shape[2]
    in_pad = _round_up(in_per_dev, LANES)
    in_width_pad = _round_up(N_DEV * in_per_dev, 4 * LANES)
    up_per_dev = w['ffn_w_up'].shape[2]
    up_pad = _round_up(up_per_dev, LANES)
    f_hidden = w['ffn_w_down'].shape[1] * N_DEV
    dims = dict(depth=depth, t_ctx=t_ctx, d_model=d_model, in_per_dev=in_per_dev, in_per_dev_pad=in_pad,
                up_per_dev=up_per_dev, up_per_dev_pad=up_pad, ffn_hidden=f_hidden)

    rows16 = 2 * SUBLANES
    silu_c_all = _all_gather(jnp.pad(jax.nn.silu(c), ((0, SUBLANES - 1), (0, 0))), 0, 'ag_c')
    silu_c_all = silu_c_all.reshape(N_DEV, SUBLANES, d_model)[:, 0]
    silu_cc = jax.nn.silu(w['c_ctx'])
    ada_in = jnp.concatenate([silu_c_all, silu_cc[None], jnp.zeros((rows16 - N_DEV - 1, d_model), F32)], axis=0)
    mod_loc = jnp.concatenate([_matmul(ada_in, w['w_ada'][i], 'nn', 'ada_fwd') for i in range(depth)], axis=0)
    mod_all = _all_gather(mod_loc, 1, 'ag_mod').reshape(depth, rows16, 6 * d_model)
    mod_all = mod_all + w['b_ada'][:, None, :]
    mod_lat = lax.dynamic_index_in_dim(mod_all, me, axis=1, keepdims=False).reshape(depth, 6, d_model)
    mod_ctx = mod_all[:, N_DEV].reshape(depth, 6, d_model)

    def pad_cols(a, to):
        return jnp.pad(a, ((0, 0), (0, to - a.shape[1])))

    gathered = {n: [] for n in BIG}
    for i in range(depth):
        w_in_dm = _all_gather_sequencer(pad_cols(w['w_in'][i], in_pad).astype(BF16), 1, f'ag_w_in{i}')
        gathered['w_in'].append(pad_cols(jnp.concatenate(
            [w_in_dm[:, j * in_pad:j * in_pad + in_per_dev] for j in range(N_DEV)], axis=1), in_width_pad))
        gathered['w_out'].append(_all_gather_sequencer(w['w_out'][i].astype(BF16), 0, f'ag_w_out{i}'))
        gathered['ffn_w_up'].append(
            _all_gather_sequencer(pad_cols(w['ffn_w_up'][i], up_pad).astype(BF16), 1, f'ag_w_up{i}'))
        gathered['ffn_w_down'].append(_all_gather_sequencer(w['ffn_w_down'][i].astype(BF16), 0, f'ag_w_down{i}'))
    sharded_small = [w[n] for n in COL_SHARDED_SMALL + ROW_SHARDED_SMALL]
    small_slab = _pack(sharded_small, PACK_ROWS)
    slab_rows = small_slab.shape[0]
    small_all = _all_gather(small_slab, 0, 'ag_small').reshape(N_DEV, slab_rows, LANES)
    per_dev = [_unpack(small_all[d], [a.shape for a in sharded_small]) for d in range(N_DEV)]
    small_full = {}
    for k, n in enumerate(COL_SHARDED_SMALL):
        small_full[n] = jnp.concatenate([per_dev[d][k] for d in range(N_DEV)], axis=-1)
    for k, n in enumerate(ROW_SHARDED_SMALL):
        small_full[n] = jnp.concatenate([per_dev[d][len(COL_SHARDED_SMALL) + k] for d in range(N_DEV)], axis=1)

    small = {n: w[n] for n in REPLICATED}
    small.update(small_full)
    taps = {n: [jnp.zeros(gathered[n][i].shape, F32) for i in range(depth)] for n in BIG}
    dp = dict(x=x[0], small=small, mod_lat=mod_lat, mod_ctx=mod_ctx, taps=taps)
    consts = dict(ctx=ctx[0], target=target[0], **gathered)
    loss_local, grads = jax.value_and_grad(functools.partial(_forward_loss, consts=consts, dims=dims))(dp)

    out_g, out_d, out_m, out_v = {}, {}, {}, {}

    def finish_big(name, i, rows, cols, axis, keep_cols):
        full = grads['taps'][name][i]
        if name == 'w_in':
            full = jnp.concatenate([pad_cols(full[:, j * in_per_dev:(j + 1) * in_per_dev], in_pad)
                                    for j in range(N_DEV)], axis=1)
        slots = _reduce_scatter_slots(full, rows, cols, axis, BF16, True, f'rs_{name}{i}')
        shard = [pad_cols(t[name][i], cols) for t in (w, m, v)]
        res = _sum_adamw(slots, *shard, f'adamw_{name}')
        return [r[:, :keep_cols] for r in res]

    big_specs = {
        'ffn_w_down': (w['ffn_w_down'].shape[1], d_model, 0, d_model),
        'ffn_w_up': (d_model, up_pad, 1, up_per_dev),
        'w_out': (w['w_out'].shape[1], d_model, 0, d_model),
        'w_in': (d_model, in_pad, 1, in_per_dev),
    }
    per_layer = {name: [None] * depth for name in big_specs}
    for i in reversed(range(depth)):
        for name, (rows, cols, axis, keep) in big_specs.items():
            per_layer[name][i] = finish_big(name, i, rows, cols, axis, keep)
    for name in big_specs:
        for k, dst in enumerate((out_g, out_d, out_m, out_v)):
            dst[name] = jnp.stack([per_layer[name][i][k] for i in range(depth)])

    d_lat = grads['mod_lat'].reshape(depth, 6 * d_model)
    d_ctx = grads['mod_ctx'].reshape(depth, 6 * d_model)
    d_rows = jnp.concatenate([d_lat, d_ctx, jnp.zeros((SUBLANES - 2 * depth, 6 * d_model), F32)], axis=0)
    d_all = _all_gather(d_rows, 0, 'ag_dmod').reshape(N_DEV, SUBLANES, 6 * d_model)
    d_lat_all = d_all[:, :depth]
    d_ctx_sum = d_all[0, depth:2 * depth]
    for d in range(1, N_DEV):
        d_ctx_sum = d_ctx_sum + d_all[d, depth:2 * depth]
    g_b_ada = d_ctx_sum
    for d in range(N_DEV):
        g_b_ada = g_b_ada + d_lat_all[d]
    g_w_ada, c_ctx_part = [], jnp.zeros((d_model,), F32)
    for i in range(depth):
        d_mat = jnp.concatenate([d_lat_all[:, i], d_ctx_sum[i][None],
                                 jnp.zeros((rows16 - N_DEV - 1, 6 * d_model), F32)], axis=0)
        d_mine = lax.dynamic_slice_in_dim(d_mat, me * ada_cols, ada_cols, axis=1)
        g_w_ada.append(_matmul(ada_in, d_mine, 'tn', 'ada_dw'))
        back = _matmul(d_mine, w['w_ada'][i], 'nt', 'ada_dx')
        c_ctx_part = c_ctx_part + back[N_DEV]
    c_ctx_part = c_ctx_part * _silu_grad(w['c_ctx'])
    g_w_ada = jnp.stack(g_w_ada).reshape(1, depth * d_model, ada_cols)
    res = _sum_adamw(g_w_ada, *[t['w_ada'].reshape(depth * d_model, ada_cols) for t in (w, m, v)], 'adamw_w_ada')
    for k, dst in enumerate((out_g, out_d, out_m, out_v)):
        dst['w_ada'] = res[k].reshape(depth, d_model, ada_cols)

    reduced_names = REPLICATED[:]
    reduced_names.remove('b_ada')
    reduced_names += list(COL_SHARDED_SMALL + ROW_SHARDED_SMALL)
    to_reduce = [grads['small'][n] for n in reduced_names] + [c_ctx_part, loss_local]
    slab = _pack(to_reduce, N_DEV * PACK_ROWS)
    chunk_rows = slab.shape[0] // N_DEV
    slots = _reduce_scatter_slots(slab, chunk_rows, LANES, 0, F32, False, 'rs_small')
    mine = _sum_slots(slots, 'sum_small')
    summed = _all_gather(mine, 0, 'ag_small_sum')
    parts = _unpack(summed, [a.shape for a in to_reduce])
    g_small = dict(zip(reduced_names, parts[:len(reduced_names)]))
    g_small['c_ctx'] = parts[-2]
    g_small['b_ada'] = g_b_ada
    loss = parts[-1]

    def my_shard(name, full):
        if name in COL_SHARDED_SMALL:
            n = full.shape[-1] // N_DEV
            return lax.dynamic_slice_in_dim(full, me * n, n, axis=full.ndim - 1)
        if name in ROW_SHARDED_SMALL:
            n = full.shape[1] // N_DEV
            return lax.dynamic_slice_in_dim(full, me * n, n, axis=1)
        return full

    small_names = [n for n in WEIGHTS if n not in BIG + ('w_ada',)]
    g_list = [my_shard(n, g_small[n]) for n in small_names]
    g_slab = _pack(g_list, PACK_ROWS)
    res = _sum_adamw(g_slab[None], *[_pack([t[n] for n in small_names], PACK_ROWS) for t in (w, m, v)],
                     'adamw_small')
    shapes = [w[n].shape for n in small_names]
    for k, dst in enumerate((out_g, out_d, out_m, out_v)):
        if k == 0:
            dst.update(dict(zip(small_names, g_list)))
        else:
            dst.update(dict(zip(small_names, _unpack(res[k], shapes))))

    grad_x = grads['x'][None]
    return (loss, grad_x, *[out_g[n] for n in WEIGHTS], *[out_d[n] for n in WEIGHTS],
            *[out_m[n] for n in WEIGHTS], *[out_v[n] for n in WEIGHTS])


def kernel(x, c, ctx, c_ctx, w_ada, b_ada, w_in, w_out, ln1_g, ln1_b, ln2_g, ln2_b, s5_a_re, s5_a_im, s5_b_re, s5_b_im, s5_c_re, s5_c_im, s5_log_step, s5_d, s5_glu_w, s5_glu_b, sg_ln_g, sg_ln_b, sg_w, sg_b, pool_w, pool_b, pool_scale, m2_conv_w, m2_conv_b, m2_dt_bias, m2_a_log, m2_d, m2_norm_w, ffn_w_up, ffn_conv_w, ffn_conv_b, ffn_w_down, loss_target, m_c_ctx, m_w_ada, m_b_ada, m_w_in, m_w_out, m_ln1_g, m_ln1_b, m_ln2_g, m_ln2_b, m_s5_a_re, m_s5_a_im, m_s5_b_re, m_s5_b_im, m_s5_c_re, m_s5_c_im, m_s5_log_step, m_s5_d, m_s5_glu_w, m_s5_glu_b, m_sg_ln_g, m_sg_ln_b, m_sg_w, m_sg_b, m_pool_w, m_pool_b, m_pool_scale, m_m2_conv_w, m_m2_conv_b, m_m2_dt_bias, m_m2_a_log, m_m2_d, m_m2_norm_w, m_ffn_w_up, m_ffn_conv_w, m_ffn_conv_b, m_ffn_w_down, v_c_ctx, v_w_ada, v_b_ada, v_w_in, v_w_out, v_ln1_g, v_ln1_b, v_ln2_g, v_ln2_b, v_s5_a_re, v_s5_a_im, v_s5_b_re, v_s5_b_im, v_s5_c_re, v_s5_c_im, v_s5_log_step, v_s5_d, v_s5_glu_w, v_s5_glu_b, v_sg_ln_g, v_sg_ln_b, v_sg_w, v_sg_b, v_pool_w, v_pool_b, v_pool_scale, v_m2_conv_w, v_m2_conv_b, v_m2_dt_bias, v_m2_a_log, v_m2_d, v_m2_norm_w, v_ffn_w_up, v_ffn_conv_w, v_ffn_conv_b, v_ffn_w_down):
    given = dict(locals())
    w = {n: given[n] for n in WEIGHTS}
    m = {n: given['m_' + n] for n in WEIGHTS}
    v = {n: given['v_' + n] for n in WEIGHTS}
    return _step(w, m, v, x, c, ctx, loss_target)
```

```python
import functools

import jax
import jax.numpy as jnp
import numpy as np
from jax import lax
from jax.experimental import pallas as pl
from jax.experimental.pallas import tpu as pltpu
from jax.experimental.pallas import tpu_sc as plsc

F32 = jnp.float32
BF16 = jnp.bfloat16
MESH = pl.DeviceIdType.MESH
ANY = pl.BlockSpec(memory_space=pl.ANY)

N_DEV = 8
N_CHIP = 4
LANES = 128
SUBLANES = 8
VMEM_LIMIT_BYTES = 48 * 1024 * 1024
MATMUL_VMEM_BUDGET = 36 * 1024 * 1024
PACK_ROWS = 512

GRID_W = 64
S5_CH = 16
S5_STATE = 64
SG_HEADS = 4
SG_CHUNK = 128
POOL_WINDOWS = (2, 4, 8, 16)
M2_HEAD_DIM = 64
M2_STATE = 128
M2_GROUPS = 2
M2_CONV = 4
M2_PAD = (M2_CONV // 2, M2_CONV - 1 - M2_CONV // 2)
M2_CHUNK = 128
FFN_CONV = 3
LN_EPS = 1e-5
RMS_EPS = 1e-5
ADAM_LR = 0.001
ADAM_B1 = 0.9
ADAM_B2 = 0.999
ADAM_EPS = 1e-08
ADAM_WD = 0.01
ADAM_STEP = 10

WEIGHTS = ['c_ctx', 'w_ada', 'b_ada', 'w_in', 'w_out', 'ln1_g', 'ln1_b', 'ln2_g', 'ln2_b', 's5_a_re', 's5_a_im',
           's5_b_re', 's5_b_im', 's5_c_re', 's5_c_im', 's5_log_step', 's5_d', 's5_glu_w', 's5_glu_b', 'sg_ln_g',
           'sg_ln_b', 'sg_w', 'sg_b', 'pool_w', 'pool_b', 'pool_scale', 'm2_conv_w', 'm2_conv_b', 'm2_dt_bias',
           'm2_a_log', 'm2_d', 'm2_norm_w', 'ffn_w_up', 'ffn_conv_w', 'ffn_conv_b', 'ffn_w_down']
BIG = ('w_in', 'w_out', 'ffn_w_up', 'ffn_w_down')
COL_SHARDED_SMALL = ('m2_conv_w', 'ffn_conv_w')
ROW_SHARDED_SMALL = ('s5_glu_w',)
REPLICATED = [n for n in WEIGHTS if n not in BIG + COL_SHARDED_SMALL + ROW_SHARDED_SMALL + ('w_ada', 'c_ctx')]


def _round_up(n, m):
    return (n + m - 1) // m * m


def _my_coords():
    return lax.axis_index('x'), lax.axis_index('y'), lax.axis_index('c')


def _my_index():
    x, y, c = _my_coords()
    return 4 * x + 2 * y + c


def _first_divisor(n, cands):
    for c in cands:
        if n % c == 0:
            return c
    return n


def _matmul_tiles(mode, m, n, k, a_bytes, b_bytes):
    lane_c = (1024, 512, 384, 256, 128)
    sub_c = (1088, 1024, 544, 512, 256, 128, 64, 32, 16)
    tm = _first_divisor(m, lane_c if mode == 'tn' else sub_c)
    tn = _first_divisor(n, lane_c)
    k_c = [c for c in (5632, 4096, 2048, 1408, 1088, 1024, 544, 512, 256, 128) if k % c == 0] or [k]
    if mode == 'tn':
        k_c = [c for c in k_c if c <= 1088] or [k_c[-1]]
    for tk in k_c:
        use = 2 * (tm * tk * a_bytes + tk * tn * b_bytes) + 3 * tm * tn * 4
        if use <= MATMUL_VMEM_BUDGET:
            return tm, tn, tk
    return tm, tn, k_c[-1]


def _matmul(a, b, mode, name, out_dtype=F32):
    if mode == 'nn':
        (m, k), (k2, n) = a.shape, b.shape
    elif mode == 'nt':
        (m, k), (n, k2) = a.shape, b.shape
    else:
        (k, m), (k2, n) = a.shape, b.shape
    assert k == k2, (mode, a.shape, b.shape)
    tm, tn, tk = _matmul_tiles(mode, m, n, k, a.dtype.itemsize, b.dtype.itemsize)
    nk = k // tk
    if mode == 'nn':
        a_spec = pl.BlockSpec((tm, tk), lambda i, j, kk: (i, kk))
        b_spec = pl.BlockSpec((tk, tn), lambda i, j, kk: (kk, j))
        dims = (((1,), (0,)), ((), ()))
    elif mode == 'nt':
        a_spec = pl.BlockSpec((tm, tk), lambda i, j, kk: (i, kk))
        b_spec = pl.BlockSpec((tn, tk), lambda i, j, kk: (j, kk))
        dims = (((1,), (1,)), ((), ()))
    else:
        a_spec = pl.BlockSpec((tk, tm), lambda i, j, kk: (kk, i))
        b_spec = pl.BlockSpec((tk, tn), lambda i, j, kk: (kk, j))
        dims = (((0,), (0,)), ((), ()))

    def body(a_ref, b_ref, o_ref, acc_ref):
        kk = pl.program_id(2)

        @pl.when(kk == 0)
        def _():
            acc_ref[...] = jnp.zeros_like(acc_ref)

        acc_ref[...] += lax.dot_general(a_ref[...].astype(BF16), b_ref[...].astype(BF16), dims,
                                        preferred_element_type=F32)

        @pl.when(kk == nk - 1)
        def _():
            o_ref[...] = acc_ref[...].astype(o_ref.dtype)

    return pl.pallas_call(
        body,
        name=name,
        grid=(m // tm, n // tn, nk),
        in_specs=[a_spec, b_spec],
        out_specs=pl.BlockSpec((tm, tn), lambda i, j, kk: (i, j)),
        out_shape=jax.ShapeDtypeStruct((m, n), out_dtype),
        scratch_shapes=[pltpu.VMEM((tm, tn), F32)],
        compiler_params=pltpu.CompilerParams(dimension_semantics=('parallel', 'parallel', 'arbitrary'),
                                             vmem_limit_bytes=VMEM_LIMIT_BYTES),
    )(a, b)


@functools.partial(jax.custom_vjp, nondiff_argnums=(2,))
def _linear(x, w, name):
    return _matmul(x, w, 'nn', name + '_fwd')


def _linear_fwd(x, w, name):
    return _matmul(x, w, 'nn', name + '_fwd'), (x, w)


def _linear_bwd(name, res, dy):
    x, w = res
    return _matmul(dy, w, 'nt', name + '_dx'), _matmul(x, dy, 'tn', name + '_dw')


_linear.defvjp(_linear_fwd, _linear_bwd)


@functools.partial(jax.custom_vjp, nondiff_argnums=(3,))
def _linear_tap(x, w, tap, name):
    del tap
    return _matmul(x, w, 'nn', name + '_fwd')


def _linear_tap_fwd(x, w, tap, name):
    del tap
    return _matmul(x, w, 'nn', name + '_fwd'), (x, w)


def _linear_tap_bwd(name, res, dy):
    x, w = res
    return _matmul(dy, w, 'nt', name + '_dx'), jnp.zeros_like(w), _matmul(x, dy, 'tn', name + '_dw')


_linear_tap.defvjp(_linear_tap_fwd, _linear_tap_bwd)


@functools.partial(jax.custom_vjp, nondiff_argnums=(4,))
def _linear_tap_carried(x, carrier, w, tap, name):
    del carrier, tap
    return _matmul(x, w, 'nn', name + '_fwd')


def _linear_tap_carried_fwd(x, carrier, w, tap, name):
    del carrier, tap
    return _matmul(x, w, 'nn', name + '_fwd'), (x, w)


def _linear_tap_carried_bwd(name, res, dy):
    x, w = res
    return (jnp.zeros_like(x), _matmul(dy, w, 'nt', name + '_dx'), jnp.zeros_like(w),
            _matmul(x, dy, 'tn', name + '_dw'))


_linear_tap_carried.defvjp(_linear_tap_carried_fwd, _linear_tap_carried_bwd)


def _blocked_matmul(a, b, mode, name):
    t = a.shape[0]
    nb, k, n = b.shape
    tm = _first_divisor(t, (1088, 1024, 544, 512, 256, 128, 64, 32, 16))
    cin, cout = (k, n) if mode == 'nn' else (n, k)
    dims = (((1,), (0,)), ((), ())) if mode == 'nn' else (((1,), (1,)), ((), ()))

    def body(a_ref, b_ref, o_ref):
        o_ref[...] = lax.dot_general(a_ref[...].astype(BF16), b_ref[0].astype(BF16), dims,
                                     preferred_element_type=F32)

    return pl.pallas_call(
        body, name=name, grid=(t // tm, nb),
        in_specs=[pl.BlockSpec((tm, cin), lambda i, j: (i, j)), pl.BlockSpec((1, k, n), lambda i, j: (j, 0, 0))],
        out_specs=pl.BlockSpec((tm, cout), lambda i, j: (i, j)),
        out_shape=jax.ShapeDtypeStruct((t, nb * cout), F32),
        compiler_params=pltpu.CompilerParams(vmem_limit_bytes=VMEM_LIMIT_BYTES,
                                             dimension_semantics=('parallel', 'parallel')),
    )(a, b)


def _blocked_weight_grad(x, dy, nb, name):
    t = x.shape[0]
    k, n = x.shape[1] // nb, dy.shape[1] // nb
    tk = _first_divisor(t, (1088, 1024, 544, 512, 256, 128, 64, 32, 16))
    steps = t // tk

    def body(x_ref, dy_ref, o_ref):
        @pl.when(pl.program_id(1) == 0)
        def _():
            o_ref[...] = jnp.zeros_like(o_ref)

        o_ref[0] += lax.dot_general(x_ref[...].astype(BF16), dy_ref[...].astype(BF16), (((0,), (0,)), ((), ())),
                                    preferred_element_type=F32)

    return pl.pallas_call(
        body, name=name, grid=(nb, steps),
        in_specs=[pl.BlockSpec((tk, k), lambda j, s: (s, j)), pl.BlockSpec((tk, n), lambda j, s: (s, j))],
        out_specs=pl.BlockSpec((1, k, n), lambda j, s: (j, 0, 0)),
        out_shape=jax.ShapeDtypeStruct((nb, k, n), F32),
        compiler_params=pltpu.CompilerParams(vmem_limit_bytes=VMEM_LIMIT_BYTES,
                                             dimension_semantics=('parallel', 'arbitrary')),
    )(x, dy)


@functools.partial(jax.custom_vjp, nondiff_argnums=(2,))
def _blocked_linear(x, w, name):
    return _blocked_matmul(x, w, 'nn', name + '_fwd')


def _blocked_linear_fwd(x, w, name):
    return _blocked_matmul(x, w, 'nn', name + '_fwd'), (x, w)


def _blocked_linear_bwd(name, res, dy):
    x, w = res
    return _blocked_matmul(dy, w, 'nt', name + '_dx'), _blocked_weight_grad(x, dy, w.shape[0], name + '_dw')


_blocked_linear.defvjp(_blocked_linear_fwd, _blocked_linear_bwd)


SCAN_ROWS = (256, 128)
SCAN_LANES = 512


def _scan_lanes(n):
    return SCAN_LANES if n % SCAN_LANES == 0 else n


def _scan_call(a, x, h, t_ctx, ctx_first, reverse_rows, name):
    t, n2 = x.shape
    n = n2 // 2
    rows = next(r for r in SCAN_ROWS if t_ctx % r == 0 and t % r == 0)
    n_ctx_blocks = t_ctx // rows
    nblk = t // rows
    nlat = nblk - n_ctx_blocks
    lanes = _scan_lanes(n)
    with_da = h is not None

    def block_of(i):
        if ctx_first:
            first_n, first_0, second_0, second_n = n_ctx_blocks, 0, n_ctx_blocks, nlat
        else:
            first_n, first_0, second_0, second_n = nlat, n_ctx_blocks, 0, n_ctx_blocks
        if reverse_rows:
            in_first = first_0 + first_n - 1 - i
            in_second = second_0 + second_n - 1 - (i - first_n)
        else:
            in_first = first_0 + i
            in_second = second_0 + (i - first_n)
        return jnp.where(i < first_n, in_first, in_second)

    groups = rows // SUBLANES
    first_row = SUBLANES - 1 if reverse_rows else 0
    to_previous = SUBLANES - 1 if reverse_rows else 1

    def body(*refs):
        if with_da:
            a_ref, x_ref, h_ref, o_ref, da_ref, st_ref = refs
        else:
            a_ref, x_ref, o_ref, st_ref = refs

        @pl.when(pl.program_id(1) == 0)
        def _():
            st_ref[...] = jnp.zeros_like(st_ref)
            if with_da:
                da_ref[...] = jnp.zeros_like(da_ref)

        row_id = lax.broadcasted_iota(jnp.int32, (SUBLANES, lanes), 0)
        behind = (SUBLANES - 1 - row_id) if reverse_rows else row_id

        def cmul(pr, pi, qr, qi):
            return pr * qr - pi * qi, pr * qi + pi * qr

        a1 = (jnp.broadcast_to(a_ref[0:1, :], (SUBLANES, lanes)), jnp.broadcast_to(a_ref[1:2, :], (SUBLANES, lanes)))
        a2 = cmul(*a1, *a1)
        a4 = cmul(*a2, *a2)
        pw = a1
        for bit, ak in ((1, a1), (2, a2), (4, a4)):
            nxt = cmul(*pw, *ak)
            pw = (jnp.where((behind & bit) != 0, nxt[0], pw[0]), jnp.where((behind & bit) != 0, nxt[1], pw[1]))

        def group(g, carry):
            in_r, in_i = carry
            gi = (groups - 1 - g) if reverse_rows else g
            start = pl.multiple_of(gi * SUBLANES, SUBLANES)
            xr = x_ref[pl.ds(start, SUBLANES), pl.ds(0, lanes)]
            xi = x_ref[pl.ds(start, SUBLANES), pl.ds(lanes, lanes)]
            for k, ak in ((1, a1), (2, a2), (4, a4)):
                shift = (SUBLANES - k) if reverse_rows else k
                pr = jnp.where(behind >= k, pltpu.roll(xr, shift, axis=0), 0.0)
                pi = jnp.where(behind >= k, pltpu.roll(xi, shift, axis=0), 0.0)
                qr, qi = cmul(*ak, pr, pi)
                xr, xi = xr + qr, xi + qi
            cr, ci = cmul(*pw, in_r, in_i)
            out_r, out_i = xr + cr, xi + ci
            last = 0 if reverse_rows else SUBLANES - 1
            sr = jnp.broadcast_to(out_r[last:last + 1, :], (SUBLANES, lanes))
            si = jnp.broadcast_to(out_i[last:last + 1, :], (SUBLANES, lanes))
            o_ref[pl.ds(start, SUBLANES), pl.ds(0, lanes)] = out_r
            o_ref[pl.ds(start, SUBLANES), pl.ds(lanes, lanes)] = out_i
            if with_da:
                pr = jnp.where(row_id == first_row, in_r, pltpu.roll(out_r, to_previous, axis=0))
                pi = jnp.where(row_id == first_row, in_i, pltpu.roll(out_i, to_previous, axis=0))
                hr = h_ref[pl.ds(start, SUBLANES), pl.ds(0, lanes)]
                hi = h_ref[pl.ds(start, SUBLANES), pl.ds(lanes, lanes)]
                da_ref[0:SUBLANES, :] += hr * pr + hi * pi
                da_ref[SUBLANES:2 * SUBLANES, :] += hr * pi - hi * pr
            return sr, si

        sr, si = lax.fori_loop(0, groups, group, (st_ref[0], st_ref[1]))
        st_ref[0] = sr
        st_ref[1] = si

    row_spec = pl.BlockSpec((rows, 2 * lanes), lambda j, i: (block_of(i), j))
    in_specs = [pl.BlockSpec((2, lanes), lambda j, i: (0, j)), row_spec]
    out_specs = [row_spec]
    out_shape = [jax.ShapeDtypeStruct((t, n2), F32)]
    operands = [a, x]
    if with_da:
        in_specs.append(row_spec)
        operands.append(h)
        out_specs.append(pl.BlockSpec((2 * SUBLANES, lanes), lambda j, i: (0, j)))
        out_shape.append(jax.ShapeDtypeStruct((2 * SUBLANES, n), F32))
    return pl.pallas_call(
        body,
        name=name,
        grid=(n // lanes, nblk),
        in_specs=in_specs,
        out_specs=out_specs,
        out_shape=out_shape,
        scratch_shapes=[pltpu.VMEM((2, SUBLANES, lanes), F32)],
        compiler_params=pltpu.CompilerParams(dimension_semantics=('parallel', 'arbitrary')),
    )(*operands)


@functools.partial(jax.custom_vjp, nondiff_argnums=(2, 3, 4))
def _s5_scan(a, bu, t_ctx, reverse, name):
    return _scan_call(a, bu, None, t_ctx, True, reverse, name + '_fwd')[0]


def _s5_scan_fwd(a, bu, t_ctx, reverse, name):
    h = _scan_call(a, bu, None, t_ctx, True, reverse, name + '_fwd')[0]
    return h, (a, h)


def _s5_scan_bwd(t_ctx, reverse, name, res, dh):
    a, h = res
    a_conj = a * jnp.array([[1.0], [-1.0]], F32)
    g, da = _scan_call(a_conj, dh, h, t_ctx, False, not reverse, name + '_bwd')
    da = jnp.stack([jnp.sum(da[:SUBLANES], axis=0), jnp.sum(da[SUBLANES:], axis=0)])
    return da, g


_s5_scan.defvjp(_s5_scan_fwd, _s5_scan_bwd)


FFN_LANES = 128
CONV_TAPS = FFN_CONV * FFN_CONV
CONV_W_ROWS = 16
GELU_K = 0.7978845608028654
GELU_C = 0.044715


def _gelu_and_slope(x):
    x2 = x * x
    th = jnp.tanh(GELU_K * (x + GELU_C * x * x2))
    cdf = 0.5 * (1.0 + th)
    slope = cdf + 0.5 * x * (1.0 - th * th) * (GELU_K * (1.0 + 3.0 * GELU_C * x2))
    return x * cdf, slope


def _band_rows(t_lat):
    return _first_divisor(t_lat // GRID_W, (8, 4, 2, 1)) * GRID_W


def _shifted_ctx(x):
    n = x.shape[0]
    row = lax.broadcasted_iota(jnp.int32, x.shape, 0)
    left = jnp.where(row == 0, 0.0, pltpu.roll(x, 1, axis=0))
    right = jnp.where(row == n - 1, 0.0, pltpu.roll(x, n - 1, axis=0))
    return left, x, right


def _shifted_band(src_ref, bidx, n_bands, band, t_ctx, t):
    start = pl.multiple_of(t_ctx + bidx * band, GRID_W)
    top_start = pl.multiple_of(jnp.maximum(start - GRID_W, 0), SUBLANES)
    bot_start = pl.multiple_of(jnp.minimum(start + band, t - GRID_W), SUBLANES)
    top = jnp.where(bidx > 0, src_ref[pl.ds(top_start, GRID_W), :], 0.0)
    bot = jnp.where(bidx < n_bands - 1, src_ref[pl.ds(bot_start, GRID_W), :], 0.0)
    ext = jnp.concatenate([top, src_ref[pl.ds(start, band), :], bot], axis=0)
    n_ext = band + 2 * GRID_W
    col = lax.broadcasted_iota(jnp.int32, ext.shape, 0) & (GRID_W - 1)
    left = jnp.where(col == 0, 0.0, pltpu.roll(ext, 1, axis=0))
    right = jnp.where(col == GRID_W - 1, 0.0, pltpu.roll(ext, n_ext - 1, axis=0))
    return left, ext, right


def _conv_sum(shifted, w_rows, band, flip):
    acc = None
    for i in range(FFN_CONV):
        for j in range(FFN_CONV):
            w = w_rows[(FFN_CONV - 1 - i) * FFN_CONV + (FFN_CONV - 1 - j)] if flip else w_rows[i * FFN_CONV + j]
            term = shifted[j][i * GRID_W:i * GRID_W + band] * w
            acc = term if acc is None else acc + term
    return acc


def _conv_operand(conv_w, conv_b):
    f = conv_b.shape[0]
    return jnp.concatenate([conv_w.reshape(CONV_TAPS, f), conv_b[None],
                            jnp.zeros((CONV_W_ROWS - CONV_TAPS - 1, f), F32)], axis=0)


def _ffn_mid_fwd(gv, w16, t_ctx, name):
    t, f2 = gv.shape
    f = f2 // 2
    lanes = FFN_LANES
    nf = f // lanes
    band = _band_rows(t - t_ctx)
    n_bands = (t - t_ctx) // band
    mid = (FFN_CONV // 2) * FFN_CONV

    def body(g_ref, v_ref, w_ref, o_ref):
        w_rows = [w_ref[k:k + 1, :] for k in range(CONV_TAPS)]
        bias = w_ref[CONV_TAPS:CONV_TAPS + 1, :]
        if t_ctx:
            sh = _shifted_ctx(g_ref[0:t_ctx, :])
            pre = sh[0] * w_rows[mid] + sh[1] * w_rows[mid + 1] + sh[2] * w_rows[mid + 2] + bias
            o_ref[0:t_ctx, :] = (_gelu_and_slope(pre)[0] * v_ref[0:t_ctx, :]).astype(o_ref.dtype)

        def one_band(b, carry):
            start = pl.multiple_of(t_ctx + b * band, GRID_W)
            pre = _conv_sum(_shifted_band(g_ref, b, n_bands, band, t_ctx, t), w_rows, band, False) + bias
            o_ref[pl.ds(start, band), :] = (_gelu_and_slope(pre)[0] * v_ref[pl.ds(start, band), :]).astype(o_ref.dtype)
            return carry

        lax.fori_loop(0, n_bands, one_band, 0)

    return pl.pallas_call(
        body, name=name,
        grid=(nf,),
        in_specs=[pl.BlockSpec((t, lanes), lambda j: (0, j)),
                  pl.BlockSpec((t, lanes), lambda j: (0, nf + j)),
                  pl.BlockSpec((CONV_W_ROWS, lanes), lambda j: (0, j))],
        out_specs=pl.BlockSpec((t, lanes), lambda j: (0, j)),
        out_shape=jax.ShapeDtypeStruct((t, f), BF16),
        compiler_params=pltpu.CompilerParams(dimension_semantics=('parallel',), vmem_limit_bytes=VMEM_LIMIT_BYTES),
    )(gv, gv, w16)


def _ffn_mid_bwd(d_act, gv, w16, t_ctx, name):
    t, f2 = gv.shape
    f = f2 // 2
    lanes = FFN_LANES
    nf = f // lanes
    band = _band_rows(t - t_ctx)
    n_bands = (t - t_ctx) // band
    mid = (FFN_CONV // 2) * FFN_CONV
    n_acc = CONV_TAPS + 1

    def tile_sum(x):
        return jnp.sum(x.reshape(x.shape[0] // SUBLANES, SUBLANES, lanes), axis=0)

    def body(da_ref, g_ref, v_ref, w_ref, dg_ref, dv_ref, dw_ref, dp_ref):
        w_rows = [w_ref[k:k + 1, :] for k in range(CONV_TAPS)]
        bias = w_ref[CONV_TAPS:CONV_TAPS + 1, :]
        acc = [jnp.zeros((SUBLANES, lanes), F32) for _ in range(n_acc)]
        if t_ctx:
            sh = _shifted_ctx(g_ref[0:t_ctx, :])
            pre = sh[0] * w_rows[mid] + sh[1] * w_rows[mid + 1] + sh[2] * w_rows[mid + 2] + bias
            ge, slope = _gelu_and_slope(pre)
            da = da_ref[0:t_ctx, :]
            dv_ref[0:t_ctx, :] = (da * ge).astype(dv_ref.dtype)
            dpre = da * v_ref[0:t_ctx, :] * slope
            dp_ref[0:t_ctx, :] = dpre
            for j in range(FFN_CONV):
                acc[mid + j] = acc[mid + j] + tile_sum(sh[j] * dpre)
            acc[CONV_TAPS] = acc[CONV_TAPS] + tile_sum(dpre)
            back = _shifted_ctx(dpre)
            dg_ref[0:t_ctx, :] = (back[2] * w_rows[mid] + back[1] * w_rows[mid + 1]
                                  + back[0] * w_rows[mid + 2]).astype(dg_ref.dtype)

        def first_pass(b, acc):
            acc = list(acc)
            start = pl.multiple_of(t_ctx + b * band, GRID_W)
            sh = _shifted_band(g_ref, b, n_bands, band, t_ctx, t)
            ge, slope = _gelu_and_slope(_conv_sum(sh, w_rows, band, False) + bias)
            da = da_ref[pl.ds(start, band), :]
            dv_ref[pl.ds(start, band), :] = (da * ge).astype(dv_ref.dtype)
            dpre = da * v_ref[pl.ds(start, band), :] * slope
            dp_ref[pl.ds(start, band), :] = dpre
            for i in range(FFN_CONV):
                for j in range(FFN_CONV):
                    k = i * FFN_CONV + j
                    acc[k] = acc[k] + tile_sum(sh[j][i * GRID_W:i * GRID_W + band] * dpre)
            acc[CONV_TAPS] = acc[CONV_TAPS] + tile_sum(dpre)
            return tuple(acc)

        acc = lax.fori_loop(0, n_bands, first_pass, tuple(acc))
        for k in range(n_acc):
            dw_ref[k * SUBLANES:(k + 1) * SUBLANES, :] = acc[k]

        def second_pass(b, carry):
            start = pl.multiple_of(t_ctx + b * band, GRID_W)
            dg_ref[pl.ds(start, band), :] = _conv_sum(_shifted_band(dp_ref, b, n_bands, band, t_ctx, t),
                                                      w_rows, band, True).astype(dg_ref.dtype)
            return carry

        lax.fori_loop(0, n_bands, second_pass, 0)

    col = pl.BlockSpec((t, lanes), lambda j: (0, j))
    return pl.pallas_call(
        body, name=name,
        grid=(nf,),
        in_specs=[col, col, pl.BlockSpec((t, lanes), lambda j: (0, nf + j)),
                  pl.BlockSpec((CONV_W_ROWS, lanes), lambda j: (0, j))],
        out_specs=[col, col, pl.BlockSpec((n_acc * SUBLANES, lanes), lambda j: (0, j))],
        out_shape=[jax.ShapeDtypeStruct((t, f), BF16), jax.ShapeDtypeStruct((t, f), BF16),
                   jax.ShapeDtypeStruct((n_acc * SUBLANES, f), F32)],
        scratch_shapes=[pltpu.VMEM((t, lanes), F32)],
        compiler_params=pltpu.CompilerParams(dimension_semantics=('parallel',), vmem_limit_bytes=VMEM_LIMIT_BYTES),
    )(d_act, gv, gv, w16)


def _gate_value(up, layout):
    per_dev, per_dev_pad, f = layout
    if per_dev == per_dev_pad:
        return up
    return jnp.concatenate([_take_cols(up, per_dev, per_dev_pad, 0, f), _take_cols(up, per_dev, per_dev_pad, f, 2 * f)],
                           axis=1)


def _ffn_forward(hm, w_up, conv_w, conv_b, w_down, t_ctx, layout, name):
    up = _matmul(hm, w_up, 'nn', name + '_up_fwd')
    gv = _gate_value(up, layout)
    w16 = _conv_operand(conv_w, conv_b)
    act = _ffn_mid_fwd(gv, w16, t_ctx, name + '_mid_fwd')
    return _matmul(act, w_down, 'nn', name + '_down_fwd'), (hm, w_up, gv, w16, w_down, act)


@functools.partial(jax.custom_vjp, nondiff_argnums=(8, 9, 10))
def _ffn_block(hm, carrier, w_up, tap_up, conv_w, conv_b, w_down, tap_down, t_ctx, layout, name):
    del carrier, tap_up, tap_down
    return _ffn_forward(hm, w_up, conv_w, conv_b, w_down, t_ctx, layout, name)[0]


def _ffn_block_fwd(hm, carrier, w_up, tap_up, conv_w, conv_b, w_down, tap_down, t_ctx, layout, name):
    del carrier, tap_up, tap_down
    return _ffn_forward(hm, w_up, conv_w, conv_b, w_down, t_ctx, layout, name)


def _ffn_block_bwd(t_ctx, layout, name, res, d_out):
    hm, w_up, gv, w16, w_down, act = res
    f = act.shape[1]
    d_act = _matmul(d_out, w_down, 'nt', name + '_down_dx')
    d_w_down = _matmul(act, d_out, 'tn', name + '_down_dw')
    d_gate, d_val, d_w = _ffn_mid_bwd(d_act, gv, w16, t_ctx, name + '_mid_bwd')
    d_gv = jnp.concatenate([d_gate, d_val], axis=1)
    if layout[0] == layout[1]:
        d_up = d_gv
    else:
        d_up = jax.vjp(lambda u: _gate_value(u, layout), jnp.zeros((gv.shape[0], w_up.shape[1]), d_gv.dtype))[1](d_gv)[0]
    d_hm = _matmul(d_up, w_up, 'nt', name + '_up_dx')
    d_w_up = _matmul(hm, d_up, 'tn', name + '_up_dw')
    d_w = jnp.sum(d_w.reshape(CONV_TAPS + 1, SUBLANES, f), axis=1)
    return (jnp.zeros_like(hm), d_hm, jnp.zeros_like(w_up), d_w_up, d_w[:CONV_TAPS].reshape(FFN_CONV, FFN_CONV, f),
            d_w[CONV_TAPS], jnp.zeros_like(w_down), d_w_down)


_ffn_block.defvjp(_ffn_block_fwd, _ffn_block_bwd)


LN_ROWS = 128
MOD_ROWS = 8


def _normalised(h, y, gate, alpha):
    r = alpha * h + gate * y
    xc = r - jnp.mean(r, axis=-1, keepdims=True)
    rstd = lax.rsqrt(jnp.mean(xc * xc, axis=-1, keepdims=True) + LN_EPS)
    return xc * rstd, rstd


def _mod_row(m_ref, k, is_ctx):
    lat = m_ref[3 + k:4 + k, :]
    return lat if is_ctx is None else jnp.where(is_ctx, m_ref[k:k + 1, :], lat)


def _res_ln_fwd_call(h, y, mods, t_ctx, alpha, with_mod, name):
    t, d = h.shape
    n_ctx_tiles = t_ctx // LN_ROWS

    def body(h_ref, y_ref, m_ref, hn_ref, *rest):
        is_ctx = (pl.program_id(0) < n_ctx_tiles) if n_ctx_tiles else None
        xhat, _ = _normalised(h_ref[...], y_ref[...], _mod_row(m_ref, 0, is_ctx), alpha)
        hn = xhat * m_ref[6:7, :] + m_ref[7:8, :]
        hn_ref[...] = hn
        if with_mod:
            hm = hn * (1.0 + _mod_row(m_ref, 2, is_ctx)) + _mod_row(m_ref, 1, is_ctx)
            rest[0][...] = hm.astype(rest[0].dtype)

    blk = pl.BlockSpec((LN_ROWS, d), lambda i: (i, 0))
    out_shape = [jax.ShapeDtypeStruct((t, d), F32)] + ([jax.ShapeDtypeStruct((t, d), BF16)] if with_mod else [])
    return pl.pallas_call(
        body, name=name,
        grid=(t // LN_ROWS,),
        in_specs=[blk, blk, pl.BlockSpec((MOD_ROWS, d), lambda i: (0, 0))],
        out_specs=[blk] * len(out_shape),
        out_shape=out_shape,
        compiler_params=pltpu.CompilerParams(dimension_semantics=('parallel',)),
    )(h, y, mods)


def _res_ln_bwd_call(h, y, mods, d_hn, d_hm, t_ctx, alpha, name):
    t, d = h.shape
    n_ctx_tiles = t_ctx // LN_ROWS
    with_mod = d_hm is not None

    def tile_sum(x):
        return jnp.sum(x.reshape(LN_ROWS // SUBLANES, SUBLANES, d), axis=0)

    def body(*refs):
        if with_mod:
            h_ref, y_ref, m_ref, dhn_ref, dhm_ref, dh_ref, dy_ref, acc_ref = refs
        else:
            h_ref, y_ref, m_ref, dhn_ref, dh_ref, dy_ref, acc_ref = refs

        @pl.when(pl.program_id(0) == 0)
        def _():
            acc_ref[...] = jnp.zeros_like(acc_ref)

        is_ctx = (pl.program_id(0) < n_ctx_tiles) if n_ctx_tiles else None
        gate = _mod_row(m_ref, 0, is_ctx)
        y = y_ref[...]
        xhat, rstd = _normalised(h_ref[...], y, gate, alpha)
        ln_g = m_ref[6:7, :]
        dhn = dhn_ref[...]
        base = 3 * SUBLANES if is_ctx is None else jnp.where(is_ctx, 0, 3 * SUBLANES)

        def add_to(row, part):
            if isinstance(row, int):
                acc_ref[row:row + SUBLANES, :] += part
            else:
                acc_ref[pl.ds(pl.multiple_of(row, SUBLANES), SUBLANES), :] += part

        if with_mod:
            dhm = dhm_ref[...]
            hn = xhat * ln_g + m_ref[7:8, :]
            add_to(base + SUBLANES, tile_sum(dhm))
            add_to(base + 2 * SUBLANES, tile_sum(dhm * hn))
            dhn = dhn + dhm * (1.0 + _mod_row(m_ref, 2, is_ctx))
        add_to(6 * SUBLANES, tile_sum(dhn * xhat))
        add_to(7 * SUBLANES, tile_sum(dhn))
        dx = dhn * ln_g
        dr = rstd * (dx - jnp.mean(dx, axis=-1, keepdims=True) - xhat * jnp.mean(dx * xhat, axis=-1, keepdims=True))
        dh_ref[...] = alpha * dr
        dy_ref[...] = gate * dr
        add_to(base, tile_sum(dr * y))

    blk = pl.BlockSpec((LN_ROWS, d), lambda i: (i, 0))
    operands = [h, y, mods, d_hn] + ([d_hm] if with_mod else [])
    return pl.pallas_call(
        body, name=name,
        grid=(t // LN_ROWS,),
        in_specs=[blk, blk, pl.BlockSpec((MOD_ROWS, d), lambda i: (0, 0)), blk] + ([blk] if with_mod else []),
        out_specs=[blk, blk, pl.BlockSpec((MOD_ROWS * SUBLANES, d), lambda i: (0, 0))],
        out_shape=[jax.ShapeDtypeStruct((t, d), F32), jax.ShapeDtypeStruct((t, d), F32),
                   jax.ShapeDtypeStruct((MOD_ROWS * SUBLANES, d), F32)],
        compiler_params=pltpu.CompilerParams(dimension_semantics=('arbitrary',)),
    )(*operands)


@functools.partial(jax.custom_vjp, nondiff_argnums=(4, 5, 6, 7))
def _res_ln_mod(h, y, mods, carrier, t_ctx, alpha, with_mod, name):
    out = tuple(_res_ln_fwd_call(h, y, mods, t_ctx, alpha, with_mod, name + '_fwd'))
    return out + (carrier,) if with_mod else out


def _res_ln_mod_fwd(h, y, mods, carrier, t_ctx, alpha, with_mod, name):
    out = tuple(_res_ln_fwd_call(h, y, mods, t_ctx, alpha, with_mod, name + '_fwd'))
    return (out + (carrier,) if with_mod else out), (h, y, mods, carrier)


def _res_ln_mod_bwd(t_ctx, alpha, with_mod, name, res, cts):
    h, y, mods, carrier = res
    d_h, d_y, acc = _res_ln_bwd_call(h, y, mods, cts[0], cts[2] if with_mod else None, t_ctx, alpha, name + '_bwd')
    return d_h, d_y, jnp.sum(acc.reshape(MOD_ROWS, SUBLANES, h.shape[1]), axis=1), jnp.zeros_like(carrier)


_res_ln_mod.defvjp(_res_ln_mod_fwd, _res_ln_mod_bwd)


def _window(ref, j, rows, cols, axis):
    if axis == 0:
        return ref.at[pl.ds(j * rows, rows), :]
    return ref.at[:, pl.ds(j * cols, cols)]


def _all_gather(block, axis, name):
    rows, cols = block.shape

    def body(x_ref, out_ref, send_sems, recv_sems, local_sem):
        x, y, c = _my_coords()
        me, sibling = (x, y, c), (x, y, 1 - c)
        chips = [(1 - x, y), (x, 1 - y), (1 - x, 1 - y)]

        def win(px, py, pc):
            return _window(out_ref, 4 * px + 2 * py + pc, rows, cols, axis)

        def copy(k, blk, to, src=None):
            return pltpu.make_async_remote_copy(
                src_ref=win(*blk) if src is None else src, dst_ref=win(*blk),
                send_sem=send_sems.at[k], recv_sem=recv_sems.at[k], device_id=to, device_id_type=MESH)

        mine = pltpu.make_async_copy(x_ref, win(*me), local_sem)
        mine.start()
        first = [copy(0, me, sibling, src=x_ref)]
        first += [copy(1 + j, me, (*chip, c), src=x_ref) for j, chip in enumerate(chips)]
        for cp in first:
            cp.start()
        passed = [copy(4 + j, (*chip, c), sibling) for j, chip in enumerate(chips)]
        for j, chip in enumerate(chips):
            copy(1 + j, (*chip, c), me).wait_recv()
            passed[j].start()
        copy(0, sibling, me).wait_recv()
        for j, chip in enumerate(chips):
            copy(4 + j, (*chip, 1 - c), me).wait_recv()
        for cp in first + passed:
            cp.wait_send()
        mine.wait()

    out_shape = (N_DEV * rows, cols) if axis == 0 else (rows, N_DEV * cols)
    return pl.pallas_call(
        body, name=name,
        out_shape=jax.ShapeDtypeStruct(out_shape, block.dtype),
        in_specs=[ANY], out_specs=ANY,
        scratch_shapes=[pltpu.SemaphoreType.DMA((7,)), pltpu.SemaphoreType.DMA((7,)), pltpu.SemaphoreType.DMA],
    )(block)


AG_COLLECTIVE_ID = 1


def _all_gather_sequencer(block, axis, name):
    rows, cols = block.shape
    out_shape = (N_DEV * rows, cols) if axis == 0 else (rows, N_DEV * cols)
    x_ref = jax.new_ref(block, memory_space=pltpu.MemorySpace.HBM)
    out_ref = jax.empty_ref(jax.ShapeDtypeStruct(out_shape, block.dtype), memory_space=pltpu.MemorySpace.HBM)

    @pl.kernel(mesh=plsc.ScalarSubcoreMesh(axis_name='sequencer', num_cores=1), name=name,
               scratch_types=(pltpu.SemaphoreType.DMA((7,)), pltpu.SemaphoreType.DMA((7,)), pltpu.SemaphoreType.DMA),
               compiler_params=pltpu.CompilerParams(collective_id=AG_COLLECTIVE_ID))
    def launch(send_sems, recv_sems, local_sem):
        x, y, c = _my_coords()
        me, sibling = (x, y, c), (x, y, 1 - c)
        chips = [(1 - x, y), (x, 1 - y), (1 - x, 1 - y)]
        barrier = pltpu.get_barrier_semaphore()
        for peer in [sibling] + [(*chip, c) for chip in chips]:
            pl.semaphore_signal(barrier, inc=1, device_id=peer, device_id_type=MESH)
        pl.semaphore_wait(barrier, 1 + len(chips))

        def win(px, py, pc):
            return _window(out_ref, 4 * px + 2 * py + pc, rows, cols, axis)

        def copy(k, blk, to, src=None):
            return pltpu.make_async_remote_copy(
                src_ref=win(*blk) if src is None else src, dst_ref=win(*blk),
                send_sem=send_sems.at[k], recv_sem=recv_sems.at[k], device_id=to, device_id_type=MESH)

        mine = pltpu.make_async_copy(x_ref, win(*me), local_sem)
        mine.start()
        first = [copy(0, me, sibling, src=x_ref)]
        first += [copy(1 + j, me, (*chip, c), src=x_ref) for j, chip in enumerate(chips)]
        for cp in first:
            cp.start()
        passed = [copy(4 + j, (*chip, c), sibling) for j, chip in enumerate(chips)]
        for j, chip in enumerate(chips):
            copy(1 + j, (*chip, c), me).wait_recv()
            passed[j].start()
        copy(0, sibling, me).wait_recv()
        for j, chip in enumerate(chips):
            copy(4 + j, (*chip, 1 - c), me).wait_recv()
        for cp in first + passed:
            cp.wait_send()
        mine.wait()

    launch()
    return out_ref[...]


PAIR_COLLECTIVE_ID = 2
CHIPS_COLLECTIVE_ID = 3


def _handshake(peers):
    barrier = pltpu.get_barrier_semaphore()
    for peer in peers:
        pl.semaphore_signal(barrier, inc=1, device_id=peer, device_id_type=MESH)
    pl.semaphore_wait(barrier, len(peers))


def _launch_on_sequencer(body, operand, result, scratch_types, collective_id, name):
    operand_ref = jax.new_ref(operand, memory_space=pltpu.MemorySpace.HBM)
    result_ref = jax.empty_ref(result, memory_space=pltpu.MemorySpace.HBM)
    pl.kernel(functools.partial(body, operand_ref, result_ref),
              mesh=plsc.ScalarSubcoreMesh(axis_name='sequencer', num_cores=1), name=name,
              scratch_types=scratch_types, compiler_params=pltpu.CompilerParams(collective_id=collective_id))()
    return result_ref[...]


def _exchange_pair(full, rows, cols, axis, sequencer, name):
    def body(g_ref, land_ref, send_sems, recv_sems):
        x, y, c = _my_coords()
        if sequencer:
            _handshake([(x, y, 1 - c)])
        copies = []
        for k in range(N_CHIP):
            j = 2 * k + (1 - c)
            copies.append(pltpu.make_async_remote_copy(
                src_ref=_window(g_ref, j, rows, cols, axis), dst_ref=land_ref.at[k],
                send_sem=send_sems.at[k], recv_sem=recv_sems.at[k], device_id=(x, y, 1 - c), device_id_type=MESH))
        for cp in copies:
            cp.start()
        for cp in copies:
            cp.wait()

    result = jax.ShapeDtypeStruct((N_CHIP, rows, cols), full.dtype)
    sems = [pltpu.SemaphoreType.DMA((N_CHIP,)), pltpu.SemaphoreType.DMA((N_CHIP,))]
    if sequencer:
        return _launch_on_sequencer(body, full, result, sems, PAIR_COLLECTIVE_ID, name)
    return pl.pallas_call(body, name=name, out_shape=result, in_specs=[ANY], out_specs=ANY, scratch_shapes=sems)(full)


ELEMENTWISE_BLOCK = 256 * 1024
ELEMENTWISE_MAX_COLS = 2048


def _elementwise_tiles(rows, cols):
    tc = cols if cols <= ELEMENTWISE_MAX_COLS else _first_divisor(cols, (2048, 1024, 512, 384, 256, 128))
    tr = next((r for r in (512, 256, 128, 64, 32, 16, 8) if rows % r == 0 and r * tc <= ELEMENTWISE_BLOCK), rows)
    return tr, tc


def _pair_add(full, land, rows, cols, axis, wire_dtype, name):
    tr, tc = _elementwise_tiles(rows, cols)
    c_arr = jnp.reshape(lax.axis_index('c'), (1,)).astype(jnp.int32)

    def full_map(k, i, j, c_ref):
        blk = 2 * k + c_ref[0]
        if axis == 0:
            return (blk * (rows // tr) + i, j)
        return (i, blk * (cols // tc) + j)

    def body(c_ref, g_ref, l_ref, o_ref):
        del c_ref
        o_ref[0] = (g_ref[...] + l_ref[0]).astype(o_ref.dtype)

    return pl.pallas_call(
        body, name=name,
        grid_spec=pltpu.PrefetchScalarGridSpec(
            num_scalar_prefetch=1,
            grid=(N_CHIP, rows // tr, cols // tc),
            in_specs=[pl.BlockSpec((tr, tc), full_map),
                      pl.BlockSpec((1, tr, tc), lambda k, i, j, c_ref: (k, i, j))],
            out_specs=pl.BlockSpec((1, tr, tc), lambda k, i, j, c_ref: (k, i, j)),
        ),
        out_shape=jax.ShapeDtypeStruct((N_CHIP, rows, cols), wire_dtype),
        compiler_params=pltpu.CompilerParams(dimension_semantics=('parallel', 'parallel', 'parallel'),
                                             vmem_limit_bytes=VMEM_LIMIT_BYTES),
    )(c_arr, full, land)


def _exchange_chips(sums, sequencer, name):
    _, rows, cols = sums.shape

    def body(s_ref, land_ref, send_sems, recv_sems, local_sem):
        x, y, c = _my_coords()
        my_chip = 2 * x + y
        chips = [(1 - x, y), (x, 1 - y), (1 - x, 1 - y)]
        if sequencer:
            _handshake([(px, py, c) for px, py in chips])
        mine = pltpu.make_async_copy(s_ref.at[my_chip], land_ref.at[my_chip], local_sem)
        mine.start()
        sends = []
        for t, (px, py) in enumerate(chips):
            sends.append(pltpu.make_async_remote_copy(
                src_ref=s_ref.at[2 * px + py], dst_ref=land_ref.at[my_chip],
                send_sem=send_sems.at[t], recv_sem=recv_sems.at[t], device_id=(px, py, c), device_id_type=MESH))
        for cp in sends:
            cp.start()
        for t, (px, py) in enumerate(chips):
            pltpu.make_async_remote_copy(
                src_ref=s_ref.at[my_chip], dst_ref=land_ref.at[2 * px + py],
                send_sem=send_sems.at[t], recv_sem=recv_sems.at[t], device_id=(px, py, c),
                device_id_type=MESH).wait_recv()
        for cp in sends:
            cp.wait_send()
        mine.wait()

    result = jax.ShapeDtypeStruct((N_CHIP, rows, cols), sums.dtype)
    sems = [pltpu.SemaphoreType.DMA((3,)), pltpu.SemaphoreType.DMA((3,)), pltpu.SemaphoreType.DMA]
    if sequencer:
        return _launch_on_sequencer(body, sums, result, sems, CHIPS_COLLECTIVE_ID, name)
    return pl.pallas_call(body, name=name, out_shape=result, in_specs=[ANY], out_specs=ANY, scratch_shapes=sems)(sums)


def _reduce_scatter_slots(full, rows, cols, axis, wire_dtype, sequencer, name):
    land = _exchange_pair(full, rows, cols, axis, sequencer, name + '_pair')
    sums = _pair_add(full, land, rows, cols, axis, wire_dtype, name + '_add')
    return _exchange_chips(sums, sequencer, name + '_chips')


def _sum_slots(slots, name):
    n_slots, rows, cols = slots.shape
    tr, tc = _elementwise_tiles(rows, cols)

    def body(s_ref, o_ref):
        g = s_ref[0]
        for s in range(1, n_slots):
            g = g + s_ref[s]
        o_ref[...] = g

    return pl.pallas_call(
        body, name=name,
        grid=(rows // tr, cols // tc),
        in_specs=[pl.BlockSpec((n_slots, tr, tc), lambda i, j: (0, i, j))],
        out_specs=pl.BlockSpec((tr, tc), lambda i, j: (i, j)),
        out_shape=jax.ShapeDtypeStruct((rows, cols), F32),
        compiler_params=pltpu.CompilerParams(dimension_semantics=('parallel', 'parallel'),
                                             vmem_limit_bytes=VMEM_LIMIT_BYTES),
    )(slots)


def _sum_adamw(slots, w, m, v, name):
    n_slots, rows, cols = slots.shape
    tr, tc = _elementwise_tiles(rows, cols)
    c1 = 1.0 - ADAM_B1 ** ADAM_STEP
    c2 = 1.0 - ADAM_B2 ** ADAM_STEP

    def body(s_ref, w_ref, m_ref, v_ref, g_out, d_out, m_out, v_out):
        g = s_ref[0].astype(F32)
        for s in range(1, n_slots):
            g = g + s_ref[s].astype(F32)
        m_new = ADAM_B1 * m_ref[...] + (1.0 - ADAM_B1) * g
        v_new = ADAM_B2 * v_ref[...] + (1.0 - ADAM_B2) * (g * g)
        m_hat = m_new / c1
        v_hat = v_new / c2
        g_out[...] = g
        d_out[...] = -ADAM_LR * (m_hat / (jnp.sqrt(v_hat) + ADAM_EPS) + ADAM_WD * w_ref[...])
        m_out[...] = m_new
        v_out[...] = v_new

    blk = pl.BlockSpec((tr, tc), lambda i, j: (i, j))
    shape = jax.ShapeDtypeStruct((rows, cols), F32)
    return pl.pallas_call(
        body, name=name,
        grid=(rows // tr, cols // tc),
        in_specs=[pl.BlockSpec((n_slots, tr, tc), lambda i, j: (0, i, j)), blk, blk, blk],
        out_specs=[blk, blk, blk, blk],
        out_shape=[shape, shape, shape, shape],
        compiler_params=pltpu.CompilerParams(dimension_semantics=('parallel', 'parallel'),
                                             vmem_limit_bytes=VMEM_LIMIT_BYTES),
    )(slots, w, m, v)


def _rows_of(shape):
    size = int(np.prod(shape)) if len(shape) else 1
    return _round_up(_round_up(size, LANES) // LANES, SUBLANES)


def _pack(arrays, rows_multiple):
    parts = []
    for arr in arrays:
        flat = jnp.ravel(arr).astype(F32)
        rows = _rows_of(arr.shape)
        parts.append(jnp.pad(flat, (0, rows * LANES - flat.shape[0])).reshape(rows, LANES))
    total = sum(p.shape[0] for p in parts)
    pad = _round_up(total, rows_multiple) - total
    if pad:
        parts.append(jnp.zeros((pad, LANES), F32))
    return jnp.concatenate(parts, axis=0)


def _unpack(slab, shapes):
    out, r0 = [], 0
    for s in shapes:
        size = int(np.prod(s)) if len(s) else 1
        rows = _rows_of(s)
        out.append(slab[r0:r0 + rows].reshape(-1)[:size].reshape(s))
        r0 += rows
    return out


def _layer_norm(x, g, b):
    mu = jnp.mean(x, -1, keepdims=True)
    var = jnp.mean(jnp.square(x - mu), -1, keepdims=True)
    return (x - mu) * lax.rsqrt(var + LN_EPS) * g + b


def _modulate(x, shift, scale):
    return x * (1 + scale) + shift


def _take_cols(p, per_dev, per_dev_padded, lo, hi):
    parts = []
    while lo < hi:
        dev, off = divmod(lo, per_dev)
        n = min(hi - lo, per_dev - off)
        parts.append(p[:, dev * per_dev_padded + off: dev * per_dev_padded + off + n])
        lo += n
    return parts[0] if len(parts) == 1 else jnp.concatenate(parts, axis=1)


def _block_diag(blocks, nb):
    g, a, b = blocks.shape
    per = g // nb
    eye = jnp.eye(per, dtype=blocks.dtype)
    return (blocks.reshape(nb, per, a, 1, b) * eye[None, :, None, :, None]).reshape(nb, per * a, per * b)


def _s5_mixer(u, t_ctx, prm, name):
    t, gw = u.shape
    groups = gw // S5_CH
    ys = []
    for direction, reverse in enumerate((False, True)):
        lam = lax.complex(prm['s5_a_re'][direction], prm['s5_a_im'][direction])
        step = jnp.exp(prm['s5_log_step'][direction])[:, None]
        a_bar = jnp.exp(lam * step)
        b_bar = ((a_bar - 1.0) / lam)[..., None] * lax.complex(prm['s5_b_re'][direction],
                                                               prm['s5_b_im'][direction])
        c_mat = lax.complex(prm['s5_c_re'][direction], prm['s5_c_im'][direction])
        nb = (groups * S5_STATE) // _scan_lanes(groups * S5_STATE)
        b_t = jnp.swapaxes(b_bar, 1, 2)
        b_blocks = jnp.concatenate([_block_diag(jnp.real(b_t), nb), _block_diag(jnp.imag(b_t), nb)], axis=2)
        c_t = jnp.swapaxes(c_mat, 1, 2)
        c_blocks = jnp.concatenate([_block_diag(jnp.real(c_t), nb), -_block_diag(jnp.imag(c_t), nb)], axis=1)
        a_rows = jnp.stack([jnp.real(a_bar).reshape(-1), jnp.imag(a_bar).reshape(-1)])
        bu = _blocked_linear(u, b_blocks, f'{name}_bu{direction}')
        h = _s5_scan(a_rows, bu, t_ctx, reverse, f'{name}_scan{direction}')
        ys.append(_blocked_linear(h, c_blocks, f'{name}_y{direction}'))
    y = ys[0] + ys[1] + prm['s5_d'][None, :] * u
    z = jax.nn.gelu(y)
    return z * jax.nn.sigmoid(_linear(z, prm['s5_glu_w'], f'{name}_glu') + prm['s5_glu_b'])


def _chunk_gating(u, v, prm):
    t, gw = u.shape
    hd = gw // SG_HEADS
    u = jax.nn.gelu(u)
    v = jax.nn.gelu(v).reshape(t // SG_CHUNK, SG_CHUNK, SG_HEADS, hd)
    v = _layer_norm(v, prm['sg_ln_g'].reshape(SG_HEADS, hd), prm['sg_ln_b'].reshape(SG_HEADS, hd))
    s = jnp.einsum('hij,cjhd->cihd', prm['sg_w'], v) + prm['sg_b'].T[None, :, :, None]
    return u * s.reshape(t, gw)


def _pool_mixer(p, prm):
    l, gw = p.shape
    pd = gw // len(POOL_WINDOWS)
    t = np.arange(l)
    outs = []
    for g, win in enumerate(POOL_WINDOWS):
        lo = np.clip(t - win // 2, 0, l - 1)
        hi = np.clip(t + win // 2 - 1, 0, l - 1)
        pg = p[:, g * pd:(g + 1) * pd]
        padded = jnp.pad(pg, ((win // 2, win // 2), (0, 0)))
        total = padded[0:l]
        for d in range(1, win):
            total = total + padded[d:d + l]
        mean = total / jnp.asarray((hi - lo + 1).astype(np.float32))[:, None]
        outs.append(jnp.einsum('lc,cd->ld', mean - pg, prm['pool_w'][g]))
    y = jnp.concatenate(outs, axis=-1) + prm['pool_b']
    return y * prm['pool_scale']


def _dw_conv1d(x, w, b, pad):
    l = x.shape[0]
    xp = jnp.pad(x, (pad, (0, 0)))
    y = xp[0:l] * w[0]
    for k in range(1, w.shape[0]):
        y = y + xp[k:k + l] * w[k]
    return y + b


def _m2_prepare(xbc, dt_raw, prm, gw):
    heads = gw // M2_HEAD_DIM
    xbc = jax.nn.silu(_dw_conv1d(xbc, prm['m2_conv_w'], prm['m2_conv_b'], M2_PAD))
    l = xbc.shape[0]
    n_bc = M2_GROUPS * M2_STATE
    rep = heads // M2_GROUPS
    xs = xbc[:, :gw].reshape(l, heads, M2_HEAD_DIM)
    bm = jnp.repeat(xbc[:, gw:gw + n_bc].reshape(l, M2_GROUPS, M2_STATE), rep, axis=1)
    cm = jnp.repeat(xbc[:, gw + n_bc:].reshape(l, M2_GROUPS, M2_STATE), rep, axis=1)
    dt = jax.nn.softplus(dt_raw.reshape(l, 2, heads) + prm['m2_dt_bias'])
    return xs, bm, cm, dt


def _walk_cumsum(x, axis, reverse):
    return jnp.flip(jnp.cumsum(jnp.flip(x, axis), axis), axis) if reverse else jnp.cumsum(x, axis)


def _ssd_scan(xs, dt, a, bm, cm, h0, need_y, reverse):
    l, nh, hp = xs.shape
    nc = l // M2_CHUNK

    def chunks(t):
        return t.reshape((nc, M2_CHUNK) + t.shape[1:])

    def visited_before(n, strictly):
        ones = jnp.ones((n, n), bool)
        return jnp.triu(ones, 1 if strictly else 0) if reverse else jnp.tril(ones, -1 if strictly else 0)

    xd = chunks(xs * dt[..., None])
    bc, cc = chunks(bm), chunks(cm)
    a_cum = _walk_cumsum(chunks(dt * a), 1, reverse)
    a_tot = a_cum[:, 0] if reverse else a_cum[:, -1]
    decay_end = jnp.exp(a_tot[:, None] - a_cum)
    chunk_states = jnp.einsum('cqhn,cqh,cqhp->chpn', bc, decay_end, xd)

    cum = _walk_cumsum(a_tot, 0, reverse)
    before = cum - a_tot
    carry_w = jnp.exp(jnp.where(visited_before(nc, True)[:, :, None], before[:, None, :] - cum[None, :, :],
                                -jnp.inf))
    whole = cum[0] if reverse else cum[-1]
    h_final = (jnp.exp(whole)[:, None, None] * h0
               + jnp.einsum('dh,dhpn->hpn', jnp.exp(whole[None, :] - cum), chunk_states,
                            precision=lax.Precision.HIGHEST))
    if not need_y:
        return None, h_final
    h_prev = (jnp.exp(before)[:, :, None, None] * h0[None]
              + jnp.einsum('cdh,dhpn->chpn', carry_w, chunk_states, precision=lax.Precision.HIGHEST))
    seg = a_cum[:, :, None, :] - a_cum[:, None, :, :]
    decay = jnp.exp(jnp.where(visited_before(M2_CHUNK, False)[None, :, :, None], seg, -jnp.inf))
    scores = jnp.einsum('cihn,cjhn->cijh', cc, bc) * decay
    y = (jnp.einsum('cijh,cjhp->cihp', scores, xd)
         + jnp.einsum('cihn,chpn->cihp', cc, h_prev) * jnp.exp(a_cum)[..., None])
    return y.reshape(l, nh, hp), h_final


def _ssd_direction(inputs, direction, a, h0, need_y):
    xs, bm, cm, dt = inputs
    return _ssd_scan(xs, dt[:, direction], a, bm, cm, h0, need_y, direction == 1)


def _gated_rmsnorm(y, z, w):
    l, gw = z.shape
    g = (y * jax.nn.silu(z)).reshape(l, M2_GROUPS, gw // M2_GROUPS)
    g = g * lax.rsqrt(jnp.mean(jnp.square(g), -1, keepdims=True) + RMS_EPS)
    return g.reshape(l, gw) * w


def _mamba2_mixer(z, xbc, dt_raw, t_ctx, prm, need_ctx):
    gw = z.shape[1]
    heads = gw // M2_HEAD_DIM
    ctx_in = _m2_prepare(xbc[:t_ctx], dt_raw[:t_ctx], prm, gw)
    lat_in = _m2_prepare(xbc[t_ctx:], dt_raw[t_ctx:], prm, gw)
    a = -jnp.exp(prm['m2_a_log'])
    ys_ctx, ys_lat = [], []
    for direction in range(2):
        h0 = jnp.zeros((heads, M2_HEAD_DIM, M2_STATE), F32)
        y_c, h_c = _ssd_direction(ctx_in, direction, a[direction], h0, need_ctx)
        y_l, _ = _ssd_direction(lat_in, direction, a[direction], h_c, True)
        ys_lat.append(y_l)
        if need_ctx:
            ys_ctx.append(y_c)
    d_h = prm['m2_d'][None, :, None]

    def finish(ys, xs, zz):
        y = (ys[0] + ys[1] + d_h * xs).reshape(zz.shape[0], gw)
        return _gated_rmsnorm(y, zz, prm['m2_norm_w'])

    y_lat = finish(ys_lat, lat_in[0], z[t_ctx:])
    if need_ctx:
        return jnp.concatenate([finish(ys_ctx, ctx_in[0], z[:t_ctx]), y_lat], axis=0)
    return y_lat


def _forward_loss(dp, consts, dims):
    depth, t_ctx, d_model = dims['depth'], dims['t_ctx'], dims['d_model']
    gw = d_model // 4
    alpha = (2 * depth) ** 0.25
    in_sizes = (gw, gw, gw, gw, gw, gw + 2 * M2_GROUPS * M2_STATE, 2 * (gw // M2_HEAD_DIM))
    in_offs = np.concatenate([[0], np.cumsum(in_sizes)])
    ml = [[dp['mod_lat'][i, k][None, :] for k in range(6)] for i in range(depth)]
    mc = [[dp['mod_ctx'][i, k][None, :] for k in range(6)] for i in range(depth)]
    zero_row = jnp.zeros((1, d_model), F32)
    h = jnp.concatenate([consts['ctx'], dp['x']], axis=0)
    hm32 = jnp.concatenate([_modulate(consts['ctx'], mc[0][0], mc[0][1]), _modulate(dp['x'], ml[0][0], ml[0][1])],
                           axis=0)
    hm, carrier = hm32.astype(BF16), hm32 - lax.stop_gradient(hm32)
    for i in range(depth):
        need_ctx = i < depth - 1
        prm = {k: v[i] for k, v in dp['small'].items()}
        p = _linear_tap_carried(hm, carrier, consts['w_in'][i], dp['taps']['w_in'][i], f'in{i}')
        seg = [p[:, int(in_offs[s]):int(in_offs[s + 1])] for s in range(7)]
        ya = _s5_mixer(seg[0], t_ctx, prm, f's5_{i}')
        yb = _chunk_gating(seg[1], seg[2], prm)
        yc = jnp.concatenate([_pool_mixer(seg[3][:t_ctx], prm), _pool_mixer(seg[3][t_ctx:], prm)], axis=0)
        yd = _mamba2_mixer(seg[4], seg[5], seg[6], t_ctx, prm, need_ctx)
        if need_ctx:
            mix_in = jnp.concatenate([ya, yb, yc, yd], axis=1)
        else:
            mix_in = jnp.concatenate([ya[t_ctx:], yb[t_ctx:], yc[t_ctx:], yd], axis=1)
        mix = _linear_tap(mix_in, consts['w_out'][i], dp['taps']['w_out'][i], f'out{i}')
        t_c = t_ctx if need_ctx else 0
        if not need_ctx and h.shape[0] != mix.shape[0]:
            h = h[t_ctx:]
        mods = jnp.concatenate([mc[i][2], mc[i][3], mc[i][4], ml[i][2], ml[i][3], ml[i][4],
                                prm['ln1_g'][None], prm['ln1_b'][None]], axis=0)
        h, hm, carrier = _res_ln_mod(h, mix, mods, jnp.zeros(h.shape, F32), t_c, alpha, True, f'ln1_{i}')

        f_out = _ffn_block(hm, carrier, consts['ffn_w_up'][i], dp['taps']['ffn_w_up'][i], prm['ffn_conv_w'],
                           prm['ffn_conv_b'], consts['ffn_w_down'][i], dp['taps']['ffn_w_down'][i], t_c,
                           (dims['up_per_dev'], dims['up_per_dev_pad'], dims['ffn_hidden']), f'ffn{i}')
        if need_ctx:
            mods = jnp.concatenate([mc[i][5], mc[i + 1][0], mc[i + 1][1], ml[i][5], ml[i + 1][0], ml[i + 1][1],
                                    prm['ln2_g'][None], prm['ln2_b'][None]], axis=0)
            h, hm, carrier = _res_ln_mod(h, f_out, mods, jnp.zeros(h.shape, F32), t_c, alpha, True, f'ln2_{i}')
        else:
            mods = jnp.concatenate([mc[i][5], zero_row, zero_row, ml[i][5], zero_row, zero_row,
                                    prm['ln2_g'][None], prm['ln2_b'][None]], axis=0)
            h = _res_ln_mod(h, f_out, mods, jnp.zeros(h.shape, F32), t_c, alpha, False, f'ln2_{i}')[0]
    err = jnp.square(h - consts['target'])
    return 0.5 * jnp.sum(jnp.mean(err, axis=-1))


def _silu_grad(x):
    s = jax.nn.sigmoid(x)
    return s * (1 + x * (1 - s))


def _step(w, m, v, x, c, ctx, target):
    depth, d_model, ada_cols = w['w_ada'].shape
    t_ctx, t_lat = ctx.shape[1], x.shape[1]
    me = _my_index()
    in_per_dev = w['w_in'].shape[2]
    in_pad = _round_up(in_per_dev, LANES)
    in_width_pad = _round_up(N_DEV * in_per_dev, 4 * LANES)
    up_per_dev = w['ffn_w_up'].shape[2]
    up_pad = _round_up(up_per_dev, LANES)
    f_hidden = w['ffn_w_down'].shape[1] * N_DEV
    dims = dict(depth=depth, t_ctx=t_ctx, d_model=d_model, in_per_dev=in_per_dev, in_per_dev_pad=in_pad,
                up_per_dev=up_per_dev, up_per_dev_pad=up_pad, ffn_hidden=f_hidden)

    rows16 = 2 * SUBLANES
    silu_c_all = _all_gather(jnp.pad(jax.nn.silu(c), ((0, SUBLANES - 1), (0, 0))), 0, 'ag_c')
    silu_c_all = silu_c_all.reshape(N_DEV, SUBLANES, d_model)[:, 0]
    silu_cc = jax.nn.silu(w['c_ctx'])
    ada_in = jnp.concatenate([silu_c_all, silu_cc[None], jnp.zeros((rows16 - N_DEV - 1, d_model), F32)], axis=0)
    mod_loc = jnp.concatenate([_matmul(ada_in, w['w_ada'][i], 'nn', 'ada_fwd') for i in range(depth)], axis=0)
    mod_all = _all_gather(mod_loc, 1, 'ag_mod').reshape(depth, rows16, 6 * d_model)
    mod_all = mod_all + w['b_ada'][:, None, :]
    mod_lat = lax.dynamic_index_in_dim(mod_all, me, axis=1, keepdims=False).reshape(depth, 6, d_model)
    mod_ctx = mod_all[:, N_DEV].reshape(depth, 6, d_model)

    def pad_cols(a, to):
        return jnp.pad(a, ((0, 0), (0, to - a.shape[1])))

    gathered = {n: [] for n in BIG}
    for i in range(depth):
        w_in_dm = _all_gather_sequencer(pad_cols(w['w_in'][i], in_pad).astype(BF16), 1, f'ag_w_in{i}')
        gathered['w_in'].append(pad_cols(jnp.concatenate(
            [w_in_dm[:, j * in_pad:j * in_pad + in_per_dev] for j in range(N_DEV)], axis=1), in_width_pad))
        gathered['w_out'].append(_all_gather_sequencer(w['w_out'][i].astype(BF16), 0, f'ag_w_out{i}'))
        gathered['ffn_w_up'].append(
            _all_gather_sequencer(pad_cols(w['ffn_w_up'][i], up_pad).astype(BF16), 1, f'ag_w_up{i}'))
        gathered['ffn_w_down'].append(_all_gather_sequencer(w['ffn_w_down'][i].astype(BF16), 0, f'ag_w_down{i}'))
    sharded_small = [w[n] for n in COL_SHARDED_SMALL + ROW_SHARDED_SMALL]
    small_slab = _pack(sharded_small, PACK_ROWS)
    slab_rows = small_slab.shape[0]
    small_all = _all_gather(small_slab, 0, 'ag_small').reshape(N_DEV, slab_rows, LANES)
    per_dev = [_unpack(small_all[d], [a.shape for a in sharded_small]) for d in range(N_DEV)]
    small_full = {}
    for k, n in enumerate(COL_SHARDED_SMALL):
        small_full[n] = jnp.concatenate([per_dev[d][k] for d in range(N_DEV)], axis=-1)
    for k, n in enumerate(ROW_SHARDED_SMALL):
        small_full[n] = jnp.concatenate([per_dev[d][len(COL_SHARDED_SMALL) + k] for d in range(N_DEV)], axis=1)

    small = {n: w[n] for n in REPLICATED}
    small.update(small_full)
    taps = {n: [jnp.zeros(gathered[n][i].shape, F32) for i in range(depth)] for n in BIG}
    dp = dict(x=x[0], small=small, mod_lat=mod_lat, mod_ctx=mod_ctx, taps=taps)
    consts = dict(ctx=ctx[0], target=target[0], **gathered)
    loss_local, grads = jax.value_and_grad(functools.partial(_forward_loss, consts=consts, dims=dims))(dp)

    out_g, out_d, out_m, out_v = {}, {}, {}, {}

    def finish_big(name, i, rows, cols, axis, keep_cols):
        full = grads['taps'][name][i]
        if name == 'w_in':
            full = jnp.concatenate([pad_cols(full[:, j * in_per_dev:(j + 1) * in_per_dev], in_pad)
                                    for j in range(N_DEV)], axis=1)
        slots = _reduce_scatter_slots(full, rows, cols, axis, BF16, True, f'rs_{name}{i}')
        shard = [pad_cols(t[name][i], cols) for t in (w, m, v)]
        res = _sum_adamw(slots, *shard, f'adamw_{name}')
        return [r[:, :keep_cols] for r in res]

    big_specs = {
        'ffn_w_down': (w['ffn_w_down'].shape[1], d_model, 0, d_model),
        'ffn_w_up': (d_model, up_pad, 1, up_per_dev),
        'w_out': (w['w_out'].shape[1], d_model, 0, d_model),
        'w_in': (d_model, in_pad, 1, in_per_dev),
    }
    per_layer = {name: [None] * depth for name in big_specs}
    for i in reversed(range(depth)):
        for name, (rows, cols, axis, keep) in big_specs.items():
            per_layer[name][i] = finish_big(name, i, rows, cols, axis, keep)
    for name in big_specs:
        for k, dst in enumerate((out_g, out_d, out_m, out_v)):
            dst[name] = jnp.stack([per_layer[name][i][k] for i in range(depth)])

    d_lat = grads['mod_lat'].reshape(depth, 6 * d_model)
    d_ctx = grads['mod_ctx'].reshape(depth, 6 * d_model)
    d_rows = jnp.concatenate([d_lat, d_ctx, jnp.zeros((SUBLANES - 2 * depth, 6 * d_model), F32)], axis=0)
    d_all = _all_gather(d_rows, 0, 'ag_dmod').reshape(N_DEV, SUBLANES, 6 * d_model)
    d_lat_all = d_all[:, :depth]
    d_ctx_sum = d_all[0, depth:2 * depth]
    for d in range(1, N_DEV):
        d_ctx_sum = d_ctx_sum + d_all[d, depth:2 * depth]
    g_b_ada = d_ctx_sum
    for d in range(N_DEV):
        g_b_ada = g_b_ada + d_lat_all[d]
    g_w_ada, c_ctx_part = [], jnp.zeros((d_model,), F32)
    for i in range(depth):
        d_mat = jnp.concatenate([d_lat_all[:, i], d_ctx_sum[i][None],
                                 jnp.zeros((rows16 - N_DEV - 1, 6 * d_model), F32)], axis=0)
        d_mine = lax.dynamic_slice_in_dim(d_mat, me * ada_cols, ada_cols, axis=1)
        g_w_ada.append(_matmul(ada_in, d_mine, 'tn', 'ada_dw'))
        back = _matmul(d_mine, w['w_ada'][i], 'nt', 'ada_dx')
        c_ctx_part = c_ctx_part + back[N_DEV]
    c_ctx_part = c_ctx_part * _silu_grad(w['c_ctx'])
    g_w_ada = jnp.stack(g_w_ada).reshape(1, depth * d_model, ada_cols)
    res = _sum_adamw(g_w_ada, *[t['w_ada'].reshape(depth * d_model, ada_cols) for t in (w, m, v)], 'adamw_w_ada')
    for k, dst in enumerate((out_g, out_d, out_m, out_v)):
        dst['w_ada'] = res[k].reshape(depth, d_model, ada_cols)

    reduced_names = REPLICATED[:]
    reduced_names.remove('b_ada')
    reduced_names += list(COL_SHARDED_SMALL + ROW_SHARDED_SMALL)
    to_reduce = [grads['small'][n] for n in reduced_names] + [c_ctx_part, loss_local]
    slab = _pack(to_reduce, N_DEV * PACK_ROWS)
    chunk_rows = slab.shape[0] // N_DEV
    slots = _reduce_scatter_slots(slab, chunk_rows, LANES, 0, F32, False, 'rs_small')
    mine = _sum_slots(slots, 'sum_small')
    summed = _all_gather(mine, 0, 'ag_small_sum')
    parts = _unpack(summed, [a.shape for a in to_reduce])
    g_small = dict(zip(reduced_names, parts[:len(reduced_names)]))
    g_small['c_ctx'] = parts[-2]
    g_small['b_ada'] = g_b_ada
    loss = parts[-1]

    def my_shard(name, full):
        if name in COL_SHARDED_SMALL:
            n = full.shape[-1] // N_DEV
            return lax.dynamic_slice_in_dim(full, me * n, n, axis=full.ndim - 1)
        if name in ROW_SHARDED_SMALL:
            n = full.shape[1] // N_DEV
            return lax.dynamic_slice_in_dim(full, me * n, n, axis=1)
        return full

    small_names = [n for n in WEIGHTS if n not in BIG + ('w_ada',)]
    g_list = [my_shard(n, g_small[n]) for n in small_names]
    g_slab = _pack(g_list, PACK_ROWS)
    res = _sum_adamw(g_slab[None], *[_pack([t[n] for n in small_names], PACK_ROWS) for t in (w, m, v)],
                     'adamw_small')
    shapes = [w[n].shape for n in small_names]
    for k, dst in enumerate((out_g, out_d, out_m, out_v)):
        if k == 0:
            dst.update(dict(zip(small_names, g_list)))
        else:
            dst.update(dict(zip(small_names, _unpack(res[k], shapes))))

    grad_x = grads['x'][None]
    return (loss, grad_x, *[out_g[n] for n in WEIGHTS], *[out_d[n] for n in WEIGHTS],
            *[out_m[n] for n in WEIGHTS], *[out_v[n] for n in WEIGHTS])


def kernel(x, c, ctx, c_ctx, w_ada, b_ada, w_in, w_out, ln1_g, ln1_b, ln2_g, ln2_b, s5_a_re, s5_a_im, s5_b_re, s5_b_im, s5_c_re, s5_c_im, s5_log_step, s5_d, s5_glu_w, s5_glu_b, sg_ln_g, sg_ln_b, sg_w, sg_b, pool_w, pool_b, pool_scale, m2_conv_w, m2_conv_b, m2_dt_bias, m2_a_log, m2_d, m2_norm_w, ffn_w_up, ffn_conv_w, ffn_conv_b, ffn_w_down, loss_target, m_c_ctx, m_w_ada, m_b_ada, m_w_in, m_w_out, m_ln1_g, m_ln1_b, m_ln2_g, m_ln2_b, m_s5_a_re, m_s5_a_im, m_s5_b_re, m_s5_b_im, m_s5_c_re, m_s5_c_im, m_s5_log_step, m_s5_d, m_s5_glu_w, m_s5_glu_b, m_sg_ln_g, m_sg_ln_b, m_sg_w, m_sg_b, m_pool_w, m_pool_b, m_pool_scale, m_m2_conv_w, m_m2_conv_b, m_m2_dt_bias, m_m2_a_log, m_m2_d, m_m2_norm_w, m_ffn_w_up, m_ffn_conv_w, m_ffn_conv_b, m_ffn_w_down, v_c_ctx, v_w_ada, v_b_ada, v_w_in, v_w_out, v_ln1_g, v_ln1_b, v_ln2_g, v_ln2_b, v_s5_a_re, v_s5_a_im, v_s5_b_re, v_s5_b_im, v_s5_c_re, v_s5_c_im, v_s5_log_step, v_s5_d, v_s5_glu_w, v_s5_glu_b, v_sg_ln_g, v_sg_ln_b, v_sg_w, v_sg_b, v_pool_w, v_pool_b, v_pool_scale, v_m2_conv_w, v_m2_conv_b, v_m2_dt_bias, v_m2_a_log, v_m2_d, v_m2_norm_w, v_ffn_w_up, v_ffn_conv_w, v_ffn_conv_b, v_ffn_w_down):
    given = dict(locals())
    w = {n: given[n] for n in WEIGHTS}
    m = {n: given['m_' + n] for n in WEIGHTS}
    v = {n: given['v_' + n] for n in WEIGHTS}
    return _step(w, m, v, x, c, ctx, loss_target)
```
